```python
import math
import jax, jax.numpy as jnp
from jax import lax
import numpy as np

D_MODEL = 1024
BATCH = 16
SEQ = 4096
DEPTH = 1
DEC_BATCH = 16
DEC_SEQ = 32
PAST_LEN = 1024

CHUNK = 64
QBLOCK = 128
H_D = 4
DK_D = 64
DV_D = 128
H_R = 4
DK_R = 128
DV_R = 128
MIX_WIDTH = H_D * DV_D + H_R * DV_R
IN_COLS = 2 * H_D * 2 * DK_D + H_D * DV_D + 2 * H_R * DK_R + 2 * H_R * DV_R
N_MEM = 256
H_X = 4
HD_X = D_MODEL // H_X
PEER_HEADS = 8
N_KEYS = 128
N_EXPERTS = N_KEYS * N_KEYS
PEER_TOPK = 16
D_KEY = 256
PEER_BLOCK = 256
EPS = 1e-6

kernel_name = 'hybrid_diffattn_retnet_peer_stream_step'


def rmsnorm(x, g):
    xf = x.astype(jnp.float32)
    y = xf * lax.rsqrt(jnp.mean(xf * xf, axis=-1, keepdims=True) + EPS)
    return (y * g.astype(jnp.float32)).astype(x.dtype)


def project_mixer(hn, w_in):
    B, T = hn.shape[0], hn.shape[1]
    y = hn @ w_in
    sizes = [H_D * 2 * DK_D, H_D * 2 * DK_D, H_D * DV_D, H_R * DK_R, H_R * DK_R, H_R * DV_R, H_R * DV_R]
    splits = np.cumsum(sizes)[:-1].tolist()
    qd, kd, vd, qr, kr, vr, gr = jnp.split(y, splits, axis=-1)
    return (qd.reshape(B, T, H_D, 2, DK_D), kd.reshape(B, T, H_D, 2, DK_D), vd.reshape(B, T, H_D, DV_D),
            qr.reshape(B, T, H_R, DK_R), kr.reshape(B, T, H_R, DK_R) * (DK_R ** -0.5),
            vr.reshape(B, T, H_R, DV_R), gr)


def alibi_slopes():
    return 2.0 ** (-8.0 * jnp.arange(1, H_D + 1, dtype=jnp.float32) / H_D)


def diff_lambda(lq1, lk1, lq2, lk2, lam_init):
    f = lambda a: a.astype(jnp.float32)
    return jnp.exp(jnp.sum(f(lq1) * f(lk1))) - jnp.exp(jnp.sum(f(lq2) * f(lk2))) + lam_init


def diff_core(q, k, v, qpos, kpos, lam, mask):
    s = jnp.einsum('bqhmd,bkhmd->mbhqk', q.astype(jnp.float32), k.astype(jnp.float32)) * (DK_D ** -0.5)
    dist = jnp.abs(qpos[:, None] - kpos[None, :]).astype(jnp.float32)
    s = s - alibi_slopes()[None, None, :, None, None] * dist
    if mask is not None:
        s = jnp.where(mask, s, -1e30)
    p = jax.nn.softmax(s, axis=-1)
    a = p[0] - lam * p[1]
    return jnp.einsum('bhqk,bkhe->bqhe', a, v.astype(jnp.float32))


def diff_post(o, lam_init, g):
    B, T = o.shape[0], o.shape[1]
    y = o * lax.rsqrt(jnp.mean(o * o, axis=-1, keepdims=True) + EPS) * g.astype(jnp.float32) * (1.0 - lam_init)
    return y.reshape(B, T, H_D * DV_D)


def ret_chunk(S, q, k, v):
    L = q.shape[1]
    log_g = jnp.log1p(-(2.0 ** (-5.0 - jnp.arange(H_R, dtype=jnp.float32))))
    i = jnp.arange(L, dtype=jnp.float32)
    diff = i[:, None] - i[None, :]
    decay = jnp.where(diff >= 0, jnp.exp(jnp.maximum(diff, 0.0)[None] * log_g[:, None, None]), 0.0)
    inner = jnp.einsum('bihd,bjhd->bhij', q, k) * decay[None]
    cross_scale = jnp.exp((i[:, None] + 1.0) * log_g[None, :])
    o = jnp.einsum('bhij,bjhe->bihe', inner, v) + jnp.einsum('bihd,bhde->bihe', q, S) * cross_scale[None, :, :, None]
    tail = jnp.exp((L - 1.0 - i)[:, None] * log_g[None, :])
    S_new = jnp.exp(L * log_g)[None, :, None, None] * S + jnp.einsum('bjhd,bjhe,jh->bhde', k, v, tail)
    return S_new, o


def ret_post(o, gate, g):
    B, T = o.shape[0], o.shape[1]
    mu = jnp.mean(o, axis=-1, keepdims=True)
    oc = o - mu
    y = oc * lax.rsqrt(jnp.mean(oc * oc, axis=-1, keepdims=True) + EPS) * g.astype(jnp.float32)
    return y.reshape(B, T, H_R * DV_R) * jax.nn.silu(gate.astype(jnp.float32))


def mem_kv(mem, g_mem, w_xk, w_xv):
    B = mem.shape[0]
    mn = rmsnorm(mem, g_mem)
    return (mn @ w_xk).reshape(B, N_MEM, H_X, HD_X), (mn @ w_xv).reshape(B, N_MEM, H_X, HD_X)


def cross_attn(hn, mk, mv, w_xq, w_xo):
    B, T = hn.shape[0], hn.shape[1]
    q = (hn @ w_xq).reshape(B, T, H_X, HD_X)
    s = jnp.einsum('bthd,bmhd->bhtm', q.astype(jnp.float32), mk.astype(jnp.float32)) * (HD_X ** -0.5)
    p = jax.nn.softmax(s, axis=-1)
    o = jnp.einsum('bhtm,bmhd->bthd', p, mv.astype(jnp.float32)).reshape(B, T, D_MODEL)
    return o.astype(hn.dtype) @ w_xo


def peer(h, w_pq, pk1, pk2, u_tab, v_tab):
    B, T, D = h.shape
    n = B * T
    pad = (-n) % PEER_BLOCK
    blocks = jnp.pad(h.reshape(n, D), ((0, pad), (0, 0))).reshape(-1, PEER_BLOCK, D)

    def one_block(hb):
        q = (hb @ w_pq).astype(jnp.float32).reshape(PEER_BLOCK, PEER_HEADS, 2, D_KEY // 2)
        s1 = jnp.einsum('tpd,pnd->tpn', q[:, :, 0], pk1.astype(jnp.float32))
        s2 = jnp.einsum('tpd,pnd->tpn', q[:, :, 1], pk2.astype(jnp.float32))
        v1, i1 = lax.top_k(s1, PEER_TOPK)
        v2, i2 = lax.top_k(s2, PEER_TOPK)
        cand = (v1[..., :, None] + v2[..., None, :]).reshape(PEER_BLOCK, PEER_HEADS, PEER_TOPK * PEER_TOPK)
        cidx = (i1[..., :, None] * N_KEYS + i2[..., None, :]).reshape(PEER_BLOCK, PEER_HEADS, PEER_TOPK * PEER_TOPK)
        sc, pos = lax.top_k(cand, PEER_TOPK)
        e = jnp.take_along_axis(cidx, pos, axis=-1)
        gate = jax.nn.softmax(sc, axis=-1)
        u = jnp.take(u_tab, e, axis=0).astype(jnp.float32)
        act = jax.nn.gelu(jnp.einsum('tpkd,td->tpk', u, hb.astype(jnp.float32)), approximate=False)
        vv = jnp.take(v_tab, e, axis=0).astype(jnp.float32)
        return jnp.einsum('tpk,tpkd->td', gate * act, vv).astype(h.dtype)

    out = lax.map(one_block, blocks).reshape(-1, D)[:n]
    return out.reshape(B, T, D)


def setup_inputs(seed: int = 0) -> dict:
    key = jax.random.key(seed)
    ks = jax.random.split(key, 32)
    nrm = lambda k, shape, s: jax.random.normal(k, shape, jnp.float32) * s
    gain = lambda k, shape: 1.0 + 0.01 * jax.random.normal(k, shape, jnp.float32)
    return {
        'x_prompt': nrm(ks[0], (BATCH, SEQ, D_MODEL), 1.0),
        'x_sample': nrm(ks[1], (DEC_BATCH, DEC_SEQ, D_MODEL), 1.0),
        'mem_prompt': nrm(ks[2], (BATCH, N_MEM, D_MODEL), 1.0),
        'cache_da_k': nrm(ks[3], (DEPTH, DEC_BATCH, PAST_LEN, H_D, 2, DK_D), 1.0),
        'cache_da_v': nrm(ks[4], (DEPTH, DEC_BATCH, PAST_LEN, H_D, DV_D), 1.0),
        'state_ret': nrm(ks[5], (DEPTH, DEC_BATCH, H_R, DK_R, DV_R), 1.0),
        'cache_mem_k': nrm(ks[6], (DEPTH, DEC_BATCH, N_MEM, H_X, HD_X), 1.0),
        'cache_mem_v': nrm(ks[7], (DEPTH, DEC_BATCH, N_MEM, H_X, HD_X), 1.0),
        'g_mix': gain(ks[8], (DEPTH, D_MODEL)),
        'w_in': nrm(ks[9], (DEPTH, D_MODEL, IN_COLS), D_MODEL ** -0.5),
        'lam_q1': nrm(ks[10], (DEPTH, DK_D), 0.1),
        'lam_k1': nrm(ks[11], (DEPTH, DK_D), 0.1),
        'lam_q2': nrm(ks[12], (DEPTH, DK_D), 0.1),
        'lam_k2': nrm(ks[13], (DEPTH, DK_D), 0.1),
        'g_da': gain(ks[14], (DEPTH, H_D, DV_D)),
        'g_ret': gain(ks[15], (DEPTH, H_R, DV_R)),
        'w_out': nrm(ks[16], (DEPTH, MIX_WIDTH, D_MODEL), MIX_WIDTH ** -0.5),
        'g_cross': gain(ks[17], (DEPTH, D_MODEL)),
        'g_mem': gain(ks[18], (DEPTH, D_MODEL)),
        'w_xq': nrm(ks[19], (DEPTH, D_MODEL, D_MODEL), D_MODEL ** -0.5),
        'w_xk': nrm(ks[20], (DEPTH, D_MODEL, D_MODEL), D_MODEL ** -0.5),
        'w_xv': nrm(ks[21], (DEPTH, D_MODEL, D_MODEL), D_MODEL ** -0.5),
        'w_xo': nrm(ks[22], (DEPTH, D_MODEL, D_MODEL), D_MODEL ** -0.5),
        'g_ffn': gain(ks[23], (DEPTH, D_MODEL)),
        'w_pq': nrm(ks[24], (DEPTH, D_MODEL, PEER_HEADS * D_KEY), D_MODEL ** -0.5),
        'peer_k1': nrm(ks[25], (DEPTH, PEER_HEADS, N_KEYS, D_KEY // 2), (D_KEY // 2) ** -0.5),
        'peer_k2': nrm(ks[26], (DEPTH, PEER_HEADS, N_KEYS, D_KEY // 2), (D_KEY // 2) ** -0.5),
        'peer_u': nrm(ks[27], (DEPTH, N_EXPERTS, D_MODEL), D_MODEL ** -0.5),
        'peer_v': nrm(ks[28], (DEPTH, N_EXPERTS, D_MODEL), PEER_HEADS ** -0.5),
        'g_final': gain(ks[29], (D_MODEL,)),
    }


def reference(x_prompt, x_sample, mem_prompt, cache_da_k, cache_da_v, state_ret, cache_mem_k, cache_mem_v,
              g_mix, w_in, lam_q1, lam_k1, lam_q2, lam_k2, g_da, g_ret, w_out, g_cross, g_mem,
              w_xq, w_xk, w_xv, w_xo, g_ffn, w_pq, peer_k1, peer_k2, peer_u, peer_v, g_final):
    x = x_prompt
    B, T = x.shape[0], x.shape[1]
    pos = jnp.arange(T)
    kp_list, vp_list, sp_list, mkp_list, mvp_list = [], [], [], [], []
    for l in range(DEPTH):
        lam_init = 0.8 - 0.6 * math.exp(-0.3 * l)
        lam = diff_lambda(lam_q1[l], lam_k1[l], lam_q2[l], lam_k2[l], lam_init)
        hn = rmsnorm(x, g_mix[l])
        qd, kd, vd, qr, kr, vr, gr = project_mixer(hn, w_in[l])

        def attn_block(i, qd=qd, kd=kd, vd=vd, lam=lam):
            qb = lax.dynamic_slice_in_dim(qd, i * QBLOCK, QBLOCK, axis=1)
            qpos = i * QBLOCK + jnp.arange(QBLOCK)
            mask = (qpos // CHUNK)[:, None] >= (pos // CHUNK)[None, :]
            return diff_core(qb, kd, vd, qpos, pos, lam, mask)

        od = lax.map(attn_block, jnp.arange(T // QBLOCK))
        od = jnp.moveaxis(od, 0, 1).reshape(B, T, H_D, DV_D)

        to_chunks = lambda a: jnp.moveaxis(a.astype(jnp.float32).reshape(B, T // CHUNK, CHUNK, a.shape[2], a.shape[3]), 1, 0)
        S0 = jnp.zeros((B, H_R, DK_R, DV_R), jnp.float32)
        S_fin, orr = lax.scan(lambda S, qkv: ret_chunk(S, qkv[0], qkv[1], qkv[2]), S0,
                              (to_chunks(qr), to_chunks(kr), to_chunks(vr)))
        orr = jnp.moveaxis(orr, 0, 1).reshape(B, T, H_R, DV_R)

        mixed = jnp.concatenate([diff_post(od, lam_init, g_da[l]), ret_post(orr, gr, g_ret[l])], axis=-1)
        x = x + mixed.astype(x.dtype) @ w_out[l]
        mk, mv = mem_kv(mem_prompt, g_mem[l], w_xk[l], w_xv[l])
        x = x + cross_attn(rmsnorm(x, g_cross[l]), mk, mv, w_xq[l], w_xo[l])
        x = x + peer(rmsnorm(x, g_ffn[l]), w_pq[l], peer_k1[l], peer_k2[l], peer_u[l], peer_v[l])
        kp_list.append(kd)
        vp_list.append(vd)
        sp_list.append(S_fin)
        mkp_list.append(mk)
        mvp_list.append(mv)
    y_prompt = rmsnorm(x, g_final)

    xs = x_sample
    Bs, Ts = xs.shape[0], xs.shape[1]
    qpos_s = PAST_LEN + jnp.arange(Ts)
    kpos_s = jnp.arange(PAST_LEN + Ts)
    ks_list, vs_list, ss_list = [], [], []
    for l in range(DEPTH):
        lam_init = 0.8 - 0.6 * math.exp(-0.3 * l)
        lam = diff_lambda(lam_q1[l], lam_k1[l], lam_q2[l], lam_k2[l], lam_init)
        hn = rmsnorm(xs, g_mix[l])
        qd, kd, vd, qr, kr, vr, gr = project_mixer(hn, w_in[l])
        k_all = jnp.concatenate([cache_da_k[l].astype(kd.dtype), kd], axis=1)
        v_all = jnp.concatenate([cache_da_v[l].astype(vd.dtype), vd], axis=1)
        od = diff_core(qd, k_all, v_all, qpos_s, kpos_s, lam, None)
        S_new, orr = ret_chunk(state_ret[l].astype(jnp.float32), qr.astype(jnp.float32),
                               kr.astype(jnp.float32), vr.astype(jnp.float32))
        mixed = jnp.concatenate([diff_post(od, lam_init, g_da[l]), ret_post(orr, gr, g_ret[l])], axis=-1)
        xs = xs + mixed.astype(xs.dtype) @ w_out[l]
        xs = xs + cross_attn(rmsnorm(xs, g_cross[l]), cache_mem_k[l], cache_mem_v[l], w_xq[l], w_xo[l])
        xs = xs + peer(rmsnorm(xs, g_ffn[l]), w_pq[l], peer_k1[l], peer_k2[l], peer_u[l], peer_v[l])
        ks_list.append(kd)
        vs_list.append(vd)
        ss_list.append(S_new)
    y_sample = rmsnorm(xs, g_final)

    new_da_k_prompt = jnp.stack(kp_list)
    new_da_v_prompt = jnp.stack(vp_list)
    new_ret_state_prompt = jnp.stack(sp_list)
    new_mem_k_prompt = jnp.stack(mkp_list)
    new_mem_v_prompt = jnp.stack(mvp_list)
    new_da_k_sample = jnp.stack(ks_list)
    new_da_v_sample = jnp.stack(vs_list)
    new_ret_state_sample = jnp.stack(ss_list)
    return (y_prompt, y_sample, new_da_k_prompt, new_da_v_prompt, new_ret_state_prompt, new_mem_k_prompt, new_mem_v_prompt, new_da_k_sample, new_da_v_sample, new_ret_state_sample)
```

```python
import functools
import math

import jax
import jax.numpy as jnp
from jax import lax
from jax.experimental import pallas as pl
from jax.experimental.pallas import tpu as pltpu

D_MODEL = 1024
CHUNK = 64
CHUNK_SHIFT = CHUNK.bit_length() - 1
assert 1 << CHUNK_SHIFT == CHUNK
H_D, DK_D, DV_D = 4, 64, 128
H_R, DK_R, DV_R = 4, 128, 128
N_MEM = 256
H_X = 4
HD_X = D_MODEL // H_X
PEER_HEADS = 8
N_KEYS = 128
N_EXPERTS = N_KEYS * N_KEYS
PEER_TOPK = 16
EPS = 1e-6
HEAD_W = 128
GROUP_W = 512
N_PAIRS = PEER_HEADS * PEER_TOPK
HALF = D_MODEL // 2
ROWS_PER_EXPERT = HALF // 128
VMEM_LIMIT = 56 * 1024 * 1024

BF16 = jnp.bfloat16
F32 = jnp.float32


def _cparams(sem):
    return pltpu.CompilerParams(dimension_semantics=sem, vmem_limit_bytes=VMEM_LIMIT)


def _rms(x, g):
    return x * lax.rsqrt(jnp.mean(x * x, axis=-1, keepdims=True) + EPS) * g


def _dot(a, b):
    return jnp.dot(a, b, preferred_element_type=F32)


def _dot_nt(a, b):
    return lax.dot_general(a, b, (((1,), (1,)), ((), ())), preferred_element_type=F32)


def _dot_tn(a, b):
    return lax.dot_general(a, b, (((0,), (0,)), ((), ())), preferred_element_type=F32)


def _select_by_head(h, values):
    out = jnp.float32(values[-1])
    for i in range(len(values) - 2, -1, -1):
        out = jnp.where(h == i, jnp.float32(values[i]), out)
    return out


def _in_proj_kernel(x_ref, g_ref, w_ref, qd_ref, kd_ref, vd_ref, kdb_ref, vdb_ref,
                    qr_ref, kr_ref, vr_ref, gr_ref):
    hb = _rms(x_ref[...], g_ref[...]).astype(BF16)
    col = lambda c: _dot(hb, w_ref[:, c * GROUP_W:(c + 1) * GROUP_W])
    qd_ref[...] = (col(0) * (DK_D ** -0.5)).astype(BF16)
    kd = col(1)
    kd_ref[...] = kd
    kdb_ref[...] = kd.astype(BF16)
    vd = col(2)
    vd_ref[...] = vd
    vdb_ref[...] = vd.astype(BF16)
    qr_ref[...] = col(3).astype(BF16)
    kr_ref[...] = (col(4) * (DK_R ** -0.5)).astype(BF16)
    vr_ref[...] = col(5).astype(BF16)
    gr_ref[...] = col(6)


def _in_proj(x2d, g, w_bf, tm):
    n = x2d.shape[0]
    blk = lambda: pl.BlockSpec((tm, GROUP_W), lambda i: (i, 0))
    sh = lambda dt: jax.ShapeDtypeStruct((n, GROUP_W), dt)
    return pl.pallas_call(
        _in_proj_kernel,
        grid=(n // tm,),
        in_specs=[pl.BlockSpec((tm, D_MODEL), lambda i: (i, 0)),
                  pl.BlockSpec((1, D_MODEL), lambda i: (0, 0)),
                  pl.BlockSpec(w_bf.shape, lambda i: (0, 0))],
        out_specs=[blk() for _ in range(9)],
        out_shape=[sh(BF16), sh(F32), sh(F32), sh(BF16), sh(BF16), sh(BF16), sh(BF16), sh(BF16), sh(F32)],
        compiler_params=_cparams(("parallel",)),
        name="in_proj",
    )(x2d, g, w_bf)


def _lambda_from(lam_ref, lam_init):
    l = lam_ref[...]
    a = jnp.exp(jnp.sum(l[0:1] * l[1:2], axis=-1, keepdims=True))
    b = jnp.exp(jnp.sum(l[2:3] * l[3:4], axis=-1, keepdims=True))
    return a - b + lam_init


def _diff_post(acc, l, lam, g, lam_init, tq):
    o = acc[:tq] / l[:tq] - lam * (acc[tq:] / l[tq:])
    return o * lax.rsqrt(jnp.mean(o * o, axis=-1, keepdims=True) + EPS) * g * (1.0 - lam_init)


def _split_maps(q):
    lane = lax.broadcasted_iota(jnp.int32, q.shape, 1)
    zero = jnp.zeros_like(q)
    return jnp.concatenate([jnp.where(lane < DK_D, q, zero), jnp.where(lane >= DK_D, q, zero)], axis=0)


def _da_prompt_kernel(lam_ref, q_ref, k_ref, v_ref, g_ref, o_ref, acc_ref, m_ref, l_ref, *, lam_init, tq, tk):
    h = pl.program_id(1)
    i = pl.program_id(2)
    slope = _select_by_head(h, [2.0 ** (-8.0 * (j + 1) / H_D) for j in range(H_D)])
    q2 = _split_maps(q_ref[...])
    rows = 2 * tq
    r = lax.broadcasted_iota(jnp.int32, (rows, tk), 0)
    qpos = i * tq + jnp.where(r >= tq, r - tq, r)
    col = lax.broadcasted_iota(jnp.int32, (rows, tk), 1)
    jd = (i * tq) // tk

    def scores(j):
        kj = k_ref[pl.ds(pl.multiple_of(j * tk, tk), tk), :]
        return _dot_nt(q2, kj), col + j * tk

    s, kpos = scores(jd)
    s = s - slope * jnp.abs(qpos - kpos).astype(F32)
    s = jnp.where((qpos >> CHUNK_SHIFT) >= (kpos >> CHUNK_SHIFT), s, -1e30)
    m0 = jnp.max(s, axis=-1, keepdims=True)
    p = jnp.exp(s - m0)
    m_ref[...] = m0
    l_ref[...] = jnp.sum(p, axis=-1, keepdims=True)
    acc_ref[...] = _dot(p.astype(BF16), v_ref[pl.ds(pl.multiple_of(jd * tk, tk), tk), :])

    def past(j, carry):
        s, kpos = scores(j)
        s = s - slope * (qpos - kpos).astype(F32)
        m_old = m_ref[...]
        m_new = jnp.maximum(m_old, jnp.max(s, axis=-1, keepdims=True))
        alpha = jnp.exp(m_old - m_new)
        p = jnp.exp(s - m_new)
        m_ref[...] = m_new
        l_ref[...] = alpha * l_ref[...] + jnp.sum(p, axis=-1, keepdims=True)
        acc_ref[...] = alpha * acc_ref[...] + _dot(p.astype(BF16), v_ref[pl.ds(pl.multiple_of(j * tk, tk), tk), :])
        return carry

    lax.fori_loop(0, jd, past, 0)
    lam = _lambda_from(lam_ref, lam_init)
    o_ref[...] = _diff_post(acc_ref[...], l_ref[...], lam, g_ref[...], lam_init, tq).astype(o_ref.dtype)


def _diff_attn_prompt(lamp, q, k, v, g_da3, lam_init, tq, tk):
    b, t, _ = q.shape
    kern = functools.partial(_da_prompt_kernel, lam_init=lam_init, tq=tq, tk=tk)
    return pl.pallas_call(
        kern,
        grid=(b, H_D, t // tq),
        in_specs=[pl.BlockSpec((4, DK_D), lambda b_, h, i: (0, 0)),
                  pl.BlockSpec((None, tq, HEAD_W), lambda b_, h, i: (b_, i, h)),
                  pl.BlockSpec((None, t, HEAD_W), lambda b_, h, i: (b_, 0, h)),
                  pl.BlockSpec((None, t, HEAD_W), lambda b_, h, i: (b_, 0, h)),
                  pl.BlockSpec((None, 1, HEAD_W), lambda b_, h, i: (h, 0, 0))],
        out_specs=pl.BlockSpec((None, tq, HEAD_W), lambda b_, h, i: (b_, i, h)),
        out_shape=jax.ShapeDtypeStruct((b, t, GROUP_W), BF16),
        scratch_shapes=[pltpu.VMEM((2 * tq, HEAD_W), F32), pltpu.VMEM((2 * tq, 1), F32), pltpu.VMEM((2 * tq, 1), F32)],
        compiler_params=_cparams(("parallel", "parallel", "arbitrary")),
        name="diff_attn_prompt",
    )(lamp, q, k, v, g_da3)


def _da_sample_kernel(lam_ref, q_ref, kc_ref, vc_ref, kn_ref, vn_ref, g_ref, o_ref, *, lam_init, ts, past_len):
    h = pl.program_id(1)
    slope = _select_by_head(h, [2.0 ** (-8.0 * (j + 1) / H_D) for j in range(H_D)])
    q2 = _split_maps(q_ref[...])
    rows = 2 * ts

    def scores(k, base, n):
        r = lax.broadcasted_iota(jnp.int32, (rows, n), 0)
        qpos = past_len + jnp.where(r >= ts, r - ts, r)
        kpos = base + lax.broadcasted_iota(jnp.int32, (rows, n), 1)
        return _dot_nt(q2, k) - slope * jnp.abs(qpos - kpos).astype(F32)

    sc = scores(kc_ref[...].astype(BF16), 0, past_len)
    sn = scores(kn_ref[...], past_len, ts)
    m = jnp.maximum(jnp.max(sc, axis=-1, keepdims=True), jnp.max(sn, axis=-1, keepdims=True))
    pc = jnp.exp(sc - m)
    pn = jnp.exp(sn - m)
    l = jnp.sum(pc, axis=-1, keepdims=True) + jnp.sum(pn, axis=-1, keepdims=True)
    acc = _dot(pc.astype(BF16), vc_ref[...].astype(BF16)) + _dot(pn.astype(BF16), vn_ref[...])
    lam = _lambda_from(lam_ref, lam_init)
    o_ref[...] = _diff_post(acc, l, lam, g_ref[...], lam_init, ts).astype(o_ref.dtype)


def _diff_attn_sample(lamp, q, kc, vc, kn, vn, g_da3, lam_init):
    b, ts, _ = q.shape
    past_len = kc.shape[1]
    kern = functools.partial(_da_sample_kernel, lam_init=lam_init, ts=ts, past_len=past_len)
    head = lambda rows: pl.BlockSpec((None, rows, HEAD_W), lambda b_, h: (b_, 0, h))
    return pl.pallas_call(
        kern,
        grid=(b, H_D),
        in_specs=[pl.BlockSpec((4, DK_D), lambda b_, h: (0, 0)),
                  head(ts), head(past_len), head(past_len), head(ts), head(ts),
                  pl.BlockSpec((None, 1, HEAD_W), lambda b_, h: (h, 0, 0))],
        out_specs=head(ts),
        out_shape=jax.ShapeDtypeStruct((b, ts, GROUP_W), BF16),
        compiler_params=_cparams(("parallel", "parallel")),
        name="diff_attn_sample",
    )(lamp, q, kc, vc, kn, vn, g_da3)


def _ret_kernel(*refs, lb, has_init):
    if has_init:
        q_ref, k_ref, v_ref, gate_ref, g_ref, s0_ref, o_ref, sfin_ref, s_ref = refs
    else:
        q_ref, k_ref, v_ref, gate_ref, g_ref, o_ref, sfin_ref, s_ref = refs
    h = pl.program_id(1)
    c = pl.program_id(2)
    lg = _select_by_head(h, [math.log1p(-(2.0 ** (-5.0 - j))) for j in range(H_R)])

    @pl.when(c == 0)
    def _():
        s_ref[...] = s0_ref[...] if has_init else jnp.zeros_like(s_ref)

    q, k, v = q_ref[...], k_ref[...], v_ref[...]
    i = lax.broadcasted_iota(jnp.int32, (lb, lb), 0)
    j = lax.broadcasted_iota(jnp.int32, (lb, lb), 1)
    d = (i - j).astype(F32)
    decay = jnp.where(d >= 0, jnp.exp(jnp.maximum(d, 0.0) * lg), 0.0)
    inner = _dot_nt(q, k) * decay
    ic = lax.broadcasted_iota(jnp.int32, (lb, 1), 0).astype(F32)
    s_old = s_ref[...]
    o = _dot(inner.astype(BF16), v) + _dot(q, s_old.astype(BF16)) * jnp.exp((ic + 1.0) * lg)
    tail = jnp.exp((lb - 1.0 - ic) * lg)
    kt = (k.astype(F32) * tail).astype(BF16)
    s_new = jnp.exp(lb * lg) * s_old + _dot_tn(kt, v)
    s_ref[...] = s_new

    @pl.when(c == pl.num_programs(2) - 1)
    def _():
        sfin_ref[...] = s_new

    oc = o - jnp.mean(o, axis=-1, keepdims=True)
    y = oc * lax.rsqrt(jnp.mean(oc * oc, axis=-1, keepdims=True) + EPS) * g_ref[...]
    gate = gate_ref[...]
    o_ref[...] = (y * (gate * jax.nn.sigmoid(gate))).astype(o_ref.dtype)


def _retention(q, k, v, gate, g_ret3, s0, lb):
    b, t, _ = q.shape
    has_init = s0 is not None
    kern = functools.partial(_ret_kernel, lb=lb, has_init=has_init)
    head = lambda: pl.BlockSpec((None, lb, HEAD_W), lambda b_, h, c: (b_, c, h))
    state = lambda: pl.BlockSpec((None, None, DK_R, DV_R), lambda b_, h, c: (b_, h, 0, 0))
    in_specs = [head(), head(), head(), head(), pl.BlockSpec((None, 1, HEAD_W), lambda b_, h, c: (h, 0, 0))]
    args = [q, k, v, gate, g_ret3]
    if has_init:
        in_specs.append(state())
        args.append(s0)
    return pl.pallas_call(
        kern,
        grid=(b, H_R, t // lb),
        in_specs=in_specs,
        out_specs=[head(), state()],
        out_shape=[jax.ShapeDtypeStruct((b, t, GROUP_W), BF16), jax.ShapeDtypeStruct((b, H_R, DK_R, DV_R), F32)],
        scratch_shapes=[pltpu.VMEM((DK_R, DV_R), F32)],
        compiler_params=_cparams(("parallel", "parallel", "arbitrary")),
        name="retention",
    )(*args)


def _out_proj_kernel(x_ref, mda_ref, mret_ref, wo_ref, g_ref, wq_ref, x1_ref, qx_ref):
    x1 = x_ref[...] + _dot(mda_ref[...], wo_ref[:GROUP_W, :]) + _dot(mret_ref[...], wo_ref[GROUP_W:, :])
    x1_ref[...] = x1
    hn = _rms(x1, g_ref[...]).astype(BF16)
    qx_ref[...] = (_dot(hn, wq_ref[...]) * (HD_X ** -0.5)).astype(BF16)


def _out_proj(x2d, mda, mret, wo_bf, g_cross, wq_bf, tm):
    n = x2d.shape[0]
    full = lambda a: pl.BlockSpec(a.shape, lambda i: (0, 0))
    return pl.pallas_call(
        _out_proj_kernel,
        grid=(n // tm,),
        in_specs=[pl.BlockSpec((tm, D_MODEL), lambda i: (i, 0)),
                  pl.BlockSpec((tm, GROUP_W), lambda i: (i, 0)),
                  pl.BlockSpec((tm, GROUP_W), lambda i: (i, 0)),
                  full(wo_bf), full(g_cross), full(wq_bf)],
        out_specs=[pl.BlockSpec((tm, D_MODEL), lambda i: (i, 0)), pl.BlockSpec((tm, D_MODEL), lambda i: (i, 0))],
        out_shape=[jax.ShapeDtypeStruct((n, D_MODEL), F32), jax.ShapeDtypeStruct((n, D_MODEL), BF16)],
        compiler_params=_cparams(("parallel",)),
        name="out_proj",
    )(x2d, mda, mret, wo_bf, g_cross, wq_bf)


def _mem_kv_kernel(m_ref, g_ref, wk_ref, wv_ref, mk_ref, mv_ref, mkb_ref, mvb_ref):
    mn = _rms(m_ref[...], g_ref[...]).astype(BF16)
    mk = _dot(mn, wk_ref[...])
    mv = _dot(mn, wv_ref[...])
    mk_ref[...] = mk
    mv_ref[...] = mv
    mkb_ref[...] = mk.astype(BF16)
    mvb_ref[...] = mv.astype(BF16)


def _mem_kv(mem2d, g_mem, wk_bf, wv_bf, tm):
    n = mem2d.shape[0]
    row = lambda: pl.BlockSpec((tm, D_MODEL), lambda i: (i, 0))
    full = lambda a: pl.BlockSpec(a.shape, lambda i: (0, 0))
    sh = lambda dt: jax.ShapeDtypeStruct((n, D_MODEL), dt)
    return pl.pallas_call(
        _mem_kv_kernel,
        grid=(n // tm,),
        in_specs=[row(), full(g_mem), full(wk_bf), full(wv_bf)],
        out_specs=[row(), row(), row(), row()],
        out_shape=[sh(F32), sh(F32), sh(BF16), sh(BF16)],
        compiler_params=_cparams(("parallel",)),
        name="mem_kv",
    )(mem2d, g_mem, wk_bf, wv_bf)


def _cross_kernel(x1_ref, qx_ref, mk_ref, mv_ref, wo_ref, g_ref, wpq_ref, x2_ref, h3_ref, qp_ref):
    q = qx_ref[...]
    heads = []
    for h in range(H_X):
        sl = slice(h * HD_X, (h + 1) * HD_X)
        kh = mk_ref[:, sl] if mk_ref.dtype == BF16 else mk_ref[:, sl].astype(BF16)
        vh = mv_ref[:, sl] if mv_ref.dtype == BF16 else mv_ref[:, sl].astype(BF16)
        s = _dot_nt(q[:, sl], kh)
        p = jnp.exp(s - jnp.max(s, axis=-1, keepdims=True))
        heads.append(_dot(p.astype(BF16), vh) / jnp.sum(p, axis=-1, keepdims=True))
    o = jnp.concatenate(heads, axis=-1).astype(BF16)
    x2 = x1_ref[...] + _dot(o, wo_ref[...])
    x2_ref[...] = x2
    h3 = _rms(x2, g_ref[...])
    h3_ref[...] = h3
    qp_ref[...] = _dot(h3.astype(BF16), wpq_ref[...]).astype(BF16)


def _cross(x1, qx, mk, mv, wo_bf, g_ffn, wpq_bf, tm):
    b, t, _ = x1.shape
    row = lambda w: pl.BlockSpec((None, tm, w), lambda b_, i: (b_, i, 0))
    mem = lambda: pl.BlockSpec((None, N_MEM, D_MODEL), lambda b_, i: (b_, 0, 0))
    full = lambda a: pl.BlockSpec(a.shape, lambda b_, i: (0, 0))
    dq = wpq_bf.shape[1]
    return pl.pallas_call(
        _cross_kernel,
        grid=(b, t // tm),
        in_specs=[row(D_MODEL), row(D_MODEL), mem(), mem(), full(wo_bf), full(g_ffn), full(wpq_bf)],
        out_specs=[row(D_MODEL), row(D_MODEL), row(dq)],
        out_shape=[jax.ShapeDtypeStruct((b, t, D_MODEL), F32), jax.ShapeDtypeStruct((b, t, D_MODEL), F32),
                   jax.ShapeDtypeStruct((b, t, dq), BF16)],
        compiler_params=_cparams(("parallel", "parallel")),
        name="cross_attn",
    )(x1, qx, mk, mv, wo_bf, g_ffn, wpq_bf)


def _topk_rows(s, payload=None):
    n = s.shape[0]
    row = lax.broadcasted_iota(jnp.int32, s.shape, 0)
    vals, idxs = [], []
    for _ in range(PEER_TOPK):
        m = jnp.max(s, axis=0, keepdims=True)
        idx = jnp.min(jnp.where(s == m, row, n), axis=0, keepdims=True)
        hit = row == idx
        vals.append(m)
        idxs.append(idx if payload is None else jnp.max(jnp.where(hit, payload, -1), axis=0, keepdims=True))
        s = jnp.where(hit, -jnp.inf, s)
    return jnp.concatenate(vals, axis=0), jnp.concatenate(idxs, axis=0)


def _route_kernel(qp_ref, k1_ref, k2_ref, e_ref, g_ref):
    half = N_KEYS
    for p in range(PEER_HEADS):
        q1 = qp_ref[:, (2 * p) * half:(2 * p + 1) * half]
        q2 = qp_ref[:, (2 * p + 1) * half:(2 * p + 2) * half]
        v1, i1 = _topk_rows(_dot_nt(k1_ref[p], q1))
        v2, i2 = _topk_rows(_dot_nt(k2_ref[p], q2))
        cand = jnp.concatenate([v1[a:a + 1] + v2 for a in range(PEER_TOPK)], axis=0)
        cidx = jnp.concatenate([i1[a:a + 1] * N_KEYS + i2 for a in range(PEER_TOPK)], axis=0)
        sc, e = _topk_rows(cand, payload=cidx)
        w = jnp.exp(sc - sc[0:1])
        sl = slice(p * PEER_TOPK, (p + 1) * PEER_TOPK)
        e_ref[sl, :] = e * ROWS_PER_EXPERT
        g_ref[sl, :] = w / jnp.sum(w, axis=0, keepdims=True)


def _route(qp2d, k1_bf, k2_bf, tt):
    n = qp2d.shape[0]
    full = lambda a: pl.BlockSpec(a.shape, lambda i: (0, 0, 0))
    return pl.pallas_call(
        _route_kernel,
        grid=(n // tt,),
        in_specs=[pl.BlockSpec((tt, qp2d.shape[1]), lambda i: (i, 0)), full(k1_bf), full(k2_bf)],
        out_specs=[pl.BlockSpec((N_PAIRS, tt), lambda i: (0, i)), pl.BlockSpec((N_PAIRS, tt), lambda i: (0, i))],
        out_shape=[jax.ShapeDtypeStruct((N_PAIRS, n), jnp.int32), jax.ShapeDtypeStruct((N_PAIRS, n), F32)],
        compiler_params=_cparams(("parallel",)),
        name="peer_route",
    )(qp2d, k1_bf, k2_bf)


def _unpack(words):
    hi = lax.bitcast_convert_type(words & jnp.uint32(0xFFFF0000), F32)
    lo = lax.bitcast_convert_type(words << 16, F32)
    return hi, lo


def _expert_rows(tab_ref, row0):
    return tab_ref[pl.ds(pl.multiple_of(row0, ROWS_PER_EXPERT), ROWS_PER_EXPERT), :]


def _peer_u_kernel(e_ref, h_ref, gate_ref, tab_ref, w_ref, prod_ref, act_ref, *, tt):
    lane = lax.broadcasted_iota(jnp.int32, (N_PAIRS, tt), 1)
    rpt = 2 * ROWS_PER_EXPERT
    act_ref[...] = jnp.zeros_like(act_ref)

    def token(t, carry):
        base = pl.multiple_of(t * rpt, rpt)
        ha = h_ref[pl.ds(base, ROWS_PER_EXPERT), :]
        hb = h_ref[pl.ds(pl.multiple_of(base + ROWS_PER_EXPERT, ROWS_PER_EXPERT), ROWS_PER_EXPERT), :]
        for k in range(N_PAIRS):
            hi, lo = _unpack(_expert_rows(tab_ref, e_ref[k, t]))
            prod_ref[k * ROWS_PER_EXPERT:(k + 1) * ROWS_PER_EXPERT, :] = hi * ha + lo * hb
        y = prod_ref[pl.ds(0, N_PAIRS, stride=ROWS_PER_EXPERT), :]
        for c in range(1, ROWS_PER_EXPERT):
            y = y + prod_ref[pl.ds(c, N_PAIRS, stride=ROWS_PER_EXPERT), :]
        colsum = jnp.sum(y, axis=1, keepdims=True)
        act_ref[...] = jnp.where(lane == t, colsum, act_ref[...])
        return carry

    lax.fori_loop(0, tt, token, 0)
    a = act_ref[...]
    gelu = 0.5 * a * (1.0 + lax.erf(a * (2.0 ** -0.5)))
    w_ref[...] = gate_ref[...] * gelu


def _peer_u(e_t, h8, gate_t, tab, tt):
    n = e_t.shape[1]
    kern = functools.partial(_peer_u_kernel, tt=tt)
    pair = lambda **kw: pl.BlockSpec((N_PAIRS, tt), lambda i: (0, i), **kw)
    return pl.pallas_call(
        kern,
        grid=(n // tt,),
        in_specs=[pair(memory_space=pltpu.SMEM),
                  pl.BlockSpec((tt * 8, 128), lambda i: (i, 0)),
                  pair(),
                  pl.BlockSpec(tab.shape, lambda i: (0, 0), pipeline_mode=pl.Buffered(1))],
        out_specs=pair(),
        out_shape=jax.ShapeDtypeStruct((N_PAIRS, n), F32),
        scratch_shapes=[pltpu.VMEM((N_PAIRS * ROWS_PER_EXPERT, 128), F32), pltpu.VMEM((N_PAIRS, tt), F32)],
        compiler_params=_cparams(("arbitrary",)),
        name="peer_u",
    )(e_t, h8, gate_t, tab)


def _peer_v_kernel(e_ref, w_ref, x_ref, tab_ref, o_ref, *, tt):
    rpt = 2 * ROWS_PER_EXPERT
    n_acc = 2

    def token(t, carry):
        acc_hi = [jnp.zeros((ROWS_PER_EXPERT, 128), F32) for _ in range(n_acc)]
        acc_lo = [jnp.zeros((ROWS_PER_EXPERT, 128), F32) for _ in range(n_acc)]
        for k in range(N_PAIRS):
            hi, lo = _unpack(_expert_rows(tab_ref, e_ref[k, t]))
            w = w_ref[k, t]
            acc_hi[k % n_acc] = acc_hi[k % n_acc] + w * hi
            acc_lo[k % n_acc] = acc_lo[k % n_acc] + w * lo
        base = pl.multiple_of(t * rpt, rpt)
        lo_base = pl.multiple_of(base + ROWS_PER_EXPERT, ROWS_PER_EXPERT)
        o_ref[pl.ds(base, ROWS_PER_EXPERT), :] = x_ref[pl.ds(base, ROWS_PER_EXPERT), :] + sum(acc_hi[1:], acc_hi[0])
        o_ref[pl.ds(lo_base, ROWS_PER_EXPERT), :] = x_ref[pl.ds(lo_base, ROWS_PER_EXPERT), :] + sum(acc_lo[1:], acc_lo[0])
        return carry

    lax.fori_loop(0, tt, token, 0)


def _peer_v(e_t, w_t, x8, tab, tt):
    n = e_t.shape[1]
    kern = functools.partial(_peer_v_kernel, tt=tt)
    pair = lambda: pl.BlockSpec((N_PAIRS, tt), lambda i: (0, i), memory_space=pltpu.SMEM)
    return pl.pallas_call(
        kern,
        grid=(n // tt,),
        in_specs=[pair(), pair(),
                  pl.BlockSpec((tt * 8, 128), lambda i: (i, 0)),
                  pl.BlockSpec(tab.shape, lambda i: (0, 0), pipeline_mode=pl.Buffered(1))],
        out_specs=pl.BlockSpec((tt * 8, 128), lambda i: (i, 0)),
        out_shape=jax.ShapeDtypeStruct(x8.shape, F32),
        compiler_params=_cparams(("arbitrary",)),
        name="peer_v",
    )(e_t, w_t, x8, tab)


def _final_kernel(x_ref, g_ref, o_ref):
    o_ref[...] = _rms(x_ref[...], g_ref[...])


def _final_norm(x2d, g, tm):
    n = x2d.shape[0]
    row = lambda: pl.BlockSpec((tm, D_MODEL), lambda i: (i, 0))
    return pl.pallas_call(
        _final_kernel,
        grid=(n // tm,),
        in_specs=[row(), pl.BlockSpec((1, D_MODEL), lambda i: (0, 0))],
        out_specs=row(),
        out_shape=jax.ShapeDtypeStruct((n, D_MODEL), F32),
        compiler_params=_cparams(("parallel",)),
        name="final_norm",
    )(x2d, g)


def _pack_table(tab):
    bits = lax.bitcast_convert_type(tab.astype(BF16), jnp.uint16).astype(jnp.uint32)
    words = (bits[:, :HALF] << 16) | bits[:, HALF:]
    return words.reshape(tab.shape[0] * ROWS_PER_EXPERT, 128)


def _row_tile(n, pref):
    while n % pref:
        pref //= 2
    return pref


def _peer_and_final(x2, h3, qp, k1_bf, k2_bf, tab_u, tab_v, g_final, tt):
    n = x2.shape[0]
    e_t, gate_t = _route(qp, k1_bf, k2_bf, tt)
    w_t = _peer_u(e_t, h3.reshape(n * 8, 128), gate_t, tab_u, tt)
    x3 = _peer_v(e_t, w_t, x2.reshape(n * 8, 128), tab_v, tt).reshape(n, D_MODEL)
    return _final_norm(x3, g_final, _row_tile(n, 512))


def kernel(x_prompt, x_sample, mem_prompt, cache_da_k, cache_da_v, state_ret, cache_mem_k, cache_mem_v, g_mix, w_in, lam_q1, lam_k1, lam_q2, lam_k2, g_da, g_ret, w_out, g_cross, g_mem, w_xq, w_xk, w_xv, w_xo, g_ffn, w_pq, peer_k1, peer_k2, peer_u, peer_v, g_final):
    depth = w_in.shape[0]
    assert depth == 1, "single-layer step"
    l = 0
    lam_init = 0.8 - 0.6 * math.exp(-0.3 * l)
    b, t, _ = x_prompt.shape
    bs, ts, _ = x_sample.shape
    past_len = cache_da_k.shape[2]

    row = lambda a: a.reshape(1, -1)
    w_in_bf = w_in[l].astype(BF16)
    w_out_bf = w_out[l].astype(BF16)
    w_xq_bf, w_xk_bf, w_xv_bf, w_xo_bf = (w[l].astype(BF16) for w in (w_xq, w_xk, w_xv, w_xo))
    w_pq_bf = w_pq[l].astype(BF16)
    k1_bf, k2_bf = peer_k1[l].astype(BF16), peer_k2[l].astype(BF16)
    tab_u, tab_v = _pack_table(peer_u[l]), _pack_table(peer_v[l])
    lamp = jnp.stack([lam_q1[l], lam_k1[l], lam_q2[l], lam_k2[l]])
    g_da3 = g_da[l].reshape(H_D, 1, DV_D)
    g_ret3 = g_ret[l].reshape(H_R, 1, DV_R)
    g_fin = row(g_final)

    def mixer_tail(x2d, mda, mret, mk, mv, bb, tt_rows):
        n = x2d.shape[0]
        tm = _row_tile(n, 512)
        x1, qx = _out_proj(x2d, mda.reshape(n, GROUP_W), mret.reshape(n, GROUP_W), w_out_bf, row(g_cross[l]), w_xq_bf, tm)
        x2, h3, qp = _cross(x1.reshape(bb, tt_rows, D_MODEL), qx.reshape(bb, tt_rows, D_MODEL), mk, mv,
                            w_xo_bf, row(g_ffn[l]), w_pq_bf, _row_tile(tt_rows, 512))
        y = _peer_and_final(x2.reshape(n, D_MODEL), h3.reshape(n, D_MODEL), qp.reshape(n, -1),
                            k1_bf, k2_bf, tab_u, tab_v, g_fin, 128)
        return y.reshape(bb, tt_rows, D_MODEL)

    n = b * t
    xp = x_prompt.reshape(n, D_MODEL)
    qd, kd, vd, kdb, vdb, qr, kr, vr, gr = _in_proj(xp, row(g_mix[l]), w_in_bf, _row_tile(n, 512))
    r3 = lambda a: a.reshape(b, t, GROUP_W)
    mda = _diff_attn_prompt(lamp, r3(qd), r3(kdb), r3(vdb), g_da3, lam_init, 256, 512)
    mret, s_fin = _retention(r3(qr), r3(kr), r3(vr), r3(gr), g_ret3, None, 256)
    mk, mv, mkb, mvb = _mem_kv(mem_prompt.reshape(b * N_MEM, D_MODEL), row(g_mem[l]), w_xk_bf, w_xv_bf, 512)
    y_prompt = mixer_tail(xp, mda, mret, mkb.reshape(b, N_MEM, D_MODEL), mvb.reshape(b, N_MEM, D_MODEL), b, t)

    ns = bs * ts
    xs = x_sample.reshape(ns, D_MODEL)
    qd_s, kd_s, vd_s, kdb_s, vdb_s, qr_s, kr_s, vr_s, gr_s = _in_proj(xs, row(g_mix[l]), w_in_bf, _row_tile(ns, 512))
    s3 = lambda a: a.reshape(bs, ts, GROUP_W)
    mda_s = _diff_attn_sample(lamp, s3(qd_s), cache_da_k[l].reshape(bs, past_len, GROUP_W),
                              cache_da_v[l].reshape(bs, past_len, GROUP_W), s3(kdb_s), s3(vdb_s), g_da3, lam_init)
    mret_s, s_new = _retention(s3(qr_s), s3(kr_s), s3(vr_s), s3(gr_s), g_ret3, state_ret[l], ts)
    y_sample = mixer_tail(xs, mda_s, mret_s, cache_mem_k[l].reshape(bs, N_MEM, D_MODEL),
                          cache_mem_v[l].reshape(bs, N_MEM, D_MODEL), bs, ts)

    return (y_prompt, y_sample,
            kd.reshape(1, b, t, H_D, 2, DK_D), vd.reshape(1, b, t, H_D, DV_D), s_fin[None],
            mk.reshape(1, b, N_MEM, H_X, HD_X), mv.reshape(1, b, N_MEM, H_X, HD_X),
            kd_s.reshape(1, bs, ts, H_D, 2, DK_D), vd_s.reshape(1, bs, ts, H_D, DV_D), s_new[None])
```

```python
import functools
import math

import jax
import jax.numpy as jnp
from jax import lax
from jax.experimental import pallas as pl
from jax.experimental.pallas import tpu as pltpu

D_MODEL = 1024
CHUNK = 64
CHUNK_SHIFT = CHUNK.bit_length() - 1
assert 1 << CHUNK_SHIFT == CHUNK
H_D, DK_D, DV_D = 4, 64, 128
H_R, DK_R, DV_R = 4, 128, 128
N_MEM = 256
H_X = 4
HD_X = D_MODEL // H_X
PEER_HEADS = 8
N_KEYS = 128
N_EXPERTS = N_KEYS * N_KEYS
PEER_TOPK = 16
EPS = 1e-6
HEAD_W = 128
GROUP_W = 512
N_PAIRS = PEER_HEADS * PEER_TOPK
HALF = D_MODEL // 2
ROWS_PER_EXPERT = HALF // 128
VMEM_LIMIT = 56 * 1024 * 1024

BF16 = jnp.bfloat16
F32 = jnp.float32


def _cparams(sem):
    return pltpu.CompilerParams(dimension_semantics=sem, vmem_limit_bytes=VMEM_LIMIT)


def _rms(x, g):
    return x * lax.rsqrt(jnp.mean(x * x, axis=-1, keepdims=True) + EPS) * g


def _dot(a, b):
    return jnp.dot(a, b, preferred_element_type=F32)


def _dot_nt(a, b):
    return lax.dot_general(a, b, (((1,), (1,)), ((), ())), preferred_element_type=F32)


def _dot_tn(a, b):
    return lax.dot_general(a, b, (((0,), (0,)), ((), ())), preferred_element_type=F32)


def _select_by_head(h, values):
    out = jnp.float32(values[-1])
    for i in range(len(values) - 2, -1, -1):
        out = jnp.where(h == i, jnp.float32(values[i]), out)
    return out


def _in_proj_kernel(x_ref, g_ref, w_ref, qd_ref, kd_ref, vd_ref, kdb_ref, vdb_ref,
                    qr_ref, kr_ref, vr_ref, gr_ref):
    hb = _rms(x_ref[...], g_ref[...]).astype(BF16)
    col = lambda c: _dot(hb, w_ref[:, c * GROUP_W:(c + 1) * GROUP_W])
    qd_ref[...] = (col(0) * (DK_D ** -0.5)).astype(BF16)
    kd = col(1)
    kd_ref[...] = kd
    kdb_ref[...] = kd.astype(BF16)
    vd = col(2)
    vd_ref[...] = vd
    vdb_ref[...] = vd.astype(BF16)
    qr_ref[...] = col(3).astype(BF16)
    kr_ref[...] = (col(4) * (DK_R ** -0.5)).astype(BF16)
    vr_ref[...] = col(5).astype(BF16)
    gr_ref[...] = col(6)


def _in_proj(x2d, g, w_bf, tm):
    n = x2d.shape[0]
    blk = lambda: pl.BlockSpec((tm, GROUP_W), lambda i: (i, 0))
    sh = lambda dt: jax.ShapeDtypeStruct((n, GROUP_W), dt)
    return pl.pallas_call(
        _in_proj_kernel,
        grid=(n // tm,),
        in_specs=[pl.BlockSpec((tm, D_MODEL), lambda i: (i, 0)),
                  pl.BlockSpec((1, D_MODEL), lambda i: (0, 0)),
                  pl.BlockSpec(w_bf.shape, lambda i: (0, 0))],
        out_specs=[blk() for _ in range(9)],
        out_shape=[sh(BF16), sh(F32), sh(F32), sh(BF16), sh(BF16), sh(BF16), sh(BF16), sh(BF16), sh(F32)],
        compiler_params=_cparams(("parallel",)),
        name="in_proj",
    )(x2d, g, w_bf)


def _lambda_from(lam_ref, lam_init):
    l = lam_ref[...]
    a = jnp.exp(jnp.sum(l[0:1] * l[1:2], axis=-1, keepdims=True))
    b = jnp.exp(jnp.sum(l[2:3] * l[3:4], axis=-1, keepdims=True))
    return a - b + lam_init


def _diff_post(acc, l, lam, g, lam_init, tq):
    o = acc[:tq] / l[:tq] - lam * (acc[tq:] / l[tq:])
    return o * lax.rsqrt(jnp.mean(o * o, axis=-1, keepdims=True) + EPS) * g * (1.0 - lam_init)


def _split_maps(q):
    lane = lax.broadcasted_iota(jnp.int32, q.shape, 1)
    zero = jnp.zeros_like(q)
    return jnp.concatenate([jnp.where(lane < DK_D, q, zero), jnp.where(lane >= DK_D, q, zero)], axis=0)


def _da_prompt_kernel(lam_ref, q_ref, k_ref, v_ref, g_ref, o_ref, kx_ref, vx_ref, acc_ref, m_ref, *, lam_init, tq, tk):
    h = pl.program_id(1)
    i = pl.program_id(2)
    t = k_ref.shape[0]
    cols = 2 * tq
    slope = _select_by_head(h, [2.0 ** (-8.0 * (j + 1) / H_D) for j in range(H_D)])

    @pl.when(i == 0)
    def _():
        pos = lax.broadcasted_iota(jnp.int32, (t, HEAD_W), 0)
        lane = lax.broadcasted_iota(jnp.int32, (t, HEAD_W), 1)
        coarse = ((pos >> CHUNK_SHIFT) << CHUNK_SHIFT).astype(F32) * slope
        fine = (pos & (CHUNK - 1)).astype(F32) * slope
        kx_ref[:, :HEAD_W] = k_ref[...]
        kx_ref[:, HEAD_W:] = jnp.where(lane == 0, coarse, jnp.where(lane == 1, fine, 0.0)).astype(BF16)
        vx_ref[:DV_D, :] = v_ref[...].astype(F32).T.astype(BF16)
        vx_ref[DV_D:, :] = jnp.ones((vx_ref.shape[0] - DV_D, t), BF16)

    q = q_ref[...]
    lane = lax.broadcasted_iota(jnp.int32, q.shape, 1)
    zero = jnp.zeros_like(q)
    ones2 = jnp.where(lane < 2, 1.0, 0.0).astype(BF16)
    q2 = jnp.concatenate([jnp.concatenate([jnp.where(lane < DK_D, q, zero), ones2], axis=1),
                          jnp.concatenate([jnp.where(lane >= DK_D, q, zero), ones2], axis=1)], axis=0)
    jd = (i * tq) // tk

    def scores(j):
        return _dot_nt(kx_ref[pl.ds(pl.multiple_of(j * tk, tk), tk), :], q2)

    def values(j):
        return vx_ref[:, pl.ds(pl.multiple_of(j * tk, tk), tk)]

    kpos = jd * tk + lax.broadcasted_iota(jnp.int32, (tk, cols), 0)
    c = lax.broadcasted_iota(jnp.int32, (tk, cols), 1)
    qpos = i * tq + jnp.where(c >= tq, c - tq, c)
    s = scores(jd) - (2.0 * slope) * jnp.maximum(kpos - qpos, 0).astype(F32)
    s = jnp.where((qpos >> CHUNK_SHIFT) >= (kpos >> CHUNK_SHIFT), s, -1e30)
    m0 = jnp.max(s, axis=0, keepdims=True)
    m_ref[...] = m0
    acc_ref[...] = _dot(values(jd), jnp.exp(s - m0).astype(BF16))

    def past(j, carry):
        s = scores(j)
        m_old = m_ref[...]
        m_new = jnp.maximum(m_old, jnp.max(s, axis=0, keepdims=True))
        m_ref[...] = m_new
        acc_ref[...] = jnp.exp(m_old - m_new) * acc_ref[...] + _dot(values(j), jnp.exp(s - m_new).astype(BF16))
        return carry

    lax.fori_loop(0, jd, past, 0)
    acc = acc_ref[...]
    num, den = acc[:DV_D], acc[DV_D:DV_D + 1]
    lam = _lambda_from(lam_ref, lam_init)
    o = (num[:, :tq] / den[:, :tq] - lam * (num[:, tq:] / den[:, tq:])).T
    o = o * lax.rsqrt(jnp.mean(o * o, axis=-1, keepdims=True) + EPS) * g_ref[...] * (1.0 - lam_init)
    o_ref[...] = o.astype(o_ref.dtype)


ONES_ROWS = 16


def _diff_attn_prompt(lamp, q, k, v, g_da3, lam_init, tq, tk):
    b, t, _ = q.shape
    kern = functools.partial(_da_prompt_kernel, lam_init=lam_init, tq=tq, tk=tk)
    return pl.pallas_call(
        kern,
        grid=(b, H_D, t // tq),
        in_specs=[pl.BlockSpec((4, DK_D), lambda b_, h, i: (0, 0)),
                  pl.BlockSpec((None, tq, HEAD_W), lambda b_, h, i: (b_, i, h)),
                  pl.BlockSpec((None, t, HEAD_W), lambda b_, h, i: (b_, 0, h)),
                  pl.BlockSpec((None, t, HEAD_W), lambda b_, h, i: (b_, 0, h)),
                  pl.BlockSpec((None, 1, HEAD_W), lambda b_, h, i: (h, 0, 0))],
        out_specs=pl.BlockSpec((None, tq, HEAD_W), lambda b_, h, i: (b_, i, h)),
        out_shape=jax.ShapeDtypeStruct((b, t, GROUP_W), BF16),
        scratch_shapes=[pltpu.VMEM((t, 2 * HEAD_W), BF16), pltpu.VMEM((DV_D + ONES_ROWS, t), BF16),
                        pltpu.VMEM((DV_D + ONES_ROWS, 2 * tq), F32), pltpu.VMEM((1, 2 * tq), F32)],
        compiler_params=_cparams(("parallel", "parallel", "arbitrary")),
        name="diff_attn_prompt",
    )(lamp, q, k, v, g_da3)


def _da_sample_kernel(lam_ref, q_ref, kc_ref, vc_ref, kn_ref, vn_ref, g_ref, o_ref, *, lam_init, ts, past_len):
    h = pl.program_id(1)
    slope = _select_by_head(h, [2.0 ** (-8.0 * (j + 1) / H_D) for j in range(H_D)])
    q2 = _split_maps(q_ref[...])
    rows = 2 * ts

    def scores(k, base, n):
        r = lax.broadcasted_iota(jnp.int32, (rows, n), 0)
        qpos = past_len + jnp.where(r >= ts, r - ts, r)
        kpos = base + lax.broadcasted_iota(jnp.int32, (rows, n), 1)
        return _dot_nt(q2, k) - slope * jnp.abs(qpos - kpos).astype(F32)

    sc = scores(kc_ref[...].astype(BF16), 0, past_len)
    sn = scores(kn_ref[...], past_len, ts)
    m = jnp.maximum(jnp.max(sc, axis=-1, keepdims=True), jnp.max(sn, axis=-1, keepdims=True))
    pc = jnp.exp(sc - m)
    pn = jnp.exp(sn - m)
    l = jnp.sum(pc, axis=-1, keepdims=True) + jnp.sum(pn, axis=-1, keepdims=True)
    acc = _dot(pc.astype(BF16), vc_ref[...].astype(BF16)) + _dot(pn.astype(BF16), vn_ref[...])
    lam = _lambda_from(lam_ref, lam_init)
    o_ref[...] = _diff_post(acc, l, lam, g_ref[...], lam_init, ts).astype(o_ref.dtype)


def _diff_attn_sample(lamp, q, kc, vc, kn, vn, g_da3, lam_init):
    b, ts, _ = q.shape
    past_len = kc.shape[1]
    kern = functools.partial(_da_sample_kernel, lam_init=lam_init, ts=ts, past_len=past_len)
    head = lambda rows: pl.BlockSpec((None, rows, HEAD_W), lambda b_, h: (b_, 0, h))
    return pl.pallas_call(
        kern,
        grid=(b, H_D),
        in_specs=[pl.BlockSpec((4, DK_D), lambda b_, h: (0, 0)),
                  head(ts), head(past_len), head(past_len), head(ts), head(ts),
                  pl.BlockSpec((None, 1, HEAD_W), lambda b_, h: (h, 0, 0))],
        out_specs=head(ts),
        out_shape=jax.ShapeDtypeStruct((b, ts, GROUP_W), BF16),
        compiler_params=_cparams(("parallel", "parallel")),
        name="diff_attn_sample",
    )(lamp, q, kc, vc, kn, vn, g_da3)


def _ret_kernel(*refs, lb, has_init):
    if has_init:
        q_ref, k_ref, v_ref, gate_ref, g_ref, s0_ref, o_ref, sfin_ref, s_ref = refs
    else:
        q_ref, k_ref, v_ref, gate_ref, g_ref, o_ref, sfin_ref, s_ref = refs
    h = pl.program_id(1)
    c = pl.program_id(2)
    lg = _select_by_head(h, [math.log1p(-(2.0 ** (-5.0 - j))) for j in range(H_R)])

    @pl.when(c == 0)
    def _():
        s_ref[...] = s0_ref[...] if has_init else jnp.zeros_like(s_ref)

    q, k, v = q_ref[...], k_ref[...], v_ref[...]
    i = lax.broadcasted_iota(jnp.int32, (lb, lb), 0)
    j = lax.broadcasted_iota(jnp.int32, (lb, lb), 1)
    d = (i - j).astype(F32)
    decay = jnp.where(d >= 0, jnp.exp(jnp.maximum(d, 0.0) * lg), 0.0)
    inner = _dot_nt(q, k) * decay
    ic = lax.broadcasted_iota(jnp.int32, (lb, 1), 0).astype(F32)
    s_old = s_ref[...]
    o = _dot(inner.astype(BF16), v) + _dot(q, s_old.astype(BF16)) * jnp.exp((ic + 1.0) * lg)
    tail = jnp.exp((lb - 1.0 - ic) * lg)
    kt = (k.astype(F32) * tail).astype(BF16)
    s_new = jnp.exp(lb * lg) * s_old + _dot_tn(kt, v)
    s_ref[...] = s_new

    @pl.when(c == pl.num_programs(2) - 1)
    def _():
        sfin_ref[...] = s_new

    oc = o - jnp.mean(o, axis=-1, keepdims=True)
    y = oc * lax.rsqrt(jnp.mean(oc * oc, axis=-1, keepdims=True) + EPS) * g_ref[...]
    gate = gate_ref[...]
    o_ref[...] = (y * (gate * jax.nn.sigmoid(gate))).astype(o_ref.dtype)


def _retention(q, k, v, gate, g_ret3, s0, lb):
    b, t, _ = q.shape
    has_init = s0 is not None
    kern = functools.partial(_ret_kernel, lb=lb, has_init=has_init)
    head = lambda: pl.BlockSpec((None, lb, HEAD_W), lambda b_, h, c: (b_, c, h))
    state = lambda: pl.BlockSpec((None, None, DK_R, DV_R), lambda b_, h, c: (b_, h, 0, 0))
    in_specs = [head(), head(), head(), head(), pl.BlockSpec((None, 1, HEAD_W), lambda b_, h, c: (h, 0, 0))]
    args = [q, k, v, gate, g_ret3]
    if has_init:
        in_specs.append(state())
        args.append(s0)
    return pl.pallas_call(
        kern,
        grid=(b, H_R, t // lb),
        in_specs=in_specs,
        out_specs=[head(), state()],
        out_shape=[jax.ShapeDtypeStruct((b, t, GROUP_W), BF16), jax.ShapeDtypeStruct((b, H_R, DK_R, DV_R), F32)],
        scratch_shapes=[pltpu.VMEM((DK_R, DV_R), F32)],
        compiler_params=_cparams(("parallel", "parallel", "arbitrary")),
        name="retention",
    )(*args)


def _out_proj_kernel(x_ref, mda_ref, mret_ref, wo_ref, g_ref, wq_ref, x1_ref, qx_ref):
    x1 = x_ref[...] + _dot(mda_ref[...], wo_ref[:GROUP_W, :]) + _dot(mret_ref[...], wo_ref[GROUP_W:, :])
    x1_ref[...] = x1
    hn = _rms(x1, g_ref[...]).astype(BF16)
    qx_ref[...] = (_dot(hn, wq_ref[...]) * (HD_X ** -0.5)).astype(BF16)


def _out_proj(x2d, mda, mret, wo_bf, g_cross, wq_bf, tm):
    n = x2d.shape[0]
    full = lambda a: pl.BlockSpec(a.shape, lambda i: (0, 0))
    return pl.pallas_call(
        _out_proj_kernel,
        grid=(n // tm,),
        in_specs=[pl.BlockSpec((tm, D_MODEL), lambda i: (i, 0)),
                  pl.BlockSpec((tm, GROUP_W), lambda i: (i, 0)),
                  pl.BlockSpec((tm, GROUP_W), lambda i: (i, 0)),
                  full(wo_bf), full(g_cross), full(wq_bf)],
        out_specs=[pl.BlockSpec((tm, D_MODEL), lambda i: (i, 0)), pl.BlockSpec((tm, D_MODEL), lambda i: (i, 0))],
        out_shape=[jax.ShapeDtypeStruct((n, D_MODEL), F32), jax.ShapeDtypeStruct((n, D_MODEL), BF16)],
        compiler_params=_cparams(("parallel",)),
        name="out_proj",
    )(x2d, mda, mret, wo_bf, g_cross, wq_bf)


def _mem_kv_kernel(m_ref, g_ref, wk_ref, wv_ref, mk_ref, mv_ref, mkb_ref, mvb_ref):
    mn = _rms(m_ref[...], g_ref[...]).astype(BF16)
    mk = _dot(mn, wk_ref[...])
    mv = _dot(mn, wv_ref[...])
    mk_ref[...] = mk
    mv_ref[...] = mv
    mkb_ref[...] = mk.astype(BF16)
    mvb_ref[...] = mv.astype(BF16)


def _mem_kv(mem2d, g_mem, wk_bf, wv_bf, tm):
    n = mem2d.shape[0]
    row = lambda: pl.BlockSpec((tm, D_MODEL), lambda i: (i, 0))
    full = lambda a: pl.BlockSpec(a.shape, lambda i: (0, 0))
    sh = lambda dt: jax.ShapeDtypeStruct((n, D_MODEL), dt)
    return pl.pallas_call(
        _mem_kv_kernel,
        grid=(n // tm,),
        in_specs=[row(), full(g_mem), full(wk_bf), full(wv_bf)],
        out_specs=[row(), row(), row(), row()],
        out_shape=[sh(F32), sh(F32), sh(BF16), sh(BF16)],
        compiler_params=_cparams(("parallel",)),
        name="mem_kv",
    )(mem2d, g_mem, wk_bf, wv_bf)


def _cross_kernel(x1_ref, qx_ref, mk_ref, mv_ref, wo_ref, g_ref, wpq_ref, x2_ref, h3_ref, qp_ref):
    q = qx_ref[...]
    heads = []
    for h in range(H_X):
        sl = slice(h * HD_X, (h + 1) * HD_X)
        kh = mk_ref[:, sl] if mk_ref.dtype == BF16 else mk_ref[:, sl].astype(BF16)
        vh = mv_ref[:, sl] if mv_ref.dtype == BF16 else mv_ref[:, sl].astype(BF16)
        s = _dot_nt(q[:, sl], kh)
        p = jnp.exp(s - jnp.max(s, axis=-1, keepdims=True))
        heads.append(_dot(p.astype(BF16), vh) / jnp.sum(p, axis=-1, keepdims=True))
    o = jnp.concatenate(heads, axis=-1).astype(BF16)
    x2 = x1_ref[...] + _dot(o, wo_ref[...])
    x2_ref[...] = x2
    h3 = _rms(x2, g_ref[...])
    h3_ref[...] = h3
    qp_ref[...] = _dot(h3.astype(BF16), wpq_ref[...]).astype(BF16)


def _cross(x1, qx, mk, mv, wo_bf, g_ffn, wpq_bf, tm):
    b, t, _ = x1.shape
    row = lambda w: pl.BlockSpec((None, tm, w), lambda b_, i: (b_, i, 0))
    mem = lambda: pl.BlockSpec((None, N_MEM, D_MODEL), lambda b_, i: (b_, 0, 0))
    full = lambda a: pl.BlockSpec(a.shape, lambda b_, i: (0, 0))
    dq = wpq_bf.shape[1]
    return pl.pallas_call(
        _cross_kernel,
        grid=(b, t // tm),
        in_specs=[row(D_MODEL), row(D_MODEL), mem(), mem(), full(wo_bf), full(g_ffn), full(wpq_bf)],
        out_specs=[row(D_MODEL), row(D_MODEL), row(dq)],
        out_shape=[jax.ShapeDtypeStruct((b, t, D_MODEL), F32), jax.ShapeDtypeStruct((b, t, D_MODEL), F32),
                   jax.ShapeDtypeStruct((b, t, dq), BF16)],
        compiler_params=_cparams(("parallel", "parallel")),
        name="cross_attn",
    )(x1, qx, mk, mv, wo_bf, g_ffn, wpq_bf)


ID_PAD = 2.0 ** 29


def _topk_rows(s, ids):
    vals, sel = [], []
    for _ in range(PEER_TOPK):
        m = jnp.max(s, axis=0, keepdims=True)
        idx = jnp.min(jnp.where(s == m, ids, ID_PAD), axis=0, keepdims=True)
        vals.append(m)
        sel.append(idx)
        s = jnp.where(ids == idx, -jnp.inf, s)
    return jnp.concatenate(vals, axis=0), jnp.concatenate(sel, axis=0)


def _candidates(v1, i1, v2, i2):
    lanes = v1.shape[1]
    b8 = lax.broadcasted_iota(jnp.int32, (8, lanes), 0)
    ident = lambda a, ia, ib, b: (a * PEER_TOPK + b) * float(N_EXPERTS) + (ia * float(N_KEYS) + ib)
    b16 = lax.broadcasted_iota(jnp.int32, (PEER_TOPK, lanes), 0).astype(F32)
    vals = [v1[0:1] + v2]
    ids = [ident(0, i1[0:1], i2, b16)]
    for a in range(1, 8):
        keep = b8 < PEER_TOPK // (a + 1)
        vals.append(jnp.where(keep, v1[a:a + 1] + v2[0:8], -jnp.inf))
        ids.append(jnp.where(keep, ident(a, i1[a:a + 1], i2[0:8], b8.astype(F32)), ID_PAD))
    a_hi = (b8 + 8).astype(F32)
    vals.append(v1[8:16] + v2[0:1])
    ids.append(ident(a_hi, i1[8:16], i2[0:1], 0.0))
    return jnp.concatenate(vals, axis=0), jnp.concatenate(ids, axis=0)


def _route_kernel(qp_ref, k1_ref, k2_ref, e_ref, g_ref):
    half = N_KEYS
    tt = qp_ref.shape[0]
    key_id = lax.broadcasted_iota(jnp.int32, (N_KEYS, tt), 0).astype(F32)
    es, gs = [], []
    for p in range(PEER_HEADS):
        q1 = qp_ref[:, (2 * p) * half:(2 * p + 1) * half]
        q2 = qp_ref[:, (2 * p + 1) * half:(2 * p + 2) * half]
        v1, i1 = _topk_rows(_dot_nt(k1_ref[p], q1), key_id)
        v2, i2 = _topk_rows(_dot_nt(k2_ref[p], q2), key_id)
        sc, sel = _topk_rows(*_candidates(v1, i1, v2, i2))
        w = jnp.exp(sc - sc[0:1])
        es.append((sel.astype(jnp.int32) & (N_EXPERTS - 1)) * ROWS_PER_EXPERT)
        gs.append(w / jnp.sum(w, axis=0, keepdims=True))
    e_ref[...] = jnp.concatenate(es, axis=0).T
    g_ref[...] = jnp.concatenate(gs, axis=0).T


def _route(qp2d, k1_bf, k2_bf, tt):
    n = qp2d.shape[0]
    full = lambda a: pl.BlockSpec(a.shape, lambda i: (0, 0, 0))
    return pl.pallas_call(
        _route_kernel,
        grid=(n // tt,),
        in_specs=[pl.BlockSpec((tt, qp2d.shape[1]), lambda i: (i, 0)), full(k1_bf), full(k2_bf)],
        out_specs=[pl.BlockSpec((tt, N_PAIRS), lambda i: (i, 0)), pl.BlockSpec((tt, N_PAIRS), lambda i: (i, 0))],
        out_shape=[jax.ShapeDtypeStruct((n, N_PAIRS), jnp.int32), jax.ShapeDtypeStruct((n, N_PAIRS), F32)],
        compiler_params=_cparams(("parallel",)),
        name="peer_route",
    )(qp2d, k1_bf, k2_bf)


def _unpack(words):
    hi = lax.bitcast_convert_type(words & jnp.uint32(0xFFFF0000), F32)
    lo = lax.bitcast_convert_type(words << 16, F32)
    return hi, lo


def _expert_rows(tab_ref, row0):
    return tab_ref[pl.ds(pl.multiple_of(row0, ROWS_PER_EXPERT), ROWS_PER_EXPERT), :]


REDUCE_TOKENS = 16


def _lane_sums_to_rows(y, n_tok):
    hi = y.astype(BF16)
    r = y - hi.astype(F32)
    mid = r.astype(BF16)
    lo = (r - mid.astype(F32)).astype(BF16)
    ones = jnp.ones((128, N_PAIRS), BF16)
    s = (_dot(hi, ones) + _dot(mid, ones) + _dot(lo, ones)).reshape(n_tok, N_PAIRS, N_PAIRS)
    eye = lax.broadcasted_iota(jnp.int32, (N_PAIRS, N_PAIRS), 0) == lax.broadcasted_iota(jnp.int32, (N_PAIRS, N_PAIRS), 1)
    return jnp.sum(jnp.where(eye[None], s, 0.0), axis=1)


def _peer_u_kernel(e_ref, h_ref, gate_ref, tab_ref, w_ref, prod_ref, ys_ref, act_ref, *, tt):
    rpt = 2 * ROWS_PER_EXPERT

    def token(t, carry):
        base = pl.multiple_of(t * rpt, rpt)
        ha = h_ref[pl.ds(base, ROWS_PER_EXPERT), :]
        hb = h_ref[pl.ds(pl.multiple_of(base + ROWS_PER_EXPERT, ROWS_PER_EXPERT), ROWS_PER_EXPERT), :]
        for k in range(N_PAIRS):
            hi, lo = _unpack(_expert_rows(tab_ref, e_ref[t, k]))
            prod_ref[k * ROWS_PER_EXPERT:(k + 1) * ROWS_PER_EXPERT, :] = hi * ha + lo * hb
        y = prod_ref[pl.ds(0, N_PAIRS, stride=ROWS_PER_EXPERT), :]
        for c in range(1, ROWS_PER_EXPERT):
            y = y + prod_ref[pl.ds(c, N_PAIRS, stride=ROWS_PER_EXPERT), :]
        ys_ref[pl.ds(pl.multiple_of(t * N_PAIRS, N_PAIRS), N_PAIRS), :] = y
        return carry

    lax.fori_loop(0, tt, token, 0)

    def group(g, carry):
        rows = REDUCE_TOKENS * N_PAIRS
        y = ys_ref[pl.ds(pl.multiple_of(g * rows, rows), rows), :]
        act_ref[pl.ds(pl.multiple_of(g * REDUCE_TOKENS, REDUCE_TOKENS), REDUCE_TOKENS), :] = _lane_sums_to_rows(y, REDUCE_TOKENS)
        return carry

    lax.fori_loop(0, tt // REDUCE_TOKENS, group, 0)
    a = act_ref[...]
    gelu = 0.5 * a * (1.0 + lax.erf(a * (2.0 ** -0.5)))
    w_ref[...] = gate_ref[...] * gelu


def _peer_u(e_t, h8, gate_t, tab, tt):
    n = e_t.shape[0]
    assert tt % REDUCE_TOKENS == 0
    kern = functools.partial(_peer_u_kernel, tt=tt)
    pair = lambda **kw: pl.BlockSpec((tt, N_PAIRS), lambda i: (i, 0), **kw)
    return pl.pallas_call(
        kern,
        grid=(n // tt,),
        in_specs=[pair(memory_space=pltpu.SMEM),
                  pl.BlockSpec((tt * 8, 128), lambda i: (i, 0)),
                  pair(),
                  pl.BlockSpec(tab.shape, lambda i: (0, 0), pipeline_mode=pl.Buffered(1))],
        out_specs=pair(),
        out_shape=jax.ShapeDtypeStruct((n, N_PAIRS), F32),
        scratch_shapes=[pltpu.VMEM((N_PAIRS * ROWS_PER_EXPERT, 128), F32),
                        pltpu.VMEM((tt * N_PAIRS, 128), F32),
                        pltpu.VMEM((tt, N_PAIRS), F32)],
        compiler_params=_cparams(("arbitrary",)),
        name="peer_u",
    )(e_t, h8, gate_t, tab)


def _peer_v_kernel(e_ref, w_ref, x_ref, tab_ref, o_ref, *, tt):
    rpt = 2 * ROWS_PER_EXPERT
    n_acc = 2

    def token(t, carry):
        acc_hi = [jnp.zeros((ROWS_PER_EXPERT, 128), F32) for _ in range(n_acc)]
        acc_lo = [jnp.zeros((ROWS_PER_EXPERT, 128), F32) for _ in range(n_acc)]
        for k in range(N_PAIRS):
            hi, lo = _unpack(_expert_rows(tab_ref, e_ref[t, k]))
            w = w_ref[t, k]
            acc_hi[k % n_acc] = acc_hi[k % n_acc] + w * hi
            acc_lo[k % n_acc] = acc_lo[k % n_acc] + w * lo
        base = pl.multiple_of(t * rpt, rpt)
        lo_base = pl.multiple_of(base + ROWS_PER_EXPERT, ROWS_PER_EXPERT)
        o_ref[pl.ds(base, ROWS_PER_EXPERT), :] = x_ref[pl.ds(base, ROWS_PER_EXPERT), :] + sum(acc_hi[1:], acc_hi[0])
        o_ref[pl.ds(lo_base, ROWS_PER_EXPERT), :] = x_ref[pl.ds(lo_base, ROWS_PER_EXPERT), :] + sum(acc_lo[1:], acc_lo[0])
        return carry

    lax.fori_loop(0, tt, token, 0)


def _peer_v(e_t, w_t, x8, tab, tt):
    n = e_t.shape[0]
    kern = functools.partial(_peer_v_kernel, tt=tt)
    pair = lambda: pl.BlockSpec((tt, N_PAIRS), lambda i: (i, 0), memory_space=pltpu.SMEM)
    return pl.pallas_call(
        kern,
        grid=(n // tt,),
        in_specs=[pair(), pair(),
                  pl.BlockSpec((tt * 8, 128), lambda i: (i, 0)),
                  pl.BlockSpec(tab.shape, lambda i: (0, 0), pipeline_mode=pl.Buffered(1))],
        out_specs=pl.BlockSpec((tt * 8, 128), lambda i: (i, 0)),
        out_shape=jax.ShapeDtypeStruct(x8.shape, F32),
        compiler_params=_cparams(("arbitrary",)),
        name="peer_v",
    )(e_t, w_t, x8, tab)


def _final_kernel(x_ref, g_ref, o_ref):
    o_ref[...] = _rms(x_ref[...], g_ref[...])


def _final_norm(x2d, g, tm):
    n = x2d.shape[0]
    row = lambda: pl.BlockSpec((tm, D_MODEL), lambda i: (i, 0))
    return pl.pallas_call(
        _final_kernel,
        grid=(n // tm,),
        in_specs=[row(), pl.BlockSpec((1, D_MODEL), lambda i: (0, 0))],
        out_specs=row(),
        out_shape=jax.ShapeDtypeStruct((n, D_MODEL), F32),
        compiler_params=_cparams(("parallel",)),
        name="final_norm",
    )(x2d, g)


def _pack_table(tab):
    bits = lax.bitcast_convert_type(tab.astype(BF16), jnp.uint16).astype(jnp.uint32)
    words = (bits[:, :HALF] << 16) | bits[:, HALF:]
    return words.reshape(tab.shape[0] * ROWS_PER_EXPERT, 128)


def _row_tile(n, pref):
    while n % pref:
        pref //= 2
    return pref


def _peer_and_final(x2, h3, qp, k1_bf, k2_bf, tab_u, tab_v, g_final, tt):
    n = x2.shape[0]
    e_t, gate_t = _route(qp, k1_bf, k2_bf, tt)
    w_t = _peer_u(e_t, h3.reshape(n * 8, 128), gate_t, tab_u, tt)
    x3 = _peer_v(e_t, w_t, x2.reshape(n * 8, 128), tab_v, tt).reshape(n, D_MODEL)
    return _final_norm(x3, g_final, _row_tile(n, 512))


def kernel(x_prompt, x_sample, mem_prompt, cache_da_k, cache_da_v, state_ret, cache_mem_k, cache_mem_v, g_mix, w_in, lam_q1, lam_k1, lam_q2, lam_k2, g_da, g_ret, w_out, g_cross, g_mem, w_xq, w_xk, w_xv, w_xo, g_ffn, w_pq, peer_k1, peer_k2, peer_u, peer_v, g_final):
    depth = w_in.shape[0]
    assert depth == 1, "single-layer step"
    l = 0
    lam_init = 0.8 - 0.6 * math.exp(-0.3 * l)
    b, t, _ = x_prompt.shape
    bs, ts, _ = x_sample.shape
    past_len = cache_da_k.shape[2]

    row = lambda a: a.reshape(1, -1)
    w_in_bf = w_in[l].astype(BF16)
    w_out_bf = w_out[l].astype(BF16)
    w_xq_bf, w_xk_bf, w_xv_bf, w_xo_bf = (w[l].astype(BF16) for w in (w_xq, w_xk, w_xv, w_xo))
    w_pq_bf = w_pq[l].astype(BF16)
    k1_bf, k2_bf = peer_k1[l].astype(BF16), peer_k2[l].astype(BF16)
    tab_u, tab_v = _pack_table(peer_u[l]), _pack_table(peer_v[l])
    lamp = jnp.stack([lam_q1[l], lam_k1[l], lam_q2[l], lam_k2[l]])
    g_da3 = g_da[l].reshape(H_D, 1, DV_D)
    g_ret3 = g_ret[l].reshape(H_R, 1, DV_R)
    g_fin = row(g_final)

    def mixer_tail(x2d, mda, mret, mk, mv, bb, tt_rows):
        n = x2d.shape[0]
        tm = _row_tile(n, 512)
        x1, qx = _out_proj(x2d, mda.reshape(n, GROUP_W), mret.reshape(n, GROUP_W), w_out_bf, row(g_cross[l]), w_xq_bf, tm)
        x2, h3, qp = _cross(x1.reshape(bb, tt_rows, D_MODEL), qx.reshape(bb, tt_rows, D_MODEL), mk, mv,
                            w_xo_bf, row(g_ffn[l]), w_pq_bf, _row_tile(tt_rows, 512))
        y = _peer_and_final(x2.reshape(n, D_MODEL), h3.reshape(n, D_MODEL), qp.reshape(n, -1),
                            k1_bf, k2_bf, tab_u, tab_v, g_fin, 128)
        return y.reshape(bb, tt_rows, D_MODEL)

    n = b * t
    xp = x_prompt.reshape(n, D_MODEL)
    qd, kd, vd, kdb, vdb, qr, kr, vr, gr = _in_proj(xp, row(g_mix[l]), w_in_bf, _row_tile(n, 512))
    r3 = lambda a: a.reshape(b, t, GROUP_W)
    mda = _diff_attn_prompt(lamp, r3(qd), r3(kdb), r3(vdb), g_da3, lam_init, 256, 512)
    mret, s_fin = _retention(r3(qr), r3(kr), r3(vr), r3(gr), g_ret3, None, 256)
    mk, mv, mkb, mvb = _mem_kv(mem_prompt.reshape(b * N_MEM, D_MODEL), row(g_mem[l]), w_xk_bf, w_xv_bf, 512)
    y_prompt = mixer_tail(xp, mda, mret, mkb.reshape(b, N_MEM, D_MODEL), mvb.reshape(b, N_MEM, D_MODEL), b, t)

    ns = bs * ts
    xs = x_sample.reshape(ns, D_MODEL)
    qd_s, kd_s, vd_s, kdb_s, vdb_s, qr_s, kr_s, vr_s, gr_s = _in_proj(xs, row(g_mix[l]), w_in_bf, _row_tile(ns, 512))
    s3 = lambda a: a.reshape(bs, ts, GROUP_W)
    mda_s = _diff_attn_sample(lamp, s3(qd_s), cache_da_k[l].reshape(bs, past_len, GROUP_W),
                              cache_da_v[l].reshape(bs, past_len, GROUP_W), s3(kdb_s), s3(vdb_s), g_da3, lam_init)
    mret_s, s_new = _retention(s3(qr_s), s3(kr_s), s3(vr_s), s3(gr_s), g_ret3, state_ret[l], ts)
    y_sample = mixer_tail(xs, mda_s, mret_s, cache_mem_k[l].reshape(bs, N_MEM, D_MODEL),
                          cache_mem_v[l].reshape(bs, N_MEM, D_MODEL), bs, ts)

    return (y_prompt, y_sample,
            kd.reshape(1, b, t, H_D, 2, DK_D), vd.reshape(1, b, t, H_D, DV_D), s_fin[None],
            mk.reshape(1, b, N_MEM, H_X, HD_X), mv.reshape(1, b, N_MEM, H_X, HD_X),
            kd_s.reshape(1, bs, ts, H_D, 2, DK_D), vd_s.reshape(1, bs, ts, H_D, DV_D), s_new[None])
```

```python
import functools
import math

import jax
import jax.numpy as jnp
from jax import lax
from jax.experimental import pallas as pl
from jax.experimental.pallas import tpu as pltpu

D_MODEL = 1024
CHUNK = 64
CHUNK_SHIFT = CHUNK.bit_length() - 1
assert 1 << CHUNK_SHIFT == CHUNK
H_D, DK_D, DV_D = 4, 64, 128
H_R, DK_R, DV_R = 4, 128, 128
N_MEM = 256
H_X = 4
HD_X = D_MODEL // H_X
PEER_HEADS = 8
N_KEYS = 128
N_EXPERTS = N_KEYS * N_KEYS
PEER_TOPK = 16
EPS = 1e-6
HEAD_W = 128
GROUP_W = 512
N_PAIRS = PEER_HEADS * PEER_TOPK
HALF = D_MODEL // 2
ROWS_PER_EXPERT = HALF // 128
VMEM_LIMIT = 56 * 1024 * 1024

BF16 = jnp.bfloat16
F32 = jnp.float32


def _cparams(sem):
    return pltpu.CompilerParams(dimension_semantics=sem, vmem_limit_bytes=VMEM_LIMIT)


def _rms(x, g):
    return x * lax.rsqrt(jnp.mean(x * x, axis=-1, keepdims=True) + EPS) * g


def _dot(a, b):
    return jnp.dot(a, b, preferred_element_type=F32)


def _dot_nt(a, b):
    return lax.dot_general(a, b, (((1,), (1,)), ((), ())), preferred_element_type=F32)


def _dot_tn(a, b):
    return lax.dot_general(a, b, (((0,), (0,)), ((), ())), preferred_element_type=F32)


def _select_by_head(h, values):
    out = jnp.float32(values[-1])
    for i in range(len(values) - 2, -1, -1):
        out = jnp.where(h == i, jnp.float32(values[i]), out)
    return out


def _in_proj_kernel(x_ref, g_ref, w_ref, qd_ref, kd_ref, vd_ref, kdb_ref, vdb_ref,
                    qr_ref, kr_ref, vr_ref, gr_ref):
    hb = _rms(x_ref[...], g_ref[...]).astype(BF16)
    col = lambda c: _dot(hb, w_ref[:, c * GROUP_W:(c + 1) * GROUP_W])
    qd_ref[...] = (col(0) * (DK_D ** -0.5)).astype(BF16)
    kd = col(1)
    kd_ref[...] = kd
    kdb_ref[...] = kd.astype(BF16)
    vd = col(2)
    vd_ref[...] = vd
    vdb_ref[...] = vd.astype(BF16)
    qr_ref[...] = col(3).astype(BF16)
    kr_ref[...] = (col(4) * (DK_R ** -0.5)).astype(BF16)
    vr_ref[...] = col(5).astype(BF16)
    gr_ref[...] = col(6)


def _in_proj(x2d, g, w_bf, tm):
    n = x2d.shape[0]
    blk = lambda: pl.BlockSpec((tm, GROUP_W), lambda i: (i, 0))
    sh = lambda dt: jax.ShapeDtypeStruct((n, GROUP_W), dt)
    return pl.pallas_call(
        _in_proj_kernel,
        grid=(n // tm,),
        in_specs=[pl.BlockSpec((tm, D_MODEL), lambda i: (i, 0)),
                  pl.BlockSpec((1, D_MODEL), lambda i: (0, 0)),
                  pl.BlockSpec(w_bf.shape, lambda i: (0, 0))],
        out_specs=[blk() for _ in range(9)],
        out_shape=[sh(BF16), sh(F32), sh(F32), sh(BF16), sh(BF16), sh(BF16), sh(BF16), sh(BF16), sh(F32)],
        compiler_params=_cparams(("parallel",)),
        name="in_proj",
    )(x2d, g, w_bf)


def _lambda_from(lam_ref, lam_init):
    l = lam_ref[...]
    a = jnp.exp(jnp.sum(l[0:1] * l[1:2], axis=-1, keepdims=True))
    b = jnp.exp(jnp.sum(l[2:3] * l[3:4], axis=-1, keepdims=True))
    return a - b + lam_init


def _diff_post(acc, l, lam, g, lam_init, tq):
    o = acc[:tq] / l[:tq] - lam * (acc[tq:] / l[tq:])
    return o * lax.rsqrt(jnp.mean(o * o, axis=-1, keepdims=True) + EPS) * g * (1.0 - lam_init)


def _split_maps(q):
    lane = lax.broadcasted_iota(jnp.int32, q.shape, 1)
    zero = jnp.zeros_like(q)
    return jnp.concatenate([jnp.where(lane < DK_D, q, zero), jnp.where(lane >= DK_D, q, zero)], axis=0)


def _da_prompt_kernel(lam_ref, q_ref, k_ref, v_ref, g_ref, o_ref, kx_ref, vx_ref, own_ref, acc_ref, m_ref, *, lam_init, tq, tk):
    h = pl.program_id(1)
    i = pl.program_id(2)
    t = k_ref.shape[0]
    slope = _select_by_head(h, [2.0 ** (-8.0 * (j + 1) / H_D) for j in range(H_D)])

    @pl.when(i == 0)
    def _():
        pos = lax.broadcasted_iota(jnp.int32, (t, HEAD_W), 0)
        lane = lax.broadcasted_iota(jnp.int32, (t, HEAD_W), 1)
        coarse = ((pos >> CHUNK_SHIFT) << CHUNK_SHIFT).astype(F32) * slope
        fine = (pos & (CHUNK - 1)).astype(F32) * slope
        kx_ref[:, :HEAD_W] = k_ref[...]
        kx_ref[:, HEAD_W:] = jnp.where(lane == 0, coarse, jnp.where(lane == 1, fine, 0.0)).astype(BF16)
        vx_ref[:DV_D, :] = v_ref[...].astype(F32).T.astype(BF16)
        vx_ref[DV_D:, :] = jnp.ones((vx_ref.shape[0] - DV_D, t), BF16)
        krel = lax.broadcasted_iota(jnp.int32, (tk, 2 * tq), 0)
        c = lax.broadcasted_iota(jnp.int32, (tk, 2 * tq), 1)
        for par in range(tk // tq):
            qrel = par * tq + jnp.where(c >= tq, c - tq, c)
            ahead = (2.0 * slope) * jnp.maximum(krel - qrel, 0).astype(F32)
            own_ref[par] = jnp.where((qrel >> CHUNK_SHIFT) >= (krel >> CHUNK_SHIFT), -ahead, -1e30)

    q = q_ref[...]
    lane = lax.broadcasted_iota(jnp.int32, q.shape, 1)
    zero = jnp.zeros_like(q)
    ones2 = jnp.where(lane < 2, 1.0, 0.0).astype(BF16)
    q2 = jnp.concatenate([jnp.concatenate([jnp.where(lane < DK_D, q, zero), ones2], axis=1),
                          jnp.concatenate([jnp.where(lane >= DK_D, q, zero), ones2], axis=1)], axis=0)
    jd = (i * tq) // tk

    def scores(j):
        return _dot_nt(kx_ref[pl.ds(pl.multiple_of(j * tk, tk), tk), :], q2)

    def values(j):
        return vx_ref[:, pl.ds(pl.multiple_of(j * tk, tk), tk)]

    s = scores(jd) + own_ref[(i * tq) % tk // tq]
    m0 = jnp.max(s, axis=0, keepdims=True)
    m_ref[...] = m0
    acc_ref[...] = _dot(values(jd), jnp.exp(s - m0).astype(BF16))

    def absorb(blocks):
        ss = [scores(j) for j in blocks]
        m_old = m_ref[...]
        m_new = m_old
        for s in ss:
            m_new = jnp.maximum(m_new, jnp.max(s, axis=0, keepdims=True))
        m_ref[...] = m_new
        acc = jnp.exp(m_old - m_new) * acc_ref[...]
        for j, s in zip(blocks, ss):
            acc = acc + _dot(values(j), jnp.exp(s - m_new).astype(BF16))
        acc_ref[...] = acc

    def past_pair(jj, carry):
        absorb([2 * jj, 2 * jj + 1])
        return carry

    lax.fori_loop(0, jd // 2, past_pair, 0)

    @pl.when(jd % 2 == 1)
    def _():
        absorb([jd - 1])

    acc = acc_ref[...]
    num, den = acc[:DV_D], acc[DV_D:DV_D + 1]
    lam = _lambda_from(lam_ref, lam_init)
    o = (num[:, :tq] / den[:, :tq] - lam * (num[:, tq:] / den[:, tq:])).T
    o = o * lax.rsqrt(jnp.mean(o * o, axis=-1, keepdims=True) + EPS) * g_ref[...] * (1.0 - lam_init)
    o_ref[...] = o.astype(o_ref.dtype)


ONES_ROWS = 16


def _diff_attn_prompt(lamp, q, k, v, g_da3, lam_init, tq, tk):
    b, t, _ = q.shape
    kern = functools.partial(_da_prompt_kernel, lam_init=lam_init, tq=tq, tk=tk)
    return pl.pallas_call(
        kern,
        grid=(b, H_D, t // tq),
        in_specs=[pl.BlockSpec((4, DK_D), lambda b_, h, i: (0, 0)),
                  pl.BlockSpec((None, tq, HEAD_W), lambda b_, h, i: (b_, i, h)),
                  pl.BlockSpec((None, t, HEAD_W), lambda b_, h, i: (b_, 0, h)),
                  pl.BlockSpec((None, t, HEAD_W), lambda b_, h, i: (b_, 0, h)),
                  pl.BlockSpec((None, 1, HEAD_W), lambda b_, h, i: (h, 0, 0))],
        out_specs=pl.BlockSpec((None, tq, HEAD_W), lambda b_, h, i: (b_, i, h)),
        out_shape=jax.ShapeDtypeStruct((b, t, GROUP_W), BF16),
        scratch_shapes=[pltpu.VMEM((t, 2 * HEAD_W), BF16), pltpu.VMEM((DV_D + ONES_ROWS, t), BF16),
                        pltpu.VMEM((tk // tq, tk, 2 * tq), F32),
                        pltpu.VMEM((DV_D + ONES_ROWS, 2 * tq), F32), pltpu.VMEM((1, 2 * tq), F32)],
        compiler_params=_cparams(("parallel", "parallel", "arbitrary")),
        name="diff_attn_prompt",
    )(lamp, q, k, v, g_da3)


def _da_sample_kernel(lam_ref, q_ref, kc_ref, vc_ref, kn_ref, vn_ref, g_ref, o_ref, *, lam_init, ts, past_len):
    h = pl.program_id(1)
    slope = _select_by_head(h, [2.0 ** (-8.0 * (j + 1) / H_D) for j in range(H_D)])
    q2 = _split_maps(q_ref[...])
    rows = 2 * ts

    def scores(k, base, n):
        r = lax.broadcasted_iota(jnp.int32, (rows, n), 0)
        qpos = past_len + jnp.where(r >= ts, r - ts, r)
        kpos = base + lax.broadcasted_iota(jnp.int32, (rows, n), 1)
        return _dot_nt(q2, k) - slope * jnp.abs(qpos - kpos).astype(F32)

    sc = scores(kc_ref[...].astype(BF16), 0, past_len)
    sn = scores(kn_ref[...], past_len, ts)
    m = jnp.maximum(jnp.max(sc, axis=-1, keepdims=True), jnp.max(sn, axis=-1, keepdims=True))
    pc = jnp.exp(sc - m)
    pn = jnp.exp(sn - m)
    l = jnp.sum(pc, axis=-1, keepdims=True) + jnp.sum(pn, axis=-1, keepdims=True)
    acc = _dot(pc.astype(BF16), vc_ref[...].astype(BF16)) + _dot(pn.astype(BF16), vn_ref[...])
    lam = _lambda_from(lam_ref, lam_init)
    o_ref[...] = _diff_post(acc, l, lam, g_ref[...], lam_init, ts).astype(o_ref.dtype)


def _diff_attn_sample(lamp, q, kc, vc, kn, vn, g_da3, lam_init):
    b, ts, _ = q.shape
    past_len = kc.shape[1]
    kern = functools.partial(_da_sample_kernel, lam_init=lam_init, ts=ts, past_len=past_len)
    head = lambda rows: pl.BlockSpec((None, rows, HEAD_W), lambda b_, h: (b_, 0, h))
    return pl.pallas_call(
        kern,
        grid=(b, H_D),
        in_specs=[pl.BlockSpec((4, DK_D), lambda b_, h: (0, 0)),
                  head(ts), head(past_len), head(past_len), head(ts), head(ts),
                  pl.BlockSpec((None, 1, HEAD_W), lambda b_, h: (h, 0, 0))],
        out_specs=head(ts),
        out_shape=jax.ShapeDtypeStruct((b, ts, GROUP_W), BF16),
        compiler_params=_cparams(("parallel", "parallel")),
        name="diff_attn_sample",
    )(lamp, q, kc, vc, kn, vn, g_da3)


def _ret_kernel(*refs, lb, has_init):
    if has_init:
        q_ref, k_ref, v_ref, gate_ref, g_ref, s0_ref, o_ref, sfin_ref, s_ref = refs
    else:
        q_ref, k_ref, v_ref, gate_ref, g_ref, o_ref, sfin_ref, s_ref = refs
    h = pl.program_id(1)
    c = pl.program_id(2)
    lg = _select_by_head(h, [math.log1p(-(2.0 ** (-5.0 - j))) for j in range(H_R)])

    @pl.when(c == 0)
    def _():
        s_ref[...] = s0_ref[...] if has_init else jnp.zeros_like(s_ref)

    q, k, v = q_ref[...], k_ref[...], v_ref[...]
    i = lax.broadcasted_iota(jnp.int32, (lb, lb), 0)
    j = lax.broadcasted_iota(jnp.int32, (lb, lb), 1)
    d = (i - j).astype(F32)
    decay = jnp.where(d >= 0, jnp.exp(jnp.maximum(d, 0.0) * lg), 0.0)
    inner = _dot_nt(q, k) * decay
    ic = lax.broadcasted_iota(jnp.int32, (lb, 1), 0).astype(F32)
    s_old = s_ref[...]
    o = _dot(inner.astype(BF16), v) + _dot(q, s_old.astype(BF16)) * jnp.exp((ic + 1.0) * lg)
    tail = jnp.exp((lb - 1.0 - ic) * lg)
    kt = (k.astype(F32) * tail).astype(BF16)
    s_new = jnp.exp(lb * lg) * s_old + _dot_tn(kt, v)
    s_ref[...] = s_new

    @pl.when(c == pl.num_programs(2) - 1)
    def _():
        sfin_ref[...] = s_new

    oc = o - jnp.mean(o, axis=-1, keepdims=True)
    y = oc * lax.rsqrt(jnp.mean(oc * oc, axis=-1, keepdims=True) + EPS) * g_ref[...]
    gate = gate_ref[...]
    o_ref[...] = (y * (gate * jax.nn.sigmoid(gate))).astype(o_ref.dtype)


def _retention(q, k, v, gate, g_ret3, s0, lb):
    b, t, _ = q.shape
    has_init = s0 is not None
    kern = functools.partial(_ret_kernel, lb=lb, has_init=has_init)
    head = lambda: pl.BlockSpec((None, lb, HEAD_W), lambda b_, h, c: (b_, c, h))
    state = lambda: pl.BlockSpec((None, None, DK_R, DV_R), lambda b_, h, c: (b_, h, 0, 0))
    in_specs = [head(), head(), head(), head(), pl.BlockSpec((None, 1, HEAD_W), lambda b_, h, c: (h, 0, 0))]
    args = [q, k, v, gate, g_ret3]
    if has_init:
        in_specs.append(state())
        args.append(s0)
    return pl.pallas_call(
        kern,
        grid=(b, H_R, t // lb),
        in_specs=in_specs,
        out_specs=[head(), state()],
        out_shape=[jax.ShapeDtypeStruct((b, t, GROUP_W), BF16), jax.ShapeDtypeStruct((b, H_R, DK_R, DV_R), F32)],
        scratch_shapes=[pltpu.VMEM((DK_R, DV_R), F32)],
        compiler_params=_cparams(("parallel", "parallel", "arbitrary")),
        name="retention",
    )(*args)


def _out_proj_kernel(x_ref, mda_ref, mret_ref, wo_ref, g_ref, wq_ref, x1_ref, qx_ref):
    x1 = x_ref[...] + _dot(mda_ref[...], wo_ref[:GROUP_W, :]) + _dot(mret_ref[...], wo_ref[GROUP_W:, :])
    x1_ref[...] = x1
    hn = _rms(x1, g_ref[...]).astype(BF16)
    qx_ref[...] = (_dot(hn, wq_ref[...]) * (HD_X ** -0.5)).astype(BF16)


def _out_proj(x2d, mda, mret, wo_bf, g_cross, wq_bf, tm):
    n = x2d.shape[0]
    full = lambda a: pl.BlockSpec(a.shape, lambda i: (0, 0))
    return pl.pallas_call(
        _out_proj_kernel,
        grid=(n // tm,),
        in_specs=[pl.BlockSpec((tm, D_MODEL), lambda i: (i, 0)),
                  pl.BlockSpec((tm, GROUP_W), lambda i: (i, 0)),
                  pl.BlockSpec((tm, GROUP_W), lambda i: (i, 0)),
                  full(wo_bf), full(g_cross), full(wq_bf)],
        out_specs=[pl.BlockSpec((tm, D_MODEL), lambda i: (i, 0)), pl.BlockSpec((tm, D_MODEL), lambda i: (i, 0))],
        out_shape=[jax.ShapeDtypeStruct((n, D_MODEL), F32), jax.ShapeDtypeStruct((n, D_MODEL), BF16)],
        compiler_params=_cparams(("parallel",)),
        name="out_proj",
    )(x2d, mda, mret, wo_bf, g_cross, wq_bf)


def _mem_kv_kernel(m_ref, g_ref, wk_ref, wv_ref, mk_ref, mv_ref, mkb_ref, mvb_ref):
    mn = _rms(m_ref[...], g_ref[...]).astype(BF16)
    mk = _dot(mn, wk_ref[...])
    mv = _dot(mn, wv_ref[...])
    mk_ref[...] = mk
    mv_ref[...] = mv
    mkb_ref[...] = mk.astype(BF16)
    mvb_ref[...] = mv.astype(BF16)


def _mem_kv(mem2d, g_mem, wk_bf, wv_bf, tm):
    n = mem2d.shape[0]
    row = lambda: pl.BlockSpec((tm, D_MODEL), lambda i: (i, 0))
    full = lambda a: pl.BlockSpec(a.shape, lambda i: (0, 0))
    sh = lambda dt: jax.ShapeDtypeStruct((n, D_MODEL), dt)
    return pl.pallas_call(
        _mem_kv_kernel,
        grid=(n // tm,),
        in_specs=[row(), full(g_mem), full(wk_bf), full(wv_bf)],
        out_specs=[row(), row(), row(), row()],
        out_shape=[sh(F32), sh(F32), sh(BF16), sh(BF16)],
        compiler_params=_cparams(("parallel",)),
        name="mem_kv",
    )(mem2d, g_mem, wk_bf, wv_bf)


def _cross_kernel(x1_ref, qx_ref, mk_ref, mv_ref, wo_ref, g_ref, wpq_ref, x2_ref, h3_ref, qp_ref):
    q = qx_ref[...]
    heads = []
    for h in range(H_X):
        sl = slice(h * HD_X, (h + 1) * HD_X)
        kh = mk_ref[:, sl] if mk_ref.dtype == BF16 else mk_ref[:, sl].astype(BF16)
        vh = mv_ref[:, sl] if mv_ref.dtype == BF16 else mv_ref[:, sl].astype(BF16)
        s = _dot_nt(q[:, sl], kh)
        p = jnp.exp(s - jnp.max(s, axis=-1, keepdims=True))
        heads.append(_dot(p.astype(BF16), vh) / jnp.sum(p, axis=-1, keepdims=True))
    o = jnp.concatenate(heads, axis=-1).astype(BF16)
    x2 = x1_ref[...] + _dot(o, wo_ref[...])
    x2_ref[...] = x2
    h3 = _rms(x2, g_ref[...])
    h3_ref[...] = h3
    qp_ref[...] = _dot(h3.astype(BF16), wpq_ref[...]).astype(BF16)


def _cross(x1, qx, mk, mv, wo_bf, g_ffn, wpq_bf, tm):
    b, t, _ = x1.shape
    row = lambda w: pl.BlockSpec((None, tm, w), lambda b_, i: (b_, i, 0))
    mem = lambda: pl.BlockSpec((None, N_MEM, D_MODEL), lambda b_, i: (b_, 0, 0))
    full = lambda a: pl.BlockSpec(a.shape, lambda b_, i: (0, 0))
    dq = wpq_bf.shape[1]
    return pl.pallas_call(
        _cross_kernel,
        grid=(b, t // tm),
        in_specs=[row(D_MODEL), row(D_MODEL), mem(), mem(), full(wo_bf), full(g_ffn), full(wpq_bf)],
        out_specs=[row(D_MODEL), row(D_MODEL), row(dq)],
        out_shape=[jax.ShapeDtypeStruct((b, t, D_MODEL), F32), jax.ShapeDtypeStruct((b, t, D_MODEL), F32),
                   jax.ShapeDtypeStruct((b, t, dq), BF16)],
        compiler_params=_cparams(("parallel", "parallel")),
        name="cross_attn",
    )(x1, qx, mk, mv, wo_bf, g_ffn, wpq_bf)


ID_PAD = 2.0 ** 29


def _topk_rows(s, ids):
    vals, sel = [], []
    for _ in range(PEER_TOPK):
        m = jnp.max(s, axis=0, keepdims=True)
        idx = jnp.min(jnp.where(s == m, ids, ID_PAD), axis=0, keepdims=True)
        vals.append(m)
        sel.append(idx)
        s = jnp.where(ids == idx, -jnp.inf, s)
    return jnp.concatenate(vals, axis=0), jnp.concatenate(sel, axis=0)


def _candidates(v1, i1, v2, i2):
    lanes = v1.shape[1]
    b8 = lax.broadcasted_iota(jnp.int32, (8, lanes), 0)
    ident = lambda a, ia, ib, b: (a * PEER_TOPK + b) * float(N_EXPERTS) + (ia * float(N_KEYS) + ib)
    b16 = lax.broadcasted_iota(jnp.int32, (PEER_TOPK, lanes), 0).astype(F32)
    vals = [v1[0:1] + v2]
    ids = [ident(0, i1[0:1], i2, b16)]
    for a in range(1, 8):
        keep = b8 < PEER_TOPK // (a + 1)
        vals.append(jnp.where(keep, v1[a:a + 1] + v2[0:8], -jnp.inf))
        ids.append(jnp.where(keep, ident(a, i1[a:a + 1], i2[0:8], b8.astype(F32)), ID_PAD))
    a_hi = (b8 + 8).astype(F32)
    vals.append(v1[8:16] + v2[0:1])
    ids.append(ident(a_hi, i1[8:16], i2[0:1], 0.0))
    return jnp.concatenate(vals, axis=0), jnp.concatenate(ids, axis=0)


def _route_kernel(qp_ref, k1_ref, k2_ref, e_ref, g_ref):
    half = N_KEYS
    tt = qp_ref.shape[0]
    key_id = lax.broadcasted_iota(jnp.int32, (N_KEYS, tt), 0).astype(F32)
    es, gs = [], []
    for p in range(PEER_HEADS):
        q1 = qp_ref[:, (2 * p) * half:(2 * p + 1) * half]
        q2 = qp_ref[:, (2 * p + 1) * half:(2 * p + 2) * half]
        v1, i1 = _topk_rows(_dot_nt(k1_ref[p], q1), key_id)
        v2, i2 = _topk_rows(_dot_nt(k2_ref[p], q2), key_id)
        sc, sel = _topk_rows(*_candidates(v1, i1, v2, i2))
        w = jnp.exp(sc - sc[0:1])
        es.append((sel.astype(jnp.int32) & (N_EXPERTS - 1)) * ROWS_PER_EXPERT)
        gs.append(w / jnp.sum(w, axis=0, keepdims=True))
    e_ref[...] = jnp.concatenate(es, axis=0).T
    g_ref[...] = jnp.concatenate(gs, axis=0).T


def _route(qp2d, k1_bf, k2_bf, tt):
    n = qp2d.shape[0]
    full = lambda a: pl.BlockSpec(a.shape, lambda i: (0, 0, 0))
    return pl.pallas_call(
        _route_kernel,
        grid=(n // tt,),
        in_specs=[pl.BlockSpec((tt, qp2d.shape[1]), lambda i: (i, 0)), full(k1_bf), full(k2_bf)],
        out_specs=[pl.BlockSpec((tt, N_PAIRS), lambda i: (i, 0)), pl.BlockSpec((tt, N_PAIRS), lambda i: (i, 0))],
        out_shape=[jax.ShapeDtypeStruct((n, N_PAIRS), jnp.int32), jax.ShapeDtypeStruct((n, N_PAIRS), F32)],
        compiler_params=_cparams(("parallel",)),
        name="peer_route",
    )(qp2d, k1_bf, k2_bf)


def _unpack(words):
    hi = lax.bitcast_convert_type(words & jnp.uint32(0xFFFF0000), F32)
    lo = lax.bitcast_convert_type(words << 16, F32)
    return hi, lo


def _expert_rows(tab_ref, row0):
    return tab_ref[pl.ds(pl.multiple_of(row0, ROWS_PER_EXPERT), ROWS_PER_EXPERT), :]


REDUCE_TOKENS = 16


def _lane_sums_to_rows(y, n_tok):
    hi = y.astype(BF16)
    r = y - hi.astype(F32)
    mid = r.astype(BF16)
    lo = (r - mid.astype(F32)).astype(BF16)
    ones = jnp.ones((128, N_PAIRS), BF16)
    s = (_dot(hi, ones) + _dot(mid, ones) + _dot(lo, ones)).reshape(n_tok, N_PAIRS, N_PAIRS)
    eye = lax.broadcasted_iota(jnp.int32, (N_PAIRS, N_PAIRS), 0) == lax.broadcasted_iota(jnp.int32, (N_PAIRS, N_PAIRS), 1)
    return jnp.sum(jnp.where(eye[None], s, 0.0), axis=1)


CHUNKS = D_MODEL // 128


def _peer_u_kernel(e_ref, h_ref, gate_ref, tab_ref, w_ref, h8_ref, prod_ref, ys_ref, act_ref, *, tt):
    for c in range(CHUNKS):
        h8_ref[pl.ds(c, tt, stride=CHUNKS), :] = h_ref[:, c * 128:(c + 1) * 128]

    def token(t, carry):
        base = pl.multiple_of(t * CHUNKS, CHUNKS)
        ha = h8_ref[pl.ds(base, ROWS_PER_EXPERT), :]
        hb = h8_ref[pl.ds(pl.multiple_of(base + ROWS_PER_EXPERT, ROWS_PER_EXPERT), ROWS_PER_EXPERT), :]
        for k in range(N_PAIRS):
            hi, lo = _unpack(_expert_rows(tab_ref, e_ref[t, k]))
            prod_ref[k * ROWS_PER_EXPERT:(k + 1) * ROWS_PER_EXPERT, :] = hi * ha + lo * hb
        y = prod_ref[pl.ds(0, N_PAIRS, stride=ROWS_PER_EXPERT), :]
        for c in range(1, ROWS_PER_EXPERT):
            y = y + prod_ref[pl.ds(c, N_PAIRS, stride=ROWS_PER_EXPERT), :]
        ys_ref[pl.ds(pl.multiple_of(t * N_PAIRS, N_PAIRS), N_PAIRS), :] = y
        return carry

    lax.fori_loop(0, tt, token, 0)

    def group(g, carry):
        rows = REDUCE_TOKENS * N_PAIRS
        y = ys_ref[pl.ds(pl.multiple_of(g * rows, rows), rows), :]
        act_ref[pl.ds(pl.multiple_of(g * REDUCE_TOKENS, REDUCE_TOKENS), REDUCE_TOKENS), :] = _lane_sums_to_rows(y, REDUCE_TOKENS)
        return carry

    lax.fori_loop(0, tt // REDUCE_TOKENS, group, 0)
    a = act_ref[...]
    gelu = 0.5 * a * (1.0 + lax.erf(a * (2.0 ** -0.5)))
    w_ref[...] = gate_ref[...] * gelu


def _peer_u(e_t, h, gate_t, tab, tt):
    n = e_t.shape[0]
    assert tt % REDUCE_TOKENS == 0
    kern = functools.partial(_peer_u_kernel, tt=tt)
    pair = lambda **kw: pl.BlockSpec((tt, N_PAIRS), lambda i: (i, 0), **kw)
    return pl.pallas_call(
        kern,
        grid=(n // tt,),
        in_specs=[pair(memory_space=pltpu.SMEM),
                  pl.BlockSpec((tt, D_MODEL), lambda i: (i, 0)),
                  pair(),
                  pl.BlockSpec(tab.shape, lambda i: (0, 0), pipeline_mode=pl.Buffered(1))],
        out_specs=pair(),
        out_shape=jax.ShapeDtypeStruct((n, N_PAIRS), F32),
        scratch_shapes=[pltpu.VMEM((tt * CHUNKS, 128), F32),
                        pltpu.VMEM((N_PAIRS * ROWS_PER_EXPERT, 128), F32),
                        pltpu.VMEM((tt * N_PAIRS, 128), F32),
                        pltpu.VMEM((tt, N_PAIRS), F32)],
        compiler_params=_cparams(("arbitrary",)),
        name="peer_u",
    )(e_t, h, gate_t, tab)


def _peer_v_kernel(e_ref, w_ref, x_ref, g_ref, tab_ref, o_ref, acc_ref, *, tt):
    n_acc = 2

    def token(t, carry):
        acc_hi = [jnp.zeros((ROWS_PER_EXPERT, 128), F32) for _ in range(n_acc)]
        acc_lo = [jnp.zeros((ROWS_PER_EXPERT, 128), F32) for _ in range(n_acc)]
        for k in range(N_PAIRS):
            hi, lo = _unpack(_expert_rows(tab_ref, e_ref[t, k]))
            w = w_ref[t, k]
            acc_hi[k % n_acc] = acc_hi[k % n_acc] + w * hi
            acc_lo[k % n_acc] = acc_lo[k % n_acc] + w * lo
        base = pl.multiple_of(t * CHUNKS, CHUNKS)
        acc_ref[pl.ds(base, ROWS_PER_EXPERT), :] = sum(acc_hi[1:], acc_hi[0])
        acc_ref[pl.ds(pl.multiple_of(base + ROWS_PER_EXPERT, ROWS_PER_EXPERT), ROWS_PER_EXPERT), :] = sum(acc_lo[1:], acc_lo[0])
        return carry

    lax.fori_loop(0, tt, token, 0)
    xs = [x_ref[:, c * 128:(c + 1) * 128] + acc_ref[pl.ds(c, tt, stride=CHUNKS), :] for c in range(CHUNKS)]
    sq = xs[0] * xs[0]
    for x in xs[1:]:
        sq = sq + x * x
    r = lax.rsqrt(jnp.sum(sq, axis=1, keepdims=True) * (1.0 / D_MODEL) + EPS)
    for c in range(CHUNKS):
        o_ref[:, c * 128:(c + 1) * 128] = xs[c] * r * g_ref[:, c * 128:(c + 1) * 128]


def _peer_v(e_t, w_t, x, g_final, tab, tt):
    n = e_t.shape[0]
    kern = functools.partial(_peer_v_kernel, tt=tt)
    pair = lambda: pl.BlockSpec((tt, N_PAIRS), lambda i: (i, 0), memory_space=pltpu.SMEM)
    row = lambda: pl.BlockSpec((tt, D_MODEL), lambda i: (i, 0))
    return pl.pallas_call(
        kern,
        grid=(n // tt,),
        in_specs=[pair(), pair(), row(),
                  pl.BlockSpec((1, D_MODEL), lambda i: (0, 0)),
                  pl.BlockSpec(tab.shape, lambda i: (0, 0), pipeline_mode=pl.Buffered(1))],
        out_specs=row(),
        out_shape=jax.ShapeDtypeStruct((n, D_MODEL), F32),
        scratch_shapes=[pltpu.VMEM((tt * CHUNKS, 128), F32)],
        compiler_params=_cparams(("arbitrary",)),
        name="peer_v",
    )(e_t, w_t, x, g_final, tab)


def _pack_table(tab):
    bits = lax.bitcast_convert_type(tab.astype(BF16), jnp.uint16).astype(jnp.uint32)
    words = (bits[:, :HALF] << 16) | bits[:, HALF:]
    return words.reshape(tab.shape[0] * ROWS_PER_EXPERT, 128)


def _row_tile(n, pref):
    while n % pref:
        pref //= 2
    return pref


def _peer_and_final(x2, h3, qp, k1_bf, k2_bf, tab_u, tab_v, g_final, tt):
    n = x2.shape[0]
    e_t, gate_t = _route(qp, k1_bf, k2_bf, tt)
    w_t = _peer_u(e_t, h3, gate_t, tab_u, tt)
    return _peer_v(e_t, w_t, x2, g_final, tab_v, tt)


def kernel(x_prompt, x_sample, mem_prompt, cache_da_k, cache_da_v, state_ret, cache_mem_k, cache_mem_v, g_mix, w_in, lam_q1, lam_k1, lam_q2, lam_k2, g_da, g_ret, w_out, g_cross, g_mem, w_xq, w_xk, w_xv, w_xo, g_ffn, w_pq, peer_k1, peer_k2, peer_u, peer_v, g_final):
    depth = w_in.shape[0]
    assert depth == 1, "single-layer step"
    l = 0
    lam_init = 0.8 - 0.6 * math.exp(-0.3 * l)
    b, t, _ = x_prompt.shape
    bs, ts, _ = x_sample.shape
    past_len = cache_da_k.shape[2]

    row = lambda a: a.reshape(1, -1)
    w_in_bf = w_in[l].astype(BF16)
    w_out_bf = w_out[l].astype(BF16)
    w_xq_bf, w_xk_bf, w_xv_bf, w_xo_bf = (w[l].astype(BF16) for w in (w_xq, w_xk, w_xv, w_xo))
    w_pq_bf = w_pq[l].astype(BF16)
    k1_bf, k2_bf = peer_k1[l].astype(BF16), peer_k2[l].astype(BF16)
    tab_u, tab_v = _pack_table(peer_u[l]), _pack_table(peer_v[l])
    lamp = jnp.stack([lam_q1[l], lam_k1[l], lam_q2[l], lam_k2[l]])
    g_da3 = g_da[l].reshape(H_D, 1, DV_D)
    g_ret3 = g_ret[l].reshape(H_R, 1, DV_R)
    g_fin = row(g_final)

    def mixer_tail(x2d, mda, mret, mk, mv, bb, tt_rows):
        n = x2d.shape[0]
        tm = _row_tile(n, 512)
        x1, qx = _out_proj(x2d, mda.reshape(n, GROUP_W), mret.reshape(n, GROUP_W), w_out_bf, row(g_cross[l]), w_xq_bf, tm)
        x2, h3, qp = _cross(x1.reshape(bb, tt_rows, D_MODEL), qx.reshape(bb, tt_rows, D_MODEL), mk, mv,
                            w_xo_bf, row(g_ffn[l]), w_pq_bf, _row_tile(tt_rows, 512))
        y = _peer_and_final(x2.reshape(n, D_MODEL), h3.reshape(n, D_MODEL), qp.reshape(n, -1),
                            k1_bf, k2_bf, tab_u, tab_v, g_fin, 128)
        return y.reshape(bb, tt_rows, D_MODEL)

    n = b * t
    xp = x_prompt.reshape(n, D_MODEL)
    qd, kd, vd, kdb, vdb, qr, kr, vr, gr = _in_proj(xp, row(g_mix[l]), w_in_bf, _row_tile(n, 512))
    r3 = lambda a: a.reshape(b, t, GROUP_W)
    mda = _diff_attn_prompt(lamp, r3(qd), r3(kdb), r3(vdb), g_da3, lam_init, 256, 512)
    mret, s_fin = _retention(r3(qr), r3(kr), r3(vr), r3(gr), g_ret3, None, 256)
    mk, mv, mkb, mvb = _mem_kv(mem_prompt.reshape(b * N_MEM, D_MODEL), row(g_mem[l]), w_xk_bf, w_xv_bf, 512)
    y_prompt = mixer_tail(xp, mda, mret, mkb.reshape(b, N_MEM, D_MODEL), mvb.reshape(b, N_MEM, D_MODEL), b, t)

    ns = bs * ts
    xs = x_sample.reshape(ns, D_MODEL)
    qd_s, kd_s, vd_s, kdb_s, vdb_s, qr_s, kr_s, vr_s, gr_s = _in_proj(xs, row(g_mix[l]), w_in_bf, _row_tile(ns, 512))
    s3 = lambda a: a.reshape(bs, ts, GROUP_W)
    mda_s = _diff_attn_sample(lamp, s3(qd_s), cache_da_k[l].reshape(bs, past_len, GROUP_W),
                              cache_da_v[l].reshape(bs, past_len, GROUP_W), s3(kdb_s), s3(vdb_s), g_da3, lam_init)
    mret_s, s_new = _retention(s3(qr_s), s3(kr_s), s3(vr_s), s3(gr_s), g_ret3, state_ret[l], ts)
    y_sample = mixer_tail(xs, mda_s, mret_s, cache_mem_k[l].reshape(bs, N_MEM, D_MODEL),
                          cache_mem_v[l].reshape(bs, N_MEM, D_MODEL), bs, ts)

    return (y_prompt, y_sample,
            kd.reshape(1, b, t, H_D, 2, DK_D), vd.reshape(1, b, t, H_D, DV_D), s_fin[None],
            mk.reshape(1, b, N_MEM, H_X, HD_X), mv.reshape(1, b, N_MEM, H_X, HD_X),
            kd_s.reshape(1, bs, ts, H_D, 2, DK_D), vd_s.reshape(1, bs, ts, H_D, DV_D), s_new[None])
```

```python
import functools
import math

import jax
import jax.numpy as jnp
from jax import lax
from jax.experimental import pallas as pl
from jax.experimental.pallas import tpu as pltpu

D_MODEL = 1024
CHUNK = 64
CHUNK_SHIFT = CHUNK.bit_length() - 1
assert 1 << CHUNK_SHIFT == CHUNK
H_D, DK_D, DV_D = 4, 64, 128
H_R, DK_R, DV_R = 4, 128, 128
N_MEM = 256
H_X = 4
HD_X = D_MODEL // H_X
PEER_HEADS = 8
N_KEYS = 128
N_EXPERTS = N_KEYS * N_KEYS
PEER_TOPK = 16
EPS = 1e-6
HEAD_W = 128
GROUP_W = 512
N_PAIRS = PEER_HEADS * PEER_TOPK
HALF = D_MODEL // 2
ROWS_PER_EXPERT = HALF // 128
VMEM_LIMIT = 56 * 1024 * 1024

BF16 = jnp.bfloat16
F32 = jnp.float32


def _cparams(sem):
    return pltpu.CompilerParams(dimension_semantics=sem, vmem_limit_bytes=VMEM_LIMIT)


def _rms(x, g):
    return x * lax.rsqrt(jnp.mean(x * x, axis=-1, keepdims=True) + EPS) * g


def _dot(a, b):
    return jnp.dot(a, b, preferred_element_type=F32)


def _dot_nt(a, b):
    return lax.dot_general(a, b, (((1,), (1,)), ((), ())), preferred_element_type=F32)


def _dot_tn(a, b):
    return lax.dot_general(a, b, (((0,), (0,)), ((), ())), preferred_element_type=F32)


def _select_by_head(h, values):
    out = jnp.float32(values[-1])
    for i in range(len(values) - 2, -1, -1):
        out = jnp.where(h == i, jnp.float32(values[i]), out)
    return out


def _in_proj_kernel(x_ref, g_ref, w_ref, qd_ref, kd_ref, vd_ref, kdb_ref, vdb_ref,
                    qr_ref, kr_ref, vr_ref, gr_ref):
    hb = _rms(x_ref[...], g_ref[...]).astype(BF16)
    col = lambda c: _dot(hb, w_ref[:, c * GROUP_W:(c + 1) * GROUP_W])
    qd_ref[...] = (col(0) * (DK_D ** -0.5)).astype(BF16)
    kd = col(1)
    kd_ref[...] = kd
    kdb_ref[...] = kd.astype(BF16)
    vd = col(2)
    vd_ref[...] = vd
    vdb_ref[...] = vd.astype(BF16)
    qr_ref[...] = col(3).astype(BF16)
    kr_ref[...] = (col(4) * (DK_R ** -0.5)).astype(BF16)
    vr_ref[...] = col(5).astype(BF16)
    gr_ref[...] = col(6)


def _in_proj(x2d, g, w_bf, tm):
    n = x2d.shape[0]
    blk = lambda: pl.BlockSpec((tm, GROUP_W), lambda i: (i, 0))
    sh = lambda dt: jax.ShapeDtypeStruct((n, GROUP_W), dt)
    return pl.pallas_call(
        _in_proj_kernel,
        grid=(n // tm,),
        in_specs=[pl.BlockSpec((tm, D_MODEL), lambda i: (i, 0)),
                  pl.BlockSpec((1, D_MODEL), lambda i: (0, 0)),
                  pl.BlockSpec(w_bf.shape, lambda i: (0, 0))],
        out_specs=[blk() for _ in range(9)],
        out_shape=[sh(BF16), sh(F32), sh(F32), sh(BF16), sh(BF16), sh(BF16), sh(BF16), sh(BF16), sh(F32)],
        compiler_params=_cparams(("parallel",)),
        name="in_proj",
    )(x2d, g, w_bf)


def _lambda_from(lam_ref, lam_init):
    l = lam_ref[...]
    a = jnp.exp(jnp.sum(l[0:1] * l[1:2], axis=-1, keepdims=True))
    b = jnp.exp(jnp.sum(l[2:3] * l[3:4], axis=-1, keepdims=True))
    return a - b + lam_init


def _diff_post(acc, l, lam, g, lam_init, tq):
    o = acc[:tq] / l[:tq] - lam * (acc[tq:] / l[tq:])
    return o * lax.rsqrt(jnp.mean(o * o, axis=-1, keepdims=True) + EPS) * g * (1.0 - lam_init)


def _split_maps(q):
    lane = lax.broadcasted_iota(jnp.int32, q.shape, 1)
    zero = jnp.zeros_like(q)
    return jnp.concatenate([jnp.where(lane < DK_D, q, zero), jnp.where(lane >= DK_D, q, zero)], axis=0)


def _da_prompt_kernel(lam_ref, q_ref, k_ref, v_ref, g_ref, o_ref, kx_ref, vx_ref, own_ref, acc_ref, m_ref, *, lam_init, tq, tk):
    h = pl.program_id(1)
    i = pl.program_id(2)
    t = k_ref.shape[0]
    slope = _select_by_head(h, [2.0 ** (-8.0 * (j + 1) / H_D) for j in range(H_D)])

    @pl.when(i == 0)
    def _():
        pos = lax.broadcasted_iota(jnp.int32, (t, HEAD_W), 0)
        lane = lax.broadcasted_iota(jnp.int32, (t, HEAD_W), 1)
        coarse = ((pos >> CHUNK_SHIFT) << CHUNK_SHIFT).astype(F32) * slope
        fine = (pos & (CHUNK - 1)).astype(F32) * slope
        kx_ref[:, :HEAD_W] = k_ref[...]
        kx_ref[:, HEAD_W:] = jnp.where(lane == 0, coarse, jnp.where(lane == 1, fine, 0.0)).astype(BF16)
        vx_ref[:DV_D, :] = v_ref[...].astype(F32).T.astype(BF16)
        vx_ref[DV_D:, :] = jnp.ones((vx_ref.shape[0] - DV_D, t), BF16)
        krel = lax.broadcasted_iota(jnp.int32, (tk, 2 * tq), 0)
        c = lax.broadcasted_iota(jnp.int32, (tk, 2 * tq), 1)
        for par in range(tk // tq):
            qrel = par * tq + jnp.where(c >= tq, c - tq, c)
            ahead = (2.0 * slope) * jnp.maximum(krel - qrel, 0).astype(F32)
            own_ref[par] = jnp.where((qrel >> CHUNK_SHIFT) >= (krel >> CHUNK_SHIFT), -ahead, -1e30)

    q = q_ref[...]
    lane = lax.broadcasted_iota(jnp.int32, q.shape, 1)
    zero = jnp.zeros_like(q)
    ones2 = jnp.where(lane < 2, 1.0, 0.0).astype(BF16)
    q2 = jnp.concatenate([jnp.concatenate([jnp.where(lane < DK_D, q, zero), ones2], axis=1),
                          jnp.concatenate([jnp.where(lane >= DK_D, q, zero), ones2], axis=1)], axis=0)
    jd = (i * tq) // tk

    def scores(j):
        return _dot_nt(kx_ref[pl.ds(pl.multiple_of(j * tk, tk), tk), :], q2)

    def values(j):
        return vx_ref[:, pl.ds(pl.multiple_of(j * tk, tk), tk)]

    s = scores(jd) + own_ref[(i * tq) % tk // tq]
    m0 = jnp.max(s, axis=0, keepdims=True)
    m_ref[...] = m0
    acc_ref[...] = _dot(values(jd), jnp.exp(s - m0).astype(BF16))

    def absorb(blocks):
        ss = [scores(j) for j in blocks]
        m_old = m_ref[...]
        m_new = m_old
        for s in ss:
            m_new = jnp.maximum(m_new, jnp.max(s, axis=0, keepdims=True))
        m_ref[...] = m_new
        acc = jnp.exp(m_old - m_new) * acc_ref[...]
        for j, s in zip(blocks, ss):
            acc = acc + _dot(values(j), jnp.exp(s - m_new).astype(BF16))
        acc_ref[...] = acc

    def past_pair(jj, carry):
        absorb([2 * jj, 2 * jj + 1])
        return carry

    lax.fori_loop(0, jd // 2, past_pair, 0)

    @pl.when(jd % 2 == 1)
    def _():
        absorb([jd - 1])

    acc = acc_ref[...]
    num, den = acc[:DV_D], acc[DV_D:DV_D + 1]
    lam = _lambda_from(lam_ref, lam_init)
    o = (num[:, :tq] / den[:, :tq] - lam * (num[:, tq:] / den[:, tq:])).T
    o = o * lax.rsqrt(jnp.mean(o * o, axis=-1, keepdims=True) + EPS) * g_ref[...] * (1.0 - lam_init)
    o_ref[...] = o.astype(o_ref.dtype)


ONES_ROWS = 16


def _diff_attn_prompt(lamp, q, k, v, g_da3, lam_init, tq, tk):
    b, t, _ = q.shape
    kern = functools.partial(_da_prompt_kernel, lam_init=lam_init, tq=tq, tk=tk)
    return pl.pallas_call(
        kern,
        grid=(b, H_D, t // tq),
        in_specs=[pl.BlockSpec((4, DK_D), lambda b_, h, i: (0, 0)),
                  pl.BlockSpec((None, tq, HEAD_W), lambda b_, h, i: (b_, i, h)),
                  pl.BlockSpec((None, t, HEAD_W), lambda b_, h, i: (b_, 0, h)),
                  pl.BlockSpec((None, t, HEAD_W), lambda b_, h, i: (b_, 0, h)),
                  pl.BlockSpec((None, 1, HEAD_W), lambda b_, h, i: (h, 0, 0))],
        out_specs=pl.BlockSpec((None, tq, HEAD_W), lambda b_, h, i: (b_, i, h)),
        out_shape=jax.ShapeDtypeStruct((b, t, GROUP_W), BF16),
        scratch_shapes=[pltpu.VMEM((t, 2 * HEAD_W), BF16), pltpu.VMEM((DV_D + ONES_ROWS, t), BF16),
                        pltpu.VMEM((tk // tq, tk, 2 * tq), F32),
                        pltpu.VMEM((DV_D + ONES_ROWS, 2 * tq), F32), pltpu.VMEM((1, 2 * tq), F32)],
        compiler_params=_cparams(("parallel", "parallel", "arbitrary")),
        name="diff_attn_prompt",
    )(lamp, q, k, v, g_da3)


def _da_sample_kernel(lam_ref, q_ref, kc_ref, vc_ref, kn_ref, vn_ref, g_ref, o_ref, *, lam_init, ts, past_len):
    h = pl.program_id(1)
    slope = _select_by_head(h, [2.0 ** (-8.0 * (j + 1) / H_D) for j in range(H_D)])
    q2 = _split_maps(q_ref[...])
    rows = 2 * ts

    def scores(k, base, n):
        r = lax.broadcasted_iota(jnp.int32, (rows, n), 0)
        qpos = past_len + jnp.where(r >= ts, r - ts, r)
        kpos = base + lax.broadcasted_iota(jnp.int32, (rows, n), 1)
        return _dot_nt(q2, k) - slope * jnp.abs(qpos - kpos).astype(F32)

    sc = scores(kc_ref[...].astype(BF16), 0, past_len)
    sn = scores(kn_ref[...], past_len, ts)
    m = jnp.maximum(jnp.max(sc, axis=-1, keepdims=True), jnp.max(sn, axis=-1, keepdims=True))
    pc = jnp.exp(sc - m)
    pn = jnp.exp(sn - m)
    l = jnp.sum(pc, axis=-1, keepdims=True) + jnp.sum(pn, axis=-1, keepdims=True)
    acc = _dot(pc.astype(BF16), vc_ref[...].astype(BF16)) + _dot(pn.astype(BF16), vn_ref[...])
    lam = _lambda_from(lam_ref, lam_init)
    o_ref[...] = _diff_post(acc, l, lam, g_ref[...], lam_init, ts).astype(o_ref.dtype)


def _diff_attn_sample(lamp, q, kc, vc, kn, vn, g_da3, lam_init):
    b, ts, _ = q.shape
    past_len = kc.shape[1]
    kern = functools.partial(_da_sample_kernel, lam_init=lam_init, ts=ts, past_len=past_len)
    head = lambda rows: pl.BlockSpec((None, rows, HEAD_W), lambda b_, h: (b_, 0, h))
    return pl.pallas_call(
        kern,
        grid=(b, H_D),
        in_specs=[pl.BlockSpec((4, DK_D), lambda b_, h: (0, 0)),
                  head(ts), head(past_len), head(past_len), head(ts), head(ts),
                  pl.BlockSpec((None, 1, HEAD_W), lambda b_, h: (h, 0, 0))],
        out_specs=head(ts),
        out_shape=jax.ShapeDtypeStruct((b, ts, GROUP_W), BF16),
        compiler_params=_cparams(("parallel", "parallel")),
        name="diff_attn_sample",
    )(lamp, q, kc, vc, kn, vn, g_da3)


def _ret_kernel(*refs, lb, has_init):
    if has_init:
        q_ref, k_ref, v_ref, gate_ref, g_ref, s0_ref, o_ref, sfin_ref, s_ref = refs
    else:
        q_ref, k_ref, v_ref, gate_ref, g_ref, o_ref, sfin_ref, s_ref = refs
    h = pl.program_id(1)
    c = pl.program_id(2)
    lg = _select_by_head(h, [math.log1p(-(2.0 ** (-5.0 - j))) for j in range(H_R)])

    @pl.when(c == 0)
    def _():
        s_ref[...] = s0_ref[...] if has_init else jnp.zeros_like(s_ref)

    q, k, v = q_ref[...], k_ref[...], v_ref[...]
    i = lax.broadcasted_iota(jnp.int32, (lb, lb), 0)
    j = lax.broadcasted_iota(jnp.int32, (lb, lb), 1)
    d = (i - j).astype(F32)
    decay = jnp.where(d >= 0, jnp.exp(jnp.maximum(d, 0.0) * lg), 0.0)
    inner = _dot_nt(q, k) * decay
    ic = lax.broadcasted_iota(jnp.int32, (lb, 1), 0).astype(F32)
    s_old = s_ref[...]
    o = _dot(inner.astype(BF16), v) + _dot(q, s_old.astype(BF16)) * jnp.exp((ic + 1.0) * lg)
    tail = jnp.exp((lb - 1.0 - ic) * lg)
    kt = (k.astype(F32) * tail).astype(BF16)
    s_new = jnp.exp(lb * lg) * s_old + _dot_tn(kt, v)
    s_ref[...] = s_new

    @pl.when(c == pl.num_programs(2) - 1)
    def _():
        sfin_ref[...] = s_new

    oc = o - jnp.mean(o, axis=-1, keepdims=True)
    y = oc * lax.rsqrt(jnp.mean(oc * oc, axis=-1, keepdims=True) + EPS) * g_ref[...]
    gate = gate_ref[...]
    o_ref[...] = (y * (gate * jax.nn.sigmoid(gate))).astype(o_ref.dtype)


def _retention(q, k, v, gate, g_ret3, s0, lb):
    b, t, _ = q.shape
    has_init = s0 is not None
    kern = functools.partial(_ret_kernel, lb=lb, has_init=has_init)
    head = lambda: pl.BlockSpec((None, lb, HEAD_W), lambda b_, h, c: (b_, c, h))
    state = lambda: pl.BlockSpec((None, None, DK_R, DV_R), lambda b_, h, c: (b_, h, 0, 0))
    in_specs = [head(), head(), head(), head(), pl.BlockSpec((None, 1, HEAD_W), lambda b_, h, c: (h, 0, 0))]
    args = [q, k, v, gate, g_ret3]
    if has_init:
        in_specs.append(state())
        args.append(s0)
    return pl.pallas_call(
        kern,
        grid=(b, H_R, t // lb),
        in_specs=in_specs,
        out_specs=[head(), state()],
        out_shape=[jax.ShapeDtypeStruct((b, t, GROUP_W), BF16), jax.ShapeDtypeStruct((b, H_R, DK_R, DV_R), F32)],
        scratch_shapes=[pltpu.VMEM((DK_R, DV_R), F32)],
        compiler_params=_cparams(("parallel", "parallel", "arbitrary")),
        name="retention",
    )(*args)


def _out_proj_kernel(x_ref, mda_ref, mret_ref, wo_ref, g_ref, wq_ref, x1_ref, qx_ref):
    x1 = x_ref[...] + _dot(mda_ref[...], wo_ref[:GROUP_W, :]) + _dot(mret_ref[...], wo_ref[GROUP_W:, :])
    x1_ref[...] = x1
    hn = _rms(x1, g_ref[...]).astype(BF16)
    qx_ref[...] = (_dot(hn, wq_ref[...]) * (HD_X ** -0.5)).astype(BF16)


def _out_proj(x2d, mda, mret, wo_bf, g_cross, wq_bf, tm):
    n = x2d.shape[0]
    full = lambda a: pl.BlockSpec(a.shape, lambda i: (0, 0))
    return pl.pallas_call(
        _out_proj_kernel,
        grid=(n // tm,),
        in_specs=[pl.BlockSpec((tm, D_MODEL), lambda i: (i, 0)),
                  pl.BlockSpec((tm, GROUP_W), lambda i: (i, 0)),
                  pl.BlockSpec((tm, GROUP_W), lambda i: (i, 0)),
                  full(wo_bf), full(g_cross), full(wq_bf)],
        out_specs=[pl.BlockSpec((tm, D_MODEL), lambda i: (i, 0)), pl.BlockSpec((tm, D_MODEL), lambda i: (i, 0))],
        out_shape=[jax.ShapeDtypeStruct((n, D_MODEL), F32), jax.ShapeDtypeStruct((n, D_MODEL), BF16)],
        compiler_params=_cparams(("parallel",)),
        name="out_proj",
    )(x2d, mda, mret, wo_bf, g_cross, wq_bf)


def _mem_kv_kernel(m_ref, g_ref, wk_ref, wv_ref, mk_ref, mv_ref, mkb_ref, mvb_ref):
    mn = _rms(m_ref[...], g_ref[...]).astype(BF16)
    mk = _dot(mn, wk_ref[...])
    mv = _dot(mn, wv_ref[...])
    mk_ref[...] = mk
    mv_ref[...] = mv
    mkb_ref[...] = mk.astype(BF16)
    mvb_ref[...] = mv.astype(BF16)


def _mem_kv(mem2d, g_mem, wk_bf, wv_bf, tm):
    n = mem2d.shape[0]
    row = lambda: pl.BlockSpec((tm, D_MODEL), lambda i: (i, 0))
    full = lambda a: pl.BlockSpec(a.shape, lambda i: (0, 0))
    sh = lambda dt: jax.ShapeDtypeStruct((n, D_MODEL), dt)
    return pl.pallas_call(
        _mem_kv_kernel,
        grid=(n // tm,),
        in_specs=[row(), full(g_mem), full(wk_bf), full(wv_bf)],
        out_specs=[row(), row(), row(), row()],
        out_shape=[sh(F32), sh(F32), sh(BF16), sh(BF16)],
        compiler_params=_cparams(("parallel",)),
        name="mem_kv",
    )(mem2d, g_mem, wk_bf, wv_bf)


def _cross_kernel(x1_ref, qx_ref, mk_ref, mv_ref, wo_ref, g_ref, wpq_ref, x2_ref, h3_ref, qp_ref):
    q = qx_ref[...]
    heads = []
    for h in range(H_X):
        sl = slice(h * HD_X, (h + 1) * HD_X)
        kh = mk_ref[:, sl] if mk_ref.dtype == BF16 else mk_ref[:, sl].astype(BF16)
        vh = mv_ref[:, sl] if mv_ref.dtype == BF16 else mv_ref[:, sl].astype(BF16)
        s = _dot_nt(q[:, sl], kh)
        p = jnp.exp(s - jnp.max(s, axis=-1, keepdims=True))
        heads.append(_dot(p.astype(BF16), vh) / jnp.sum(p, axis=-1, keepdims=True))
    o = jnp.concatenate(heads, axis=-1).astype(BF16)
    x2 = x1_ref[...] + _dot(o, wo_ref[...])
    x2_ref[...] = x2
    h3 = _rms(x2, g_ref[...])
    h3_ref[...] = h3
    qp_ref[...] = _dot(h3.astype(BF16), wpq_ref[...]).astype(BF16)


def _cross(x1, qx, mk, mv, wo_bf, g_ffn, wpq_bf, tm):
    b, t, _ = x1.shape
    row = lambda w: pl.BlockSpec((None, tm, w), lambda b_, i: (b_, i, 0))
    mem = lambda: pl.BlockSpec((None, N_MEM, D_MODEL), lambda b_, i: (b_, 0, 0))
    full = lambda a: pl.BlockSpec(a.shape, lambda b_, i: (0, 0))
    dq = wpq_bf.shape[1]
    return pl.pallas_call(
        _cross_kernel,
        grid=(b, t // tm),
        in_specs=[row(D_MODEL), row(D_MODEL), mem(), mem(), full(wo_bf), full(g_ffn), full(wpq_bf)],
        out_specs=[row(D_MODEL), row(D_MODEL), row(dq)],
        out_shape=[jax.ShapeDtypeStruct((b, t, D_MODEL), F32), jax.ShapeDtypeStruct((b, t, D_MODEL), F32),
                   jax.ShapeDtypeStruct((b, t, dq), BF16)],
        compiler_params=_cparams(("parallel", "parallel")),
        name="cross_attn",
    )(x1, qx, mk, mv, wo_bf, g_ffn, wpq_bf)


ID_PAD = 2.0 ** 29


def _topk_rows(s, ids):
    vals, sel = [], []
    for _ in range(PEER_TOPK):
        m = jnp.max(s, axis=0, keepdims=True)
        idx = jnp.min(jnp.where(s == m, ids, ID_PAD), axis=0, keepdims=True)
        vals.append(m)
        sel.append(idx)
        s = jnp.where(ids == idx, -jnp.inf, s)
    return jnp.concatenate(vals, axis=0), jnp.concatenate(sel, axis=0)


def _candidates(v1, i1, v2, i2):
    lanes = v1.shape[1]
    b8 = lax.broadcasted_iota(jnp.int32, (8, lanes), 0)
    ident = lambda a, ia, ib, b: (a * PEER_TOPK + b) * float(N_EXPERTS) + (ia * float(N_KEYS) + ib)
    b16 = lax.broadcasted_iota(jnp.int32, (PEER_TOPK, lanes), 0).astype(F32)
    vals = [v1[0:1] + v2]
    ids = [ident(0, i1[0:1], i2, b16)]
    for a in range(1, 8):
        keep = b8 < PEER_TOPK // (a + 1)
        vals.append(jnp.where(keep, v1[a:a + 1] + v2[0:8], -jnp.inf))
        ids.append(jnp.where(keep, ident(a, i1[a:a + 1], i2[0:8], b8.astype(F32)), ID_PAD))
    a_hi = (b8 + 8).astype(F32)
    vals.append(v1[8:16] + v2[0:1])
    ids.append(ident(a_hi, i1[8:16], i2[0:1], 0.0))
    return jnp.concatenate(vals, axis=0), jnp.concatenate(ids, axis=0)


def _route_kernel(qp_ref, k1_ref, k2_ref, e_ref, g_ref):
    half = N_KEYS
    tt = qp_ref.shape[0]
    key_id = lax.broadcasted_iota(jnp.int32, (N_KEYS, tt), 0).astype(F32)
    es, gs = [], []
    for p in range(PEER_HEADS):
        q1 = qp_ref[:, (2 * p) * half:(2 * p + 1) * half]
        q2 = qp_ref[:, (2 * p + 1) * half:(2 * p + 2) * half]
        v1, i1 = _topk_rows(_dot_nt(k1_ref[p], q1), key_id)
        v2, i2 = _topk_rows(_dot_nt(k2_ref[p], q2), key_id)
        sc, sel = _topk_rows(*_candidates(v1, i1, v2, i2))
        w = jnp.exp(sc - sc[0:1])
        es.append((sel.astype(jnp.int32) & (N_EXPERTS - 1)) * ROWS_PER_EXPERT)
        gs.append(w / jnp.sum(w, axis=0, keepdims=True))
    e_ref[...] = jnp.concatenate(es, axis=0).T
    g_ref[...] = jnp.concatenate(gs, axis=0).T


def _route(qp2d, k1_bf, k2_bf, tt):
    n = qp2d.shape[0]
    full = lambda a: pl.BlockSpec(a.shape, lambda i: (0, 0, 0))
    return pl.pallas_call(
        _route_kernel,
        grid=(n // tt,),
        in_specs=[pl.BlockSpec((tt, qp2d.shape[1]), lambda i: (i, 0)), full(k1_bf), full(k2_bf)],
        out_specs=[pl.BlockSpec((tt, N_PAIRS), lambda i: (i, 0)), pl.BlockSpec((tt, N_PAIRS), lambda i: (i, 0))],
        out_shape=[jax.ShapeDtypeStruct((n, N_PAIRS), jnp.int32), jax.ShapeDtypeStruct((n, N_PAIRS), F32)],
        compiler_params=_cparams(("parallel",)),
        name="peer_route",
    )(qp2d, k1_bf, k2_bf)


def _unpack(words):
    hi = lax.bitcast_convert_type(words & jnp.uint32(0xFFFF0000), F32)
    lo = lax.bitcast_convert_type(words << 16, F32)
    return hi, lo


def _expert_rows(tab_ref, row0):
    return tab_ref[pl.ds(pl.multiple_of(row0, ROWS_PER_EXPERT), ROWS_PER_EXPERT), :]


REDUCE_TOKENS = 16


def _lane_sums_to_rows(y, n_tok):
    hi = y.astype(BF16)
    lo = (y - hi.astype(F32)).astype(BF16)
    ones = jnp.ones((128, N_PAIRS), BF16)
    s = (_dot(hi, ones) + _dot(lo, ones)).reshape(n_tok, N_PAIRS, N_PAIRS)
    eye = lax.broadcasted_iota(jnp.int32, (N_PAIRS, N_PAIRS), 0) == lax.broadcasted_iota(jnp.int32, (N_PAIRS, N_PAIRS), 1)
    return jnp.sum(jnp.where(eye[None], s, 0.0), axis=1)


CHUNKS = D_MODEL // 128


def _peer_u_kernel(e_ref, h_ref, gate_ref, tab_ref, w_ref, h8_ref, prod_ref, ys_ref, act_ref, *, tt):
    for c in range(CHUNKS):
        h8_ref[pl.ds(c, tt, stride=CHUNKS), :] = h_ref[:, c * 128:(c + 1) * 128]

    def token(t, slot):
        base = pl.multiple_of(t * CHUNKS, CHUNKS)
        ha = h8_ref[pl.ds(base, ROWS_PER_EXPERT), :]
        hb = h8_ref[pl.ds(pl.multiple_of(base + ROWS_PER_EXPERT, ROWS_PER_EXPERT), ROWS_PER_EXPERT), :]
        prod = prod_ref.at[slot]
        for k in range(N_PAIRS):
            hi, lo = _unpack(_expert_rows(tab_ref, e_ref[t, k]))
            prod[k * ROWS_PER_EXPERT:(k + 1) * ROWS_PER_EXPERT, :] = hi * ha + lo * hb
        y = prod[pl.ds(0, N_PAIRS, stride=ROWS_PER_EXPERT), :]
        for c in range(1, ROWS_PER_EXPERT):
            y = y + prod[pl.ds(c, N_PAIRS, stride=ROWS_PER_EXPERT), :]
        ys_ref[pl.ds(pl.multiple_of(t * N_PAIRS, N_PAIRS), N_PAIRS), :] = y

    def two_tokens(j, carry):
        token(2 * j, 0)
        token(2 * j + 1, 1)
        return carry

    lax.fori_loop(0, tt // 2, two_tokens, 0)

    def group(g, carry):
        rows = REDUCE_TOKENS * N_PAIRS
        y = ys_ref[pl.ds(pl.multiple_of(g * rows, rows), rows), :]
        act_ref[pl.ds(pl.multiple_of(g * REDUCE_TOKENS, REDUCE_TOKENS), REDUCE_TOKENS), :] = _lane_sums_to_rows(y, REDUCE_TOKENS)
        return carry

    lax.fori_loop(0, tt // REDUCE_TOKENS, group, 0)
    a = act_ref[...]
    gelu = 0.5 * a * (1.0 + lax.erf(a * (2.0 ** -0.5)))
    w_ref[...] = gate_ref[...] * gelu


def _peer_u(e_t, h, gate_t, tab, tt):
    n = e_t.shape[0]
    assert tt % REDUCE_TOKENS == 0
    kern = functools.partial(_peer_u_kernel, tt=tt)
    pair = lambda **kw: pl.BlockSpec((tt, N_PAIRS), lambda i: (i, 0), **kw)
    return pl.pallas_call(
        kern,
        grid=(n // tt,),
        in_specs=[pair(memory_space=pltpu.SMEM),
                  pl.BlockSpec((tt, D_MODEL), lambda i: (i, 0)),
                  pair(),
                  pl.BlockSpec(tab.shape, lambda i: (0, 0), pipeline_mode=pl.Buffered(1))],
        out_specs=pair(),
        out_shape=jax.ShapeDtypeStruct((n, N_PAIRS), F32),
        scratch_shapes=[pltpu.VMEM((tt * CHUNKS, 128), F32),
                        pltpu.VMEM((2, N_PAIRS * ROWS_PER_EXPERT, 128), F32),
                        pltpu.VMEM((tt * N_PAIRS, 128), F32),
                        pltpu.VMEM((tt, N_PAIRS), F32)],
        compiler_params=_cparams(("arbitrary",)),
        name="peer_u",
    )(e_t, h, gate_t, tab)


def _peer_v_kernel(e_ref, w_ref, x_ref, g_ref, tab_ref, o_ref, acc_ref, *, tt):
    n_acc = 2

    def token(t, carry):
        acc_hi = [jnp.zeros((ROWS_PER_EXPERT, 128), F32) for _ in range(n_acc)]
        acc_lo = [jnp.zeros((ROWS_PER_EXPERT, 128), F32) for _ in range(n_acc)]
        for k in range(N_PAIRS):
            hi, lo = _unpack(_expert_rows(tab_ref, e_ref[t, k]))
            w = w_ref[t, k]
            acc_hi[k % n_acc] = acc_hi[k % n_acc] + w * hi
            acc_lo[k % n_acc] = acc_lo[k % n_acc] + w * lo
        base = pl.multiple_of(t * CHUNKS, CHUNKS)
        acc_ref[pl.ds(base, ROWS_PER_EXPERT), :] = sum(acc_hi[1:], acc_hi[0])
        acc_ref[pl.ds(pl.multiple_of(base + ROWS_PER_EXPERT, ROWS_PER_EXPERT), ROWS_PER_EXPERT), :] = sum(acc_lo[1:], acc_lo[0])
        return carry

    lax.fori_loop(0, tt, token, 0)
    xs = [x_ref[:, c * 128:(c + 1) * 128] + acc_ref[pl.ds(c, tt, stride=CHUNKS), :] for c in range(CHUNKS)]
    sq = xs[0] * xs[0]
    for x in xs[1:]:
        sq = sq + x * x
    r = lax.rsqrt(jnp.sum(sq, axis=1, keepdims=True) * (1.0 / D_MODEL) + EPS)
    for c in range(CHUNKS):
        o_ref[:, c * 128:(c + 1) * 128] = xs[c] * r * g_ref[:, c * 128:(c + 1) * 128]


def _peer_v(e_t, w_t, x, g_final, tab, tt):
    n = e_t.shape[0]
    kern = functools.partial(_peer_v_kernel, tt=tt)
    pair = lambda: pl.BlockSpec((tt, N_PAIRS), lambda i: (i, 0), memory_space=pltpu.SMEM)
    row = lambda: pl.BlockSpec((tt, D_MODEL), lambda i: (i, 0))
    return pl.pallas_call(
        kern,
        grid=(n // tt,),
        in_specs=[pair(), pair(), row(),
                  pl.BlockSpec((1, D_MODEL), lambda i: (0, 0)),
                  pl.BlockSpec(tab.shape, lambda i: (0, 0), pipeline_mode=pl.Buffered(1))],
        out_specs=row(),
        out_shape=jax.ShapeDtypeStruct((n, D_MODEL), F32),
        scratch_shapes=[pltpu.VMEM((tt * CHUNKS, 128), F32)],
        compiler_params=_cparams(("arbitrary",)),
        name="peer_v",
    )(e_t, w_t, x, g_final, tab)


def _pack_table(tab):
    bits = lax.bitcast_convert_type(tab.astype(BF16), jnp.uint16).astype(jnp.uint32)
    words = (bits[:, :HALF] << 16) | bits[:, HALF:]
    return words.reshape(tab.shape[0] * ROWS_PER_EXPERT, 128)


def _row_tile(n, pref):
    while n % pref:
        pref //= 2
    return pref


def _peer_and_final(x2, h3, qp, k1_bf, k2_bf, tab_u, tab_v, g_final, tt):
    n = x2.shape[0]
    e_t, gate_t = _route(qp, k1_bf, k2_bf, tt)
    w_t = _peer_u(e_t, h3, gate_t, tab_u, tt)
    return _peer_v(e_t, w_t, x2, g_final, tab_v, tt)


def kernel(x_prompt, x_sample, mem_prompt, cache_da_k, cache_da_v, state_ret, cache_mem_k, cache_mem_v, g_mix, w_in, lam_q1, lam_k1, lam_q2, lam_k2, g_da, g_ret, w_out, g_cross, g_mem, w_xq, w_xk, w_xv, w_xo, g_ffn, w_pq, peer_k1, peer_k2, peer_u, peer_v, g_final):
    depth = w_in.shape[0]
    assert depth == 1, "single-layer step"
    l = 0
    lam_init = 0.8 - 0.6 * math.exp(-0.3 * l)
    b, t, _ = x_prompt.shape
    bs, ts, _ = x_sample.shape
    past_len = cache_da_k.shape[2]

    row = lambda a: a.reshape(1, -1)
    w_in_bf = w_in[l].astype(BF16)
    w_out_bf = w_out[l].astype(BF16)
    w_xq_bf, w_xk_bf, w_xv_bf, w_xo_bf = (w[l].astype(BF16) for w in (w_xq, w_xk, w_xv, w_xo))
    w_pq_bf = w_pq[l].astype(BF16)
    k1_bf, k2_bf = peer_k1[l].astype(BF16), peer_k2[l].astype(BF16)
    tab_u, tab_v = _pack_table(peer_u[l]), _pack_table(peer_v[l])
    lamp = jnp.stack([lam_q1[l], lam_k1[l], lam_q2[l], lam_k2[l]])
    g_da3 = g_da[l].reshape(H_D, 1, DV_D)
    g_ret3 = g_ret[l].reshape(H_R, 1, DV_R)
    g_fin = row(g_final)

    def mixer_tail(x2d, mda, mret, mk, mv, bb, tt_rows):
        n = x2d.shape[0]
        tm = _row_tile(n, 512)
        x1, qx = _out_proj(x2d, mda.reshape(n, GROUP_W), mret.reshape(n, GROUP_W), w_out_bf, row(g_cross[l]), w_xq_bf, tm)
        x2, h3, qp = _cross(x1.reshape(bb, tt_rows, D_MODEL), qx.reshape(bb, tt_rows, D_MODEL), mk, mv,
                            w_xo_bf, row(g_ffn[l]), w_pq_bf, _row_tile(tt_rows, 512))
        y = _peer_and_final(x2.reshape(n, D_MODEL), h3.reshape(n, D_MODEL), qp.reshape(n, -1),
                            k1_bf, k2_bf, tab_u, tab_v, g_fin, 128)
        return y.reshape(bb, tt_rows, D_MODEL)

    n = b * t
    xp = x_prompt.reshape(n, D_MODEL)
    qd, kd, vd, kdb, vdb, qr, kr, vr, gr = _in_proj(xp, row(g_mix[l]), w_in_bf, _row_tile(n, 512))
    r3 = lambda a: a.reshape(b, t, GROUP_W)
    mda = _diff_attn_prompt(lamp, r3(qd), r3(kdb), r3(vdb), g_da3, lam_init, 512, 512)
    mret, s_fin = _retention(r3(qr), r3(kr), r3(vr), r3(gr), g_ret3, None, 256)
    mk, mv, mkb, mvb = _mem_kv(mem_prompt.reshape(b * N_MEM, D_MODEL), row(g_mem[l]), w_xk_bf, w_xv_bf, 512)
    y_prompt = mixer_tail(xp, mda, mret, mkb.reshape(b, N_MEM, D_MODEL), mvb.reshape(b, N_MEM, D_MODEL), b, t)

    ns = bs * ts
    xs = x_sample.reshape(ns, D_MODEL)
    qd_s, kd_s, vd_s, kdb_s, vdb_s, qr_s, kr_s, vr_s, gr_s = _in_proj(xs, row(g_mix[l]), w_in_bf, _row_tile(ns, 512))
    s3 = lambda a: a.reshape(bs, ts, GROUP_W)
    mda_s = _diff_attn_sample(lamp, s3(qd_s), cache_da_k[l].reshape(bs, past_len, GROUP_W),
                              cache_da_v[l].reshape(bs, past_len, GROUP_W), s3(kdb_s), s3(vdb_s), g_da3, lam_init)
    mret_s, s_new = _retention(s3(qr_s), s3(kr_s), s3(vr_s), s3(gr_s), g_ret3, state_ret[l], ts)
    y_sample = mixer_tail(xs, mda_s, mret_s, cache_mem_k[l].reshape(bs, N_MEM, D_MODEL),
                          cache_mem_v[l].reshape(bs, N_MEM, D_MODEL), bs, ts)

    return (y_prompt, y_sample,
            kd.reshape(1, b, t, H_D, 2, DK_D), vd.reshape(1, b, t, H_D, DV_D), s_fin[None],
            mk.reshape(1, b, N_MEM, H_X, HD_X), mv.reshape(1, b, N_MEM, H_X, HD_X),
            kd_s.reshape(1, bs, ts, H_D, 2, DK_D), vd_s.reshape(1, bs, ts, H_D, DV_D), s_new[None])
```

```python
import functools
import math

import jax
import jax.numpy as jnp
from jax import lax
from jax.experimental import pallas as pl
from jax.experimental.pallas import tpu as pltpu
from jax.experimental.pallas import tpu_sc as plsc

D_MODEL = 1024
CHUNK = 64
CHUNK_SHIFT = CHUNK.bit_length() - 1
assert 1 << CHUNK_SHIFT == CHUNK
H_D, DK_D, DV_D = 4, 64, 128
H_R, DK_R, DV_R = 4, 128, 128
N_MEM = 256
H_X = 4
HD_X = D_MODEL // H_X
PEER_HEADS = 8
N_KEYS = 128
N_EXPERTS = N_KEYS * N_KEYS
PEER_TOPK = 16
EPS = 1e-6
HEAD_W = 128
GROUP_W = 512
N_PAIRS = PEER_HEADS * PEER_TOPK
HALF = D_MODEL // 2
ROWS_PER_EXPERT = HALF // 128
VMEM_LIMIT = 56 * 1024 * 1024

BF16 = jnp.bfloat16
F32 = jnp.float32


def _cparams(sem):
    return pltpu.CompilerParams(dimension_semantics=sem, vmem_limit_bytes=VMEM_LIMIT)


def _rms(x, g):
    return x * lax.rsqrt(jnp.mean(x * x, axis=-1, keepdims=True) + EPS) * g


def _dot(a, b):
    return jnp.dot(a, b, preferred_element_type=F32)


def _dot_nt(a, b):
    return lax.dot_general(a, b, (((1,), (1,)), ((), ())), preferred_element_type=F32)


def _dot_tn(a, b):
    return lax.dot_general(a, b, (((0,), (0,)), ((), ())), preferred_element_type=F32)


def _select_by_head(h, values):
    out = jnp.float32(values[-1])
    for i in range(len(values) - 2, -1, -1):
        out = jnp.where(h == i, jnp.float32(values[i]), out)
    return out


def _in_proj_kernel(x_ref, g_ref, w_ref, qd_ref, kd_ref, vd_ref, kdb_ref, vdb_ref,
                    qr_ref, kr_ref, vr_ref, gr_ref):
    hb = _rms(x_ref[...], g_ref[...]).astype(BF16)
    col = lambda c: _dot(hb, w_ref[:, c * GROUP_W:(c + 1) * GROUP_W])
    qd_ref[...] = (col(0) * (DK_D ** -0.5)).astype(BF16)
    kd = col(1)
    kd_ref[...] = kd
    kdb_ref[...] = kd.astype(BF16)
    vd = col(2)
    vd_ref[...] = vd
    vdb_ref[...] = vd.astype(BF16)
    qr_ref[...] = col(3).astype(BF16)
    kr_ref[...] = (col(4) * (DK_R ** -0.5)).astype(BF16)
    vr_ref[...] = col(5).astype(BF16)
    gr_ref[...] = col(6)


def _in_proj(x2d, g, w_bf, tm):
    n = x2d.shape[0]
    blk = lambda: pl.BlockSpec((tm, GROUP_W), lambda i: (i, 0))
    sh = lambda dt: jax.ShapeDtypeStruct((n, GROUP_W), dt)
    return pl.pallas_call(
        _in_proj_kernel,
        grid=(n // tm,),
        in_specs=[pl.BlockSpec((tm, D_MODEL), lambda i: (i, 0)),
                  pl.BlockSpec((1, D_MODEL), lambda i: (0, 0)),
                  pl.BlockSpec(w_bf.shape, lambda i: (0, 0))],
        out_specs=[blk() for _ in range(9)],
        out_shape=[sh(BF16), sh(F32), sh(F32), sh(BF16), sh(BF16), sh(BF16), sh(BF16), sh(BF16), sh(F32)],
        compiler_params=_cparams(("parallel",)),
        name="in_proj",
    )(x2d, g, w_bf)


def _lambda_from(lam_ref, lam_init):
    l = lam_ref[...]
    a = jnp.exp(jnp.sum(l[0:1] * l[1:2], axis=-1, keepdims=True))
    b = jnp.exp(jnp.sum(l[2:3] * l[3:4], axis=-1, keepdims=True))
    return a - b + lam_init


def _diff_post(acc, l, lam, g, lam_init, tq):
    o = acc[:tq] / l[:tq] - lam * (acc[tq:] / l[tq:])
    return o * lax.rsqrt(jnp.mean(o * o, axis=-1, keepdims=True) + EPS) * g * (1.0 - lam_init)


def _split_maps(q):
    lane = lax.broadcasted_iota(jnp.int32, q.shape, 1)
    zero = jnp.zeros_like(q)
    return jnp.concatenate([jnp.where(lane < DK_D, q, zero), jnp.where(lane >= DK_D, q, zero)], axis=0)


def _da_prompt_kernel(lam_ref, q_ref, k_ref, v_ref, g_ref, o_ref, kx_ref, vx_ref, own_ref, acc_ref, m_ref, *, lam_init, tq, tk):
    h = pl.program_id(1)
    i = pl.program_id(2)
    t = k_ref.shape[0]
    slope = _select_by_head(h, [2.0 ** (-8.0 * (j + 1) / H_D) for j in range(H_D)])

    @pl.when(i == 0)
    def _():
        pos = lax.broadcasted_iota(jnp.int32, (t, HEAD_W), 0)
        lane = lax.broadcasted_iota(jnp.int32, (t, HEAD_W), 1)
        coarse = ((pos >> CHUNK_SHIFT) << CHUNK_SHIFT).astype(F32) * slope
        fine = (pos & (CHUNK - 1)).astype(F32) * slope
        kx_ref[:, :HEAD_W] = k_ref[...]
        kx_ref[:, HEAD_W:] = jnp.where(lane == 0, coarse, jnp.where(lane == 1, fine, 0.0)).astype(BF16)
        vx_ref[:DV_D, :] = v_ref[...].astype(F32).T.astype(BF16)
        vx_ref[DV_D:, :] = jnp.ones((vx_ref.shape[0] - DV_D, t), BF16)
        krel = lax.broadcasted_iota(jnp.int32, (tk, 2 * tq), 0)
        c = lax.broadcasted_iota(jnp.int32, (tk, 2 * tq), 1)
        for par in range(tk // tq):
            qrel = par * tq + jnp.where(c >= tq, c - tq, c)
            ahead = (2.0 * slope) * jnp.maximum(krel - qrel, 0).astype(F32)
            own_ref[par] = jnp.where((qrel >> CHUNK_SHIFT) >= (krel >> CHUNK_SHIFT), -ahead, -1e30)

    q = q_ref[...]
    lane = lax.broadcasted_iota(jnp.int32, q.shape, 1)
    zero = jnp.zeros_like(q)
    ones2 = jnp.where(lane < 2, 1.0, 0.0).astype(BF16)
    q2 = jnp.concatenate([jnp.concatenate([jnp.where(lane < DK_D, q, zero), ones2], axis=1),
                          jnp.concatenate([jnp.where(lane >= DK_D, q, zero), ones2], axis=1)], axis=0)
    jd = (i * tq) // tk

    def scores(j):
        return _dot_nt(kx_ref[pl.ds(pl.multiple_of(j * tk, tk), tk), :], q2)

    def values(j):
        return vx_ref[:, pl.ds(pl.multiple_of(j * tk, tk), tk)]

    s = scores(jd) + own_ref[(i * tq) % tk // tq]
    m0 = jnp.max(s, axis=0, keepdims=True)
    m_ref[...] = m0
    acc_ref[...] = _dot(values(jd), jnp.exp(s - m0).astype(BF16))

    def absorb(blocks):
        ss = [scores(j) for j in blocks]
        m_old = m_ref[...]
        m_new = m_old
        for s in ss:
            m_new = jnp.maximum(m_new, jnp.max(s, axis=0, keepdims=True))
        m_ref[...] = m_new
        acc = jnp.exp(m_old - m_new) * acc_ref[...]
        for j, s in zip(blocks, ss):
            acc = acc + _dot(values(j), jnp.exp(s - m_new).astype(BF16))
        acc_ref[...] = acc

    def past_pair(jj, carry):
        absorb([2 * jj, 2 * jj + 1])
        return carry

    lax.fori_loop(0, jd // 2, past_pair, 0)

    @pl.when(jd % 2 == 1)
    def _():
        absorb([jd - 1])

    acc = acc_ref[...]
    num, den = acc[:DV_D], acc[DV_D:DV_D + 1]
    lam = _lambda_from(lam_ref, lam_init)
    o = (num[:, :tq] / den[:, :tq] - lam * (num[:, tq:] / den[:, tq:])).T
    o = o * lax.rsqrt(jnp.mean(o * o, axis=-1, keepdims=True) + EPS) * g_ref[...] * (1.0 - lam_init)
    o_ref[...] = o.astype(o_ref.dtype)


ONES_ROWS = 16


def _diff_attn_prompt(lamp, q, k, v, g_da3, lam_init, tq, tk):
    b, t, _ = q.shape
    kern = functools.partial(_da_prompt_kernel, lam_init=lam_init, tq=tq, tk=tk)
    return pl.pallas_call(
        kern,
        grid=(b, H_D, t // tq),
        in_specs=[pl.BlockSpec((4, DK_D), lambda b_, h, i: (0, 0)),
                  pl.BlockSpec((None, tq, HEAD_W), lambda b_, h, i: (b_, i, h)),
                  pl.BlockSpec((None, t, HEAD_W), lambda b_, h, i: (b_, 0, h)),
                  pl.BlockSpec((None, t, HEAD_W), lambda b_, h, i: (b_, 0, h)),
                  pl.BlockSpec((None, 1, HEAD_W), lambda b_, h, i: (h, 0, 0))],
        out_specs=pl.BlockSpec((None, tq, HEAD_W), lambda b_, h, i: (b_, i, h)),
        out_shape=jax.ShapeDtypeStruct((b, t, GROUP_W), BF16),
        scratch_shapes=[pltpu.VMEM((t, 2 * HEAD_W), BF16), pltpu.VMEM((DV_D + ONES_ROWS, t), BF16),
                        pltpu.VMEM((tk // tq, tk, 2 * tq), F32),
                        pltpu.VMEM((DV_D + ONES_ROWS, 2 * tq), F32), pltpu.VMEM((1, 2 * tq), F32)],
        compiler_params=_cparams(("parallel", "parallel", "arbitrary")),
        name="diff_attn_prompt",
    )(lamp, q, k, v, g_da3)


def _da_sample_kernel(lam_ref, q_ref, kc_ref, vc_ref, kn_ref, vn_ref, g_ref, o_ref, *, lam_init, ts, past_len):
    h = pl.program_id(1)
    slope = _select_by_head(h, [2.0 ** (-8.0 * (j + 1) / H_D) for j in range(H_D)])
    q2 = _split_maps(q_ref[...])
    rows = 2 * ts

    def scores(k, base, n):
        r = lax.broadcasted_iota(jnp.int32, (rows, n), 0)
        qpos = past_len + jnp.where(r >= ts, r - ts, r)
        kpos = base + lax.broadcasted_iota(jnp.int32, (rows, n), 1)
        return _dot_nt(q2, k) - slope * jnp.abs(qpos - kpos).astype(F32)

    sc = scores(kc_ref[...].astype(BF16), 0, past_len)
    sn = scores(kn_ref[...], past_len, ts)
    m = jnp.maximum(jnp.max(sc, axis=-1, keepdims=True), jnp.max(sn, axis=-1, keepdims=True))
    pc = jnp.exp(sc - m)
    pn = jnp.exp(sn - m)
    l = jnp.sum(pc, axis=-1, keepdims=True) + jnp.sum(pn, axis=-1, keepdims=True)
    acc = _dot(pc.astype(BF16), vc_ref[...].astype(BF16)) + _dot(pn.astype(BF16), vn_ref[...])
    lam = _lambda_from(lam_ref, lam_init)
    o_ref[...] = _diff_post(acc, l, lam, g_ref[...], lam_init, ts).astype(o_ref.dtype)


def _diff_attn_sample(lamp, q, kc, vc, kn, vn, g_da3, lam_init):
    b, ts, _ = q.shape
    past_len = kc.shape[1]
    kern = functools.partial(_da_sample_kernel, lam_init=lam_init, ts=ts, past_len=past_len)
    head = lambda rows: pl.BlockSpec((None, rows, HEAD_W), lambda b_, h: (b_, 0, h))
    return pl.pallas_call(
        kern,
        grid=(b, H_D),
        in_specs=[pl.BlockSpec((4, DK_D), lambda b_, h: (0, 0)),
                  head(ts), head(past_len), head(past_len), head(ts), head(ts),
                  pl.BlockSpec((None, 1, HEAD_W), lambda b_, h: (h, 0, 0))],
        out_specs=head(ts),
        out_shape=jax.ShapeDtypeStruct((b, ts, GROUP_W), BF16),
        compiler_params=_cparams(("parallel", "parallel")),
        name="diff_attn_sample",
    )(lamp, q, kc, vc, kn, vn, g_da3)


def _ret_kernel(*refs, lb, has_init):
    if has_init:
        q_ref, k_ref, v_ref, gate_ref, g_ref, s0_ref, o_ref, sfin_ref, s_ref = refs
    else:
        q_ref, k_ref, v_ref, gate_ref, g_ref, o_ref, sfin_ref, s_ref = refs
    h = pl.program_id(1)
    c = pl.program_id(2)
    lg = _select_by_head(h, [math.log1p(-(2.0 ** (-5.0 - j))) for j in range(H_R)])

    @pl.when(c == 0)
    def _():
        s_ref[...] = s0_ref[...] if has_init else jnp.zeros_like(s_ref)

    q, k, v = q_ref[...], k_ref[...], v_ref[...]
    i = lax.broadcasted_iota(jnp.int32, (lb, lb), 0)
    j = lax.broadcasted_iota(jnp.int32, (lb, lb), 1)
    d = (i - j).astype(F32)
    decay = jnp.where(d >= 0, jnp.exp(jnp.maximum(d, 0.0) * lg), 0.0)
    inner = _dot_nt(q, k) * decay
    ic = lax.broadcasted_iota(jnp.int32, (lb, 1), 0).astype(F32)
    s_old = s_ref[...]
    o = _dot(inner.astype(BF16), v) + _dot(q, s_old.astype(BF16)) * jnp.exp((ic + 1.0) * lg)
    tail = jnp.exp((lb - 1.0 - ic) * lg)
    kt = (k.astype(F32) * tail).astype(BF16)
    s_new = jnp.exp(lb * lg) * s_old + _dot_tn(kt, v)
    s_ref[...] = s_new

    @pl.when(c == pl.num_programs(2) - 1)
    def _():
        sfin_ref[...] = s_new

    oc = o - jnp.mean(o, axis=-1, keepdims=True)
    y = oc * lax.rsqrt(jnp.mean(oc * oc, axis=-1, keepdims=True) + EPS) * g_ref[...]
    gate = gate_ref[...]
    o_ref[...] = (y * (gate * jax.nn.sigmoid(gate))).astype(o_ref.dtype)


def _retention(q, k, v, gate, g_ret3, s0, lb):
    b, t, _ = q.shape
    has_init = s0 is not None
    kern = functools.partial(_ret_kernel, lb=lb, has_init=has_init)
    head = lambda: pl.BlockSpec((None, lb, HEAD_W), lambda b_, h, c: (b_, c, h))
    state = lambda: pl.BlockSpec((None, None, DK_R, DV_R), lambda b_, h, c: (b_, h, 0, 0))
    in_specs = [head(), head(), head(), head(), pl.BlockSpec((None, 1, HEAD_W), lambda b_, h, c: (h, 0, 0))]
    args = [q, k, v, gate, g_ret3]
    if has_init:
        in_specs.append(state())
        args.append(s0)
    return pl.pallas_call(
        kern,
        grid=(b, H_R, t // lb),
        in_specs=in_specs,
        out_specs=[head(), state()],
        out_shape=[jax.ShapeDtypeStruct((b, t, GROUP_W), BF16), jax.ShapeDtypeStruct((b, H_R, DK_R, DV_R), F32)],
        scratch_shapes=[pltpu.VMEM((DK_R, DV_R), F32)],
        compiler_params=_cparams(("parallel", "parallel", "arbitrary")),
        name="retention",
    )(*args)


def _out_proj_kernel(x_ref, mda_ref, mret_ref, wo_ref, g_ref, wq_ref, x1_ref, qx_ref):
    x1 = x_ref[...] + _dot(mda_ref[...], wo_ref[:GROUP_W, :]) + _dot(mret_ref[...], wo_ref[GROUP_W:, :])
    x1_ref[...] = x1
    hn = _rms(x1, g_ref[...]).astype(BF16)
    qx_ref[...] = (_dot(hn, wq_ref[...]) * (HD_X ** -0.5)).astype(BF16)


def _out_proj(x2d, mda, mret, wo_bf, g_cross, wq_bf, tm):
    n = x2d.shape[0]
    full = lambda a: pl.BlockSpec(a.shape, lambda i: (0, 0))
    return pl.pallas_call(
        _out_proj_kernel,
        grid=(n // tm,),
        in_specs=[pl.BlockSpec((tm, D_MODEL), lambda i: (i, 0)),
                  pl.BlockSpec((tm, GROUP_W), lambda i: (i, 0)),
                  pl.BlockSpec((tm, GROUP_W), lambda i: (i, 0)),
                  full(wo_bf), full(g_cross), full(wq_bf)],
        out_specs=[pl.BlockSpec((tm, D_MODEL), lambda i: (i, 0)), pl.BlockSpec((tm, D_MODEL), lambda i: (i, 0))],
        out_shape=[jax.ShapeDtypeStruct((n, D_MODEL), F32), jax.ShapeDtypeStruct((n, D_MODEL), BF16)],
        compiler_params=_cparams(("parallel",)),
        name="out_proj",
    )(x2d, mda, mret, wo_bf, g_cross, wq_bf)


def _mem_kv_kernel(m_ref, g_ref, wk_ref, wv_ref, mk_ref, mv_ref, mkb_ref, mvb_ref):
    mn = _rms(m_ref[...], g_ref[...]).astype(BF16)
    mk = _dot(mn, wk_ref[...])
    mv = _dot(mn, wv_ref[...])
    mk_ref[...] = mk
    mv_ref[...] = mv
    mkb_ref[...] = mk.astype(BF16)
    mvb_ref[...] = mv.astype(BF16)


def _mem_kv(mem2d, g_mem, wk_bf, wv_bf, tm):
    n = mem2d.shape[0]
    row = lambda: pl.BlockSpec((tm, D_MODEL), lambda i: (i, 0))
    full = lambda a: pl.BlockSpec(a.shape, lambda i: (0, 0))
    sh = lambda dt: jax.ShapeDtypeStruct((n, D_MODEL), dt)
    return pl.pallas_call(
        _mem_kv_kernel,
        grid=(n // tm,),
        in_specs=[row(), full(g_mem), full(wk_bf), full(wv_bf)],
        out_specs=[row(), row(), row(), row()],
        out_shape=[sh(F32), sh(F32), sh(BF16), sh(BF16)],
        compiler_params=_cparams(("parallel",)),
        name="mem_kv",
    )(mem2d, g_mem, wk_bf, wv_bf)


def _cross_kernel(x1_ref, qx_ref, mk_ref, mv_ref, wo_ref, g_ref, wpq_ref, x2_ref, h3_ref, qp_ref):
    q = qx_ref[...]
    heads = []
    for h in range(H_X):
        sl = slice(h * HD_X, (h + 1) * HD_X)
        kh = mk_ref[:, sl] if mk_ref.dtype == BF16 else mk_ref[:, sl].astype(BF16)
        vh = mv_ref[:, sl] if mv_ref.dtype == BF16 else mv_ref[:, sl].astype(BF16)
        s = _dot_nt(q[:, sl], kh)
        p = jnp.exp(s - jnp.max(s, axis=-1, keepdims=True))
        heads.append(_dot(p.astype(BF16), vh) / jnp.sum(p, axis=-1, keepdims=True))
    o = jnp.concatenate(heads, axis=-1).astype(BF16)
    x2 = x1_ref[...] + _dot(o, wo_ref[...])
    x2_ref[...] = x2
    h3 = _rms(x2, g_ref[...])
    h3_ref[...] = h3
    qp_ref[...] = _dot(h3.astype(BF16), wpq_ref[...]).astype(BF16)


def _cross(x1, qx, mk, mv, wo_bf, g_ffn, wpq_bf, tm):
    b, t, _ = x1.shape
    row = lambda w: pl.BlockSpec((None, tm, w), lambda b_, i: (b_, i, 0))
    mem = lambda: pl.BlockSpec((None, N_MEM, D_MODEL), lambda b_, i: (b_, 0, 0))
    full = lambda a: pl.BlockSpec(a.shape, lambda b_, i: (0, 0))
    dq = wpq_bf.shape[1]
    return pl.pallas_call(
        _cross_kernel,
        grid=(b, t // tm),
        in_specs=[row(D_MODEL), row(D_MODEL), mem(), mem(), full(wo_bf), full(g_ffn), full(wpq_bf)],
        out_specs=[row(D_MODEL), row(D_MODEL), row(dq)],
        out_shape=[jax.ShapeDtypeStruct((b, t, D_MODEL), F32), jax.ShapeDtypeStruct((b, t, D_MODEL), F32),
                   jax.ShapeDtypeStruct((b, t, dq), BF16)],
        compiler_params=_cparams(("parallel", "parallel")),
        name="cross_attn",
    )(x1, qx, mk, mv, wo_bf, g_ffn, wpq_bf)


ID_PAD = 2.0 ** 29


def _topk_rows(s, ids):
    vals, sel = [], []
    for _ in range(PEER_TOPK):
        m = jnp.max(s, axis=0, keepdims=True)
        idx = jnp.min(jnp.where(s == m, ids, ID_PAD), axis=0, keepdims=True)
        vals.append(m)
        sel.append(idx)
        s = jnp.where(ids == idx, -jnp.inf, s)
    return jnp.concatenate(vals, axis=0), jnp.concatenate(sel, axis=0)


def _candidates(v1, i1, v2, i2):
    lanes = v1.shape[1]
    b8 = lax.broadcasted_iota(jnp.int32, (8, lanes), 0)
    ident = lambda a, ia, ib, b: (a * PEER_TOPK + b) * float(N_EXPERTS) + (ia * float(N_KEYS) + ib)
    b16 = lax.broadcasted_iota(jnp.int32, (PEER_TOPK, lanes), 0).astype(F32)
    vals = [v1[0:1] + v2]
    ids = [ident(0, i1[0:1], i2, b16)]
    for a in range(1, 8):
        keep = b8 < PEER_TOPK // (a + 1)
        vals.append(jnp.where(keep, v1[a:a + 1] + v2[0:8], -jnp.inf))
        ids.append(jnp.where(keep, ident(a, i1[a:a + 1], i2[0:8], b8.astype(F32)), ID_PAD))
    a_hi = (b8 + 8).astype(F32)
    vals.append(v1[8:16] + v2[0:1])
    ids.append(ident(a_hi, i1[8:16], i2[0:1], 0.0))
    return jnp.concatenate(vals, axis=0), jnp.concatenate(ids, axis=0)


def _route_kernel(qp_ref, k1_ref, k2_ref, e_ref, g_ref):
    half = N_KEYS
    tt = qp_ref.shape[0]
    key_id = lax.broadcasted_iota(jnp.int32, (N_KEYS, tt), 0).astype(F32)
    es, gs = [], []
    for p in range(PEER_HEADS):
        q1 = qp_ref[:, (2 * p) * half:(2 * p + 1) * half]
        q2 = qp_ref[:, (2 * p + 1) * half:(2 * p + 2) * half]
        v1, i1 = _topk_rows(_dot_nt(k1_ref[p], q1), key_id)
        v2, i2 = _topk_rows(_dot_nt(k2_ref[p], q2), key_id)
        sc, sel = _topk_rows(*_candidates(v1, i1, v2, i2))
        w = jnp.exp(sc - sc[0:1])
        es.append((sel.astype(jnp.int32) & (N_EXPERTS - 1)) * ROWS_PER_EXPERT)
        gs.append(w / jnp.sum(w, axis=0, keepdims=True))
    e_ref[...] = jnp.concatenate(es, axis=0).T
    g_ref[...] = jnp.concatenate(gs, axis=0).T


def _route(qp2d, k1_bf, k2_bf, tt):
    n = qp2d.shape[0]
    full = lambda a: pl.BlockSpec(a.shape, lambda i: (0, 0, 0))
    return pl.pallas_call(
        _route_kernel,
        grid=(n // tt,),
        in_specs=[pl.BlockSpec((tt, qp2d.shape[1]), lambda i: (i, 0)), full(k1_bf), full(k2_bf)],
        out_specs=[pl.BlockSpec((tt, N_PAIRS), lambda i: (i, 0)), pl.BlockSpec((tt, N_PAIRS), lambda i: (i, 0))],
        out_shape=[jax.ShapeDtypeStruct((n, N_PAIRS), jnp.int32), jax.ShapeDtypeStruct((n, N_PAIRS), F32)],
        compiler_params=_cparams(("parallel",)),
        name="peer_route",
    )(qp2d, k1_bf, k2_bf)


def _unpack(words):
    hi = lax.bitcast_convert_type(words & jnp.uint32(0xFFFF0000), F32)
    lo = lax.bitcast_convert_type(words << 16, F32)
    return hi, lo


def _expert_rows(tab_ref, row0):
    return tab_ref[pl.ds(pl.multiple_of(row0, ROWS_PER_EXPERT), ROWS_PER_EXPERT), :]


REDUCE_TOKENS = 16


def _lane_sums_to_rows(y, n_tok):
    hi = y.astype(BF16)
    lo = (y - hi.astype(F32)).astype(BF16)
    ones = jnp.ones((128, N_PAIRS), BF16)
    s = (_dot(hi, ones) + _dot(lo, ones)).reshape(n_tok, N_PAIRS, N_PAIRS)
    eye = lax.broadcasted_iota(jnp.int32, (N_PAIRS, N_PAIRS), 0) == lax.broadcasted_iota(jnp.int32, (N_PAIRS, N_PAIRS), 1)
    return jnp.sum(jnp.where(eye[None], s, 0.0), axis=1)


CHUNKS = D_MODEL // 128


def _peer_u_kernel(e_ref, h_ref, gate_ref, tab_ref, w_ref, h8_ref, prod_ref, ys_ref, act_ref, *, tt):
    for c in range(CHUNKS):
        h8_ref[pl.ds(c, tt, stride=CHUNKS), :] = h_ref[:, c * 128:(c + 1) * 128]

    def token(t, slot):
        base = pl.multiple_of(t * CHUNKS, CHUNKS)
        ha = h8_ref[pl.ds(base, ROWS_PER_EXPERT), :]
        hb = h8_ref[pl.ds(pl.multiple_of(base + ROWS_PER_EXPERT, ROWS_PER_EXPERT), ROWS_PER_EXPERT), :]
        prod = prod_ref.at[slot]
        for k in range(N_PAIRS):
            hi, lo = _unpack(_expert_rows(tab_ref, e_ref[t, k]))
            prod[k * ROWS_PER_EXPERT:(k + 1) * ROWS_PER_EXPERT, :] = hi * ha + lo * hb
        y = prod[pl.ds(0, N_PAIRS, stride=ROWS_PER_EXPERT), :]
        for c in range(1, ROWS_PER_EXPERT):
            y = y + prod[pl.ds(c, N_PAIRS, stride=ROWS_PER_EXPERT), :]
        ys_ref[pl.ds(pl.multiple_of(t * N_PAIRS, N_PAIRS), N_PAIRS), :] = y

    def two_tokens(j, carry):
        token(2 * j, 0)
        token(2 * j + 1, 1)
        return carry

    lax.fori_loop(0, tt // 2, two_tokens, 0)

    def group(g, carry):
        rows = REDUCE_TOKENS * N_PAIRS
        y = ys_ref[pl.ds(pl.multiple_of(g * rows, rows), rows), :]
        act_ref[pl.ds(pl.multiple_of(g * REDUCE_TOKENS, REDUCE_TOKENS), REDUCE_TOKENS), :] = _lane_sums_to_rows(y, REDUCE_TOKENS)
        return carry

    lax.fori_loop(0, tt // REDUCE_TOKENS, group, 0)
    a = act_ref[...]
    gelu = 0.5 * a * (1.0 + lax.erf(a * (2.0 ** -0.5)))
    w_ref[...] = gate_ref[...] * gelu


def _peer_u(e_t, h, gate_t, tab, tt):
    n = e_t.shape[0]
    assert tt % REDUCE_TOKENS == 0
    kern = functools.partial(_peer_u_kernel, tt=tt)
    pair = lambda **kw: pl.BlockSpec((tt, N_PAIRS), lambda i: (i, 0), **kw)
    return pl.pallas_call(
        kern,
        grid=(n // tt,),
        in_specs=[pair(memory_space=pltpu.SMEM),
                  pl.BlockSpec((tt, D_MODEL), lambda i: (i, 0)),
                  pair(),
                  pl.BlockSpec(tab.shape, lambda i: (0, 0), pipeline_mode=pl.Buffered(1))],
        out_specs=pair(),
        out_shape=jax.ShapeDtypeStruct((n, N_PAIRS), F32),
        scratch_shapes=[pltpu.VMEM((tt * CHUNKS, 128), F32),
                        pltpu.VMEM((2, N_PAIRS * ROWS_PER_EXPERT, 128), F32),
                        pltpu.VMEM((tt * N_PAIRS, 128), F32),
                        pltpu.VMEM((tt, N_PAIRS), F32)],
        compiler_params=_cparams(("arbitrary",)),
        name="peer_u",
    )(e_t, h, gate_t, tab)


def _peer_v_kernel(e_ref, w_ref, x_ref, g_ref, tab_ref, o_ref, acc_ref, *, tt):
    n_acc = 2

    def token(t, carry):
        acc_hi = [jnp.zeros((ROWS_PER_EXPERT, 128), F32) for _ in range(n_acc)]
        acc_lo = [jnp.zeros((ROWS_PER_EXPERT, 128), F32) for _ in range(n_acc)]
        for k in range(N_PAIRS):
            hi, lo = _unpack(_expert_rows(tab_ref, e_ref[t, k]))
            w = w_ref[t, k]
            acc_hi[k % n_acc] = acc_hi[k % n_acc] + w * hi
            acc_lo[k % n_acc] = acc_lo[k % n_acc] + w * lo
        base = pl.multiple_of(t * CHUNKS, CHUNKS)
        acc_ref[pl.ds(base, ROWS_PER_EXPERT), :] = sum(acc_hi[1:], acc_hi[0])
        acc_ref[pl.ds(pl.multiple_of(base + ROWS_PER_EXPERT, ROWS_PER_EXPERT), ROWS_PER_EXPERT), :] = sum(acc_lo[1:], acc_lo[0])
        return carry

    lax.fori_loop(0, tt, token, 0)
    _residual_rms_store(x_ref, acc_ref, g_ref, o_ref, tt)


def _residual_rms_store(x_ref, add_ref, g_ref, o_ref, tt):
    xs = [x_ref[:, c * 128:(c + 1) * 128] + add_ref[pl.ds(c, tt, stride=CHUNKS), :] for c in range(CHUNKS)]
    sq = xs[0] * xs[0]
    for x in xs[1:]:
        sq = sq + x * x
    r = lax.rsqrt(jnp.sum(sq, axis=1, keepdims=True) * (1.0 / D_MODEL) + EPS)
    for c in range(CHUNKS):
        o_ref[:, c * 128:(c + 1) * 128] = xs[c] * r * g_ref[:, c * 128:(c + 1) * 128]


def _peer_v(e_t, w_t, x, g_final, tab, tt, n_tokens):
    kern = functools.partial(_peer_v_kernel, tt=tt)
    pair = lambda: pl.BlockSpec((tt, N_PAIRS), lambda i: (i, 0), memory_space=pltpu.SMEM)
    row = lambda: pl.BlockSpec((tt, D_MODEL), lambda i: (i, 0))
    return pl.pallas_call(
        kern,
        grid=(n_tokens // tt,),
        in_specs=[pair(), pair(), row(),
                  pl.BlockSpec((1, D_MODEL), lambda i: (0, 0)),
                  pl.BlockSpec(tab.shape, lambda i: (0, 0), pipeline_mode=pl.Buffered(1))],
        out_specs=row(),
        out_shape=jax.ShapeDtypeStruct(x.shape, F32),
        scratch_shapes=[pltpu.VMEM((tt * CHUNKS, 128), F32)],
        compiler_params=_cparams(("arbitrary",)),
        name="peer_v",
    )(e_t, w_t, x, g_final, tab)


def _residual_norm_kernel(x_ref, add_ref, g_ref, y_any_ref, o_ref, *, tt):
    del y_any_ref
    _residual_rms_store(x_ref, add_ref, g_ref, o_ref, tt)


def _residual_norm(x, add8, g_final, y, first_token, tt):
    n_rows = add8.shape[0] // CHUNKS
    off = first_token // tt
    row = lambda: pl.BlockSpec((tt, D_MODEL), lambda i: (i + off, 0))
    return pl.pallas_call(
        functools.partial(_residual_norm_kernel, tt=tt),
        grid=(n_rows // tt,),
        in_specs=[row(), pl.BlockSpec((tt * CHUNKS, 128), lambda i: (i, 0)),
                  pl.BlockSpec((1, D_MODEL), lambda i: (0, 0)),
                  pl.BlockSpec(memory_space=pl.ANY)],
        out_specs=row(),
        out_shape=jax.ShapeDtypeStruct(y.shape, F32),
        input_output_aliases={3: 0},
        compiler_params=_cparams(("arbitrary",)),
        name="peer_residual_norm",
    )(x, add8, g_final, y)


SC_CORES = 2
SC_SUBCORES = 16
SC_LANES = 16
WORD_ROWS = 128 // SC_LANES


def _peer_v_sc(e_t, w_t, tab):
    n = e_t.shape[0]
    workers = SC_CORES * SC_SUBCORES
    assert n % workers == 0
    per = n // workers
    mesh = plsc.VectorSubcoreMesh(core_axis_name="core", subcore_axis_name="subcore",
                                  num_cores=SC_CORES, num_subcores=SC_SUBCORES)

    @pl.kernel(out_type=jax.ShapeDtypeStruct((n * CHUNKS, 128), F32), mesh=mesh,
               scratch_types=[pltpu.VMEM((N_PAIRS,), jnp.int32), pltpu.VMEM((N_PAIRS,), F32),
                              pltpu.VMEM((ROWS_PER_EXPERT, PEER_TOPK, 128), jnp.uint32),
                              pltpu.VMEM((CHUNKS, 128), F32)],
               compiler_params=pltpu.CompilerParams(needs_layout_passes=False),
               name="peer_v_sc")
    def body(e_hbm, w_hbm, tab_hbm, o_hbm, e_v, w_v, rows_v, acc_v):
        wid = lax.axis_index("core") * SC_SUBCORES + lax.axis_index("subcore")

        @pl.loop(0, per)
        def _(i):
            t = wid * per + i
            pltpu.sync_copy(e_hbm.at[t], e_v)
            pltpu.sync_copy(w_hbm.at[t], w_v)
            for r in range(CHUNKS):
                for j in range(WORD_ROWS):
                    acc_v[r, pl.ds(j * SC_LANES, SC_LANES)] = jnp.zeros((SC_LANES,), F32)

            @pl.loop(0, PEER_HEADS)
            def _(p):
                first = e_v[pl.ds(p * PEER_TOPK, PEER_TOPK)]
                for r in range(ROWS_PER_EXPERT):
                    pltpu.sync_copy(tab_hbm.at[first + r], rows_v.at[r])
                lane_k = lax.broadcasted_iota(jnp.int32, (SC_LANES,), 0) * 0 + p * PEER_TOPK
                ws = [plsc.load_gather(w_v, [lane_k + k]) for k in range(PEER_TOPK)]
                for r in range(ROWS_PER_EXPERT):
                    @pl.loop(0, WORD_ROWS)
                    def _(j):
                        sl = pl.ds(j * SC_LANES, SC_LANES)
                        a_hi = acc_v[r, sl]
                        a_lo = acc_v[ROWS_PER_EXPERT + r, sl]
                        for k in range(PEER_TOPK):
                            hi, lo = _unpack(rows_v[r, k, sl])
                            a_hi = a_hi + ws[k] * hi
                            a_lo = a_lo + ws[k] * lo
                        acc_v[r, sl] = a_hi
                        acc_v[ROWS_PER_EXPERT + r, sl] = a_lo

            pltpu.sync_copy(acc_v, o_hbm.at[pl.ds(t * CHUNKS, CHUNKS)])

    return body(e_t, w_t, tab)


def _pack_table(tab):
    bits = lax.bitcast_convert_type(tab.astype(BF16), jnp.uint16).astype(jnp.uint32)
    words = (bits[:, :HALF] << 16) | bits[:, HALF:]
    return words.reshape(tab.shape[0] * ROWS_PER_EXPERT, 128)


PEER_TOKENS = 128
SC_SHARE = 4
SC_MIN_TOKENS = 4096


def _sc_tokens(n):
    if n < SC_MIN_TOKENS:
        return 0
    unit = PEER_TOKENS * SC_CORES * SC_SUBCORES // math.gcd(PEER_TOKENS, SC_CORES * SC_SUBCORES)
    return (n // SC_SHARE) // unit * unit


def _row_tile(n, pref):
    while n % pref:
        pref //= 2
    return pref


def _peer_and_final(x2, h3, qp, k1_bf, k2_bf, tab_u, tab_v, g_final, tt, n_sc=0):
    n = x2.shape[0]
    e_t, gate_t = _route(qp, k1_bf, k2_bf, tt)
    w_t = _peer_u(e_t, h3, gate_t, tab_u, tt)
    y = _peer_v(e_t, w_t, x2, g_final, tab_v, tt, n - n_sc)
    if n_sc:
        add8 = _peer_v_sc(e_t[n - n_sc:], w_t[n - n_sc:], tab_v)
        y = _residual_norm(x2, add8, g_final, y, n - n_sc, tt)
    return y


def kernel(x_prompt, x_sample, mem_prompt, cache_da_k, cache_da_v, state_ret, cache_mem_k, cache_mem_v, g_mix, w_in, lam_q1, lam_k1, lam_q2, lam_k2, g_da, g_ret, w_out, g_cross, g_mem, w_xq, w_xk, w_xv, w_xo, g_ffn, w_pq, peer_k1, peer_k2, peer_u, peer_v, g_final):
    depth = w_in.shape[0]
    assert depth == 1, "single-layer step"
    l = 0
    lam_init = 0.8 - 0.6 * math.exp(-0.3 * l)
    b, t, _ = x_prompt.shape
    bs, ts, _ = x_sample.shape
    past_len = cache_da_k.shape[2]

    row = lambda a: a.reshape(1, -1)
    w_in_bf = w_in[l].astype(BF16)
    w_out_bf = w_out[l].astype(BF16)
    w_xq_bf, w_xk_bf, w_xv_bf, w_xo_bf = (w[l].astype(BF16) for w in (w_xq, w_xk, w_xv, w_xo))
    w_pq_bf = w_pq[l].astype(BF16)
    k1_bf, k2_bf = peer_k1[l].astype(BF16), peer_k2[l].astype(BF16)
    tab_u, tab_v = _pack_table(peer_u[l]), _pack_table(peer_v[l])
    lamp = jnp.stack([lam_q1[l], lam_k1[l], lam_q2[l], lam_k2[l]])
    g_da3 = g_da[l].reshape(H_D, 1, DV_D)
    g_ret3 = g_ret[l].reshape(H_R, 1, DV_R)
    g_fin = row(g_final)

    def mixer_tail(x2d, mda, mret, mk, mv, bb, tt_rows, n_sc):
        n = x2d.shape[0]
        tm = _row_tile(n, 512)
        x1, qx = _out_proj(x2d, mda.reshape(n, GROUP_W), mret.reshape(n, GROUP_W), w_out_bf, row(g_cross[l]), w_xq_bf, tm)
        x2, h3, qp = _cross(x1.reshape(bb, tt_rows, D_MODEL), qx.reshape(bb, tt_rows, D_MODEL), mk, mv,
                            w_xo_bf, row(g_ffn[l]), w_pq_bf, _row_tile(tt_rows, 512))
        y = _peer_and_final(x2.reshape(n, D_MODEL), h3.reshape(n, D_MODEL), qp.reshape(n, -1),
                            k1_bf, k2_bf, tab_u, tab_v, g_fin, PEER_TOKENS, n_sc)
        return y.reshape(bb, tt_rows, D_MODEL)

    n = b * t
    xp = x_prompt.reshape(n, D_MODEL)
    qd, kd, vd, kdb, vdb, qr, kr, vr, gr = _in_proj(xp, row(g_mix[l]), w_in_bf, _row_tile(n, 512))
    r3 = lambda a: a.reshape(b, t, GROUP_W)
    mda = _diff_attn_prompt(lamp, r3(qd), r3(kdb), r3(vdb), g_da3, lam_init, 512, 512)
    mret, s_fin = _retention(r3(qr), r3(kr), r3(vr), r3(gr), g_ret3, None, 256)
    mk, mv, mkb, mvb = _mem_kv(mem_prompt.reshape(b * N_MEM, D_MODEL), row(g_mem[l]), w_xk_bf, w_xv_bf, 512)
    y_prompt = mixer_tail(xp, mda, mret, mkb.reshape(b, N_MEM, D_MODEL), mvb.reshape(b, N_MEM, D_MODEL), b, t, _sc_tokens(n))

    ns = bs * ts
    xs = x_sample.reshape(ns, D_MODEL)
    qd_s, kd_s, vd_s, kdb_s, vdb_s, qr_s, kr_s, vr_s, gr_s = _in_proj(xs, row(g_mix[l]), w_in_bf, _row_tile(ns, 512))
    s3 = lambda a: a.reshape(bs, ts, GROUP_W)
    mda_s = _diff_attn_sample(lamp, s3(qd_s), cache_da_k[l].reshape(bs, past_len, GROUP_W),
                              cache_da_v[l].reshape(bs, past_len, GROUP_W), s3(kdb_s), s3(vdb_s), g_da3, lam_init)
    mret_s, s_new = _retention(s3(qr_s), s3(kr_s), s3(vr_s), s3(gr_s), g_ret3, state_ret[l], ts)
    y_sample = mixer_tail(xs, mda_s, mret_s, cache_mem_k[l].reshape(bs, N_MEM, D_MODEL),
                          cache_mem_v[l].reshape(bs, N_MEM, D_MODEL), bs, ts, _sc_tokens(ns))

    return (y_prompt, y_sample,
            kd.reshape(1, b, t, H_D, 2, DK_D), vd.reshape(1, b, t, H_D, DV_D), s_fin[None],
            mk.reshape(1, b, N_MEM, H_X, HD_X), mv.reshape(1, b, N_MEM, H_X, HD_X),
            kd_s.reshape(1, bs, ts, H_D, 2, DK_D), vd_s.reshape(1, bs, ts, H_D, DV_D), s_new[None])
```

```python
import functools
import math

import jax
import jax.numpy as jnp
from jax import lax
from jax.experimental import pallas as pl
from jax.experimental.pallas import tpu as pltpu
from jax.experimental.pallas import tpu_sc as plsc

D_MODEL = 1024
CHUNK = 64
CHUNK_SHIFT = CHUNK.bit_length() - 1
assert 1 << CHUNK_SHIFT == CHUNK
H_D, DK_D, DV_D = 4, 64, 128
H_R, DK_R, DV_R = 4, 128, 128
N_MEM = 256
H_X = 4
HD_X = D_MODEL // H_X
PEER_HEADS = 8
N_KEYS = 128
N_EXPERTS = N_KEYS * N_KEYS
PEER_TOPK = 16
EPS = 1e-6
HEAD_W = 128
GROUP_W = 512
N_PAIRS = PEER_HEADS * PEER_TOPK
HALF = D_MODEL // 2
ROWS_PER_EXPERT = HALF // 128
VMEM_LIMIT = 56 * 1024 * 1024

BF16 = jnp.bfloat16
F32 = jnp.float32


def _cparams(sem):
    return pltpu.CompilerParams(dimension_semantics=sem, vmem_limit_bytes=VMEM_LIMIT)


def _rms(x, g):
    return x * lax.rsqrt(jnp.mean(x * x, axis=-1, keepdims=True) + EPS) * g


def _dot(a, b):
    return jnp.dot(a, b, preferred_element_type=F32)


def _dot_nt(a, b):
    return lax.dot_general(a, b, (((1,), (1,)), ((), ())), preferred_element_type=F32)


def _dot_tn(a, b):
    return lax.dot_general(a, b, (((0,), (0,)), ((), ())), preferred_element_type=F32)


def _select_by_head(h, values):
    out = jnp.float32(values[-1])
    for i in range(len(values) - 2, -1, -1):
        out = jnp.where(h == i, jnp.float32(values[i]), out)
    return out


def _in_proj_kernel(x_ref, g_ref, w_ref, qd_ref, kd_ref, vd_ref, kdb_ref, vdb_ref,
                    qr_ref, kr_ref, vr_ref, gr_ref):
    hb = _rms(x_ref[...], g_ref[...]).astype(BF16)
    col = lambda c: _dot(hb, w_ref[:, c * GROUP_W:(c + 1) * GROUP_W])
    qd_ref[...] = (col(0) * (DK_D ** -0.5)).astype(BF16)
    kd = col(1)
    kd_ref[...] = kd
    kdb_ref[...] = kd.astype(BF16)
    vd = col(2)
    vd_ref[...] = vd
    vdb_ref[...] = vd.astype(BF16)
    qr_ref[...] = col(3).astype(BF16)
    kr_ref[...] = (col(4) * (DK_R ** -0.5)).astype(BF16)
    vr_ref[...] = col(5).astype(BF16)
    gr_ref[...] = col(6)


def _in_proj(x2d, g, w_bf, tm):
    n = x2d.shape[0]
    blk = lambda: pl.BlockSpec((tm, GROUP_W), lambda i: (i, 0))
    sh = lambda dt: jax.ShapeDtypeStruct((n, GROUP_W), dt)
    return pl.pallas_call(
        _in_proj_kernel,
        grid=(n // tm,),
        in_specs=[pl.BlockSpec((tm, D_MODEL), lambda i: (i, 0)),
                  pl.BlockSpec((1, D_MODEL), lambda i: (0, 0)),
                  pl.BlockSpec(w_bf.shape, lambda i: (0, 0))],
        out_specs=[blk() for _ in range(9)],
        out_shape=[sh(BF16), sh(F32), sh(F32), sh(BF16), sh(BF16), sh(BF16), sh(BF16), sh(BF16), sh(F32)],
        compiler_params=_cparams(("parallel",)),
        name="in_proj",
    )(x2d, g, w_bf)


def _lambda_from(lam_ref, lam_init):
    l = lam_ref[...]
    a = jnp.exp(jnp.sum(l[0:1] * l[1:2], axis=-1, keepdims=True))
    b = jnp.exp(jnp.sum(l[2:3] * l[3:4], axis=-1, keepdims=True))
    return a - b + lam_init


def _diff_post(acc, l, lam, g, lam_init, tq):
    o = acc[:tq] / l[:tq] - lam * (acc[tq:] / l[tq:])
    return o * lax.rsqrt(jnp.mean(o * o, axis=-1, keepdims=True) + EPS) * g * (1.0 - lam_init)


def _split_maps(q):
    lane = lax.broadcasted_iota(jnp.int32, q.shape, 1)
    zero = jnp.zeros_like(q)
    return jnp.concatenate([jnp.where(lane < DK_D, q, zero), jnp.where(lane >= DK_D, q, zero)], axis=0)


def _da_prompt_kernel(lam_ref, q_ref, k_ref, v_ref, g_ref, o_ref, kx_ref, vx_ref, own_ref, acc_ref, m_ref, *, lam_init, tq, tk):
    h = pl.program_id(1)
    i = pl.program_id(2)
    t = k_ref.shape[0]
    slope = _select_by_head(h, [2.0 ** (-8.0 * (j + 1) / H_D) for j in range(H_D)])

    @pl.when(i == 0)
    def _():
        pos = lax.broadcasted_iota(jnp.int32, (t, HEAD_W), 0)
        lane = lax.broadcasted_iota(jnp.int32, (t, HEAD_W), 1)
        coarse = ((pos >> CHUNK_SHIFT) << CHUNK_SHIFT).astype(F32) * slope
        fine = (pos & (CHUNK - 1)).astype(F32) * slope
        kx_ref[:, :HEAD_W] = k_ref[...]
        kx_ref[:, HEAD_W:] = jnp.where(lane == 0, coarse, jnp.where(lane == 1, fine, 0.0)).astype(BF16)
        vx_ref[:DV_D, :] = v_ref[...].astype(F32).T.astype(BF16)
        vx_ref[DV_D:, :] = jnp.ones((vx_ref.shape[0] - DV_D, t), BF16)
        krel = lax.broadcasted_iota(jnp.int32, (tk, 2 * tq), 0)
        c = lax.broadcasted_iota(jnp.int32, (tk, 2 * tq), 1)
        for par in range(tk // tq):
            qrel = par * tq + jnp.where(c >= tq, c - tq, c)
            ahead = (2.0 * slope) * jnp.maximum(krel - qrel, 0).astype(F32)
            own_ref[par] = jnp.where((qrel >> CHUNK_SHIFT) >= (krel >> CHUNK_SHIFT), -ahead, -1e30)

    q = q_ref[...]
    lane = lax.broadcasted_iota(jnp.int32, q.shape, 1)
    zero = jnp.zeros_like(q)
    ones2 = jnp.where(lane < 2, 1.0, 0.0).astype(BF16)
    q2 = jnp.concatenate([jnp.concatenate([jnp.where(lane < DK_D, q, zero), ones2], axis=1),
                          jnp.concatenate([jnp.where(lane >= DK_D, q, zero), ones2], axis=1)], axis=0)
    jd = (i * tq) // tk

    def scores(j):
        return _dot_nt(kx_ref[pl.ds(pl.multiple_of(j * tk, tk), tk), :], q2)

    def values(j):
        return vx_ref[:, pl.ds(pl.multiple_of(j * tk, tk), tk)]

    s = scores(jd) + own_ref[(i * tq) % tk // tq]
    m0 = jnp.max(s, axis=0, keepdims=True)
    m_ref[...] = m0
    acc_ref[...] = _dot(values(jd), jnp.exp(s - m0).astype(BF16))

    def absorb(blocks):
        ss = [scores(j) for j in blocks]
        m_old = m_ref[...]
        m_new = m_old
        for s in ss:
            m_new = jnp.maximum(m_new, jnp.max(s, axis=0, keepdims=True))
        m_ref[...] = m_new
        acc = jnp.exp(m_old - m_new) * acc_ref[...]
        for j, s in zip(blocks, ss):
            acc = acc + _dot(values(j), jnp.exp(s - m_new).astype(BF16))
        acc_ref[...] = acc

    def past_pair(jj, carry):
        absorb([2 * jj, 2 * jj + 1])
        return carry

    lax.fori_loop(0, jd // 2, past_pair, 0)

    @pl.when(jd % 2 == 1)
    def _():
        absorb([jd - 1])

    acc = acc_ref[...]
    num, den = acc[:DV_D], acc[DV_D:DV_D + 1]
    lam = _lambda_from(lam_ref, lam_init)
    o = (num[:, :tq] / den[:, :tq] - lam * (num[:, tq:] / den[:, tq:])).T
    o = o * lax.rsqrt(jnp.mean(o * o, axis=-1, keepdims=True) + EPS) * g_ref[...] * (1.0 - lam_init)
    o_ref[...] = o.astype(o_ref.dtype)


ONES_ROWS = 16


def _diff_attn_prompt(lamp, q, k, v, g_da3, lam_init, tq, tk):
    b, t, _ = q.shape
    kern = functools.partial(_da_prompt_kernel, lam_init=lam_init, tq=tq, tk=tk)
    return pl.pallas_call(
        kern,
        grid=(b, H_D, t // tq),
        in_specs=[pl.BlockSpec((4, DK_D), lambda b_, h, i: (0, 0)),
                  pl.BlockSpec((None, tq, HEAD_W), lambda b_, h, i: (b_, i, h)),
                  pl.BlockSpec((None, t, HEAD_W), lambda b_, h, i: (b_, 0, h)),
                  pl.BlockSpec((None, t, HEAD_W), lambda b_, h, i: (b_, 0, h)),
                  pl.BlockSpec((None, 1, HEAD_W), lambda b_, h, i: (h, 0, 0))],
        out_specs=pl.BlockSpec((None, tq, HEAD_W), lambda b_, h, i: (b_, i, h)),
        out_shape=jax.ShapeDtypeStruct((b, t, GROUP_W), BF16),
        scratch_shapes=[pltpu.VMEM((t, 2 * HEAD_W), BF16), pltpu.VMEM((DV_D + ONES_ROWS, t), BF16),
                        pltpu.VMEM((tk // tq, tk, 2 * tq), F32),
                        pltpu.VMEM((DV_D + ONES_ROWS, 2 * tq), F32), pltpu.VMEM((1, 2 * tq), F32)],
        compiler_params=_cparams(("parallel", "parallel", "arbitrary")),
        name="diff_attn_prompt",
    )(lamp, q, k, v, g_da3)


def _da_sample_kernel(lam_ref, q_ref, kc_ref, vc_ref, kn_ref, vn_ref, g_ref, o_ref, *, lam_init, ts, past_len):
    h = pl.program_id(1)
    slope = _select_by_head(h, [2.0 ** (-8.0 * (j + 1) / H_D) for j in range(H_D)])
    q2 = _split_maps(q_ref[...])
    rows = 2 * ts

    def scores(k, base, n):
        r = lax.broadcasted_iota(jnp.int32, (rows, n), 0)
        qpos = past_len + jnp.where(r >= ts, r - ts, r)
        kpos = base + lax.broadcasted_iota(jnp.int32, (rows, n), 1)
        return _dot_nt(q2, k) - slope * jnp.abs(qpos - kpos).astype(F32)

    sc = scores(kc_ref[...].astype(BF16), 0, past_len)
    sn = scores(kn_ref[...], past_len, ts)
    m = jnp.maximum(jnp.max(sc, axis=-1, keepdims=True), jnp.max(sn, axis=-1, keepdims=True))
    pc = jnp.exp(sc - m)
    pn = jnp.exp(sn - m)
    l = jnp.sum(pc, axis=-1, keepdims=True) + jnp.sum(pn, axis=-1, keepdims=True)
    acc = _dot(pc.astype(BF16), vc_ref[...].astype(BF16)) + _dot(pn.astype(BF16), vn_ref[...])
    lam = _lambda_from(lam_ref, lam_init)
    o_ref[...] = _diff_post(acc, l, lam, g_ref[...], lam_init, ts).astype(o_ref.dtype)


def _diff_attn_sample(lamp, q, kc, vc, kn, vn, g_da3, lam_init):
    b, ts, _ = q.shape
    past_len = kc.shape[1]
    kern = functools.partial(_da_sample_kernel, lam_init=lam_init, ts=ts, past_len=past_len)
    head = lambda rows: pl.BlockSpec((None, rows, HEAD_W), lambda b_, h: (b_, 0, h))
    return pl.pallas_call(
        kern,
        grid=(b, H_D),
        in_specs=[pl.BlockSpec((4, DK_D), lambda b_, h: (0, 0)),
                  head(ts), head(past_len), head(past_len), head(ts), head(ts),
                  pl.BlockSpec((None, 1, HEAD_W), lambda b_, h: (h, 0, 0))],
        out_specs=head(ts),
        out_shape=jax.ShapeDtypeStruct((b, ts, GROUP_W), BF16),
        compiler_params=_cparams(("parallel", "parallel")),
        name="diff_attn_sample",
    )(lamp, q, kc, vc, kn, vn, g_da3)


def _ret_kernel(*refs, lb, has_init):
    if has_init:
        q_ref, k_ref, v_ref, gate_ref, g_ref, s0_ref, o_ref, sfin_ref, s_ref = refs
    else:
        q_ref, k_ref, v_ref, gate_ref, g_ref, o_ref, sfin_ref, s_ref = refs
    h = pl.program_id(1)
    c = pl.program_id(2)
    lg = _select_by_head(h, [math.log1p(-(2.0 ** (-5.0 - j))) for j in range(H_R)])

    @pl.when(c == 0)
    def _():
        s_ref[...] = s0_ref[...] if has_init else jnp.zeros_like(s_ref)

    q, k, v = q_ref[...], k_ref[...], v_ref[...]
    i = lax.broadcasted_iota(jnp.int32, (lb, lb), 0)
    j = lax.broadcasted_iota(jnp.int32, (lb, lb), 1)
    d = (i - j).astype(F32)
    decay = jnp.where(d >= 0, jnp.exp(jnp.maximum(d, 0.0) * lg), 0.0)
    inner = _dot_nt(q, k) * decay
    ic = lax.broadcasted_iota(jnp.int32, (lb, 1), 0).astype(F32)
    s_old = s_ref[...]
    o = _dot(inner.astype(BF16), v) + _dot(q, s_old.astype(BF16)) * jnp.exp((ic + 1.0) * lg)
    tail = jnp.exp((lb - 1.0 - ic) * lg)
    kt = (k.astype(F32) * tail).astype(BF16)
    s_new = jnp.exp(lb * lg) * s_old + _dot_tn(kt, v)
    s_ref[...] = s_new

    @pl.when(c == pl.num_programs(2) - 1)
    def _():
        sfin_ref[...] = s_new

    oc = o - jnp.mean(o, axis=-1, keepdims=True)
    y = oc * lax.rsqrt(jnp.mean(oc * oc, axis=-1, keepdims=True) + EPS) * g_ref[...]
    gate = gate_ref[...]
    o_ref[...] = (y * (gate * jax.nn.sigmoid(gate))).astype(o_ref.dtype)


def _retention(q, k, v, gate, g_ret3, s0, lb):
    b, t, _ = q.shape
    has_init = s0 is not None
    kern = functools.partial(_ret_kernel, lb=lb, has_init=has_init)
    head = lambda: pl.BlockSpec((None, lb, HEAD_W), lambda b_, h, c: (b_, c, h))
    state = lambda: pl.BlockSpec((None, None, DK_R, DV_R), lambda b_, h, c: (b_, h, 0, 0))
    in_specs = [head(), head(), head(), head(), pl.BlockSpec((None, 1, HEAD_W), lambda b_, h, c: (h, 0, 0))]
    args = [q, k, v, gate, g_ret3]
    if has_init:
        in_specs.append(state())
        args.append(s0)
    return pl.pallas_call(
        kern,
        grid=(b, H_R, t // lb),
        in_specs=in_specs,
        out_specs=[head(), state()],
        out_shape=[jax.ShapeDtypeStruct((b, t, GROUP_W), BF16), jax.ShapeDtypeStruct((b, H_R, DK_R, DV_R), F32)],
        scratch_shapes=[pltpu.VMEM((DK_R, DV_R), F32)],
        compiler_params=_cparams(("parallel", "parallel", "arbitrary")),
        name="retention",
    )(*args)


def _out_proj_kernel(x_ref, mda_ref, mret_ref, wo_ref, g_ref, wq_ref, x1_ref, qx_ref):
    x1 = x_ref[...] + _dot(mda_ref[...], wo_ref[:GROUP_W, :]) + _dot(mret_ref[...], wo_ref[GROUP_W:, :])
    x1_ref[...] = x1
    hn = _rms(x1, g_ref[...]).astype(BF16)
    qx_ref[...] = (_dot(hn, wq_ref[...]) * (HD_X ** -0.5)).astype(BF16)


def _out_proj(x2d, mda, mret, wo_bf, g_cross, wq_bf, tm):
    n = x2d.shape[0]
    full = lambda a: pl.BlockSpec(a.shape, lambda i: (0, 0))
    return pl.pallas_call(
        _out_proj_kernel,
        grid=(n // tm,),
        in_specs=[pl.BlockSpec((tm, D_MODEL), lambda i: (i, 0)),
                  pl.BlockSpec((tm, GROUP_W), lambda i: (i, 0)),
                  pl.BlockSpec((tm, GROUP_W), lambda i: (i, 0)),
                  full(wo_bf), full(g_cross), full(wq_bf)],
        out_specs=[pl.BlockSpec((tm, D_MODEL), lambda i: (i, 0)), pl.BlockSpec((tm, D_MODEL), lambda i: (i, 0))],
        out_shape=[jax.ShapeDtypeStruct((n, D_MODEL), F32), jax.ShapeDtypeStruct((n, D_MODEL), BF16)],
        compiler_params=_cparams(("parallel",)),
        name="out_proj",
    )(x2d, mda, mret, wo_bf, g_cross, wq_bf)


def _mem_kv_kernel(m_ref, g_ref, wk_ref, wv_ref, mk_ref, mv_ref, mkb_ref, mvb_ref):
    mn = _rms(m_ref[...], g_ref[...]).astype(BF16)
    mk = _dot(mn, wk_ref[...])
    mv = _dot(mn, wv_ref[...])
    mk_ref[...] = mk
    mv_ref[...] = mv
    mkb_ref[...] = mk.astype(BF16)
    mvb_ref[...] = mv.astype(BF16)


def _mem_kv(mem2d, g_mem, wk_bf, wv_bf, tm):
    n = mem2d.shape[0]
    row = lambda: pl.BlockSpec((tm, D_MODEL), lambda i: (i, 0))
    full = lambda a: pl.BlockSpec(a.shape, lambda i: (0, 0))
    sh = lambda dt: jax.ShapeDtypeStruct((n, D_MODEL), dt)
    return pl.pallas_call(
        _mem_kv_kernel,
        grid=(n // tm,),
        in_specs=[row(), full(g_mem), full(wk_bf), full(wv_bf)],
        out_specs=[row(), row(), row(), row()],
        out_shape=[sh(F32), sh(F32), sh(BF16), sh(BF16)],
        compiler_params=_cparams(("parallel",)),
        name="mem_kv",
    )(mem2d, g_mem, wk_bf, wv_bf)


def _cross_kernel(x1_ref, qx_ref, mk_ref, mv_ref, wo_ref, g_ref, wpq_ref, x2_ref, h3_ref, qp_ref):
    q = qx_ref[...]
    heads = []
    for h in range(H_X):
        sl = slice(h * HD_X, (h + 1) * HD_X)
        kh = mk_ref[:, sl] if mk_ref.dtype == BF16 else mk_ref[:, sl].astype(BF16)
        vh = mv_ref[:, sl] if mv_ref.dtype == BF16 else mv_ref[:, sl].astype(BF16)
        s = _dot_nt(q[:, sl], kh)
        p = jnp.exp(s - jnp.max(s, axis=-1, keepdims=True))
        heads.append(_dot(p.astype(BF16), vh) / jnp.sum(p, axis=-1, keepdims=True))
    o = jnp.concatenate(heads, axis=-1).astype(BF16)
    x2 = x1_ref[...] + _dot(o, wo_ref[...])
    x2_ref[...] = x2
    h3 = _rms(x2, g_ref[...])
    h3_ref[...] = h3
    qp_ref[...] = _dot(h3.astype(BF16), wpq_ref[...]).astype(BF16)


def _cross(x1, qx, mk, mv, wo_bf, g_ffn, wpq_bf, tm):
    b, t, _ = x1.shape
    row = lambda w: pl.BlockSpec((None, tm, w), lambda b_, i: (b_, i, 0))
    mem = lambda: pl.BlockSpec((None, N_MEM, D_MODEL), lambda b_, i: (b_, 0, 0))
    full = lambda a: pl.BlockSpec(a.shape, lambda b_, i: (0, 0))
    dq = wpq_bf.shape[1]
    return pl.pallas_call(
        _cross_kernel,
        grid=(b, t // tm),
        in_specs=[row(D_MODEL), row(D_MODEL), mem(), mem(), full(wo_bf), full(g_ffn), full(wpq_bf)],
        out_specs=[row(D_MODEL), row(D_MODEL), row(dq)],
        out_shape=[jax.ShapeDtypeStruct((b, t, D_MODEL), F32), jax.ShapeDtypeStruct((b, t, D_MODEL), F32),
                   jax.ShapeDtypeStruct((b, t, dq), BF16)],
        compiler_params=_cparams(("parallel", "parallel")),
        name="cross_attn",
    )(x1, qx, mk, mv, wo_bf, g_ffn, wpq_bf)


ID_PAD = 2.0 ** 29


def _topk_rows(s, ids):
    vals, sel = [], []
    for _ in range(PEER_TOPK):
        m = jnp.max(s, axis=0, keepdims=True)
        idx = jnp.min(jnp.where(s == m, ids, ID_PAD), axis=0, keepdims=True)
        vals.append(m)
        sel.append(idx)
        s = jnp.where(ids == idx, -jnp.inf, s)
    return jnp.concatenate(vals, axis=0), jnp.concatenate(sel, axis=0)


def _candidates(v1, i1, v2, i2):
    lanes = v1.shape[1]
    b8 = lax.broadcasted_iota(jnp.int32, (8, lanes), 0)
    ident = lambda a, ia, ib, b: (a * PEER_TOPK + b) * float(N_EXPERTS) + (ia * float(N_KEYS) + ib)
    b16 = lax.broadcasted_iota(jnp.int32, (PEER_TOPK, lanes), 0).astype(F32)
    vals = [v1[0:1] + v2]
    ids = [ident(0, i1[0:1], i2, b16)]
    for a in range(1, 8):
        keep = b8 < PEER_TOPK // (a + 1)
        vals.append(jnp.where(keep, v1[a:a + 1] + v2[0:8], -jnp.inf))
        ids.append(jnp.where(keep, ident(a, i1[a:a + 1], i2[0:8], b8.astype(F32)), ID_PAD))
    a_hi = (b8 + 8).astype(F32)
    vals.append(v1[8:16] + v2[0:1])
    ids.append(ident(a_hi, i1[8:16], i2[0:1], 0.0))
    return jnp.concatenate(vals, axis=0), jnp.concatenate(ids, axis=0)


def _route_kernel(qp_ref, k1_ref, k2_ref, e_ref, g_ref):
    half = N_KEYS
    tt = qp_ref.shape[0]
    key_id = lax.broadcasted_iota(jnp.int32, (N_KEYS, tt), 0).astype(F32)
    es, gs = [], []
    for p in range(PEER_HEADS):
        q1 = qp_ref[:, (2 * p) * half:(2 * p + 1) * half]
        q2 = qp_ref[:, (2 * p + 1) * half:(2 * p + 2) * half]
        v1, i1 = _topk_rows(_dot_nt(k1_ref[p], q1), key_id)
        v2, i2 = _topk_rows(_dot_nt(k2_ref[p], q2), key_id)
        sc, sel = _topk_rows(*_candidates(v1, i1, v2, i2))
        w = jnp.exp(sc - sc[0:1])
        es.append((sel.astype(jnp.int32) & (N_EXPERTS - 1)) * ROWS_PER_EXPERT)
        gs.append(w / jnp.sum(w, axis=0, keepdims=True))
    e_ref[...] = jnp.concatenate(es, axis=0).T
    g_ref[...] = jnp.concatenate(gs, axis=0).T


def _route(qp2d, k1_bf, k2_bf, tt):
    n = qp2d.shape[0]
    full = lambda a: pl.BlockSpec(a.shape, lambda i: (0, 0, 0))
    return pl.pallas_call(
        _route_kernel,
        grid=(n // tt,),
        in_specs=[pl.BlockSpec((tt, qp2d.shape[1]), lambda i: (i, 0)), full(k1_bf), full(k2_bf)],
        out_specs=[pl.BlockSpec((tt, N_PAIRS), lambda i: (i, 0)), pl.BlockSpec((tt, N_PAIRS), lambda i: (i, 0))],
        out_shape=[jax.ShapeDtypeStruct((n, N_PAIRS), jnp.int32), jax.ShapeDtypeStruct((n, N_PAIRS), F32)],
        compiler_params=_cparams(("parallel",)),
        name="peer_route",
    )(qp2d, k1_bf, k2_bf)


def _unpack(words):
    hi = lax.bitcast_convert_type(words & jnp.uint32(0xFFFF0000), F32)
    lo = lax.bitcast_convert_type(words << 16, F32)
    return hi, lo


def _expert_rows(tab_ref, row0):
    return tab_ref[pl.ds(pl.multiple_of(row0, ROWS_PER_EXPERT), ROWS_PER_EXPERT), :]


REDUCE_TOKENS = 16


def _lane_sums_to_rows(y, n_tok):
    hi = y.astype(BF16)
    lo = (y - hi.astype(F32)).astype(BF16)
    ones = jnp.ones((128, N_PAIRS), BF16)
    s = (_dot(hi, ones) + _dot(lo, ones)).reshape(n_tok, N_PAIRS, N_PAIRS)
    eye = lax.broadcasted_iota(jnp.int32, (N_PAIRS, N_PAIRS), 0) == lax.broadcasted_iota(jnp.int32, (N_PAIRS, N_PAIRS), 1)
    return jnp.sum(jnp.where(eye[None], s, 0.0), axis=1)


CHUNKS = D_MODEL // 128


def _peer_u_kernel(e_ref, h_ref, gate_ref, tab_ref, w_ref, h8_ref, prod_ref, ys_ref, act_ref, *, tt):
    for c in range(CHUNKS):
        h8_ref[pl.ds(c, tt, stride=CHUNKS), :] = h_ref[:, c * 128:(c + 1) * 128]

    def token(t, slot):
        base = pl.multiple_of(t * CHUNKS, CHUNKS)
        ha = h8_ref[pl.ds(base, ROWS_PER_EXPERT), :]
        hb = h8_ref[pl.ds(pl.multiple_of(base + ROWS_PER_EXPERT, ROWS_PER_EXPERT), ROWS_PER_EXPERT), :]
        prod = prod_ref.at[slot]
        for k in range(N_PAIRS):
            hi, lo = _unpack(_expert_rows(tab_ref, e_ref[t, k]))
            prod[k * ROWS_PER_EXPERT:(k + 1) * ROWS_PER_EXPERT, :] = hi * ha + lo * hb
        y = prod[pl.ds(0, N_PAIRS, stride=ROWS_PER_EXPERT), :]
        for c in range(1, ROWS_PER_EXPERT):
            y = y + prod[pl.ds(c, N_PAIRS, stride=ROWS_PER_EXPERT), :]
        ys_ref[pl.ds(pl.multiple_of(t * N_PAIRS, N_PAIRS), N_PAIRS), :] = y

    def two_tokens(j, carry):
        token(2 * j, 0)
        token(2 * j + 1, 1)
        return carry

    lax.fori_loop(0, tt // 2, two_tokens, 0)

    def group(g, carry):
        rows = REDUCE_TOKENS * N_PAIRS
        y = ys_ref[pl.ds(pl.multiple_of(g * rows, rows), rows), :]
        act_ref[pl.ds(pl.multiple_of(g * REDUCE_TOKENS, REDUCE_TOKENS), REDUCE_TOKENS), :] = _lane_sums_to_rows(y, REDUCE_TOKENS)
        return carry

    lax.fori_loop(0, tt // REDUCE_TOKENS, group, 0)
    a = act_ref[...]
    gelu = 0.5 * a * (1.0 + lax.erf(a * (2.0 ** -0.5)))
    w_ref[...] = gate_ref[...] * gelu


def _peer_u(e_t, h, gate_t, tab, tt):
    n = e_t.shape[0]
    assert tt % REDUCE_TOKENS == 0
    kern = functools.partial(_peer_u_kernel, tt=tt)
    pair = lambda **kw: pl.BlockSpec((tt, N_PAIRS), lambda i: (i, 0), **kw)
    return pl.pallas_call(
        kern,
        grid=(n // tt,),
        in_specs=[pair(memory_space=pltpu.SMEM),
                  pl.BlockSpec((tt, D_MODEL), lambda i: (i, 0)),
                  pair(),
                  pl.BlockSpec(tab.shape, lambda i: (0, 0), pipeline_mode=pl.Buffered(1))],
        out_specs=pair(),
        out_shape=jax.ShapeDtypeStruct((n, N_PAIRS), F32),
        scratch_shapes=[pltpu.VMEM((tt * CHUNKS, 128), F32),
                        pltpu.VMEM((2, N_PAIRS * ROWS_PER_EXPERT, 128), F32),
                        pltpu.VMEM((tt * N_PAIRS, 128), F32),
                        pltpu.VMEM((tt, N_PAIRS), F32)],
        compiler_params=_cparams(("arbitrary",)),
        name="peer_u",
    )(e_t, h, gate_t, tab)


def _peer_v_kernel(e_ref, w_ref, x_ref, g_ref, tab_ref, o_ref, acc_ref, *, tt):
    n_acc = 2

    def token(t, carry):
        acc_hi = [jnp.zeros((ROWS_PER_EXPERT, 128), F32) for _ in range(n_acc)]
        acc_lo = [jnp.zeros((ROWS_PER_EXPERT, 128), F32) for _ in range(n_acc)]
        for k in range(N_PAIRS):
            hi, lo = _unpack(_expert_rows(tab_ref, e_ref[t, k]))
            w = w_ref[t, k]
            acc_hi[k % n_acc] = acc_hi[k % n_acc] + w * hi
            acc_lo[k % n_acc] = acc_lo[k % n_acc] + w * lo
        base = pl.multiple_of(t * CHUNKS, CHUNKS)
        acc_ref[pl.ds(base, ROWS_PER_EXPERT), :] = sum(acc_hi[1:], acc_hi[0])
        acc_ref[pl.ds(pl.multiple_of(base + ROWS_PER_EXPERT, ROWS_PER_EXPERT), ROWS_PER_EXPERT), :] = sum(acc_lo[1:], acc_lo[0])
        return carry

    lax.fori_loop(0, tt, token, 0)
    _residual_rms_store(x_ref, acc_ref, g_ref, o_ref, tt)


def _residual_rms_store(x_ref, add_ref, g_ref, o_ref, tt):
    xs = [x_ref[:, c * 128:(c + 1) * 128] + add_ref[pl.ds(c, tt, stride=CHUNKS), :] for c in range(CHUNKS)]
    sq = xs[0] * xs[0]
    for x in xs[1:]:
        sq = sq + x * x
    r = lax.rsqrt(jnp.sum(sq, axis=1, keepdims=True) * (1.0 / D_MODEL) + EPS)
    for c in range(CHUNKS):
        o_ref[:, c * 128:(c + 1) * 128] = xs[c] * r * g_ref[:, c * 128:(c + 1) * 128]


def _peer_v(e_t, w_t, x, g_final, tab, tt, n_tokens):
    kern = functools.partial(_peer_v_kernel, tt=tt)
    pair = lambda: pl.BlockSpec((tt, N_PAIRS), lambda i: (i, 0), memory_space=pltpu.SMEM)
    row = lambda: pl.BlockSpec((tt, D_MODEL), lambda i: (i, 0))
    return pl.pallas_call(
        kern,
        grid=(n_tokens // tt,),
        in_specs=[pair(), pair(), row(),
                  pl.BlockSpec((1, D_MODEL), lambda i: (0, 0)),
                  pl.BlockSpec(tab.shape, lambda i: (0, 0), pipeline_mode=pl.Buffered(1))],
        out_specs=row(),
        out_shape=jax.ShapeDtypeStruct(x.shape, F32),
        scratch_shapes=[pltpu.VMEM((tt * CHUNKS, 128), F32)],
        compiler_params=_cparams(("arbitrary",)),
        name="peer_v",
    )(e_t, w_t, x, g_final, tab)


def _residual_norm_kernel(x_ref, add_ref, g_ref, y_any_ref, o_ref, *, tt):
    del y_any_ref
    _residual_rms_store(x_ref, add_ref, g_ref, o_ref, tt)


def _residual_norm(x, add8, g_final, y, first_token, tt):
    n_rows = add8.shape[0] // CHUNKS
    off = first_token // tt
    row = lambda: pl.BlockSpec((tt, D_MODEL), lambda i: (i + off, 0))
    return pl.pallas_call(
        functools.partial(_residual_norm_kernel, tt=tt),
        grid=(n_rows // tt,),
        in_specs=[row(), pl.BlockSpec((tt * CHUNKS, 128), lambda i: (i, 0)),
                  pl.BlockSpec((1, D_MODEL), lambda i: (0, 0)),
                  pl.BlockSpec(memory_space=pl.ANY)],
        out_specs=row(),
        out_shape=jax.ShapeDtypeStruct(y.shape, F32),
        input_output_aliases={3: 0},
        compiler_params=_cparams(("arbitrary",)),
        name="peer_residual_norm",
    )(x, add8, g_final, y)


SC_CORES = 2
SC_SUBCORES = 16
SC_LANES = 16
WORD_ROWS = 128 // SC_LANES
SC_TOKEN_CHUNK = 16
SC_RING = 4


def _peer_v_sc(e_t, w_t, tab):
    n = e_t.shape[0]
    workers = SC_CORES * SC_SUBCORES
    assert n % workers == 0
    per = n // workers
    mesh = plsc.VectorSubcoreMesh(core_axis_name="core", subcore_axis_name="subcore",
                                  num_cores=SC_CORES, num_subcores=SC_SUBCORES)

    assert per % SC_TOKEN_CHUNK == 0 and PEER_HEADS % SC_RING == 0
    head_rows = ROWS_PER_EXPERT * PEER_TOPK

    @pl.kernel(out_type=jax.ShapeDtypeStruct((n * CHUNKS, 128), F32), mesh=mesh,
               scratch_types=[pltpu.VMEM((SC_TOKEN_CHUNK, N_PAIRS), jnp.int32),
                              pltpu.VMEM((SC_TOKEN_CHUNK, N_PAIRS), F32),
                              pltpu.VMEM((SC_RING, head_rows, 128), jnp.uint32),
                              pltpu.VMEM((SC_TOKEN_CHUNK * CHUNKS, 128), F32),
                              pltpu.SemaphoreType.DMA((SC_RING,))],
               compiler_params=pltpu.CompilerParams(needs_layout_passes=False),
               name="peer_v_sc")
    def body(e_hbm, w_hbm, tab_hbm, o_hbm, e_v, w_v, rows_v, out_v, sems):
        wid = lax.axis_index("core") * SC_SUBCORES + lax.axis_index("subcore")

        def gathers(i, p, slot):
            first = e_v[i, pl.ds(p * PEER_TOPK, PEER_TOPK)]
            return [pltpu.make_async_copy(tab_hbm.at[first + r],
                                          rows_v.at[slot, pl.ds(r * PEER_TOPK, PEER_TOPK)], sems.at[slot])
                    for r in range(ROWS_PER_EXPERT)]

        def start_gather(i, p, slot):
            for d in gathers(i, p, slot):
                d.start()

        def accumulate(i, p, slot):
            tok = lax.broadcasted_iota(jnp.int32, (SC_LANES,), 0) * 0 + i
            col = lax.broadcasted_iota(jnp.int32, (SC_LANES,), 0) * 0 + p * PEER_TOPK
            ws = [plsc.load_gather(w_v, [tok, col + k]) for k in range(PEER_TOPK)]
            for r in range(ROWS_PER_EXPERT):
                @pl.loop(0, WORD_ROWS)
                def _(j):
                    sl = pl.ds(j * SC_LANES, SC_LANES)
                    if p == 0:
                        a_hi = jnp.zeros((SC_LANES,), F32)
                        a_lo = jnp.zeros((SC_LANES,), F32)
                    else:
                        a_hi = out_v[i * CHUNKS + r, sl]
                        a_lo = out_v[i * CHUNKS + ROWS_PER_EXPERT + r, sl]
                    for k in range(PEER_TOPK):
                        hi, lo = _unpack(rows_v[slot, r * PEER_TOPK + k, sl])
                        a_hi = a_hi + ws[k] * hi
                        a_lo = a_lo + ws[k] * lo
                    out_v[i * CHUNKS + r, sl] = a_hi
                    out_v[i * CHUNKS + ROWS_PER_EXPERT + r, sl] = a_lo

        @pl.loop(0, per // SC_TOKEN_CHUNK)
        def _(c):
            t0 = wid * per + c * SC_TOKEN_CHUNK
            pltpu.sync_copy(e_hbm.at[pl.ds(t0, SC_TOKEN_CHUNK)], e_v)
            pltpu.sync_copy(w_hbm.at[pl.ds(t0, SC_TOKEN_CHUNK)], w_v)
            for p in range(SC_RING):
                start_gather(0, p, p)

            @pl.loop(0, SC_TOKEN_CHUNK)
            def _(i):
                for p in range(PEER_HEADS):
                    slot = p % SC_RING
                    for d in gathers(i, p, slot):
                        d.wait()
                    accumulate(i, p, slot)
                    if p + SC_RING < PEER_HEADS:
                        start_gather(i, p + SC_RING, slot)
                    else:
                        @pl.when(i + 1 < SC_TOKEN_CHUNK)
                        def _():
                            start_gather(i + 1, p + SC_RING - PEER_HEADS, slot)

            pltpu.sync_copy(out_v, o_hbm.at[pl.ds(t0 * CHUNKS, SC_TOKEN_CHUNK * CHUNKS)])

    return body(e_t, w_t, tab)


def _pack_table(tab):
    bits = lax.bitcast_convert_type(tab.astype(BF16), jnp.uint16).astype(jnp.uint32)
    words = (bits[:, :HALF] << 16) | bits[:, HALF:]
    return words.reshape(tab.shape[0] * ROWS_PER_EXPERT, 128)


PEER_TOKENS = 128
SC_SHARE = 4
SC_MIN_TOKENS = 4096


def _sc_tokens(n):
    if n < SC_MIN_TOKENS:
        return 0
    unit = PEER_TOKENS * SC_CORES * SC_SUBCORES // math.gcd(PEER_TOKENS, SC_CORES * SC_SUBCORES)
    return (n // SC_SHARE) // unit * unit


def _row_tile(n, pref):
    while n % pref:
        pref //= 2
    return pref


def _peer_and_final(x2, h3, qp, k1_bf, k2_bf, tab_u, tab_v, g_final, tt, n_sc=0):
    n = x2.shape[0]
    e_t, gate_t = _route(qp, k1_bf, k2_bf, tt)
    w_t = _peer_u(e_t, h3, gate_t, tab_u, tt)
    y = _peer_v(e_t, w_t, x2, g_final, tab_v, tt, n - n_sc)
    if n_sc:
        add8 = _peer_v_sc(e_t[n - n_sc:], w_t[n - n_sc:], tab_v)
        y = _residual_norm(x2, add8, g_final, y, n - n_sc, tt)
    return y


def kernel(x_prompt, x_sample, mem_prompt, cache_da_k, cache_da_v, state_ret, cache_mem_k, cache_mem_v, g_mix, w_in, lam_q1, lam_k1, lam_q2, lam_k2, g_da, g_ret, w_out, g_cross, g_mem, w_xq, w_xk, w_xv, w_xo, g_ffn, w_pq, peer_k1, peer_k2, peer_u, peer_v, g_final):
    depth = w_in.shape[0]
    assert depth == 1, "single-layer step"
    l = 0
    lam_init = 0.8 - 0.6 * math.exp(-0.3 * l)
    b, t, _ = x_prompt.shape
    bs, ts, _ = x_sample.shape
    past_len = cache_da_k.shape[2]

    row = lambda a: a.reshape(1, -1)
    w_in_bf = w_in[l].astype(BF16)
    w_out_bf = w_out[l].astype(BF16)
    w_xq_bf, w_xk_bf, w_xv_bf, w_xo_bf = (w[l].astype(BF16) for w in (w_xq, w_xk, w_xv, w_xo))
    w_pq_bf = w_pq[l].astype(BF16)
    k1_bf, k2_bf = peer_k1[l].astype(BF16), peer_k2[l].astype(BF16)
    tab_u, tab_v = _pack_table(peer_u[l]), _pack_table(peer_v[l])
    lamp = jnp.stack([lam_q1[l], lam_k1[l], lam_q2[l], lam_k2[l]])
    g_da3 = g_da[l].reshape(H_D, 1, DV_D)
    g_ret3 = g_ret[l].reshape(H_R, 1, DV_R)
    g_fin = row(g_final)

    def mixer_tail(x2d, mda, mret, mk, mv, bb, tt_rows, n_sc):
        n = x2d.shape[0]
        tm = _row_tile(n, 512)
        x1, qx = _out_proj(x2d, mda.reshape(n, GROUP_W), mret.reshape(n, GROUP_W), w_out_bf, row(g_cross[l]), w_xq_bf, tm)
        x2, h3, qp = _cross(x1.reshape(bb, tt_rows, D_MODEL), qx.reshape(bb, tt_rows, D_MODEL), mk, mv,
                            w_xo_bf, row(g_ffn[l]), w_pq_bf, _row_tile(tt_rows, 512))
        y = _peer_and_final(x2.reshape(n, D_MODEL), h3.reshape(n, D_MODEL), qp.reshape(n, -1),
                            k1_bf, k2_bf, tab_u, tab_v, g_fin, PEER_TOKENS, n_sc)
        return y.reshape(bb, tt_rows, D_MODEL)

    n = b * t
    xp = x_prompt.reshape(n, D_MODEL)
    qd, kd, vd, kdb, vdb, qr, kr, vr, gr = _in_proj(xp, row(g_mix[l]), w_in_bf, _row_tile(n, 512))
    r3 = lambda a: a.reshape(b, t, GROUP_W)
    mda = _diff_attn_prompt(lamp, r3(qd), r3(kdb), r3(vdb), g_da3, lam_init, 512, 512)
    mret, s_fin = _retention(r3(qr), r3(kr), r3(vr), r3(gr), g_ret3, None, 256)
    mk, mv, mkb, mvb = _mem_kv(mem_prompt.reshape(b * N_MEM, D_MODEL), row(g_mem[l]), w_xk_bf, w_xv_bf, 512)
    y_prompt = mixer_tail(xp, mda, mret, mkb.reshape(b, N_MEM, D_MODEL), mvb.reshape(b, N_MEM, D_MODEL), b, t, _sc_tokens(n))

    ns = bs * ts
    xs = x_sample.reshape(ns, D_MODEL)
    qd_s, kd_s, vd_s, kdb_s, vdb_s, qr_s, kr_s, vr_s, gr_s = _in_proj(xs, row(g_mix[l]), w_in_bf, _row_tile(ns, 512))
    s3 = lambda a: a.reshape(bs, ts, GROUP_W)
    mda_s = _diff_attn_sample(lamp, s3(qd_s), cache_da_k[l].reshape(bs, past_len, GROUP_W),
                              cache_da_v[l].reshape(bs, past_len, GROUP_W), s3(kdb_s), s3(vdb_s), g_da3, lam_init)
    mret_s, s_new = _retention(s3(qr_s), s3(kr_s), s3(vr_s), s3(gr_s), g_ret3, state_ret[l], ts)
    y_sample = mixer_tail(xs, mda_s, mret_s, cache_mem_k[l].reshape(bs, N_MEM, D_MODEL),
                          cache_mem_v[l].reshape(bs, N_MEM, D_MODEL), bs, ts, _sc_tokens(ns))

    return (y_prompt, y_sample,
            kd.reshape(1, b, t, H_D, 2, DK_D), vd.reshape(1, b, t, H_D, DV_D), s_fin[None],
            mk.reshape(1, b, N_MEM, H_X, HD_X), mv.reshape(1, b, N_MEM, H_X, HD_X),
            kd_s.reshape(1, bs, ts, H_D, 2, DK_D), vd_s.reshape(1, bs, ts, H_D, DV_D), s_new[None])
```

```python
import functools
import math

import jax
import jax.numpy as jnp
from jax import lax
from jax.experimental import pallas as pl
from jax.experimental.pallas import tpu as pltpu
from jax.experimental.pallas import tpu_sc as plsc

D_MODEL = 1024
CHUNK = 64
CHUNK_SHIFT = CHUNK.bit_length() - 1
assert 1 << CHUNK_SHIFT == CHUNK
H_D, DK_D, DV_D = 4, 64, 128
H_R, DK_R, DV_R = 4, 128, 128
N_MEM = 256
H_X = 4
HD_X = D_MODEL // H_X
PEER_HEADS = 8
N_KEYS = 128
N_EXPERTS = N_KEYS * N_KEYS
PEER_TOPK = 16
EPS = 1e-6
HEAD_W = 128
GROUP_W = 512
N_PAIRS = PEER_HEADS * PEER_TOPK
HALF = D_MODEL // 2
ROWS_PER_EXPERT = HALF // 128
VMEM_LIMIT = 56 * 1024 * 1024

BF16 = jnp.bfloat16
F32 = jnp.float32


def _cparams(sem):
    return pltpu.CompilerParams(dimension_semantics=sem, vmem_limit_bytes=VMEM_LIMIT)


def _rms(x, g):
    return x * lax.rsqrt(jnp.mean(x * x, axis=-1, keepdims=True) + EPS) * g


def _dot(a, b):
    return jnp.dot(a, b, preferred_element_type=F32)


def _dot_nt(a, b):
    return lax.dot_general(a, b, (((1,), (1,)), ((), ())), preferred_element_type=F32)


def _dot_tn(a, b):
    return lax.dot_general(a, b, (((0,), (0,)), ((), ())), preferred_element_type=F32)


def _select_by_head(h, values):
    out = jnp.float32(values[-1])
    for i in range(len(values) - 2, -1, -1):
        out = jnp.where(h == i, jnp.float32(values[i]), out)
    return out


def _in_proj_kernel(x_ref, g_ref, w_ref, qd_ref, kd_ref, vd_ref, kdb_ref, vdb_ref,
                    qr_ref, kr_ref, vr_ref, gr_ref):
    hb = _rms(x_ref[...], g_ref[...]).astype(BF16)
    col = lambda c: _dot(hb, w_ref[:, c * GROUP_W:(c + 1) * GROUP_W])
    qd_ref[...] = (col(0) * (DK_D ** -0.5)).astype(BF16)
    kd = col(1)
    kd_ref[...] = kd
    kdb_ref[...] = kd.astype(BF16)
    vd = col(2)
    vd_ref[...] = vd
    vdb_ref[...] = vd.astype(BF16)
    qr_ref[...] = col(3).astype(BF16)
    kr_ref[...] = (col(4) * (DK_R ** -0.5)).astype(BF16)
    vr_ref[...] = col(5).astype(BF16)
    gr_ref[...] = col(6)


def _in_proj(x2d, g, w_bf, tm):
    n = x2d.shape[0]
    blk = lambda: pl.BlockSpec((tm, GROUP_W), lambda i: (i, 0))
    sh = lambda dt: jax.ShapeDtypeStruct((n, GROUP_W), dt)
    return pl.pallas_call(
        _in_proj_kernel,
        grid=(n // tm,),
        in_specs=[pl.BlockSpec((tm, D_MODEL), lambda i: (i, 0)),
                  pl.BlockSpec((1, D_MODEL), lambda i: (0, 0)),
                  pl.BlockSpec(w_bf.shape, lambda i: (0, 0))],
        out_specs=[blk() for _ in range(9)],
        out_shape=[sh(BF16), sh(F32), sh(F32), sh(BF16), sh(BF16), sh(BF16), sh(BF16), sh(BF16), sh(F32)],
        compiler_params=_cparams(("parallel",)),
        name="in_proj",
    )(x2d, g, w_bf)


def _lambda_from(lam_ref, lam_init):
    l = lam_ref[...]
    a = jnp.exp(jnp.sum(l[0:1] * l[1:2], axis=-1, keepdims=True))
    b = jnp.exp(jnp.sum(l[2:3] * l[3:4], axis=-1, keepdims=True))
    return a - b + lam_init


def _diff_post(acc, l, lam, g, lam_init, tq):
    o = acc[:tq] / l[:tq] - lam * (acc[tq:] / l[tq:])
    return o * lax.rsqrt(jnp.mean(o * o, axis=-1, keepdims=True) + EPS) * g * (1.0 - lam_init)


def _split_maps(q):
    lane = lax.broadcasted_iota(jnp.int32, q.shape, 1)
    zero = jnp.zeros_like(q)
    return jnp.concatenate([jnp.where(lane < DK_D, q, zero), jnp.where(lane >= DK_D, q, zero)], axis=0)


def _da_prompt_kernel(lam_ref, q_ref, k_ref, v_ref, g_ref, o_ref, kx_ref, vx_ref, own_ref, acc_ref, m_ref, *, lam_init, tq, tk):
    h = pl.program_id(1)
    i = pl.program_id(2)
    t = k_ref.shape[0]
    slope = _select_by_head(h, [2.0 ** (-8.0 * (j + 1) / H_D) for j in range(H_D)])

    @pl.when(i == 0)
    def _():
        pos = lax.broadcasted_iota(jnp.int32, (t, HEAD_W), 0)
        lane = lax.broadcasted_iota(jnp.int32, (t, HEAD_W), 1)
        coarse = ((pos >> CHUNK_SHIFT) << CHUNK_SHIFT).astype(F32) * slope
        fine = (pos & (CHUNK - 1)).astype(F32) * slope
        kx_ref[:, :HEAD_W] = k_ref[...]
        kx_ref[:, HEAD_W:] = jnp.where(lane == 0, coarse, jnp.where(lane == 1, fine, 0.0)).astype(BF16)
        vx_ref[:DV_D, :] = v_ref[...].astype(F32).T.astype(BF16)
        vx_ref[DV_D:, :] = jnp.ones((vx_ref.shape[0] - DV_D, t), BF16)
        krel = lax.broadcasted_iota(jnp.int32, (tk, 2 * tq), 0)
        c = lax.broadcasted_iota(jnp.int32, (tk, 2 * tq), 1)
        for par in range(tk // tq):
            qrel = par * tq + jnp.where(c >= tq, c - tq, c)
            ahead = (2.0 * slope) * jnp.maximum(krel - qrel, 0).astype(F32)
            own_ref[par] = jnp.where((qrel >> CHUNK_SHIFT) >= (krel >> CHUNK_SHIFT), -ahead, -1e30)

    q = q_ref[...]
    lane = lax.broadcasted_iota(jnp.int32, q.shape, 1)
    zero = jnp.zeros_like(q)
    ones2 = jnp.where(lane < 2, 1.0, 0.0).astype(BF16)
    q2 = jnp.concatenate([jnp.concatenate([jnp.where(lane < DK_D, q, zero), ones2], axis=1),
                          jnp.concatenate([jnp.where(lane >= DK_D, q, zero), ones2], axis=1)], axis=0)
    jd = (i * tq) // tk

    def scores(j):
        return _dot_nt(kx_ref[pl.ds(pl.multiple_of(j * tk, tk), tk), :], q2)

    def values(j):
        return vx_ref[:, pl.ds(pl.multiple_of(j * tk, tk), tk)]

    s = scores(jd) + own_ref[(i * tq) % tk // tq]
    m0 = jnp.max(s, axis=0, keepdims=True)
    m_ref[...] = m0
    acc_ref[...] = _dot(values(jd), jnp.exp(s - m0).astype(BF16))

    def absorb(blocks):
        ss = [scores(j) for j in blocks]
        m_old = m_ref[...]
        m_new = m_old
        for s in ss:
            m_new = jnp.maximum(m_new, jnp.max(s, axis=0, keepdims=True))
        m_ref[...] = m_new
        acc = jnp.exp(m_old - m_new) * acc_ref[...]
        for j, s in zip(blocks, ss):
            acc = acc + _dot(values(j), jnp.exp(s - m_new).astype(BF16))
        acc_ref[...] = acc

    def past_pair(jj, carry):
        absorb([2 * jj, 2 * jj + 1])
        return carry

    lax.fori_loop(0, jd // 2, past_pair, 0)

    @pl.when(jd % 2 == 1)
    def _():
        absorb([jd - 1])

    acc = acc_ref[...]
    num, den = acc[:DV_D], acc[DV_D:DV_D + 1]
    lam = _lambda_from(lam_ref, lam_init)
    o = (num[:, :tq] / den[:, :tq] - lam * (num[:, tq:] / den[:, tq:])).T
    o = o * lax.rsqrt(jnp.mean(o * o, axis=-1, keepdims=True) + EPS) * g_ref[...] * (1.0 - lam_init)
    o_ref[...] = o.astype(o_ref.dtype)


ONES_ROWS = 16


def _diff_attn_prompt(lamp, q, k, v, g_da3, lam_init, tq, tk):
    b, t, _ = q.shape
    kern = functools.partial(_da_prompt_kernel, lam_init=lam_init, tq=tq, tk=tk)
    return pl.pallas_call(
        kern,
        grid=(b, H_D, t // tq),
        in_specs=[pl.BlockSpec((4, DK_D), lambda b_, h, i: (0, 0)),
                  pl.BlockSpec((None, tq, HEAD_W), lambda b_, h, i: (b_, i, h)),
                  pl.BlockSpec((None, t, HEAD_W), lambda b_, h, i: (b_, 0, h)),
                  pl.BlockSpec((None, t, HEAD_W), lambda b_, h, i: (b_, 0, h)),
                  pl.BlockSpec((None, 1, HEAD_W), lambda b_, h, i: (h, 0, 0))],
        out_specs=pl.BlockSpec((None, tq, HEAD_W), lambda b_, h, i: (b_, i, h)),
        out_shape=jax.ShapeDtypeStruct((b, t, GROUP_W), BF16),
        scratch_shapes=[pltpu.VMEM((t, 2 * HEAD_W), BF16), pltpu.VMEM((DV_D + ONES_ROWS, t), BF16),
                        pltpu.VMEM((tk // tq, tk, 2 * tq), F32),
                        pltpu.VMEM((DV_D + ONES_ROWS, 2 * tq), F32), pltpu.VMEM((1, 2 * tq), F32)],
        compiler_params=_cparams(("parallel", "parallel", "arbitrary")),
        name="diff_attn_prompt",
    )(lamp, q, k, v, g_da3)


def _da_sample_kernel(lam_ref, q_ref, kc_ref, vc_ref, kn_ref, vn_ref, g_ref, o_ref, *, lam_init, ts, past_len):
    h = pl.program_id(1)
    slope = _select_by_head(h, [2.0 ** (-8.0 * (j + 1) / H_D) for j in range(H_D)])
    q2 = _split_maps(q_ref[...])
    rows = 2 * ts

    def scores(k, base, n):
        r = lax.broadcasted_iota(jnp.int32, (rows, n), 0)
        qpos = past_len + jnp.where(r >= ts, r - ts, r)
        kpos = base + lax.broadcasted_iota(jnp.int32, (rows, n), 1)
        return _dot_nt(q2, k) - slope * jnp.abs(qpos - kpos).astype(F32)

    sc = scores(kc_ref[...].astype(BF16), 0, past_len)
    sn = scores(kn_ref[...], past_len, ts)
    m = jnp.maximum(jnp.max(sc, axis=-1, keepdims=True), jnp.max(sn, axis=-1, keepdims=True))
    pc = jnp.exp(sc - m)
    pn = jnp.exp(sn - m)
    l = jnp.sum(pc, axis=-1, keepdims=True) + jnp.sum(pn, axis=-1, keepdims=True)
    acc = _dot(pc.astype(BF16), vc_ref[...].astype(BF16)) + _dot(pn.astype(BF16), vn_ref[...])
    lam = _lambda_from(lam_ref, lam_init)
    o_ref[...] = _diff_post(acc, l, lam, g_ref[...], lam_init, ts).astype(o_ref.dtype)


def _diff_attn_sample(lamp, q, kc, vc, kn, vn, g_da3, lam_init):
    b, ts, _ = q.shape
    past_len = kc.shape[1]
    kern = functools.partial(_da_sample_kernel, lam_init=lam_init, ts=ts, past_len=past_len)
    head = lambda rows: pl.BlockSpec((None, rows, HEAD_W), lambda b_, h: (b_, 0, h))
    return pl.pallas_call(
        kern,
        grid=(b, H_D),
        in_specs=[pl.BlockSpec((4, DK_D), lambda b_, h: (0, 0)),
                  head(ts), head(past_len), head(past_len), head(ts), head(ts),
                  pl.BlockSpec((None, 1, HEAD_W), lambda b_, h: (h, 0, 0))],
        out_specs=head(ts),
        out_shape=jax.ShapeDtypeStruct((b, ts, GROUP_W), BF16),
        compiler_params=_cparams(("parallel", "parallel")),
        name="diff_attn_sample",
    )(lamp, q, kc, vc, kn, vn, g_da3)


def _ret_kernel(*refs, lb, has_init):
    if has_init:
        q_ref, k_ref, v_ref, gate_ref, g_ref, s0_ref, o_ref, sfin_ref, s_ref = refs
    else:
        q_ref, k_ref, v_ref, gate_ref, g_ref, o_ref, sfin_ref, s_ref = refs
    h = pl.program_id(1)
    c = pl.program_id(2)
    lg = _select_by_head(h, [math.log1p(-(2.0 ** (-5.0 - j))) for j in range(H_R)])

    @pl.when(c == 0)
    def _():
        s_ref[...] = s0_ref[...] if has_init else jnp.zeros_like(s_ref)

    q, k, v = q_ref[...], k_ref[...], v_ref[...]
    i = lax.broadcasted_iota(jnp.int32, (lb, lb), 0)
    j = lax.broadcasted_iota(jnp.int32, (lb, lb), 1)
    d = (i - j).astype(F32)
    decay = jnp.where(d >= 0, jnp.exp(jnp.maximum(d, 0.0) * lg), 0.0)
    inner = _dot_nt(q, k) * decay
    ic = lax.broadcasted_iota(jnp.int32, (lb, 1), 0).astype(F32)
    s_old = s_ref[...]
    o = _dot(inner.astype(BF16), v) + _dot(q, s_old.astype(BF16)) * jnp.exp((ic + 1.0) * lg)
    tail = jnp.exp((lb - 1.0 - ic) * lg)
    kt = (k.astype(F32) * tail).astype(BF16)
    s_new = jnp.exp(lb * lg) * s_old + _dot_tn(kt, v)
    s_ref[...] = s_new

    @pl.when(c == pl.num_programs(2) - 1)
    def _():
        sfin_ref[...] = s_new

    oc = o - jnp.mean(o, axis=-1, keepdims=True)
    y = oc * lax.rsqrt(jnp.mean(oc * oc, axis=-1, keepdims=True) + EPS) * g_ref[...]
    gate = gate_ref[...]
    o_ref[...] = (y * (gate * jax.nn.sigmoid(gate))).astype(o_ref.dtype)


def _retention(q, k, v, gate, g_ret3, s0, lb):
    b, t, _ = q.shape
    has_init = s0 is not None
    kern = functools.partial(_ret_kernel, lb=lb, has_init=has_init)
    head = lambda: pl.BlockSpec((None, lb, HEAD_W), lambda b_, h, c: (b_, c, h))
    state = lambda: pl.BlockSpec((None, None, DK_R, DV_R), lambda b_, h, c: (b_, h, 0, 0))
    in_specs = [head(), head(), head(), head(), pl.BlockSpec((None, 1, HEAD_W), lambda b_, h, c: (h, 0, 0))]
    args = [q, k, v, gate, g_ret3]
    if has_init:
        in_specs.append(state())
        args.append(s0)
    return pl.pallas_call(
        kern,
        grid=(b, H_R, t // lb),
        in_specs=in_specs,
        out_specs=[head(), state()],
        out_shape=[jax.ShapeDtypeStruct((b, t, GROUP_W), BF16), jax.ShapeDtypeStruct((b, H_R, DK_R, DV_R), F32)],
        scratch_shapes=[pltpu.VMEM((DK_R, DV_R), F32)],
        compiler_params=_cparams(("parallel", "parallel", "arbitrary")),
        name="retention",
    )(*args)


def _out_proj_kernel(x_ref, mda_ref, mret_ref, wo_ref, g_ref, wq_ref, x1_ref, qx_ref):
    x1 = x_ref[...] + _dot(mda_ref[...], wo_ref[:GROUP_W, :]) + _dot(mret_ref[...], wo_ref[GROUP_W:, :])
    x1_ref[...] = x1
    hn = _rms(x1, g_ref[...]).astype(BF16)
    qx_ref[...] = (_dot(hn, wq_ref[...]) * (HD_X ** -0.5)).astype(BF16)


def _out_proj(x2d, mda, mret, wo_bf, g_cross, wq_bf, tm):
    n = x2d.shape[0]
    full = lambda a: pl.BlockSpec(a.shape, lambda i: (0, 0))
    return pl.pallas_call(
        _out_proj_kernel,
        grid=(n // tm,),
        in_specs=[pl.BlockSpec((tm, D_MODEL), lambda i: (i, 0)),
                  pl.BlockSpec((tm, GROUP_W), lambda i: (i, 0)),
                  pl.BlockSpec((tm, GROUP_W), lambda i: (i, 0)),
                  full(wo_bf), full(g_cross), full(wq_bf)],
        out_specs=[pl.BlockSpec((tm, D_MODEL), lambda i: (i, 0)), pl.BlockSpec((tm, D_MODEL), lambda i: (i, 0))],
        out_shape=[jax.ShapeDtypeStruct((n, D_MODEL), F32), jax.ShapeDtypeStruct((n, D_MODEL), BF16)],
        compiler_params=_cparams(("parallel",)),
        name="out_proj",
    )(x2d, mda, mret, wo_bf, g_cross, wq_bf)


def _mem_kv_kernel(m_ref, g_ref, wk_ref, wv_ref, mk_ref, mv_ref, mkb_ref, mvb_ref):
    mn = _rms(m_ref[...], g_ref[...]).astype(BF16)
    mk = _dot(mn, wk_ref[...])
    mv = _dot(mn, wv_ref[...])
    mk_ref[...] = mk
    mv_ref[...] = mv
    mkb_ref[...] = mk.astype(BF16)
    mvb_ref[...] = mv.astype(BF16)


def _mem_kv(mem2d, g_mem, wk_bf, wv_bf, tm):
    n = mem2d.shape[0]
    row = lambda: pl.BlockSpec((tm, D_MODEL), lambda i: (i, 0))
    full = lambda a: pl.BlockSpec(a.shape, lambda i: (0, 0))
    sh = lambda dt: jax.ShapeDtypeStruct((n, D_MODEL), dt)
    return pl.pallas_call(
        _mem_kv_kernel,
        grid=(n // tm,),
        in_specs=[row(), full(g_mem), full(wk_bf), full(wv_bf)],
        out_specs=[row(), row(), row(), row()],
        out_shape=[sh(F32), sh(F32), sh(BF16), sh(BF16)],
        compiler_params=_cparams(("parallel",)),
        name="mem_kv",
    )(mem2d, g_mem, wk_bf, wv_bf)


def _cross_kernel(x1_ref, qx_ref, mk_ref, mv_ref, wo_ref, g_ref, wpq_ref, x2_ref, h3_ref, qp_ref):
    q = qx_ref[...]
    heads = []
    for h in range(H_X):
        sl = slice(h * HD_X, (h + 1) * HD_X)
        kh = mk_ref[:, sl] if mk_ref.dtype == BF16 else mk_ref[:, sl].astype(BF16)
        vh = mv_ref[:, sl] if mv_ref.dtype == BF16 else mv_ref[:, sl].astype(BF16)
        s = _dot_nt(q[:, sl], kh)
        p = jnp.exp(s - jnp.max(s, axis=-1, keepdims=True))
        heads.append(_dot(p.astype(BF16), vh) / jnp.sum(p, axis=-1, keepdims=True))
    o = jnp.concatenate(heads, axis=-1).astype(BF16)
    x2 = x1_ref[...] + _dot(o, wo_ref[...])
    x2_ref[...] = x2
    h3 = _rms(x2, g_ref[...])
    h3_ref[...] = h3
    qp_ref[...] = _dot(h3.astype(BF16), wpq_ref[...]).astype(BF16)


def _cross(x1, qx, mk, mv, wo_bf, g_ffn, wpq_bf, tm):
    b, t, _ = x1.shape
    row = lambda w: pl.BlockSpec((None, tm, w), lambda b_, i: (b_, i, 0))
    mem = lambda: pl.BlockSpec((None, N_MEM, D_MODEL), lambda b_, i: (b_, 0, 0))
    full = lambda a: pl.BlockSpec(a.shape, lambda b_, i: (0, 0))
    dq = wpq_bf.shape[1]
    return pl.pallas_call(
        _cross_kernel,
        grid=(b, t // tm),
        in_specs=[row(D_MODEL), row(D_MODEL), mem(), mem(), full(wo_bf), full(g_ffn), full(wpq_bf)],
        out_specs=[row(D_MODEL), row(D_MODEL), row(dq)],
        out_shape=[jax.ShapeDtypeStruct((b, t, D_MODEL), F32), jax.ShapeDtypeStruct((b, t, D_MODEL), F32),
                   jax.ShapeDtypeStruct((b, t, dq), BF16)],
        compiler_params=_cparams(("parallel", "parallel")),
        name="cross_attn",
    )(x1, qx, mk, mv, wo_bf, g_ffn, wpq_bf)


ID_PAD = 2.0 ** 29


def _topk_rows(s, ids):
    vals, sel = [], []
    for _ in range(PEER_TOPK):
        m = jnp.max(s, axis=0, keepdims=True)
        idx = jnp.min(jnp.where(s == m, ids, ID_PAD), axis=0, keepdims=True)
        vals.append(m)
        sel.append(idx)
        s = jnp.where(ids == idx, -jnp.inf, s)
    return jnp.concatenate(vals, axis=0), jnp.concatenate(sel, axis=0)


def _candidates(v1, i1, v2, i2):
    lanes = v1.shape[1]
    b8 = lax.broadcasted_iota(jnp.int32, (8, lanes), 0)
    ident = lambda a, ia, ib, b: (a * PEER_TOPK + b) * float(N_EXPERTS) + (ia * float(N_KEYS) + ib)
    b16 = lax.broadcasted_iota(jnp.int32, (PEER_TOPK, lanes), 0).astype(F32)
    vals = [v1[0:1] + v2]
    ids = [ident(0, i1[0:1], i2, b16)]
    for a in range(1, 8):
        keep = b8 < PEER_TOPK // (a + 1)
        vals.append(jnp.where(keep, v1[a:a + 1] + v2[0:8], -jnp.inf))
        ids.append(jnp.where(keep, ident(a, i1[a:a + 1], i2[0:8], b8.astype(F32)), ID_PAD))
    a_hi = (b8 + 8).astype(F32)
    vals.append(v1[8:16] + v2[0:1])
    ids.append(ident(a_hi, i1[8:16], i2[0:1], 0.0))
    return jnp.concatenate(vals, axis=0), jnp.concatenate(ids, axis=0)


def _route_kernel(qp_ref, k1_ref, k2_ref, e_ref, g_ref):
    half = N_KEYS
    tt = qp_ref.shape[0]
    key_id = lax.broadcasted_iota(jnp.int32, (N_KEYS, tt), 0).astype(F32)
    es, gs = [], []
    for p in range(PEER_HEADS):
        q1 = qp_ref[:, (2 * p) * half:(2 * p + 1) * half]
        q2 = qp_ref[:, (2 * p + 1) * half:(2 * p + 2) * half]
        v1, i1 = _topk_rows(_dot_nt(k1_ref[p], q1), key_id)
        v2, i2 = _topk_rows(_dot_nt(k2_ref[p], q2), key_id)
        sc, sel = _topk_rows(*_candidates(v1, i1, v2, i2))
        w = jnp.exp(sc - sc[0:1])
        es.append((sel.astype(jnp.int32) & (N_EXPERTS - 1)) * ROWS_PER_EXPERT)
        gs.append(w / jnp.sum(w, axis=0, keepdims=True))
    e_ref[...] = jnp.concatenate(es, axis=0).T
    g_ref[...] = jnp.concatenate(gs, axis=0).T


def _route(qp2d, k1_bf, k2_bf, tt):
    n = qp2d.shape[0]
    full = lambda a: pl.BlockSpec(a.shape, lambda i: (0, 0, 0))
    return pl.pallas_call(
        _route_kernel,
        grid=(n // tt,),
        in_specs=[pl.BlockSpec((tt, qp2d.shape[1]), lambda i: (i, 0)), full(k1_bf), full(k2_bf)],
        out_specs=[pl.BlockSpec((tt, N_PAIRS), lambda i: (i, 0)), pl.BlockSpec((tt, N_PAIRS), lambda i: (i, 0))],
        out_shape=[jax.ShapeDtypeStruct((n, N_PAIRS), jnp.int32), jax.ShapeDtypeStruct((n, N_PAIRS), F32)],
        compiler_params=_cparams(("parallel",)),
        name="peer_route",
    )(qp2d, k1_bf, k2_bf)


def _unpack(words):
    hi = lax.bitcast_convert_type(words & jnp.uint32(0xFFFF0000), F32)
    lo = lax.bitcast_convert_type(words << 16, F32)
    return hi, lo


def _expert_rows(tab_ref, row0):
    return tab_ref[pl.ds(pl.multiple_of(row0, ROWS_PER_EXPERT), ROWS_PER_EXPERT), :]


REDUCE_TOKENS = 16


def _lane_sums_to_rows(y, n_tok):
    hi = y.astype(BF16)
    lo = (y - hi.astype(F32)).astype(BF16)
    ones = jnp.ones((128, N_PAIRS), BF16)
    s = (_dot(hi, ones) + _dot(lo, ones)).reshape(n_tok, N_PAIRS, N_PAIRS)
    eye = lax.broadcasted_iota(jnp.int32, (N_PAIRS, N_PAIRS), 0) == lax.broadcasted_iota(jnp.int32, (N_PAIRS, N_PAIRS), 1)
    return jnp.sum(jnp.where(eye[None], s, 0.0), axis=1)


CHUNKS = D_MODEL // 128


def _peer_u_kernel(e_ref, h_ref, gate_ref, tab_ref, w_ref, h8_ref, prod_ref, ys_ref, act_ref, *, tt):
    for c in range(CHUNKS):
        h8_ref[pl.ds(c, tt, stride=CHUNKS), :] = h_ref[:, c * 128:(c + 1) * 128]

    def token(t, slot):
        base = pl.multiple_of(t * CHUNKS, CHUNKS)
        ha = h8_ref[pl.ds(base, ROWS_PER_EXPERT), :]
        hb = h8_ref[pl.ds(pl.multiple_of(base + ROWS_PER_EXPERT, ROWS_PER_EXPERT), ROWS_PER_EXPERT), :]
        prod = prod_ref.at[slot]
        for k in range(N_PAIRS):
            hi, lo = _unpack(_expert_rows(tab_ref, e_ref[t, k]))
            prod[k * ROWS_PER_EXPERT:(k + 1) * ROWS_PER_EXPERT, :] = hi * ha + lo * hb
        y = prod[pl.ds(0, N_PAIRS, stride=ROWS_PER_EXPERT), :]
        for c in range(1, ROWS_PER_EXPERT):
            y = y + prod[pl.ds(c, N_PAIRS, stride=ROWS_PER_EXPERT), :]
        ys_ref[pl.ds(pl.multiple_of(t * N_PAIRS, N_PAIRS), N_PAIRS), :] = y

    def two_tokens(j, carry):
        token(2 * j, 0)
        token(2 * j + 1, 1)
        return carry

    lax.fori_loop(0, tt // 2, two_tokens, 0)

    def group(g, carry):
        rows = REDUCE_TOKENS * N_PAIRS
        y = ys_ref[pl.ds(pl.multiple_of(g * rows, rows), rows), :]
        act_ref[pl.ds(pl.multiple_of(g * REDUCE_TOKENS, REDUCE_TOKENS), REDUCE_TOKENS), :] = _lane_sums_to_rows(y, REDUCE_TOKENS)
        return carry

    lax.fori_loop(0, tt // REDUCE_TOKENS, group, 0)
    w_ref[...] = _gated_gelu(gate_ref[...], act_ref[...])


def _gated_gelu(gate, a):
    return gate * (0.5 * a * (1.0 + lax.erf(a * (2.0 ** -0.5))))


def _gate_act_kernel(act_ref, gate_ref, w_any_ref, o_ref):
    del w_any_ref
    o_ref[...] = _gated_gelu(gate_ref[...], act_ref[...])


def _gate_act(act, gate_t, w_t, first_token, tt):
    off = first_token // tt
    return pl.pallas_call(
        _gate_act_kernel,
        grid=(act.shape[0] // tt,),
        in_specs=[pl.BlockSpec((tt, N_PAIRS), lambda i: (i, 0)),
                  pl.BlockSpec((tt, N_PAIRS), lambda i: (i + off, 0)),
                  pl.BlockSpec(memory_space=pl.ANY)],
        out_specs=pl.BlockSpec((tt, N_PAIRS), lambda i: (i + off, 0)),
        out_shape=jax.ShapeDtypeStruct(w_t.shape, F32),
        input_output_aliases={2: 0},
        compiler_params=_cparams(("arbitrary",)),
        name="peer_gate_act",
    )(act, gate_t, w_t)


def _peer_u(e_t, h, gate_t, tab, tt, n_tokens):
    n = e_t.shape[0]
    assert tt % REDUCE_TOKENS == 0
    kern = functools.partial(_peer_u_kernel, tt=tt)
    pair = lambda **kw: pl.BlockSpec((tt, N_PAIRS), lambda i: (i, 0), **kw)
    return pl.pallas_call(
        kern,
        grid=(n_tokens // tt,),
        in_specs=[pair(memory_space=pltpu.SMEM),
                  pl.BlockSpec((tt, D_MODEL), lambda i: (i, 0)),
                  pair(),
                  pl.BlockSpec(tab.shape, lambda i: (0, 0), pipeline_mode=pl.Buffered(1))],
        out_specs=pair(),
        out_shape=jax.ShapeDtypeStruct((n, N_PAIRS), F32),
        scratch_shapes=[pltpu.VMEM((tt * CHUNKS, 128), F32),
                        pltpu.VMEM((2, N_PAIRS * ROWS_PER_EXPERT, 128), F32),
                        pltpu.VMEM((tt * N_PAIRS, 128), F32),
                        pltpu.VMEM((tt, N_PAIRS), F32)],
        compiler_params=_cparams(("arbitrary",)),
        name="peer_u",
    )(e_t, h, gate_t, tab)


def _peer_v_kernel(e_ref, w_ref, x_ref, g_ref, tab_ref, o_ref, acc_ref, *, tt):
    n_acc = 2

    def token(t, carry):
        acc_hi = [jnp.zeros((ROWS_PER_EXPERT, 128), F32) for _ in range(n_acc)]
        acc_lo = [jnp.zeros((ROWS_PER_EXPERT, 128), F32) for _ in range(n_acc)]
        for k in range(N_PAIRS):
            hi, lo = _unpack(_expert_rows(tab_ref, e_ref[t, k]))
            w = w_ref[t, k]
            acc_hi[k % n_acc] = acc_hi[k % n_acc] + w * hi
            acc_lo[k % n_acc] = acc_lo[k % n_acc] + w * lo
        base = pl.multiple_of(t * CHUNKS, CHUNKS)
        acc_ref[pl.ds(base, ROWS_PER_EXPERT), :] = sum(acc_hi[1:], acc_hi[0])
        acc_ref[pl.ds(pl.multiple_of(base + ROWS_PER_EXPERT, ROWS_PER_EXPERT), ROWS_PER_EXPERT), :] = sum(acc_lo[1:], acc_lo[0])
        return carry

    lax.fori_loop(0, tt, token, 0)
    _residual_rms_store(x_ref, acc_ref, g_ref, o_ref, tt)


def _residual_rms_store(x_ref, add_ref, g_ref, o_ref, tt):
    xs = [x_ref[:, c * 128:(c + 1) * 128] + add_ref[pl.ds(c, tt, stride=CHUNKS), :] for c in range(CHUNKS)]
    sq = xs[0] * xs[0]
    for x in xs[1:]:
        sq = sq + x * x
    r = lax.rsqrt(jnp.sum(sq, axis=1, keepdims=True) * (1.0 / D_MODEL) + EPS)
    for c in range(CHUNKS):
        o_ref[:, c * 128:(c + 1) * 128] = xs[c] * r * g_ref[:, c * 128:(c + 1) * 128]


def _peer_v(e_t, w_t, x, g_final, tab, tt, n_tokens):
    kern = functools.partial(_peer_v_kernel, tt=tt)
    pair = lambda: pl.BlockSpec((tt, N_PAIRS), lambda i: (i, 0), memory_space=pltpu.SMEM)
    row = lambda: pl.BlockSpec((tt, D_MODEL), lambda i: (i, 0))
    return pl.pallas_call(
        kern,
        grid=(n_tokens // tt,),
        in_specs=[pair(), pair(), row(),
                  pl.BlockSpec((1, D_MODEL), lambda i: (0, 0)),
                  pl.BlockSpec(tab.shape, lambda i: (0, 0), pipeline_mode=pl.Buffered(1))],
        out_specs=row(),
        out_shape=jax.ShapeDtypeStruct(x.shape, F32),
        scratch_shapes=[pltpu.VMEM((tt * CHUNKS, 128), F32)],
        compiler_params=_cparams(("arbitrary",)),
        name="peer_v",
    )(e_t, w_t, x, g_final, tab)


def _residual_norm_kernel(x_ref, add_ref, g_ref, y_any_ref, o_ref, *, tt):
    del y_any_ref
    _residual_rms_store(x_ref, add_ref, g_ref, o_ref, tt)


def _residual_norm(x, add8, g_final, y, first_token, tt):
    n_rows = add8.shape[0] // CHUNKS
    off = first_token // tt
    row = lambda: pl.BlockSpec((tt, D_MODEL), lambda i: (i + off, 0))
    return pl.pallas_call(
        functools.partial(_residual_norm_kernel, tt=tt),
        grid=(n_rows // tt,),
        in_specs=[row(), pl.BlockSpec((tt * CHUNKS, 128), lambda i: (i, 0)),
                  pl.BlockSpec((1, D_MODEL), lambda i: (0, 0)),
                  pl.BlockSpec(memory_space=pl.ANY)],
        out_specs=row(),
        out_shape=jax.ShapeDtypeStruct(y.shape, F32),
        input_output_aliases={3: 0},
        compiler_params=_cparams(("arbitrary",)),
        name="peer_residual_norm",
    )(x, add8, g_final, y)


SC_CORES = 2
SC_SUBCORES = 16
SC_LANES = 16
WORD_ROWS = 128 // SC_LANES
SC_TOKEN_CHUNK = 16
SC_RING = 4


def _peer_sc(e_t, aux, tab, down):
    n = e_t.shape[0]
    workers = SC_CORES * SC_SUBCORES
    assert n % (workers * SC_TOKEN_CHUNK) == 0 and PEER_HEADS % SC_RING == 0
    per = n // workers
    head_rows = ROWS_PER_EXPERT * PEER_TOPK
    aux_rows = CHUNKS if down else 1
    out_rows = 1 if down else CHUNKS
    mesh = plsc.VectorSubcoreMesh(core_axis_name="core", subcore_axis_name="subcore",
                                  num_cores=SC_CORES, num_subcores=SC_SUBCORES)

    @pl.kernel(out_type=jax.ShapeDtypeStruct((n * out_rows, 128), F32), mesh=mesh,
               scratch_types=[pltpu.VMEM((SC_TOKEN_CHUNK, N_PAIRS), jnp.int32),
                              pltpu.VMEM((SC_TOKEN_CHUNK * aux_rows, 128), F32),
                              pltpu.VMEM((SC_RING, head_rows, 128), jnp.uint32),
                              pltpu.VMEM((SC_TOKEN_CHUNK * out_rows, 128), F32),
                              pltpu.VMEM((PEER_TOPK, SC_LANES), F32),
                              pltpu.SemaphoreType.DMA((SC_RING,))],
               compiler_params=pltpu.CompilerParams(needs_layout_passes=False),
               name="peer_u_sc" if down else "peer_v_sc")
    def body(e_hbm, aux_hbm, tab_hbm, o_hbm, e_v, aux_v, rows_v, out_v, fold_v, sems):
        wid = lax.axis_index("core") * SC_SUBCORES + lax.axis_index("subcore")
        lane = lax.broadcasted_iota(jnp.int32, (SC_LANES,), 0)

        def gathers(i, p, slot):
            first = e_v[i, pl.ds(p * PEER_TOPK, PEER_TOPK)]
            return [pltpu.make_async_copy(tab_hbm.at[first + r],
                                          rows_v.at[slot, pl.ds(r * PEER_TOPK, PEER_TOPK)], sems.at[slot])
                    for r in range(ROWS_PER_EXPERT)]

        def start_gather(i, p, slot):
            for d in gathers(i, p, slot):
                d.start()

        def weighted_sum(i, p, slot):
            ws = [plsc.load_gather(aux_v, [lane * 0 + i, lane * 0 + (p * PEER_TOPK + k)])
                  for k in range(PEER_TOPK)]
            for r in range(ROWS_PER_EXPERT):
                @pl.loop(0, WORD_ROWS)
                def _(j):
                    sl = pl.ds(j * SC_LANES, SC_LANES)
                    if p == 0:
                        a_hi = jnp.zeros((SC_LANES,), F32)
                        a_lo = jnp.zeros((SC_LANES,), F32)
                    else:
                        a_hi = out_v[i * CHUNKS + r, sl]
                        a_lo = out_v[i * CHUNKS + ROWS_PER_EXPERT + r, sl]
                    for k in range(PEER_TOPK):
                        hi, lo = _unpack(rows_v[slot, r * PEER_TOPK + k, sl])
                        a_hi = a_hi + ws[k] * hi
                        a_lo = a_lo + ws[k] * lo
                    out_v[i * CHUNKS + r, sl] = a_hi
                    out_v[i * CHUNKS + ROWS_PER_EXPERT + r, sl] = a_lo

        def pair_dots(i, p, slot):
            accs = tuple(jnp.zeros((SC_LANES,), F32) for _ in range(PEER_TOPK))
            for r in range(ROWS_PER_EXPERT):
                def piece(j, accs, r=r):
                    sl = pl.ds(j * SC_LANES, SC_LANES)
                    ha = aux_v[i * CHUNKS + r, sl]
                    hb = aux_v[i * CHUNKS + ROWS_PER_EXPERT + r, sl]
                    out = []
                    for k in range(PEER_TOPK):
                        hi, lo = _unpack(rows_v[slot, r * PEER_TOPK + k, sl])
                        out.append(accs[k] + (hi * ha + lo * hb))
                    return tuple(out)
                accs = lax.fori_loop(0, WORD_ROWS, piece, accs)
            for k in range(PEER_TOPK):
                fold_v[k, :] = accs[k]
            tot = plsc.load_gather(fold_v, [lane, lane * 0])
            for l in range(1, SC_LANES):
                tot = tot + plsc.load_gather(fold_v, [lane, lane * 0 + l])
            out_v[i, pl.ds(p * PEER_TOPK, PEER_TOPK)] = tot

        consume = pair_dots if down else weighted_sum

        @pl.loop(0, per // SC_TOKEN_CHUNK)
        def _(c):
            t0 = wid * per + c * SC_TOKEN_CHUNK
            pltpu.sync_copy(e_hbm.at[pl.ds(t0, SC_TOKEN_CHUNK)], e_v)
            pltpu.sync_copy(aux_hbm.at[pl.ds(t0 * aux_rows, SC_TOKEN_CHUNK * aux_rows)], aux_v)
            for p in range(SC_RING):
                start_gather(0, p, p)

            @pl.loop(0, SC_TOKEN_CHUNK)
            def _(i):
                for p in range(PEER_HEADS):
                    slot = p % SC_RING
                    for d in gathers(i, p, slot):
                        d.wait()
                    consume(i, p, slot)
                    if p + SC_RING < PEER_HEADS:
                        start_gather(i, p + SC_RING, slot)
                    else:
                        @pl.when(i + 1 < SC_TOKEN_CHUNK)
                        def _():
                            start_gather(i + 1, p + SC_RING - PEER_HEADS, slot)

            pltpu.sync_copy(out_v, o_hbm.at[pl.ds(t0 * out_rows, SC_TOKEN_CHUNK * out_rows)])

    return body(e_t, aux, tab)


def _pack_table(tab):
    bits = lax.bitcast_convert_type(tab.astype(BF16), jnp.uint16).astype(jnp.uint32)
    words = (bits[:, :HALF] << 16) | bits[:, HALF:]
    return words.reshape(tab.shape[0] * ROWS_PER_EXPERT, 128)


PEER_TOKENS = 128
SC_SHARE = (3, 8)
SC_MIN_TOKENS = 4096


def _sc_tokens(n):
    if n < SC_MIN_TOKENS:
        return 0
    unit = math.lcm(PEER_TOKENS, SC_CORES * SC_SUBCORES * SC_TOKEN_CHUNK)
    return n * SC_SHARE[0] // SC_SHARE[1] // unit * unit


def _row_tile(n, pref):
    while n % pref:
        pref //= 2
    return pref


def _peer_and_final(x2, h3, qp, k1_bf, k2_bf, tab_u, tab_v, g_final, tt, n_sc=0):
    n = x2.shape[0]
    n_tc = n - n_sc
    e_t, gate_t = _route(qp, k1_bf, k2_bf, tt)
    w_t = _peer_u(e_t, h3, gate_t, tab_u, tt, n_tc)
    if n_sc:
        act = _peer_sc(e_t[n_tc:], h3[n_tc:].reshape(n_sc * CHUNKS, 128), tab_u, down=True)
        w_t = _gate_act(act, gate_t, w_t, n_tc, tt)
    y = _peer_v(e_t, w_t, x2, g_final, tab_v, tt, n_tc)
    if n_sc:
        add8 = _peer_sc(e_t[n_tc:], w_t[n_tc:], tab_v, down=False)
        y = _residual_norm(x2, add8, g_final, y, n_tc, tt)
    return y


def kernel(x_prompt, x_sample, mem_prompt, cache_da_k, cache_da_v, state_ret, cache_mem_k, cache_mem_v, g_mix, w_in, lam_q1, lam_k1, lam_q2, lam_k2, g_da, g_ret, w_out, g_cross, g_mem, w_xq, w_xk, w_xv, w_xo, g_ffn, w_pq, peer_k1, peer_k2, peer_u, peer_v, g_final):
    depth = w_in.shape[0]
    assert depth == 1, "single-layer step"
    l = 0
    lam_init = 0.8 - 0.6 * math.exp(-0.3 * l)
    b, t, _ = x_prompt.shape
    bs, ts, _ = x_sample.shape
    past_len = cache_da_k.shape[2]

    row = lambda a: a.reshape(1, -1)
    w_in_bf = w_in[l].astype(BF16)
    w_out_bf = w_out[l].astype(BF16)
    w_xq_bf, w_xk_bf, w_xv_bf, w_xo_bf = (w[l].astype(BF16) for w in (w_xq, w_xk, w_xv, w_xo))
    w_pq_bf = w_pq[l].astype(BF16)
    k1_bf, k2_bf = peer_k1[l].astype(BF16), peer_k2[l].astype(BF16)
    tab_u, tab_v = _pack_table(peer_u[l]), _pack_table(peer_v[l])
    lamp = jnp.stack([lam_q1[l], lam_k1[l], lam_q2[l], lam_k2[l]])
    g_da3 = g_da[l].reshape(H_D, 1, DV_D)
    g_ret3 = g_ret[l].reshape(H_R, 1, DV_R)
    g_fin = row(g_final)

    def mixer_tail(x2d, mda, mret, mk, mv, bb, tt_rows, n_sc):
        n = x2d.shape[0]
        tm = _row_tile(n, 512)
        x1, qx = _out_proj(x2d, mda.reshape(n, GROUP_W), mret.reshape(n, GROUP_W), w_out_bf, row(g_cross[l]), w_xq_bf, tm)
        x2, h3, qp = _cross(x1.reshape(bb, tt_rows, D_MODEL), qx.reshape(bb, tt_rows, D_MODEL), mk, mv,
                            w_xo_bf, row(g_ffn[l]), w_pq_bf, _row_tile(tt_rows, 512))
        y = _peer_and_final(x2.reshape(n, D_MODEL), h3.reshape(n, D_MODEL), qp.reshape(n, -1),
                            k1_bf, k2_bf, tab_u, tab_v, g_fin, PEER_TOKENS, n_sc)
        return y.reshape(bb, tt_rows, D_MODEL)

    n = b * t
    xp = x_prompt.reshape(n, D_MODEL)
    qd, kd, vd, kdb, vdb, qr, kr, vr, gr = _in_proj(xp, row(g_mix[l]), w_in_bf, _row_tile(n, 512))
    r3 = lambda a: a.reshape(b, t, GROUP_W)
    mda = _diff_attn_prompt(lamp, r3(qd), r3(kdb), r3(vdb), g_da3, lam_init, 512, 512)
    mret, s_fin = _retention(r3(qr), r3(kr), r3(vr), r3(gr), g_ret3, None, 256)
    mk, mv, mkb, mvb = _mem_kv(mem_prompt.reshape(b * N_MEM, D_MODEL), row(g_mem[l]), w_xk_bf, w_xv_bf, 512)
    y_prompt = mixer_tail(xp, mda, mret, mkb.reshape(b, N_MEM, D_MODEL), mvb.reshape(b, N_MEM, D_MODEL), b, t, _sc_tokens(n))

    ns = bs * ts
    xs = x_sample.reshape(ns, D_MODEL)
    qd_s, kd_s, vd_s, kdb_s, vdb_s, qr_s, kr_s, vr_s, gr_s = _in_proj(xs, row(g_mix[l]), w_in_bf, _row_tile(ns, 512))
    s3 = lambda a: a.reshape(bs, ts, GROUP_W)
    mda_s = _diff_attn_sample(lamp, s3(qd_s), cache_da_k[l].reshape(bs, past_len, GROUP_W),
                              cache_da_v[l].reshape(bs, past_len, GROUP_W), s3(kdb_s), s3(vdb_s), g_da3, lam_init)
    mret_s, s_new = _retention(s3(qr_s), s3(kr_s), s3(vr_s), s3(gr_s), g_ret3, state_ret[l], ts)
    y_sample = mixer_tail(xs, mda_s, mret_s, cache_mem_k[l].reshape(bs, N_MEM, D_MODEL),
                          cache_mem_v[l].reshape(bs, N_MEM, D_MODEL), bs, ts, _sc_tokens(ns))

    return (y_prompt, y_sample,
            kd.reshape(1, b, t, H_D, 2, DK_D), vd.reshape(1, b, t, H_D, DV_D), s_fin[None],
            mk.reshape(1, b, N_MEM, H_X, HD_X), mv.reshape(1, b, N_MEM, H_X, HD_X),
            kd_s.reshape(1, bs, ts, H_D, 2, DK_D), vd_s.reshape(1, bs, ts, H_D, DV_D), s_new[None])
```

```python
import functools
import math

import jax
import jax.numpy as jnp
from jax import lax
from jax.experimental import pallas as pl
from jax.experimental.pallas import tpu as pltpu
from jax.experimental.pallas import tpu_sc as plsc

D_MODEL = 1024
CHUNK = 64
CHUNK_SHIFT = CHUNK.bit_length() - 1
assert 1 << CHUNK_SHIFT == CHUNK
H_D, DK_D, DV_D = 4, 64, 128
H_R, DK_R, DV_R = 4, 128, 128
N_MEM = 256
H_X = 4
HD_X = D_MODEL // H_X
PEER_HEADS = 8
N_KEYS = 128
N_EXPERTS = N_KEYS * N_KEYS
PEER_TOPK = 16
EPS = 1e-6
HEAD_W = 128
GROUP_W = 512
N_PAIRS = PEER_HEADS * PEER_TOPK
HALF = D_MODEL // 2
ROWS_PER_EXPERT = HALF // 128
VMEM_LIMIT = 56 * 1024 * 1024

BF16 = jnp.bfloat16
F32 = jnp.float32


def _cparams(sem):
    return pltpu.CompilerParams(dimension_semantics=sem, vmem_limit_bytes=VMEM_LIMIT)


def _rms(x, g):
    return x * lax.rsqrt(jnp.mean(x * x, axis=-1, keepdims=True) + EPS) * g


def _dot(a, b):
    return jnp.dot(a, b, preferred_element_type=F32)


def _dot_nt(a, b):
    return lax.dot_general(a, b, (((1,), (1,)), ((), ())), preferred_element_type=F32)


def _dot_tn(a, b):
    return lax.dot_general(a, b, (((0,), (0,)), ((), ())), preferred_element_type=F32)


def _select_by_head(h, values):
    out = jnp.float32(values[-1])
    for i in range(len(values) - 2, -1, -1):
        out = jnp.where(h == i, jnp.float32(values[i]), out)
    return out


def _in_proj_kernel(x_ref, g_ref, w_ref, qd_ref, kd_ref, vd_ref, kdb_ref, vdb_ref,
                    qr_ref, kr_ref, vr_ref, gr_ref):
    hb = _rms(x_ref[...], g_ref[...]).astype(BF16)
    col = lambda c: _dot(hb, w_ref[:, c * GROUP_W:(c + 1) * GROUP_W])
    qd_ref[...] = (col(0) * (DK_D ** -0.5)).astype(BF16)
    kd = col(1)
    kd_ref[...] = kd
    kdb_ref[...] = kd.astype(BF16)
    vd = col(2)
    vd_ref[...] = vd
    vdb_ref[...] = vd.astype(BF16)
    qr_ref[...] = col(3).astype(BF16)
    kr_ref[...] = (col(4) * (DK_R ** -0.5)).astype(BF16)
    vr_ref[...] = col(5).astype(BF16)
    gr_ref[...] = col(6)


def _in_proj(x2d, g, w_bf, tm):
    n = x2d.shape[0]
    blk = lambda: pl.BlockSpec((tm, GROUP_W), lambda i: (i, 0))
    sh = lambda dt: jax.ShapeDtypeStruct((n, GROUP_W), dt)
    return pl.pallas_call(
        _in_proj_kernel,
        grid=(n // tm,),
        in_specs=[pl.BlockSpec((tm, D_MODEL), lambda i: (i, 0)),
                  pl.BlockSpec((1, D_MODEL), lambda i: (0, 0)),
                  pl.BlockSpec(w_bf.shape, lambda i: (0, 0))],
        out_specs=[blk() for _ in range(9)],
        out_shape=[sh(BF16), sh(F32), sh(F32), sh(BF16), sh(BF16), sh(BF16), sh(BF16), sh(BF16), sh(F32)],
        compiler_params=_cparams(("parallel",)),
        name="in_proj",
    )(x2d, g, w_bf)


def _lambda_from(lam_ref, lam_init):
    l = lam_ref[...]
    a = jnp.exp(jnp.sum(l[0:1] * l[1:2], axis=-1, keepdims=True))
    b = jnp.exp(jnp.sum(l[2:3] * l[3:4], axis=-1, keepdims=True))
    return a - b + lam_init


def _diff_post(acc, l, lam, g, lam_init, tq):
    o = acc[:tq] / l[:tq] - lam * (acc[tq:] / l[tq:])
    return o * lax.rsqrt(jnp.mean(o * o, axis=-1, keepdims=True) + EPS) * g * (1.0 - lam_init)


def _split_maps(q):
    lane = lax.broadcasted_iota(jnp.int32, q.shape, 1)
    zero = jnp.zeros_like(q)
    return jnp.concatenate([jnp.where(lane < DK_D, q, zero), jnp.where(lane >= DK_D, q, zero)], axis=0)


def _da_prompt_kernel(lam_ref, q_ref, k_ref, v_ref, g_ref, o_ref, kx_ref, vx_ref, own_ref, acc_ref, m_ref, *, lam_init, tq, tk):
    h = pl.program_id(1)
    i = pl.program_id(2)
    t = k_ref.shape[0]
    slope = _select_by_head(h, [2.0 ** (-8.0 * (j + 1) / H_D) for j in range(H_D)])

    @pl.when(i == 0)
    def _():
        pos = lax.broadcasted_iota(jnp.int32, (t, HEAD_W), 0)
        lane = lax.broadcasted_iota(jnp.int32, (t, HEAD_W), 1)
        coarse = ((pos >> CHUNK_SHIFT) << CHUNK_SHIFT).astype(F32) * slope
        fine = (pos & (CHUNK - 1)).astype(F32) * slope
        kx_ref[:, :HEAD_W] = k_ref[...]
        kx_ref[:, HEAD_W:] = jnp.where(lane == 0, coarse, jnp.where(lane == 1, fine, 0.0)).astype(BF16)
        vx_ref[:DV_D, :] = v_ref[...].astype(F32).T.astype(BF16)
        vx_ref[DV_D:, :] = jnp.ones((vx_ref.shape[0] - DV_D, t), BF16)
        krel = lax.broadcasted_iota(jnp.int32, (tk, 2 * tq), 0)
        c = lax.broadcasted_iota(jnp.int32, (tk, 2 * tq), 1)
        for par in range(tk // tq):
            qrel = par * tq + jnp.where(c >= tq, c - tq, c)
            ahead = (2.0 * slope) * jnp.maximum(krel - qrel, 0).astype(F32)
            own_ref[par] = jnp.where((qrel >> CHUNK_SHIFT) >= (krel >> CHUNK_SHIFT), -ahead, -1e30)

    q = q_ref[...]
    lane = lax.broadcasted_iota(jnp.int32, q.shape, 1)
    zero = jnp.zeros_like(q)
    ones2 = jnp.where(lane < 2, 1.0, 0.0).astype(BF16)
    q2 = jnp.concatenate([jnp.concatenate([jnp.where(lane < DK_D, q, zero), ones2], axis=1),
                          jnp.concatenate([jnp.where(lane >= DK_D, q, zero), ones2], axis=1)], axis=0)
    jd = (i * tq) // tk

    def scores(j):
        return _dot_nt(kx_ref[pl.ds(pl.multiple_of(j * tk, tk), tk), :], q2)

    def values(j):
        return vx_ref[:, pl.ds(pl.multiple_of(j * tk, tk), tk)]

    s = scores(jd) + own_ref[(i * tq) % tk // tq]
    m0 = jnp.max(s, axis=0, keepdims=True)
    m_ref[...] = m0
    acc_ref[...] = _dot(values(jd), jnp.exp(s - m0).astype(BF16))

    def absorb(blocks):
        ss = [scores(j) for j in blocks]
        m_old = m_ref[...]
        m_new = m_old
        for s in ss:
            m_new = jnp.maximum(m_new, jnp.max(s, axis=0, keepdims=True))
        m_ref[...] = m_new
        acc = jnp.exp(m_old - m_new) * acc_ref[...]
        for j, s in zip(blocks, ss):
            acc = acc + _dot(values(j), jnp.exp(s - m_new).astype(BF16))
        acc_ref[...] = acc

    def past_pair(jj, carry):
        absorb([2 * jj, 2 * jj + 1])
        return carry

    lax.fori_loop(0, jd // 2, past_pair, 0)

    @pl.when(jd % 2 == 1)
    def _():
        absorb([jd - 1])

    acc = acc_ref[...]
    num, den = acc[:DV_D], acc[DV_D:DV_D + 1]
    lam = _lambda_from(lam_ref, lam_init)
    o = (num[:, :tq] / den[:, :tq] - lam * (num[:, tq:] / den[:, tq:])).T
    o = o * lax.rsqrt(jnp.mean(o * o, axis=-1, keepdims=True) + EPS) * g_ref[...] * (1.0 - lam_init)
    o_ref[...] = o.astype(o_ref.dtype)


ONES_ROWS = 16


def _diff_attn_prompt(lamp, q, k, v, g_da3, lam_init, tq, tk):
    b, t, _ = q.shape
    kern = functools.partial(_da_prompt_kernel, lam_init=lam_init, tq=tq, tk=tk)
    return pl.pallas_call(
        kern,
        grid=(b, H_D, t // tq),
        in_specs=[pl.BlockSpec((4, DK_D), lambda b_, h, i: (0, 0)),
                  pl.BlockSpec((None, tq, HEAD_W), lambda b_, h, i: (b_, i, h)),
                  pl.BlockSpec((None, t, HEAD_W), lambda b_, h, i: (b_, 0, h)),
                  pl.BlockSpec((None, t, HEAD_W), lambda b_, h, i: (b_, 0, h)),
                  pl.BlockSpec((None, 1, HEAD_W), lambda b_, h, i: (h, 0, 0))],
        out_specs=pl.BlockSpec((None, tq, HEAD_W), lambda b_, h, i: (b_, i, h)),
        out_shape=jax.ShapeDtypeStruct((b, t, GROUP_W), BF16),
        scratch_shapes=[pltpu.VMEM((t, 2 * HEAD_W), BF16), pltpu.VMEM((DV_D + ONES_ROWS, t), BF16),
                        pltpu.VMEM((tk // tq, tk, 2 * tq), F32),
                        pltpu.VMEM((DV_D + ONES_ROWS, 2 * tq), F32), pltpu.VMEM((1, 2 * tq), F32)],
        compiler_params=_cparams(("parallel", "parallel", "arbitrary")),
        name="diff_attn_prompt",
    )(lamp, q, k, v, g_da3)


def _da_sample_kernel(lam_ref, q_ref, kc_ref, vc_ref, kn_ref, vn_ref, g_ref, o_ref, *, lam_init, ts, past_len):
    h = pl.program_id(1)
    slope = _select_by_head(h, [2.0 ** (-8.0 * (j + 1) / H_D) for j in range(H_D)])
    q2 = _split_maps(q_ref[...])
    rows = 2 * ts

    def scores(k, base, n):
        r = lax.broadcasted_iota(jnp.int32, (rows, n), 0)
        qpos = past_len + jnp.where(r >= ts, r - ts, r)
        kpos = base + lax.broadcasted_iota(jnp.int32, (rows, n), 1)
        return _dot_nt(q2, k) - slope * jnp.abs(qpos - kpos).astype(F32)

    sc = scores(kc_ref[...].astype(BF16), 0, past_len)
    sn = scores(kn_ref[...], past_len, ts)
    m = jnp.maximum(jnp.max(sc, axis=-1, keepdims=True), jnp.max(sn, axis=-1, keepdims=True))
    pc = jnp.exp(sc - m)
    pn = jnp.exp(sn - m)
    l = jnp.sum(pc, axis=-1, keepdims=True) + jnp.sum(pn, axis=-1, keepdims=True)
    acc = _dot(pc.astype(BF16), vc_ref[...].astype(BF16)) + _dot(pn.astype(BF16), vn_ref[...])
    lam = _lambda_from(lam_ref, lam_init)
    o_ref[...] = _diff_post(acc, l, lam, g_ref[...], lam_init, ts).astype(o_ref.dtype)


def _diff_attn_sample(lamp, q, kc, vc, kn, vn, g_da3, lam_init):
    b, ts, _ = q.shape
    past_len = kc.shape[1]
    kern = functools.partial(_da_sample_kernel, lam_init=lam_init, ts=ts, past_len=past_len)
    head = lambda rows: pl.BlockSpec((None, rows, HEAD_W), lambda b_, h: (b_, 0, h))
    return pl.pallas_call(
        kern,
        grid=(b, H_D),
        in_specs=[pl.BlockSpec((4, DK_D), lambda b_, h: (0, 0)),
                  head(ts), head(past_len), head(past_len), head(ts), head(ts),
                  pl.BlockSpec((None, 1, HEAD_W), lambda b_, h: (h, 0, 0))],
        out_specs=head(ts),
        out_shape=jax.ShapeDtypeStruct((b, ts, GROUP_W), BF16),
        compiler_params=_cparams(("parallel", "parallel")),
        name="diff_attn_sample",
    )(lamp, q, kc, vc, kn, vn, g_da3)


def _ret_kernel(*refs, lb, has_init):
    if has_init:
        q_ref, k_ref, v_ref, gate_ref, g_ref, s0_ref, o_ref, sfin_ref, s_ref = refs
    else:
        q_ref, k_ref, v_ref, gate_ref, g_ref, o_ref, sfin_ref, s_ref = refs
    h = pl.program_id(1)
    c = pl.program_id(2)
    lg = _select_by_head(h, [math.log1p(-(2.0 ** (-5.0 - j))) for j in range(H_R)])

    @pl.when(c == 0)
    def _():
        s_ref[...] = s0_ref[...] if has_init else jnp.zeros_like(s_ref)

    q, k, v = q_ref[...], k_ref[...], v_ref[...]
    i = lax.broadcasted_iota(jnp.int32, (lb, lb), 0)
    j = lax.broadcasted_iota(jnp.int32, (lb, lb), 1)
    d = (i - j).astype(F32)
    decay = jnp.where(d >= 0, jnp.exp(jnp.maximum(d, 0.0) * lg), 0.0)
    inner = _dot_nt(q, k) * decay
    ic = lax.broadcasted_iota(jnp.int32, (lb, 1), 0).astype(F32)
    s_old = s_ref[...]
    o = _dot(inner.astype(BF16), v) + _dot(q, s_old.astype(BF16)) * jnp.exp((ic + 1.0) * lg)
    tail = jnp.exp((lb - 1.0 - ic) * lg)
    kt = (k.astype(F32) * tail).astype(BF16)
    s_new = jnp.exp(lb * lg) * s_old + _dot_tn(kt, v)
    s_ref[...] = s_new

    @pl.when(c == pl.num_programs(2) - 1)
    def _():
        sfin_ref[...] = s_new

    oc = o - jnp.mean(o, axis=-1, keepdims=True)
    y = oc * lax.rsqrt(jnp.mean(oc * oc, axis=-1, keepdims=True) + EPS) * g_ref[...]
    gate = gate_ref[...]
    o_ref[...] = (y * (gate * jax.nn.sigmoid(gate))).astype(o_ref.dtype)


def _retention(q, k, v, gate, g_ret3, s0, lb):
    b, t, _ = q.shape
    has_init = s0 is not None
    kern = functools.partial(_ret_kernel, lb=lb, has_init=has_init)
    head = lambda: pl.BlockSpec((None, lb, HEAD_W), lambda b_, h, c: (b_, c, h))
    state = lambda: pl.BlockSpec((None, None, DK_R, DV_R), lambda b_, h, c: (b_, h, 0, 0))
    in_specs = [head(), head(), head(), head(), pl.BlockSpec((None, 1, HEAD_W), lambda b_, h, c: (h, 0, 0))]
    args = [q, k, v, gate, g_ret3]
    if has_init:
        in_specs.append(state())
        args.append(s0)
    return pl.pallas_call(
        kern,
        grid=(b, H_R, t // lb),
        in_specs=in_specs,
        out_specs=[head(), state()],
        out_shape=[jax.ShapeDtypeStruct((b, t, GROUP_W), BF16), jax.ShapeDtypeStruct((b, H_R, DK_R, DV_R), F32)],
        scratch_shapes=[pltpu.VMEM((DK_R, DV_R), F32)],
        compiler_params=_cparams(("parallel", "parallel", "arbitrary")),
        name="retention",
    )(*args)


def _out_proj_kernel(x_ref, mda_ref, mret_ref, wo_ref, g_ref, wq_ref, x1_ref, qx_ref):
    x1 = x_ref[...] + _dot(mda_ref[...], wo_ref[:GROUP_W, :]) + _dot(mret_ref[...], wo_ref[GROUP_W:, :])
    x1_ref[...] = x1
    hn = _rms(x1, g_ref[...]).astype(BF16)
    qx_ref[...] = (_dot(hn, wq_ref[...]) * (HD_X ** -0.5)).astype(BF16)


def _out_proj(x2d, mda, mret, wo_bf, g_cross, wq_bf, tm):
    n = x2d.shape[0]
    full = lambda a: pl.BlockSpec(a.shape, lambda i: (0, 0))
    return pl.pallas_call(
        _out_proj_kernel,
        grid=(n // tm,),
        in_specs=[pl.BlockSpec((tm, D_MODEL), lambda i: (i, 0)),
                  pl.BlockSpec((tm, GROUP_W), lambda i: (i, 0)),
                  pl.BlockSpec((tm, GROUP_W), lambda i: (i, 0)),
                  full(wo_bf), full(g_cross), full(wq_bf)],
        out_specs=[pl.BlockSpec((tm, D_MODEL), lambda i: (i, 0)), pl.BlockSpec((tm, D_MODEL), lambda i: (i, 0))],
        out_shape=[jax.ShapeDtypeStruct((n, D_MODEL), F32), jax.ShapeDtypeStruct((n, D_MODEL), BF16)],
        compiler_params=_cparams(("parallel",)),
        name="out_proj",
    )(x2d, mda, mret, wo_bf, g_cross, wq_bf)


def _mem_kv_kernel(m_ref, g_ref, wk_ref, wv_ref, mk_ref, mv_ref, mkb_ref, mvb_ref):
    mn = _rms(m_ref[...], g_ref[...]).astype(BF16)
    mk = _dot(mn, wk_ref[...])
    mv = _dot(mn, wv_ref[...])
    mk_ref[...] = mk
    mv_ref[...] = mv
    mkb_ref[...] = mk.astype(BF16)
    mvb_ref[...] = mv.astype(BF16)


def _mem_kv(mem2d, g_mem, wk_bf, wv_bf, tm):
    n = mem2d.shape[0]
    row = lambda: pl.BlockSpec((tm, D_MODEL), lambda i: (i, 0))
    full = lambda a: pl.BlockSpec(a.shape, lambda i: (0, 0))
    sh = lambda dt: jax.ShapeDtypeStruct((n, D_MODEL), dt)
    return pl.pallas_call(
        _mem_kv_kernel,
        grid=(n // tm,),
        in_specs=[row(), full(g_mem), full(wk_bf), full(wv_bf)],
        out_specs=[row(), row(), row(), row()],
        out_shape=[sh(F32), sh(F32), sh(BF16), sh(BF16)],
        compiler_params=_cparams(("parallel",)),
        name="mem_kv",
    )(mem2d, g_mem, wk_bf, wv_bf)


def _cross_kernel(x1_ref, qx_ref, mk_ref, mv_ref, wo_ref, g_ref, wpq_ref, x2_ref, h3_ref, qp_ref):
    q = qx_ref[...]
    heads = []
    for h in range(H_X):
        sl = slice(h * HD_X, (h + 1) * HD_X)
        kh = mk_ref[:, sl] if mk_ref.dtype == BF16 else mk_ref[:, sl].astype(BF16)
        vh = mv_ref[:, sl] if mv_ref.dtype == BF16 else mv_ref[:, sl].astype(BF16)
        s = _dot_nt(q[:, sl], kh)
        p = jnp.exp(s - jnp.max(s, axis=-1, keepdims=True))
        heads.append(_dot(p.astype(BF16), vh) / jnp.sum(p, axis=-1, keepdims=True))
    o = jnp.concatenate(heads, axis=-1).astype(BF16)
    x2 = x1_ref[...] + _dot(o, wo_ref[...])
    x2_ref[...] = x2
    h3 = _rms(x2, g_ref[...])
    h3_ref[...] = h3
    qp_ref[...] = _dot(h3.astype(BF16), wpq_ref[...]).astype(BF16)


def _cross(x1, qx, mk, mv, wo_bf, g_ffn, wpq_bf, tm):
    b, t, _ = x1.shape
    row = lambda w: pl.BlockSpec((None, tm, w), lambda b_, i: (b_, i, 0))
    mem = lambda: pl.BlockSpec((None, N_MEM, D_MODEL), lambda b_, i: (b_, 0, 0))
    full = lambda a: pl.BlockSpec(a.shape, lambda b_, i: (0, 0))
    dq = wpq_bf.shape[1]
    return pl.pallas_call(
        _cross_kernel,
        grid=(b, t // tm),
        in_specs=[row(D_MODEL), row(D_MODEL), mem(), mem(), full(wo_bf), full(g_ffn), full(wpq_bf)],
        out_specs=[row(D_MODEL), row(D_MODEL), row(dq)],
        out_shape=[jax.ShapeDtypeStruct((b, t, D_MODEL), F32), jax.ShapeDtypeStruct((b, t, D_MODEL), F32),
                   jax.ShapeDtypeStruct((b, t, dq), BF16)],
        compiler_params=_cparams(("parallel", "parallel")),
        name="cross_attn",
    )(x1, qx, mk, mv, wo_bf, g_ffn, wpq_bf)


ID_PAD = 2.0 ** 29


def _topk_rows(s, ids):
    vals, sel = [], []
    for _ in range(PEER_TOPK):
        m = jnp.max(s, axis=0, keepdims=True)
        idx = jnp.min(jnp.where(s == m, ids, ID_PAD), axis=0, keepdims=True)
        vals.append(m)
        sel.append(idx)
        s = jnp.where(ids == idx, -jnp.inf, s)
    return jnp.concatenate(vals, axis=0), jnp.concatenate(sel, axis=0)


def _candidates(v1, i1, v2, i2):
    lanes = v1.shape[1]
    b8 = lax.broadcasted_iota(jnp.int32, (8, lanes), 0)
    ident = lambda a, ia, ib, b: (a * PEER_TOPK + b) * float(N_EXPERTS) + (ia * float(N_KEYS) + ib)
    b16 = lax.broadcasted_iota(jnp.int32, (PEER_TOPK, lanes), 0).astype(F32)
    vals = [v1[0:1] + v2]
    ids = [ident(0, i1[0:1], i2, b16)]
    for a in range(1, 8):
        keep = b8 < PEER_TOPK // (a + 1)
        vals.append(jnp.where(keep, v1[a:a + 1] + v2[0:8], -jnp.inf))
        ids.append(jnp.where(keep, ident(a, i1[a:a + 1], i2[0:8], b8.astype(F32)), ID_PAD))
    a_hi = (b8 + 8).astype(F32)
    vals.append(v1[8:16] + v2[0:1])
    ids.append(ident(a_hi, i1[8:16], i2[0:1], 0.0))
    return jnp.concatenate(vals, axis=0), jnp.concatenate(ids, axis=0)


def _route_kernel(qp_ref, k1_ref, k2_ref, e_ref, g_ref):
    half = N_KEYS
    tt = qp_ref.shape[0]
    key_id = lax.broadcasted_iota(jnp.int32, (N_KEYS, tt), 0).astype(F32)
    es, gs = [], []
    for p in range(PEER_HEADS):
        q1 = qp_ref[:, (2 * p) * half:(2 * p + 1) * half]
        q2 = qp_ref[:, (2 * p + 1) * half:(2 * p + 2) * half]
        v1, i1 = _topk_rows(_dot_nt(k1_ref[p], q1), key_id)
        v2, i2 = _topk_rows(_dot_nt(k2_ref[p], q2), key_id)
        sc, sel = _topk_rows(*_candidates(v1, i1, v2, i2))
        w = jnp.exp(sc - sc[0:1])
        es.append((sel.astype(jnp.int32) & (N_EXPERTS - 1)) * ROWS_PER_EXPERT)
        gs.append(w / jnp.sum(w, axis=0, keepdims=True))
    e_ref[...] = jnp.concatenate(es, axis=0).T
    g_ref[...] = jnp.concatenate(gs, axis=0).T


def _route_into_kernel(qp_ref, k1_ref, k2_ref, e_any_ref, g_any_ref, e_ref, g_ref):
    del e_any_ref, g_any_ref
    _route_kernel(qp_ref, k1_ref, k2_ref, e_ref, g_ref)


def _route(qp2d, k1_bf, k2_bf, tt, first_token=0, n_tokens=None, into=None):
    n = qp2d.shape[0]
    n_tokens = n if n_tokens is None else n_tokens
    off = first_token // tt
    full = lambda a: pl.BlockSpec(a.shape, lambda i: (0, 0, 0))
    pair = lambda: pl.BlockSpec((tt, N_PAIRS), lambda i: (i + off, 0))
    in_specs = [pl.BlockSpec((tt, qp2d.shape[1]), lambda i: (i + off, 0)), full(k1_bf), full(k2_bf)]
    args = [qp2d, k1_bf, k2_bf]
    aliases = {}
    if into is not None:
        in_specs += [pl.BlockSpec(memory_space=pl.ANY), pl.BlockSpec(memory_space=pl.ANY)]
        args += list(into)
        aliases = {3: 0, 4: 1}
    return pl.pallas_call(
        _route_kernel if into is None else _route_into_kernel,
        grid=(n_tokens // tt,),
        in_specs=in_specs,
        out_specs=[pair(), pair()],
        out_shape=[jax.ShapeDtypeStruct((n, N_PAIRS), jnp.int32), jax.ShapeDtypeStruct((n, N_PAIRS), F32)],
        input_output_aliases=aliases,
        compiler_params=_cparams(("parallel",)),
        name="peer_route",
    )(*args)


def _unpack(words):
    hi = lax.bitcast_convert_type(words & jnp.uint32(0xFFFF0000), F32)
    lo = lax.bitcast_convert_type(words << 16, F32)
    return hi, lo


def _expert_rows(tab_ref, row0):
    return tab_ref[pl.ds(pl.multiple_of(row0, ROWS_PER_EXPERT), ROWS_PER_EXPERT), :]


REDUCE_TOKENS = 16


def _lane_sums_to_rows(y, n_tok):
    hi = y.astype(BF16)
    lo = (y - hi.astype(F32)).astype(BF16)
    ones = jnp.ones((128, N_PAIRS), BF16)
    s = (_dot(hi, ones) + _dot(lo, ones)).reshape(n_tok, N_PAIRS, N_PAIRS)
    eye = lax.broadcasted_iota(jnp.int32, (N_PAIRS, N_PAIRS), 0) == lax.broadcasted_iota(jnp.int32, (N_PAIRS, N_PAIRS), 1)
    return jnp.sum(jnp.where(eye[None], s, 0.0), axis=1)


CHUNKS = D_MODEL // 128


def _peer_u_kernel(e_ref, h_ref, gate_ref, tab_ref, w_ref, h8_ref, prod_ref, ys_ref, act_ref, *, tt):
    for c in range(CHUNKS):
        h8_ref[pl.ds(c, tt, stride=CHUNKS), :] = h_ref[:, c * 128:(c + 1) * 128]

    def token(t, slot):
        base = pl.multiple_of(t * CHUNKS, CHUNKS)
        ha = h8_ref[pl.ds(base, ROWS_PER_EXPERT), :]
        hb = h8_ref[pl.ds(pl.multiple_of(base + ROWS_PER_EXPERT, ROWS_PER_EXPERT), ROWS_PER_EXPERT), :]
        prod = prod_ref.at[slot]
        for k in range(N_PAIRS):
            hi, lo = _unpack(_expert_rows(tab_ref, e_ref[t, k]))
            prod[k * ROWS_PER_EXPERT:(k + 1) * ROWS_PER_EXPERT, :] = hi * ha + lo * hb
        y = prod[pl.ds(0, N_PAIRS, stride=ROWS_PER_EXPERT), :]
        for c in range(1, ROWS_PER_EXPERT):
            y = y + prod[pl.ds(c, N_PAIRS, stride=ROWS_PER_EXPERT), :]
        ys_ref[pl.ds(pl.multiple_of(t * N_PAIRS, N_PAIRS), N_PAIRS), :] = y

    def two_tokens(j, carry):
        token(2 * j, 0)
        token(2 * j + 1, 1)
        return carry

    lax.fori_loop(0, tt // 2, two_tokens, 0)

    def group(g, carry):
        rows = REDUCE_TOKENS * N_PAIRS
        y = ys_ref[pl.ds(pl.multiple_of(g * rows, rows), rows), :]
        act_ref[pl.ds(pl.multiple_of(g * REDUCE_TOKENS, REDUCE_TOKENS), REDUCE_TOKENS), :] = _lane_sums_to_rows(y, REDUCE_TOKENS)
        return carry

    lax.fori_loop(0, tt // REDUCE_TOKENS, group, 0)
    w_ref[...] = _gated_gelu(gate_ref[...], act_ref[...])


def _gated_gelu(gate, a):
    return gate * (0.5 * a * (1.0 + lax.erf(a * (2.0 ** -0.5))))


def _gate_act_kernel(act_ref, gate_ref, w_any_ref, o_ref):
    del w_any_ref
    o_ref[...] = _gated_gelu(gate_ref[...], act_ref[...])


def _gate_act(act, gate_t, w_t, first_token, tt):
    off = first_token // tt
    return pl.pallas_call(
        _gate_act_kernel,
        grid=(act.shape[0] // tt,),
        in_specs=[pl.BlockSpec((tt, N_PAIRS), lambda i: (i, 0)),
                  pl.BlockSpec((tt, N_PAIRS), lambda i: (i + off, 0)),
                  pl.BlockSpec(memory_space=pl.ANY)],
        out_specs=pl.BlockSpec((tt, N_PAIRS), lambda i: (i + off, 0)),
        out_shape=jax.ShapeDtypeStruct(w_t.shape, F32),
        input_output_aliases={2: 0},
        compiler_params=_cparams(("arbitrary",)),
        name="peer_gate_act",
    )(act, gate_t, w_t)


def _peer_u(e_t, h, gate_t, tab, tt, n_tokens):
    n = e_t.shape[0]
    assert tt % REDUCE_TOKENS == 0
    kern = functools.partial(_peer_u_kernel, tt=tt)
    pair = lambda **kw: pl.BlockSpec((tt, N_PAIRS), lambda i: (i, 0), **kw)
    return pl.pallas_call(
        kern,
        grid=(n_tokens // tt,),
        in_specs=[pair(memory_space=pltpu.SMEM),
                  pl.BlockSpec((tt, D_MODEL), lambda i: (i, 0)),
                  pair(),
                  pl.BlockSpec(tab.shape, lambda i: (0, 0), pipeline_mode=pl.Buffered(1))],
        out_specs=pair(),
        out_shape=jax.ShapeDtypeStruct((n, N_PAIRS), F32),
        scratch_shapes=[pltpu.VMEM((tt * CHUNKS, 128), F32),
                        pltpu.VMEM((2, N_PAIRS * ROWS_PER_EXPERT, 128), F32),
                        pltpu.VMEM((tt * N_PAIRS, 128), F32),
                        pltpu.VMEM((tt, N_PAIRS), F32)],
        compiler_params=_cparams(("arbitrary",)),
        name="peer_u",
    )(e_t, h, gate_t, tab)


def _peer_v_kernel(e_ref, w_ref, x_ref, g_ref, tab_ref, o_ref, acc_ref, *, tt):
    n_acc = 2

    def token(t, carry):
        acc_hi = [jnp.zeros((ROWS_PER_EXPERT, 128), F32) for _ in range(n_acc)]
        acc_lo = [jnp.zeros((ROWS_PER_EXPERT, 128), F32) for _ in range(n_acc)]
        for k in range(N_PAIRS):
            hi, lo = _unpack(_expert_rows(tab_ref, e_ref[t, k]))
            w = w_ref[t, k]
            acc_hi[k % n_acc] = acc_hi[k % n_acc] + w * hi
            acc_lo[k % n_acc] = acc_lo[k % n_acc] + w * lo
        base = pl.multiple_of(t * CHUNKS, CHUNKS)
        acc_ref[pl.ds(base, ROWS_PER_EXPERT), :] = sum(acc_hi[1:], acc_hi[0])
        acc_ref[pl.ds(pl.multiple_of(base + ROWS_PER_EXPERT, ROWS_PER_EXPERT), ROWS_PER_EXPERT), :] = sum(acc_lo[1:], acc_lo[0])
        return carry

    lax.fori_loop(0, tt, token, 0)
    _residual_rms_store(x_ref, acc_ref, g_ref, o_ref, tt)


def _residual_rms_store(x_ref, add_ref, g_ref, o_ref, tt):
    xs = [x_ref[:, c * 128:(c + 1) * 128] + add_ref[pl.ds(c, tt, stride=CHUNKS), :] for c in range(CHUNKS)]
    sq = xs[0] * xs[0]
    for x in xs[1:]:
        sq = sq + x * x
    r = lax.rsqrt(jnp.sum(sq, axis=1, keepdims=True) * (1.0 / D_MODEL) + EPS)
    for c in range(CHUNKS):
        o_ref[:, c * 128:(c + 1) * 128] = xs[c] * r * g_ref[:, c * 128:(c + 1) * 128]


def _peer_v(e_t, w_t, x, g_final, tab, tt, n_tokens):
    kern = functools.partial(_peer_v_kernel, tt=tt)
    pair = lambda: pl.BlockSpec((tt, N_PAIRS), lambda i: (i, 0), memory_space=pltpu.SMEM)
    row = lambda: pl.BlockSpec((tt, D_MODEL), lambda i: (i, 0))
    return pl.pallas_call(
        kern,
        grid=(n_tokens // tt,),
        in_specs=[pair(), pair(), row(),
                  pl.BlockSpec((1, D_MODEL), lambda i: (0, 0)),
                  pl.BlockSpec(tab.shape, lambda i: (0, 0), pipeline_mode=pl.Buffered(1))],
        out_specs=row(),
        out_shape=jax.ShapeDtypeStruct(x.shape, F32),
        scratch_shapes=[pltpu.VMEM((tt * CHUNKS, 128), F32)],
        compiler_params=_cparams(("arbitrary",)),
        name="peer_v",
    )(e_t, w_t, x, g_final, tab)


def _residual_norm_kernel(x_ref, add_ref, g_ref, y_any_ref, o_ref, *, tt):
    del y_any_ref
    _residual_rms_store(x_ref, add_ref, g_ref, o_ref, tt)


def _residual_norm(x, add8, g_final, y, first_token, tt):
    n_rows = add8.shape[0] // CHUNKS
    off = first_token // tt
    row = lambda: pl.BlockSpec((tt, D_MODEL), lambda i: (i + off, 0))
    return pl.pallas_call(
        functools.partial(_residual_norm_kernel, tt=tt),
        grid=(n_rows // tt,),
        in_specs=[row(), pl.BlockSpec((tt * CHUNKS, 128), lambda i: (i, 0)),
                  pl.BlockSpec((1, D_MODEL), lambda i: (0, 0)),
                  pl.BlockSpec(memory_space=pl.ANY)],
        out_specs=row(),
        out_shape=jax.ShapeDtypeStruct(y.shape, F32),
        input_output_aliases={3: 0},
        compiler_params=_cparams(("arbitrary",)),
        name="peer_residual_norm",
    )(x, add8, g_final, y)


SC_CORES = 2
SC_SUBCORES = 16
SC_LANES = 16
WORD_ROWS = 128 // SC_LANES
SC_TOKEN_CHUNK = 16
SC_RING = 4


def _peer_sc(e_t, aux, tab, down):
    n = e_t.shape[0]
    workers = SC_CORES * SC_SUBCORES
    assert n % (workers * SC_TOKEN_CHUNK) == 0 and PEER_HEADS % SC_RING == 0
    per = n // workers
    head_rows = ROWS_PER_EXPERT * PEER_TOPK
    aux_rows = CHUNKS if down else 1
    out_rows = 1 if down else CHUNKS
    mesh = plsc.VectorSubcoreMesh(core_axis_name="core", subcore_axis_name="subcore",
                                  num_cores=SC_CORES, num_subcores=SC_SUBCORES)

    @pl.kernel(out_type=jax.ShapeDtypeStruct((n * out_rows, 128), F32), mesh=mesh,
               scratch_types=[pltpu.VMEM((SC_TOKEN_CHUNK, N_PAIRS), jnp.int32),
                              pltpu.VMEM((SC_TOKEN_CHUNK * aux_rows, 128), F32),
                              pltpu.VMEM((SC_RING, head_rows, 128), jnp.uint32),
                              pltpu.VMEM((SC_TOKEN_CHUNK * out_rows, 128), F32),
                              pltpu.VMEM((PEER_TOPK, SC_LANES), F32),
                              pltpu.SemaphoreType.DMA((SC_RING,))],
               compiler_params=pltpu.CompilerParams(needs_layout_passes=False),
               name="peer_u_sc" if down else "peer_v_sc")
    def body(e_hbm, aux_hbm, tab_hbm, o_hbm, e_v, aux_v, rows_v, out_v, fold_v, sems):
        wid = lax.axis_index("core") * SC_SUBCORES + lax.axis_index("subcore")
        lane = lax.broadcasted_iota(jnp.int32, (SC_LANES,), 0)

        def gathers(i, p, slot):
            first = e_v[i, pl.ds(p * PEER_TOPK, PEER_TOPK)]
            return [pltpu.make_async_copy(tab_hbm.at[first + r],
                                          rows_v.at[slot, pl.ds(r * PEER_TOPK, PEER_TOPK)], sems.at[slot])
                    for r in range(ROWS_PER_EXPERT)]

        def start_gather(i, p, slot):
            for d in gathers(i, p, slot):
                d.start()

        def weighted_sum(i, p, slot):
            ws = [plsc.load_gather(aux_v, [lane * 0 + i, lane * 0 + (p * PEER_TOPK + k)])
                  for k in range(PEER_TOPK)]
            for r in range(ROWS_PER_EXPERT):
                @pl.loop(0, WORD_ROWS)
                def _(j):
                    sl = pl.ds(j * SC_LANES, SC_LANES)
                    if p == 0:
                        a_hi = jnp.zeros((SC_LANES,), F32)
                        a_lo = jnp.zeros((SC_LANES,), F32)
                    else:
                        a_hi = out_v[i * CHUNKS + r, sl]
                        a_lo = out_v[i * CHUNKS + ROWS_PER_EXPERT + r, sl]
                    for k in range(PEER_TOPK):
                        hi, lo = _unpack(rows_v[slot, r * PEER_TOPK + k, sl])
                        a_hi = a_hi + ws[k] * hi
                        a_lo = a_lo + ws[k] * lo
                    out_v[i * CHUNKS + r, sl] = a_hi
                    out_v[i * CHUNKS + ROWS_PER_EXPERT + r, sl] = a_lo

        def pair_dots(i, p, slot):
            accs = tuple(jnp.zeros((SC_LANES,), F32) for _ in range(PEER_TOPK))
            for r in range(ROWS_PER_EXPERT):
                def piece(j, accs, r=r):
                    sl = pl.ds(j * SC_LANES, SC_LANES)
                    ha = aux_v[i * CHUNKS + r, sl]
                    hb = aux_v[i * CHUNKS + ROWS_PER_EXPERT + r, sl]
                    out = []
                    for k in range(PEER_TOPK):
                        hi, lo = _unpack(rows_v[slot, r * PEER_TOPK + k, sl])
                        out.append(accs[k] + (hi * ha + lo * hb))
                    return tuple(out)
                accs = lax.fori_loop(0, WORD_ROWS, piece, accs)
            for k in range(PEER_TOPK):
                fold_v[k, :] = accs[k]
            tot = plsc.load_gather(fold_v, [lane, lane * 0])
            for l in range(1, SC_LANES):
                tot = tot + plsc.load_gather(fold_v, [lane, lane * 0 + l])
            out_v[i, pl.ds(p * PEER_TOPK, PEER_TOPK)] = tot

        consume = pair_dots if down else weighted_sum

        @pl.loop(0, per // SC_TOKEN_CHUNK)
        def _(c):
            t0 = wid * per + c * SC_TOKEN_CHUNK
            pltpu.sync_copy(e_hbm.at[pl.ds(t0, SC_TOKEN_CHUNK)], e_v)
            pltpu.sync_copy(aux_hbm.at[pl.ds(t0 * aux_rows, SC_TOKEN_CHUNK * aux_rows)], aux_v)
            for p in range(SC_RING):
                start_gather(0, p, p)

            @pl.loop(0, SC_TOKEN_CHUNK)
            def _(i):
                for p in range(PEER_HEADS):
                    slot = p % SC_RING
                    for d in gathers(i, p, slot):
                        d.wait()
                    consume(i, p, slot)
                    if p + SC_RING < PEER_HEADS:
                        start_gather(i, p + SC_RING, slot)
                    else:
                        @pl.when(i + 1 < SC_TOKEN_CHUNK)
                        def _():
                            start_gather(i + 1, p + SC_RING - PEER_HEADS, slot)

            pltpu.sync_copy(out_v, o_hbm.at[pl.ds(t0 * out_rows, SC_TOKEN_CHUNK * out_rows)])

    return body(e_t, aux, tab)


def _pack_table(tab):
    bits = lax.bitcast_convert_type(tab.astype(BF16), jnp.uint16).astype(jnp.uint32)
    words = (bits[:, :HALF] << 16) | bits[:, HALF:]
    return words.reshape(tab.shape[0] * ROWS_PER_EXPERT, 128)


PEER_TOKENS = 128
SC_SHARE_DOWN = (15, 32)
SC_SHARE_UP = (47, 128)
SC_MIN_TOKENS = 4096


def _sc_tokens(n):
    if n < SC_MIN_TOKENS:
        return (0, 0)
    unit = math.lcm(PEER_TOKENS, SC_CORES * SC_SUBCORES * SC_TOKEN_CHUNK)
    return tuple(n * num // den // unit * unit for num, den in (SC_SHARE_DOWN, SC_SHARE_UP))


def _row_tile(n, pref):
    while n % pref:
        pref //= 2
    return pref


def _peer_and_final(x2, h3, qp, k1_bf, k2_bf, tab_u, tab_v, g_final, tt, n_sc=(0, 0)):
    n = x2.shape[0]
    n_sc_down, n_sc_up = n_sc
    tc_down, tc_up = n - n_sc_down, n - n_sc_up
    if n_sc_down:
        e_t, gate_t = _route(qp, k1_bf, k2_bf, tt, tc_down, n_sc_down)
        act = _peer_sc(e_t[tc_down:], h3[tc_down:].reshape(n_sc_down * CHUNKS, 128), tab_u, down=True)
        e_t, gate_t = _route(qp, k1_bf, k2_bf, tt, 0, tc_down, into=(e_t, gate_t))
    else:
        e_t, gate_t = _route(qp, k1_bf, k2_bf, tt)
    w_t = _peer_u(e_t, h3, gate_t, tab_u, tt, tc_down)
    if n_sc_down:
        w_t = _gate_act(act, gate_t, w_t, tc_down, tt)
    y = _peer_v(e_t, w_t, x2, g_final, tab_v, tt, tc_up)
    if n_sc_up:
        add8 = _peer_sc(e_t[tc_up:], w_t[tc_up:], tab_v, down=False)
        y = _residual_norm(x2, add8, g_final, y, tc_up, tt)
    return y


def kernel(x_prompt, x_sample, mem_prompt, cache_da_k, cache_da_v, state_ret, cache_mem_k, cache_mem_v, g_mix, w_in, lam_q1, lam_k1, lam_q2, lam_k2, g_da, g_ret, w_out, g_cross, g_mem, w_xq, w_xk, w_xv, w_xo, g_ffn, w_pq, peer_k1, peer_k2, peer_u, peer_v, g_final):
    depth = w_in.shape[0]
    assert depth == 1, "single-layer step"
    l = 0
    lam_init = 0.8 - 0.6 * math.exp(-0.3 * l)
    b, t, _ = x_prompt.shape
    bs, ts, _ = x_sample.shape
    past_len = cache_da_k.shape[2]

    row = lambda a: a.reshape(1, -1)
    w_in_bf = w_in[l].astype(BF16)
    w_out_bf = w_out[l].astype(BF16)
    w_xq_bf, w_xk_bf, w_xv_bf, w_xo_bf = (w[l].astype(BF16) for w in (w_xq, w_xk, w_xv, w_xo))
    w_pq_bf = w_pq[l].astype(BF16)
    k1_bf, k2_bf = peer_k1[l].astype(BF16), peer_k2[l].astype(BF16)
    tab_u, tab_v = _pack_table(peer_u[l]), _pack_table(peer_v[l])
    lamp = jnp.stack([lam_q1[l], lam_k1[l], lam_q2[l], lam_k2[l]])
    g_da3 = g_da[l].reshape(H_D, 1, DV_D)
    g_ret3 = g_ret[l].reshape(H_R, 1, DV_R)
    g_fin = row(g_final)

    def mixer_tail(x2d, mda, mret, mk, mv, bb, tt_rows, n_sc):
        n = x2d.shape[0]
        tm = _row_tile(n, 512)
        x1, qx = _out_proj(x2d, mda.reshape(n, GROUP_W), mret.reshape(n, GROUP_W), w_out_bf, row(g_cross[l]), w_xq_bf, tm)
        x2, h3, qp = _cross(x1.reshape(bb, tt_rows, D_MODEL), qx.reshape(bb, tt_rows, D_MODEL), mk, mv,
                            w_xo_bf, row(g_ffn[l]), w_pq_bf, _row_tile(tt_rows, 512))
        y = _peer_and_final(x2.reshape(n, D_MODEL), h3.reshape(n, D_MODEL), qp.reshape(n, -1),
                            k1_bf, k2_bf, tab_u, tab_v, g_fin, PEER_TOKENS, n_sc)
        return y.reshape(bb, tt_rows, D_MODEL)

    n = b * t
    xp = x_prompt.reshape(n, D_MODEL)
    qd, kd, vd, kdb, vdb, qr, kr, vr, gr = _in_proj(xp, row(g_mix[l]), w_in_bf, _row_tile(n, 512))
    r3 = lambda a: a.reshape(b, t, GROUP_W)
    mda = _diff_attn_prompt(lamp, r3(qd), r3(kdb), r3(vdb), g_da3, lam_init, 512, 512)
    mret, s_fin = _retention(r3(qr), r3(kr), r3(vr), r3(gr), g_ret3, None, 256)
    mk, mv, mkb, mvb = _mem_kv(mem_prompt.reshape(b * N_MEM, D_MODEL), row(g_mem[l]), w_xk_bf, w_xv_bf, 512)
    y_prompt = mixer_tail(xp, mda, mret, mkb.reshape(b, N_MEM, D_MODEL), mvb.reshape(b, N_MEM, D_MODEL), b, t, _sc_tokens(n))

    ns = bs * ts
    xs = x_sample.reshape(ns, D_MODEL)
    qd_s, kd_s, vd_s, kdb_s, vdb_s, qr_s, kr_s, vr_s, gr_s = _in_proj(xs, row(g_mix[l]), w_in_bf, _row_tile(ns, 512))
    s3 = lambda a: a.reshape(bs, ts, GROUP_W)
    mda_s = _diff_attn_sample(lamp, s3(qd_s), cache_da_k[l].reshape(bs, past_len, GROUP_W),
                              cache_da_v[l].reshape(bs, past_len, GROUP_W), s3(kdb_s), s3(vdb_s), g_da3, lam_init)
    mret_s, s_new = _retention(s3(qr_s), s3(kr_s), s3(vr_s), s3(gr_s), g_ret3, state_ret[l], ts)
    y_sample = mixer_tail(xs, mda_s, mret_s, cache_mem_k[l].reshape(bs, N_MEM, D_MODEL),
                          cache_mem_v[l].reshape(bs, N_MEM, D_MODEL), bs, ts, _sc_tokens(ns))

    return (y_prompt, y_sample,
            kd.reshape(1, b, t, H_D, 2, DK_D), vd.reshape(1, b, t, H_D, DV_D), s_fin[None],
            mk.reshape(1, b, N_MEM, H_X, HD_X), mv.reshape(1, b, N_MEM, H_X, HD_X),
            kd_s.reshape(1, bs, ts, H_D, 2, DK_D), vd_s.reshape(1, bs, ts, H_D, DV_D), s_new[None])
```

```python
import functools
import math

import jax
import jax.numpy as jnp
from jax import lax
from jax.experimental import pallas as pl
from jax.experimental.pallas import tpu as pltpu
from jax.experimental.pallas import tpu_sc as plsc

D_MODEL = 1024
CHUNK = 64
CHUNK_SHIFT = CHUNK.bit_length() - 1
assert 1 << CHUNK_SHIFT == CHUNK
H_D, DK_D, DV_D = 4, 64, 128
H_R, DK_R, DV_R = 4, 128, 128
N_MEM = 256
H_X = 4
HD_X = D_MODEL // H_X
PEER_HEADS = 8
N_KEYS = 128
N_EXPERTS = N_KEYS * N_KEYS
PEER_TOPK = 16
EPS = 1e-6
HEAD_W = 128
GROUP_W = 512
N_PAIRS = PEER_HEADS * PEER_TOPK
HALF = D_MODEL // 2
ROWS_PER_EXPERT = HALF // 128
VMEM_LIMIT = 56 * 1024 * 1024

BF16 = jnp.bfloat16
F32 = jnp.float32


def _cparams(sem):
    return pltpu.CompilerParams(dimension_semantics=sem, vmem_limit_bytes=VMEM_LIMIT)


def _rms(x, g):
    return x * lax.rsqrt(jnp.mean(x * x, axis=-1, keepdims=True) + EPS) * g


def _dot(a, b):
    return jnp.dot(a, b, preferred_element_type=F32)


def _dot_nt(a, b):
    return lax.dot_general(a, b, (((1,), (1,)), ((), ())), preferred_element_type=F32)


def _dot_tn(a, b):
    return lax.dot_general(a, b, (((0,), (0,)), ((), ())), preferred_element_type=F32)


def _select_by_head(h, values):
    out = jnp.float32(values[-1])
    for i in range(len(values) - 2, -1, -1):
        out = jnp.where(h == i, jnp.float32(values[i]), out)
    return out


def _in_proj_kernel(x_ref, g_ref, w_ref, qd_ref, kd_ref, vd_ref, kdb_ref, vdb_ref,
                    qr_ref, kr_ref, vr_ref, gr_ref):
    hb = _rms(x_ref[...], g_ref[...]).astype(BF16)
    col = lambda c: _dot(hb, w_ref[:, c * GROUP_W:(c + 1) * GROUP_W])
    qd_ref[...] = (col(0) * (DK_D ** -0.5)).astype(BF16)
    kd = col(1)
    kd_ref[...] = kd
    kdb_ref[...] = kd.astype(BF16)
    vd = col(2)
    vd_ref[...] = vd
    vdb_ref[...] = vd.astype(BF16)
    qr_ref[...] = col(3).astype(BF16)
    kr_ref[...] = (col(4) * (DK_R ** -0.5)).astype(BF16)
    vr_ref[...] = col(5).astype(BF16)
    gr_ref[...] = col(6)


def _in_proj(x2d, g, w_bf, tm):
    n = x2d.shape[0]
    blk = lambda: pl.BlockSpec((tm, GROUP_W), lambda i: (i, 0))
    sh = lambda dt: jax.ShapeDtypeStruct((n, GROUP_W), dt)
    return pl.pallas_call(
        _in_proj_kernel,
        grid=(n // tm,),
        in_specs=[pl.BlockSpec((tm, D_MODEL), lambda i: (i, 0)),
                  pl.BlockSpec((1, D_MODEL), lambda i: (0, 0)),
                  pl.BlockSpec(w_bf.shape, lambda i: (0, 0))],
        out_specs=[blk() for _ in range(9)],
        out_shape=[sh(BF16), sh(F32), sh(F32), sh(BF16), sh(BF16), sh(BF16), sh(BF16), sh(BF16), sh(F32)],
        compiler_params=_cparams(("parallel",)),
        name="in_proj",
    )(x2d, g, w_bf)


def _lambda_from(lam_ref, lam_init):
    l = lam_ref[...]
    a = jnp.exp(jnp.sum(l[0:1] * l[1:2], axis=-1, keepdims=True))
    b = jnp.exp(jnp.sum(l[2:3] * l[3:4], axis=-1, keepdims=True))
    return a - b + lam_init


def _diff_post(acc, l, lam, g, lam_init, tq):
    o = acc[:tq] / l[:tq] - lam * (acc[tq:] / l[tq:])
    return o * lax.rsqrt(jnp.mean(o * o, axis=-1, keepdims=True) + EPS) * g * (1.0 - lam_init)


def _split_maps(q):
    lane = lax.broadcasted_iota(jnp.int32, q.shape, 1)
    zero = jnp.zeros_like(q)
    return jnp.concatenate([jnp.where(lane < DK_D, q, zero), jnp.where(lane >= DK_D, q, zero)], axis=0)


def _da_prompt_kernel(lam_ref, q_ref, k_ref, v_ref, g_ref, o_ref, kx_ref, vx_ref, own_ref, acc_ref, m_ref, *, lam_init, tq, tk):
    h = pl.program_id(1)
    i = pl.program_id(2)
    t = k_ref.shape[0]
    slope = _select_by_head(h, [2.0 ** (-8.0 * (j + 1) / H_D) for j in range(H_D)])

    @pl.when(i == 0)
    def _():
        pos = lax.broadcasted_iota(jnp.int32, (t, HEAD_W), 0)
        lane = lax.broadcasted_iota(jnp.int32, (t, HEAD_W), 1)
        coarse = ((pos >> CHUNK_SHIFT) << CHUNK_SHIFT).astype(F32) * slope
        fine = (pos & (CHUNK - 1)).astype(F32) * slope
        kx_ref[:, :HEAD_W] = k_ref[...]
        kx_ref[:, HEAD_W:] = jnp.where(lane == 0, coarse, jnp.where(lane == 1, fine, 0.0)).astype(BF16)
        vx_ref[:DV_D, :] = v_ref[...].astype(F32).T.astype(BF16)
        vx_ref[DV_D:, :] = jnp.ones((vx_ref.shape[0] - DV_D, t), BF16)
        krel = lax.broadcasted_iota(jnp.int32, (tk, 2 * tq), 0)
        c = lax.broadcasted_iota(jnp.int32, (tk, 2 * tq), 1)
        for par in range(tk // tq):
            qrel = par * tq + jnp.where(c >= tq, c - tq, c)
            ahead = (2.0 * slope) * jnp.maximum(krel - qrel, 0).astype(F32)
            own_ref[par] = jnp.where((qrel >> CHUNK_SHIFT) >= (krel >> CHUNK_SHIFT), -ahead, -1e30)

    q = q_ref[...]
    lane = lax.broadcasted_iota(jnp.int32, q.shape, 1)
    zero = jnp.zeros_like(q)
    ones2 = jnp.where(lane < 2, 1.0, 0.0).astype(BF16)
    q2 = jnp.concatenate([jnp.concatenate([jnp.where(lane < DK_D, q, zero), ones2], axis=1),
                          jnp.concatenate([jnp.where(lane >= DK_D, q, zero), ones2], axis=1)], axis=0)
    jd = (i * tq) // tk

    def scores(j):
        return _dot_nt(kx_ref[pl.ds(pl.multiple_of(j * tk, tk), tk), :], q2)

    def values(j):
        return vx_ref[:, pl.ds(pl.multiple_of(j * tk, tk), tk)]

    s = scores(jd) + own_ref[(i * tq) % tk // tq]
    m0 = jnp.max(s, axis=0, keepdims=True)
    m_ref[...] = m0
    acc_ref[...] = _dot(values(jd), jnp.exp(s - m0).astype(BF16))

    def absorb(blocks):
        ss = [scores(j) for j in blocks]
        m_old = m_ref[...]
        m_new = m_old
        for s in ss:
            m_new = jnp.maximum(m_new, jnp.max(s, axis=0, keepdims=True))
        m_ref[...] = m_new
        acc = jnp.exp(m_old - m_new) * acc_ref[...]
        for j, s in zip(blocks, ss):
            acc = acc + _dot(values(j), jnp.exp(s - m_new).astype(BF16))
        acc_ref[...] = acc

    def past_pair(jj, carry):
        absorb([2 * jj, 2 * jj + 1])
        return carry

    lax.fori_loop(0, jd // 2, past_pair, 0)

    @pl.when(jd % 2 == 1)
    def _():
        absorb([jd - 1])

    acc = acc_ref[...]
    num, den = acc[:DV_D], acc[DV_D:DV_D + 1]
    lam = _lambda_from(lam_ref, lam_init)
    o = (num[:, :tq] / den[:, :tq] - lam * (num[:, tq:] / den[:, tq:])).T
    o = o * lax.rsqrt(jnp.mean(o * o, axis=-1, keepdims=True) + EPS) * g_ref[...] * (1.0 - lam_init)
    o_ref[...] = o.astype(o_ref.dtype)


ONES_ROWS = 16


def _diff_attn_prompt(lamp, q, k, v, g_da3, lam_init, tq, tk):
    b, t, _ = q.shape
    kern = functools.partial(_da_prompt_kernel, lam_init=lam_init, tq=tq, tk=tk)
    return pl.pallas_call(
        kern,
        grid=(b, H_D, t // tq),
        in_specs=[pl.BlockSpec((4, DK_D), lambda b_, h, i: (0, 0)),
                  pl.BlockSpec((None, tq, HEAD_W), lambda b_, h, i: (b_, i, h)),
                  pl.BlockSpec((None, t, HEAD_W), lambda b_, h, i: (b_, 0, h)),
                  pl.BlockSpec((None, t, HEAD_W), lambda b_, h, i: (b_, 0, h)),
                  pl.BlockSpec((None, 1, HEAD_W), lambda b_, h, i: (h, 0, 0))],
        out_specs=pl.BlockSpec((None, tq, HEAD_W), lambda b_, h, i: (b_, i, h)),
        out_shape=jax.ShapeDtypeStruct((b, t, GROUP_W), BF16),
        scratch_shapes=[pltpu.VMEM((t, 2 * HEAD_W), BF16), pltpu.VMEM((DV_D + ONES_ROWS, t), BF16),
                        pltpu.VMEM((tk // tq, tk, 2 * tq), F32),
                        pltpu.VMEM((DV_D + ONES_ROWS, 2 * tq), F32), pltpu.VMEM((1, 2 * tq), F32)],
        compiler_params=_cparams(("parallel", "parallel", "arbitrary")),
        name="diff_attn_prompt",
    )(lamp, q, k, v, g_da3)


def _da_sample_kernel(lam_ref, q_ref, kc_ref, vc_ref, kn_ref, vn_ref, g_ref, o_ref, *, lam_init, ts, past_len):
    h = pl.program_id(1)
    slope = _select_by_head(h, [2.0 ** (-8.0 * (j + 1) / H_D) for j in range(H_D)])
    q2 = _split_maps(q_ref[...])
    rows = 2 * ts

    def scores(k, base, n):
        r = lax.broadcasted_iota(jnp.int32, (rows, n), 0)
        qpos = past_len + jnp.where(r >= ts, r - ts, r)
        kpos = base + lax.broadcasted_iota(jnp.int32, (rows, n), 1)
        return _dot_nt(q2, k) - slope * jnp.abs(qpos - kpos).astype(F32)

    sc = scores(kc_ref[...].astype(BF16), 0, past_len)
    sn = scores(kn_ref[...], past_len, ts)
    m = jnp.maximum(jnp.max(sc, axis=-1, keepdims=True), jnp.max(sn, axis=-1, keepdims=True))
    pc = jnp.exp(sc - m)
    pn = jnp.exp(sn - m)
    l = jnp.sum(pc, axis=-1, keepdims=True) + jnp.sum(pn, axis=-1, keepdims=True)
    acc = _dot(pc.astype(BF16), vc_ref[...].astype(BF16)) + _dot(pn.astype(BF16), vn_ref[...])
    lam = _lambda_from(lam_ref, lam_init)
    o_ref[...] = _diff_post(acc, l, lam, g_ref[...], lam_init, ts).astype(o_ref.dtype)


def _diff_attn_sample(lamp, q, kc, vc, kn, vn, g_da3, lam_init):
    b, ts, _ = q.shape
    past_len = kc.shape[1]
    kern = functools.partial(_da_sample_kernel, lam_init=lam_init, ts=ts, past_len=past_len)
    head = lambda rows: pl.BlockSpec((None, rows, HEAD_W), lambda b_, h: (b_, 0, h))
    return pl.pallas_call(
        kern,
        grid=(b, H_D),
        in_specs=[pl.BlockSpec((4, DK_D), lambda b_, h: (0, 0)),
                  head(ts), head(past_len), head(past_len), head(ts), head(ts),
                  pl.BlockSpec((None, 1, HEAD_W), lambda b_, h: (h, 0, 0))],
        out_specs=head(ts),
        out_shape=jax.ShapeDtypeStruct((b, ts, GROUP_W), BF16),
        compiler_params=_cparams(("parallel", "parallel")),
        name="diff_attn_sample",
    )(lamp, q, kc, vc, kn, vn, g_da3)


def _ret_kernel(*refs, lb, has_init):
    if has_init:
        q_ref, k_ref, v_ref, gate_ref, g_ref, s0_ref, o_ref, sfin_ref, s_ref = refs
    else:
        q_ref, k_ref, v_ref, gate_ref, g_ref, o_ref, sfin_ref, s_ref = refs
    h = pl.program_id(1)
    c = pl.program_id(2)
    lg = _select_by_head(h, [math.log1p(-(2.0 ** (-5.0 - j))) for j in range(H_R)])

    @pl.when(c == 0)
    def _():
        s_ref[...] = s0_ref[...] if has_init else jnp.zeros_like(s_ref)

    q, k, v = q_ref[...], k_ref[...], v_ref[...]
    i = lax.broadcasted_iota(jnp.int32, (lb, lb), 0)
    j = lax.broadcasted_iota(jnp.int32, (lb, lb), 1)
    d = (i - j).astype(F32)
    decay = jnp.where(d >= 0, jnp.exp(jnp.maximum(d, 0.0) * lg), 0.0)
    inner = _dot_nt(q, k) * decay
    ic = lax.broadcasted_iota(jnp.int32, (lb, 1), 0).astype(F32)
    s_old = s_ref[...]
    o = _dot(inner.astype(BF16), v) + _dot(q, s_old.astype(BF16)) * jnp.exp((ic + 1.0) * lg)
    tail = jnp.exp((lb - 1.0 - ic) * lg)
    kt = (k.astype(F32) * tail).astype(BF16)
    s_new = jnp.exp(lb * lg) * s_old + _dot_tn(kt, v)
    s_ref[...] = s_new

    @pl.when(c == pl.num_programs(2) - 1)
    def _():
        sfin_ref[...] = s_new

    oc = o - jnp.mean(o, axis=-1, keepdims=True)
    y = oc * lax.rsqrt(jnp.mean(oc * oc, axis=-1, keepdims=True) + EPS) * g_ref[...]
    gate = gate_ref[...]
    o_ref[...] = (y * (gate * jax.nn.sigmoid(gate))).astype(o_ref.dtype)


def _retention(q, k, v, gate, g_ret3, s0, lb):
    b, t, _ = q.shape
    has_init = s0 is not None
    kern = functools.partial(_ret_kernel, lb=lb, has_init=has_init)
    head = lambda: pl.BlockSpec((None, lb, HEAD_W), lambda b_, h, c: (b_, c, h))
    state = lambda: pl.BlockSpec((None, None, DK_R, DV_R), lambda b_, h, c: (b_, h, 0, 0))
    in_specs = [head(), head(), head(), head(), pl.BlockSpec((None, 1, HEAD_W), lambda b_, h, c: (h, 0, 0))]
    args = [q, k, v, gate, g_ret3]
    if has_init:
        in_specs.append(state())
        args.append(s0)
    return pl.pallas_call(
        kern,
        grid=(b, H_R, t // lb),
        in_specs=in_specs,
        out_specs=[head(), state()],
        out_shape=[jax.ShapeDtypeStruct((b, t, GROUP_W), BF16), jax.ShapeDtypeStruct((b, H_R, DK_R, DV_R), F32)],
        scratch_shapes=[pltpu.VMEM((DK_R, DV_R), F32)],
        compiler_params=_cparams(("parallel", "parallel", "arbitrary")),
        name="retention",
    )(*args)


def _out_proj_kernel(x_ref, mda_ref, mret_ref, wo_ref, g_ref, wq_ref, x1_ref, qx_ref):
    x1 = x_ref[...] + _dot(mda_ref[...], wo_ref[:GROUP_W, :]) + _dot(mret_ref[...], wo_ref[GROUP_W:, :])
    x1_ref[...] = x1
    hn = _rms(x1, g_ref[...]).astype(BF16)
    qx_ref[...] = (_dot(hn, wq_ref[...]) * (HD_X ** -0.5)).astype(BF16)


def _out_proj(x2d, mda, mret, wo_bf, g_cross, wq_bf, tm):
    n = x2d.shape[0]
    full = lambda a: pl.BlockSpec(a.shape, lambda i: (0, 0))
    return pl.pallas_call(
        _out_proj_kernel,
        grid=(n // tm,),
        in_specs=[pl.BlockSpec((tm, D_MODEL), lambda i: (i, 0)),
                  pl.BlockSpec((tm, GROUP_W), lambda i: (i, 0)),
                  pl.BlockSpec((tm, GROUP_W), lambda i: (i, 0)),
                  full(wo_bf), full(g_cross), full(wq_bf)],
        out_specs=[pl.BlockSpec((tm, D_MODEL), lambda i: (i, 0)), pl.BlockSpec((tm, D_MODEL), lambda i: (i, 0))],
        out_shape=[jax.ShapeDtypeStruct((n, D_MODEL), F32), jax.ShapeDtypeStruct((n, D_MODEL), BF16)],
        compiler_params=_cparams(("parallel",)),
        name="out_proj",
    )(x2d, mda, mret, wo_bf, g_cross, wq_bf)


def _mem_kv_kernel(m_ref, g_ref, wk_ref, wv_ref, mk_ref, mv_ref, mkb_ref, mvb_ref):
    mn = _rms(m_ref[...], g_ref[...]).astype(BF16)
    mk = _dot(mn, wk_ref[...])
    mv = _dot(mn, wv_ref[...])
    mk_ref[...] = mk
    mv_ref[...] = mv
    mkb_ref[...] = mk.astype(BF16)
    mvb_ref[...] = mv.astype(BF16)


def _mem_kv(mem2d, g_mem, wk_bf, wv_bf, tm):
    n = mem2d.shape[0]
    row = lambda: pl.BlockSpec((tm, D_MODEL), lambda i: (i, 0))
    full = lambda a: pl.BlockSpec(a.shape, lambda i: (0, 0))
    sh = lambda dt: jax.ShapeDtypeStruct((n, D_MODEL), dt)
    return pl.pallas_call(
        _mem_kv_kernel,
        grid=(n // tm,),
        in_specs=[row(), full(g_mem), full(wk_bf), full(wv_bf)],
        out_specs=[row(), row(), row(), row()],
        out_shape=[sh(F32), sh(F32), sh(BF16), sh(BF16)],
        compiler_params=_cparams(("parallel",)),
        name="mem_kv",
    )(mem2d, g_mem, wk_bf, wv_bf)


def _cross_kernel(x1_ref, qx_ref, mk_ref, mv_ref, wo_ref, g_ref, wpq_ref, x2_ref, h3_ref, qp_ref):
    q = qx_ref[...]
    heads = []
    for h in range(H_X):
        sl = slice(h * HD_X, (h + 1) * HD_X)
        kh = mk_ref[:, sl] if mk_ref.dtype == BF16 else mk_ref[:, sl].astype(BF16)
        vh = mv_ref[:, sl] if mv_ref.dtype == BF16 else mv_ref[:, sl].astype(BF16)
        s = _dot_nt(q[:, sl], kh)
        p = jnp.exp(s - jnp.max(s, axis=-1, keepdims=True))
        heads.append(_dot(p.astype(BF16), vh) / jnp.sum(p, axis=-1, keepdims=True))
    o = jnp.concatenate(heads, axis=-1).astype(BF16)
    x2 = x1_ref[...] + _dot(o, wo_ref[...])
    x2_ref[...] = x2
    h3 = _rms(x2, g_ref[...])
    h3_ref[...] = h3
    qp_ref[...] = _dot(h3.astype(BF16), wpq_ref[...]).astype(BF16)


def _cross(x1, qx, mk, mv, wo_bf, g_ffn, wpq_bf, tm):
    b, t, _ = x1.shape
    row = lambda w: pl.BlockSpec((None, tm, w), lambda b_, i: (b_, i, 0))
    mem = lambda: pl.BlockSpec((None, N_MEM, D_MODEL), lambda b_, i: (b_, 0, 0))
    full = lambda a: pl.BlockSpec(a.shape, lambda b_, i: (0, 0))
    dq = wpq_bf.shape[1]
    return pl.pallas_call(
        _cross_kernel,
        grid=(b, t // tm),
        in_specs=[row(D_MODEL), row(D_MODEL), mem(), mem(), full(wo_bf), full(g_ffn), full(wpq_bf)],
        out_specs=[row(D_MODEL), row(D_MODEL), row(dq)],
        out_shape=[jax.ShapeDtypeStruct((b, t, D_MODEL), F32), jax.ShapeDtypeStruct((b, t, D_MODEL), F32),
                   jax.ShapeDtypeStruct((b, t, dq), BF16)],
        compiler_params=_cparams(("parallel", "parallel")),
        name="cross_attn",
    )(x1, qx, mk, mv, wo_bf, g_ffn, wpq_bf)


ID_PAD = 2.0 ** 29


def _topk_rows(s, ids):
    vals, sel = [], []
    for _ in range(PEER_TOPK):
        m = jnp.max(s, axis=0, keepdims=True)
        idx = jnp.min(jnp.where(s == m, ids, ID_PAD), axis=0, keepdims=True)
        vals.append(m)
        sel.append(idx)
        s = jnp.where(ids == idx, -jnp.inf, s)
    return jnp.concatenate(vals, axis=0), jnp.concatenate(sel, axis=0)


def _candidates(v1, i1, v2, i2):
    lanes = v1.shape[1]
    b8 = lax.broadcasted_iota(jnp.int32, (8, lanes), 0)
    ident = lambda a, ia, ib, b: (a * PEER_TOPK + b) * float(N_EXPERTS) + (ia * float(N_KEYS) + ib)
    b16 = lax.broadcasted_iota(jnp.int32, (PEER_TOPK, lanes), 0).astype(F32)
    vals = [v1[0:1] + v2]
    ids = [ident(0, i1[0:1], i2, b16)]
    for a in range(1, 8):
        keep = b8 < PEER_TOPK // (a + 1)
        vals.append(jnp.where(keep, v1[a:a + 1] + v2[0:8], -jnp.inf))
        ids.append(jnp.where(keep, ident(a, i1[a:a + 1], i2[0:8], b8.astype(F32)), ID_PAD))
    a_hi = (b8 + 8).astype(F32)
    vals.append(v1[8:16] + v2[0:1])
    ids.append(ident(a_hi, i1[8:16], i2[0:1], 0.0))
    return jnp.concatenate(vals, axis=0), jnp.concatenate(ids, axis=0)


def _route_kernel(qp_ref, k1_ref, k2_ref, e_ref, g_ref):
    half = N_KEYS
    tt = qp_ref.shape[0]
    key_id = lax.broadcasted_iota(jnp.int32, (N_KEYS, tt), 0).astype(F32)
    es, gs = [], []
    for p in range(PEER_HEADS):
        q1 = qp_ref[:, (2 * p) * half:(2 * p + 1) * half]
        q2 = qp_ref[:, (2 * p + 1) * half:(2 * p + 2) * half]
        v1, i1 = _topk_rows(_dot_nt(k1_ref[p], q1), key_id)
        v2, i2 = _topk_rows(_dot_nt(k2_ref[p], q2), key_id)
        sc, sel = _topk_rows(*_candidates(v1, i1, v2, i2))
        w = jnp.exp(sc - sc[0:1])
        es.append((sel.astype(jnp.int32) & (N_EXPERTS - 1)) * ROWS_PER_EXPERT)
        gs.append(w / jnp.sum(w, axis=0, keepdims=True))
    e_ref[...] = jnp.concatenate(es, axis=0).T
    g_ref[...] = jnp.concatenate(gs, axis=0).T


def _route_extra_kernel(qp_ref, k1_ref, k2_ref, *rest):
    _route_kernel(qp_ref, k1_ref, k2_ref, rest[-2], rest[-1])


def _route(qp2d, k1_bf, k2_bf, tt, first_token=0, n_tokens=None, into=None, after=()):
    n = qp2d.shape[0]
    n_tokens = n if n_tokens is None else n_tokens
    off = first_token // tt
    full = lambda a: pl.BlockSpec(a.shape, lambda i: (0, 0, 0))
    pair = lambda: pl.BlockSpec((tt, N_PAIRS), lambda i: (i + off, 0))
    in_specs = [pl.BlockSpec((tt, qp2d.shape[1]), lambda i: (i + off, 0)), full(k1_bf), full(k2_bf)]
    args = [qp2d, k1_bf, k2_bf]
    aliases = {}
    if into is not None:
        in_specs += [pl.BlockSpec(memory_space=pl.ANY), pl.BlockSpec(memory_space=pl.ANY)]
        args += list(into)
        aliases = {3: 0, 4: 1}
    in_specs += [pl.BlockSpec(memory_space=pl.ANY)] * len(after)
    args += list(after)
    return pl.pallas_call(
        _route_kernel if len(args) == 3 else _route_extra_kernel,
        grid=(n_tokens // tt,),
        in_specs=in_specs,
        out_specs=[pair(), pair()],
        out_shape=[jax.ShapeDtypeStruct((n, N_PAIRS), jnp.int32), jax.ShapeDtypeStruct((n, N_PAIRS), F32)],
        input_output_aliases=aliases,
        compiler_params=_cparams(("parallel",)),
        name="peer_route",
    )(*args)


def _unpack(words):
    hi = lax.bitcast_convert_type(words & jnp.uint32(0xFFFF0000), F32)
    lo = lax.bitcast_convert_type(words << 16, F32)
    return hi, lo


def _expert_rows(tab_ref, row0):
    return tab_ref[pl.ds(pl.multiple_of(row0, ROWS_PER_EXPERT), ROWS_PER_EXPERT), :]


REDUCE_TOKENS = 16


def _lane_sums_to_rows(y, n_tok):
    hi = y.astype(BF16)
    lo = (y - hi.astype(F32)).astype(BF16)
    ones = jnp.ones((128, N_PAIRS), BF16)
    s = (_dot(hi, ones) + _dot(lo, ones)).reshape(n_tok, N_PAIRS, N_PAIRS)
    eye = lax.broadcasted_iota(jnp.int32, (N_PAIRS, N_PAIRS), 0) == lax.broadcasted_iota(jnp.int32, (N_PAIRS, N_PAIRS), 1)
    return jnp.sum(jnp.where(eye[None], s, 0.0), axis=1)


CHUNKS = D_MODEL // 128


def _peer_u_kernel(e_ref, h_ref, gate_ref, tab_ref, w_ref, h8_ref, prod_ref, ys_ref, act_ref, *, tt):
    for c in range(CHUNKS):
        h8_ref[pl.ds(c, tt, stride=CHUNKS), :] = h_ref[:, c * 128:(c + 1) * 128]

    def token(t, slot):
        base = pl.multiple_of(t * CHUNKS, CHUNKS)
        ha = h8_ref[pl.ds(base, ROWS_PER_EXPERT), :]
        hb = h8_ref[pl.ds(pl.multiple_of(base + ROWS_PER_EXPERT, ROWS_PER_EXPERT), ROWS_PER_EXPERT), :]
        prod = prod_ref.at[slot]
        for k in range(N_PAIRS):
            hi, lo = _unpack(_expert_rows(tab_ref, e_ref[t, k]))
            prod[k * ROWS_PER_EXPERT:(k + 1) * ROWS_PER_EXPERT, :] = hi * ha + lo * hb
        y = prod[pl.ds(0, N_PAIRS, stride=ROWS_PER_EXPERT), :]
        for c in range(1, ROWS_PER_EXPERT):
            y = y + prod[pl.ds(c, N_PAIRS, stride=ROWS_PER_EXPERT), :]
        ys_ref[pl.ds(pl.multiple_of(t * N_PAIRS, N_PAIRS), N_PAIRS), :] = y

    def two_tokens(j, carry):
        token(2 * j, 0)
        token(2 * j + 1, 1)
        return carry

    lax.fori_loop(0, tt // 2, two_tokens, 0)

    def group(g, carry):
        rows = REDUCE_TOKENS * N_PAIRS
        y = ys_ref[pl.ds(pl.multiple_of(g * rows, rows), rows), :]
        act_ref[pl.ds(pl.multiple_of(g * REDUCE_TOKENS, REDUCE_TOKENS), REDUCE_TOKENS), :] = _lane_sums_to_rows(y, REDUCE_TOKENS)
        return carry

    lax.fori_loop(0, tt // REDUCE_TOKENS, group, 0)
    w_ref[...] = _gated_gelu(gate_ref[...], act_ref[...])


def _gated_gelu(gate, a):
    return gate * (0.5 * a * (1.0 + lax.erf(a * (2.0 ** -0.5))))


def _gate_act_kernel(act_ref, gate_ref, *rest):
    rest[-1][...] = _gated_gelu(gate_ref[...], act_ref[...])


def _gate_act(act, gate_t, w_t, first_token, tt, after=()):
    off = first_token // tt
    return pl.pallas_call(
        _gate_act_kernel,
        grid=(act.shape[0] // tt,),
        in_specs=[pl.BlockSpec((tt, N_PAIRS), lambda i: (i, 0)),
                  pl.BlockSpec((tt, N_PAIRS), lambda i: (i + off, 0))]
                 + [pl.BlockSpec(memory_space=pl.ANY)] * (1 + len(after)),
        out_specs=pl.BlockSpec((tt, N_PAIRS), lambda i: (i + off, 0)),
        out_shape=jax.ShapeDtypeStruct(w_t.shape, F32),
        input_output_aliases={2: 0},
        compiler_params=_cparams(("arbitrary",)),
        name="peer_gate_act",
    )(act, gate_t, w_t, *after)


def _peer_u(e_t, h, gate_t, tab, tt, n_tokens):
    n = e_t.shape[0]
    assert tt % REDUCE_TOKENS == 0
    kern = functools.partial(_peer_u_kernel, tt=tt)
    pair = lambda **kw: pl.BlockSpec((tt, N_PAIRS), lambda i: (i, 0), **kw)
    return pl.pallas_call(
        kern,
        grid=(n_tokens // tt,),
        in_specs=[pair(memory_space=pltpu.SMEM),
                  pl.BlockSpec((tt, D_MODEL), lambda i: (i, 0)),
                  pair(),
                  pl.BlockSpec(tab.shape, lambda i: (0, 0), pipeline_mode=pl.Buffered(1))],
        out_specs=pair(),
        out_shape=jax.ShapeDtypeStruct((n, N_PAIRS), F32),
        scratch_shapes=[pltpu.VMEM((tt * CHUNKS, 128), F32),
                        pltpu.VMEM((2, N_PAIRS * ROWS_PER_EXPERT, 128), F32),
                        pltpu.VMEM((tt * N_PAIRS, 128), F32),
                        pltpu.VMEM((tt, N_PAIRS), F32)],
        compiler_params=_cparams(("arbitrary",)),
        name="peer_u",
    )(e_t, h, gate_t, tab)


def _peer_v_kernel(e_ref, w_ref, x_ref, g_ref, tab_ref, o_ref, acc_ref, *, tt):
    n_acc = 2

    def token(t):
        acc_hi = [jnp.zeros((ROWS_PER_EXPERT, 128), F32) for _ in range(n_acc)]
        acc_lo = [jnp.zeros((ROWS_PER_EXPERT, 128), F32) for _ in range(n_acc)]
        for k in range(N_PAIRS):
            hi, lo = _unpack(_expert_rows(tab_ref, e_ref[t, k]))
            w = w_ref[t, k]
            acc_hi[k % n_acc] = acc_hi[k % n_acc] + w * hi
            acc_lo[k % n_acc] = acc_lo[k % n_acc] + w * lo
        base = pl.multiple_of(t * CHUNKS, CHUNKS)
        acc_ref[pl.ds(base, ROWS_PER_EXPERT), :] = sum(acc_hi[1:], acc_hi[0])
        acc_ref[pl.ds(pl.multiple_of(base + ROWS_PER_EXPERT, ROWS_PER_EXPERT), ROWS_PER_EXPERT), :] = sum(acc_lo[1:], acc_lo[0])

    def two_tokens(j, carry):
        token(2 * j)
        token(2 * j + 1)
        return carry

    lax.fori_loop(0, tt // 2, two_tokens, 0)
    _residual_rms_store(x_ref, acc_ref, g_ref, o_ref, tt)


def _residual_rms_store(x_ref, add_ref, g_ref, o_ref, tt):
    xs = [x_ref[:, c * 128:(c + 1) * 128] + add_ref[pl.ds(c, tt, stride=CHUNKS), :] for c in range(CHUNKS)]
    sq = xs[0] * xs[0]
    for x in xs[1:]:
        sq = sq + x * x
    r = lax.rsqrt(jnp.sum(sq, axis=1, keepdims=True) * (1.0 / D_MODEL) + EPS)
    for c in range(CHUNKS):
        o_ref[:, c * 128:(c + 1) * 128] = xs[c] * r * g_ref[:, c * 128:(c + 1) * 128]


def _peer_v(e_t, w_t, x, g_final, tab, tt, n_tokens):
    kern = functools.partial(_peer_v_kernel, tt=tt)
    pair = lambda: pl.BlockSpec((tt, N_PAIRS), lambda i: (i, 0), memory_space=pltpu.SMEM)
    row = lambda: pl.BlockSpec((tt, D_MODEL), lambda i: (i, 0))
    return pl.pallas_call(
        kern,
        grid=(n_tokens // tt,),
        in_specs=[pair(), pair(), row(),
                  pl.BlockSpec((1, D_MODEL), lambda i: (0, 0)),
                  pl.BlockSpec(tab.shape, lambda i: (0, 0), pipeline_mode=pl.Buffered(1))],
        out_specs=row(),
        out_shape=jax.ShapeDtypeStruct(x.shape, F32),
        scratch_shapes=[pltpu.VMEM((tt * CHUNKS, 128), F32)],
        compiler_params=_cparams(("arbitrary",)),
        name="peer_v",
    )(e_t, w_t, x, g_final, tab)


def _residual_norm_kernel(x_ref, add_ref, g_ref, y_any_ref, o_ref, *, tt):
    del y_any_ref
    _residual_rms_store(x_ref, add_ref, g_ref, o_ref, tt)


def _residual_norm(x, add8, g_final, y, first_token, tt):
    n_rows = add8.shape[0] // CHUNKS
    off = first_token // tt
    row = lambda: pl.BlockSpec((tt, D_MODEL), lambda i: (i + off, 0))
    return pl.pallas_call(
        functools.partial(_residual_norm_kernel, tt=tt),
        grid=(n_rows // tt,),
        in_specs=[row(), pl.BlockSpec((tt * CHUNKS, 128), lambda i: (i, 0)),
                  pl.BlockSpec((1, D_MODEL), lambda i: (0, 0)),
                  pl.BlockSpec(memory_space=pl.ANY)],
        out_specs=row(),
        out_shape=jax.ShapeDtypeStruct(y.shape, F32),
        input_output_aliases={3: 0},
        compiler_params=_cparams(("arbitrary",)),
        name="peer_residual_norm",
    )(x, add8, g_final, y)


SC_CORES = 2
SC_SUBCORES = 16
SC_LANES = 16
WORD_ROWS = 128 // SC_LANES
SC_TOKEN_CHUNK = 32
SC_RING = 4


def _peer_sc(e_t, aux, tab, down):
    n = e_t.shape[0]
    workers = SC_CORES * SC_SUBCORES
    assert n % (workers * SC_TOKEN_CHUNK) == 0 and PEER_HEADS % SC_RING == 0
    per = n // workers
    head_rows = ROWS_PER_EXPERT * PEER_TOPK
    aux_rows = CHUNKS if down else 1
    out_rows = 1 if down else CHUNKS
    mesh = plsc.VectorSubcoreMesh(core_axis_name="core", subcore_axis_name="subcore",
                                  num_cores=SC_CORES, num_subcores=SC_SUBCORES)

    @pl.kernel(out_type=jax.ShapeDtypeStruct((n * out_rows, 128), F32), mesh=mesh,
               scratch_types=[pltpu.VMEM((SC_TOKEN_CHUNK, N_PAIRS), jnp.int32),
                              pltpu.VMEM((SC_TOKEN_CHUNK * aux_rows, 128), F32),
                              pltpu.VMEM((SC_RING, head_rows, 128), jnp.uint32),
                              pltpu.VMEM((SC_TOKEN_CHUNK * out_rows, 128), F32),
                              pltpu.VMEM((PEER_TOPK, SC_LANES), F32),
                              pltpu.SemaphoreType.DMA((SC_RING,))],
               compiler_params=pltpu.CompilerParams(needs_layout_passes=False),
               name="peer_u_sc" if down else "peer_v_sc")
    def body(e_hbm, aux_hbm, tab_hbm, o_hbm, e_v, aux_v, rows_v, out_v, fold_v, sems):
        wid = lax.axis_index("core") * SC_SUBCORES + lax.axis_index("subcore")
        lane = lax.broadcasted_iota(jnp.int32, (SC_LANES,), 0)

        def gathers(i, p, slot):
            first = e_v[i, pl.ds(p * PEER_TOPK, PEER_TOPK)]
            return [pltpu.make_async_copy(tab_hbm.at[first + r],
                                          rows_v.at[slot, pl.ds(r * PEER_TOPK, PEER_TOPK)], sems.at[slot])
                    for r in range(ROWS_PER_EXPERT)]

        def start_gather(i, p, slot):
            for d in gathers(i, p, slot):
                d.start()

        def weighted_sum(i, p, slot):
            ws = [plsc.load_gather(aux_v, [lane * 0 + i, lane * 0 + (p * PEER_TOPK + k)])
                  for k in range(PEER_TOPK)]
            for r in range(ROWS_PER_EXPERT):
                @pl.loop(0, WORD_ROWS)
                def _(j):
                    sl = pl.ds(j * SC_LANES, SC_LANES)
                    if p == 0:
                        a_hi = jnp.zeros((SC_LANES,), F32)
                        a_lo = jnp.zeros((SC_LANES,), F32)
                    else:
                        a_hi = out_v[i * CHUNKS + r, sl]
                        a_lo = out_v[i * CHUNKS + ROWS_PER_EXPERT + r, sl]
                    for k in range(PEER_TOPK):
                        hi, lo = _unpack(rows_v[slot, r * PEER_TOPK + k, sl])
                        a_hi = a_hi + ws[k] * hi
                        a_lo = a_lo + ws[k] * lo
                    out_v[i * CHUNKS + r, sl] = a_hi
                    out_v[i * CHUNKS + ROWS_PER_EXPERT + r, sl] = a_lo

        def pair_dots(i, p, slot):
            accs = tuple(jnp.zeros((SC_LANES,), F32) for _ in range(PEER_TOPK))
            for r in range(ROWS_PER_EXPERT):
                def piece(j, accs, r=r):
                    sl = pl.ds(j * SC_LANES, SC_LANES)
                    ha = aux_v[i * CHUNKS + r, sl]
                    hb = aux_v[i * CHUNKS + ROWS_PER_EXPERT + r, sl]
                    out = []
                    for k in range(PEER_TOPK):
                        hi, lo = _unpack(rows_v[slot, r * PEER_TOPK + k, sl])
                        out.append(accs[k] + (hi * ha + lo * hb))
                    return tuple(out)
                accs = lax.fori_loop(0, WORD_ROWS, piece, accs)
            for k in range(PEER_TOPK):
                fold_v[k, :] = accs[k]
            tot = plsc.load_gather(fold_v, [lane, lane * 0])
            for l in range(1, SC_LANES):
                tot = tot + plsc.load_gather(fold_v, [lane, lane * 0 + l])
            out_v[i, pl.ds(p * PEER_TOPK, PEER_TOPK)] = tot

        consume = pair_dots if down else weighted_sum

        @pl.loop(0, per // SC_TOKEN_CHUNK)
        def _(c):
            t0 = wid * per + c * SC_TOKEN_CHUNK
            pltpu.sync_copy(e_hbm.at[pl.ds(t0, SC_TOKEN_CHUNK)], e_v)
            pltpu.sync_copy(aux_hbm.at[pl.ds(t0 * aux_rows, SC_TOKEN_CHUNK * aux_rows)], aux_v)
            for p in range(SC_RING):
                start_gather(0, p, p)

            @pl.loop(0, SC_TOKEN_CHUNK)
            def _(i):
                for p in range(PEER_HEADS):
                    slot = p % SC_RING
                    for d in gathers(i, p, slot):
                        d.wait()
                    consume(i, p, slot)
                    if p + SC_RING < PEER_HEADS:
                        start_gather(i, p + SC_RING, slot)
                    else:
                        @pl.when(i + 1 < SC_TOKEN_CHUNK)
                        def _():
                            start_gather(i + 1, p + SC_RING - PEER_HEADS, slot)

            pltpu.sync_copy(out_v, o_hbm.at[pl.ds(t0 * out_rows, SC_TOKEN_CHUNK * out_rows)])

    return body(e_t, aux, tab)


def _pack_table(tab):
    bits = lax.bitcast_convert_type(tab.astype(BF16), jnp.uint16).astype(jnp.uint32)
    words = (bits[:, :HALF] << 16) | bits[:, HALF:]
    return words.reshape(tab.shape[0] * ROWS_PER_EXPERT, 128)


PEER_TOKENS = 128
SC_LEAD_BATCHES = (1, 4)
SC_SHARE_DOWN = (1, 3)
SC_SHARE_UP = (3, 8)
SC_MIN_TOKENS = 4096


def _sc_tokens(n):
    if n < SC_MIN_TOKENS:
        return (0, 0)
    unit = math.lcm(PEER_TOKENS, SC_CORES * SC_SUBCORES * SC_TOKEN_CHUNK)
    return tuple(n * num // den // unit * unit for num, den in (SC_SHARE_DOWN, SC_SHARE_UP))


def _row_tile(n, pref):
    while n % pref:
        pref //= 2
    return pref


def _peer_on_sc_down(h3, qp, k1_bf, k2_bf, tab_u, tt):
    n = h3.shape[0]
    e_t, gate_t = _route(qp, k1_bf, k2_bf, tt)
    return e_t, gate_t, _peer_sc(e_t, h3.reshape(n * CHUNKS, 128), tab_u, down=True)


def _peer_on_sc_up(x2, e_t, gate_t, act, tab_v, g_final, tt, gate_after):
    w_t = _gate_act(act, gate_t, act, 0, tt, after=gate_after)
    add8 = _peer_sc(e_t, w_t, tab_v, down=False)
    return _residual_norm(x2, add8, g_final, x2, 0, tt), w_t


def _peer_and_final(x2, h3, qp, k1_bf, k2_bf, tab_u, tab_v, g_final, tt, n_sc=(0, 0), route_after=()):
    n = x2.shape[0]
    n_sc_down, n_sc_up = n_sc
    tc_down, tc_up = n - n_sc_down, n - n_sc_up
    if n_sc_down:
        e_t, gate_t = _route(qp, k1_bf, k2_bf, tt, tc_down, n_sc_down, after=route_after)
        act = _peer_sc(e_t[tc_down:], h3[tc_down:].reshape(n_sc_down * CHUNKS, 128), tab_u, down=True)
        e_t, gate_t = _route(qp, k1_bf, k2_bf, tt, 0, tc_down, into=(e_t, gate_t))
    else:
        e_t, gate_t = _route(qp, k1_bf, k2_bf, tt)
    w_t = _peer_u(e_t, h3, gate_t, tab_u, tt, tc_down)
    if n_sc_down:
        w_t = _gate_act(act, gate_t, w_t, tc_down, tt)
    y = _peer_v(e_t, w_t, x2, g_final, tab_v, tt, tc_up)
    if n_sc_up:
        add8 = _peer_sc(e_t[tc_up:], w_t[tc_up:], tab_v, down=False)
        y = _residual_norm(x2, add8, g_final, y, tc_up, tt)
    return y


def kernel(x_prompt, x_sample, mem_prompt, cache_da_k, cache_da_v, state_ret, cache_mem_k, cache_mem_v, g_mix, w_in, lam_q1, lam_k1, lam_q2, lam_k2, g_da, g_ret, w_out, g_cross, g_mem, w_xq, w_xk, w_xv, w_xo, g_ffn, w_pq, peer_k1, peer_k2, peer_u, peer_v, g_final):
    depth = w_in.shape[0]
    assert depth == 1, "single-layer step"
    l = 0
    lam_init = 0.8 - 0.6 * math.exp(-0.3 * l)
    b, t, _ = x_prompt.shape
    bs, ts, _ = x_sample.shape
    past_len = cache_da_k.shape[2]

    row = lambda a: a.reshape(1, -1)
    w_in_bf = w_in[l].astype(BF16)
    w_out_bf = w_out[l].astype(BF16)
    w_xq_bf, w_xk_bf, w_xv_bf, w_xo_bf = (w[l].astype(BF16) for w in (w_xq, w_xk, w_xv, w_xo))
    w_pq_bf = w_pq[l].astype(BF16)
    k1_bf, k2_bf = peer_k1[l].astype(BF16), peer_k2[l].astype(BF16)
    tab_u, tab_v = _pack_table(peer_u[l]), _pack_table(peer_v[l])
    lamp = jnp.stack([lam_q1[l], lam_k1[l], lam_q2[l], lam_k2[l]])
    g_da3 = g_da[l].reshape(H_D, 1, DV_D)
    g_ret3 = g_ret[l].reshape(H_R, 1, DV_R)
    g_fin = row(g_final)

    def mixer_mid(x2d, mda, mret, mk, mv, bb, tt_rows):
        n = x2d.shape[0]
        tm = _row_tile(n, 512)
        x1, qx = _out_proj(x2d, mda.reshape(n, GROUP_W), mret.reshape(n, GROUP_W), w_out_bf, row(g_cross[l]), w_xq_bf, tm)
        x2, h3, qp = _cross(x1.reshape(bb, tt_rows, D_MODEL), qx.reshape(bb, tt_rows, D_MODEL), mk, mv,
                            w_xo_bf, row(g_ffn[l]), w_pq_bf, _row_tile(tt_rows, 512))
        return x2.reshape(n, D_MODEL), h3.reshape(n, D_MODEL), qp.reshape(n, -1)

    peer_args = (k1_bf, k2_bf, tab_u, tab_v, g_fin, PEER_TOKENS)

    def prompt_dense(xg, mkb_g, mvb_g):
        bg = xg.shape[0]
        xp = xg.reshape(bg * t, D_MODEL)
        qd, kd, vd, kdb, vdb, qr, kr, vr, gr = _in_proj(xp, row(g_mix[l]), w_in_bf, _row_tile(bg * t, 512))
        r3 = lambda a: a.reshape(bg, t, GROUP_W)
        mda = _diff_attn_prompt(lamp, r3(qd), r3(kdb), r3(vdb), g_da3, lam_init, 512, 512)
        mret, s_fin = _retention(r3(qr), r3(kr), r3(vr), r3(gr), g_ret3, None, 512)
        return mixer_mid(xp, mda, mret, mkb_g, mvb_g, bg, t), (kd, vd, s_fin)

    mk, mv, mkb, mvb = _mem_kv(mem_prompt.reshape(b * N_MEM, D_MODEL), row(g_mem[l]), w_xk_bf, w_xv_bf, 512)
    mkb, mvb = mkb.reshape(b, N_MEM, D_MODEL), mvb.reshape(b, N_MEM, D_MODEL)
    b_rest = b - b * SC_LEAD_BATCHES[0] // SC_LEAD_BATCHES[1]
    if b_rest < b:
        (x2_l, h3_l, qp_l), (kd_l, vd_l, s_l) = prompt_dense(x_prompt[b_rest:], mkb[b_rest:], mvb[b_rest:])
        e_l, gate_l, act_l = _peer_on_sc_down(h3_l, qp_l, k1_bf, k2_bf, tab_u, PEER_TOKENS)
        peer_rest, (kd_r, vd_r, s_r) = prompt_dense(x_prompt[:b_rest], mkb[:b_rest], mvb[:b_rest])
        y_lead, w_lead = _peer_on_sc_up(x2_l, e_l, gate_l, act_l, tab_v, g_fin, PEER_TOKENS, gate_after=(peer_rest[2],))
        y_rest = _peer_and_final(*peer_rest, *peer_args, _sc_tokens(b_rest * t), route_after=(w_lead,))
        y_prompt = jnp.concatenate([y_rest, y_lead]).reshape(b, t, D_MODEL)
        kd, vd, s_fin = (jnp.concatenate(p) for p in ((kd_r, kd_l), (vd_r, vd_l), (s_r, s_l)))
    else:
        peer_all, (kd, vd, s_fin) = prompt_dense(x_prompt, mkb, mvb)
        y_prompt = _peer_and_final(*peer_all, *peer_args, _sc_tokens(b * t)).reshape(b, t, D_MODEL)

    ns = bs * ts
    xs = x_sample.reshape(ns, D_MODEL)
    qd_s, kd_s, vd_s, kdb_s, vdb_s, qr_s, kr_s, vr_s, gr_s = _in_proj(xs, row(g_mix[l]), w_in_bf, _row_tile(ns, 512))
    s3 = lambda a: a.reshape(bs, ts, GROUP_W)
    mda_s = _diff_attn_sample(lamp, s3(qd_s), cache_da_k[l].reshape(bs, past_len, GROUP_W),
                              cache_da_v[l].reshape(bs, past_len, GROUP_W), s3(kdb_s), s3(vdb_s), g_da3, lam_init)
    mret_s, s_new = _retention(s3(qr_s), s3(kr_s), s3(vr_s), s3(gr_s), g_ret3, state_ret[l], ts)
    peer_s = mixer_mid(xs, mda_s, mret_s, cache_mem_k[l].reshape(bs, N_MEM, D_MODEL),
                       cache_mem_v[l].reshape(bs, N_MEM, D_MODEL), bs, ts)
    y_sample = _peer_and_final(*peer_s, *peer_args).reshape(bs, ts, D_MODEL)

    return (y_prompt, y_sample,
            kd.reshape(1, b, t, H_D, 2, DK_D), vd.reshape(1, b, t, H_D, DV_D), s_fin[None],
            mk.reshape(1, b, N_MEM, H_X, HD_X), mv.reshape(1, b, N_MEM, H_X, HD_X),
            kd_s.reshape(1, bs, ts, H_D, 2, DK_D), vd_s.reshape(1, bs, ts, H_D, DV_D), s_new[None])
```

```python
import functools
import math

import jax
import jax.numpy as jnp
from jax import lax
from jax.experimental import pallas as pl
from jax.experimental.pallas import tpu as pltpu
from jax.experimental.pallas import tpu_sc as plsc

D_MODEL = 1024
CHUNK = 64
CHUNK_SHIFT = CHUNK.bit_length() - 1
assert 1 << CHUNK_SHIFT == CHUNK
H_D, DK_D, DV_D = 4, 64, 128
H_R, DK_R, DV_R = 4, 128, 128
N_MEM = 256
H_X = 4
HD_X = D_MODEL // H_X
PEER_HEADS = 8
N_KEYS = 128
N_EXPERTS = N_KEYS * N_KEYS
PEER_TOPK = 16
EPS = 1e-6
HEAD_W = 128
GROUP_W = 512
N_PAIRS = PEER_HEADS * PEER_TOPK
HALF = D_MODEL // 2
ROWS_PER_EXPERT = HALF // 128
VMEM_LIMIT = 56 * 1024 * 1024

BF16 = jnp.bfloat16
F32 = jnp.float32


def _cparams(sem):
    return pltpu.CompilerParams(dimension_semantics=sem, vmem_limit_bytes=VMEM_LIMIT)


def _rms(x, g):
    return x * lax.rsqrt(jnp.mean(x * x, axis=-1, keepdims=True) + EPS) * g


def _dot(a, b):
    return jnp.dot(a, b, preferred_element_type=F32)


def _dot_nt(a, b):
    return lax.dot_general(a, b, (((1,), (1,)), ((), ())), preferred_element_type=F32)


def _dot_tn(a, b):
    return lax.dot_general(a, b, (((0,), (0,)), ((), ())), preferred_element_type=F32)


def _select_by_head(h, values):
    out = jnp.float32(values[-1])
    for i in range(len(values) - 2, -1, -1):
        out = jnp.where(h == i, jnp.float32(values[i]), out)
    return out


def _in_proj_kernel(x_ref, g_ref, w_ref, *rest):
    qd_ref, kd_ref, vd_ref, kdb_ref, vdb_ref, qr_ref, kr_ref, vr_ref, gr_ref = rest[-9:]
    hb = _rms(x_ref[...], g_ref[...]).astype(BF16)
    col = lambda c: _dot(hb, w_ref[:, c * GROUP_W:(c + 1) * GROUP_W])
    qd_ref[...] = (col(0) * (DK_D ** -0.5)).astype(BF16)
    kd = col(1)
    kd_ref[...] = kd
    kdb_ref[...] = kd.astype(BF16)
    vd = col(2)
    vd_ref[...] = vd
    vdb_ref[...] = vd.astype(BF16)
    qr_ref[...] = col(3).astype(BF16)
    kr_ref[...] = (col(4) * (DK_R ** -0.5)).astype(BF16)
    vr_ref[...] = col(5).astype(BF16)
    gr_ref[...] = col(6)


def _in_proj(x2d, g, w_bf, tm, first_row=0, n_rows=None, kv_into=None):
    n_all = x2d.shape[0]
    n = n_all if n_rows is None else n_rows
    off = first_row // tm
    blk = lambda: pl.BlockSpec((tm, GROUP_W), lambda i: (i, 0))
    blk_all = lambda: pl.BlockSpec((tm, GROUP_W), lambda i: (i + off, 0))
    sh = lambda dt: jax.ShapeDtypeStruct((n, GROUP_W), dt)
    sh_all = jax.ShapeDtypeStruct((n_all, GROUP_W), F32)
    args = [x2d, g, w_bf]
    in_specs = [pl.BlockSpec((tm, D_MODEL), lambda i: (i + off, 0)),
                pl.BlockSpec((1, D_MODEL), lambda i: (0, 0)),
                pl.BlockSpec(w_bf.shape, lambda i: (0, 0))]
    aliases = {}
    if kv_into is not None:
        args += list(kv_into)
        in_specs += [pl.BlockSpec(memory_space=pl.ANY)] * 2
        aliases = {3: 1, 4: 2}
    return pl.pallas_call(
        _in_proj_kernel,
        grid=(n // tm,),
        in_specs=in_specs,
        out_specs=[blk(), blk_all(), blk_all()] + [blk() for _ in range(6)],
        out_shape=[sh(BF16), sh_all, sh_all, sh(BF16), sh(BF16), sh(BF16), sh(BF16), sh(BF16), sh(F32)],
        input_output_aliases=aliases,
        compiler_params=_cparams(("parallel",)),
        name="in_proj",
    )(*args)


def _lambda_from(lam_ref, lam_init):
    l = lam_ref[...]
    a = jnp.exp(jnp.sum(l[0:1] * l[1:2], axis=-1, keepdims=True))
    b = jnp.exp(jnp.sum(l[2:3] * l[3:4], axis=-1, keepdims=True))
    return a - b + lam_init


def _diff_post(acc, l, lam, g, lam_init, tq):
    o = acc[:tq] / l[:tq] - lam * (acc[tq:] / l[tq:])
    return o * lax.rsqrt(jnp.mean(o * o, axis=-1, keepdims=True) + EPS) * g * (1.0 - lam_init)


def _split_maps(q):
    lane = lax.broadcasted_iota(jnp.int32, q.shape, 1)
    zero = jnp.zeros_like(q)
    return jnp.concatenate([jnp.where(lane < DK_D, q, zero), jnp.where(lane >= DK_D, q, zero)], axis=0)


def _da_prompt_kernel(lam_ref, q_ref, k_ref, v_ref, g_ref, o_ref, kx_ref, vx_ref, own_ref, acc_ref, m_ref, *, lam_init, tq, tk):
    h = pl.program_id(1)
    i = pl.program_id(2)
    t = k_ref.shape[0]
    slope = _select_by_head(h, [2.0 ** (-8.0 * (j + 1) / H_D) for j in range(H_D)])

    @pl.when(i == 0)
    def _():
        pos = lax.broadcasted_iota(jnp.int32, (t, HEAD_W), 0)
        lane = lax.broadcasted_iota(jnp.int32, (t, HEAD_W), 1)
        coarse = ((pos >> CHUNK_SHIFT) << CHUNK_SHIFT).astype(F32) * slope
        fine = (pos & (CHUNK - 1)).astype(F32) * slope
        kx_ref[:, :HEAD_W] = k_ref[...]
        kx_ref[:, HEAD_W:] = jnp.where(lane == 0, coarse, jnp.where(lane == 1, fine, 0.0)).astype(BF16)
        vx_ref[:DV_D, :] = v_ref[...].astype(F32).T.astype(BF16)
        vx_ref[DV_D:, :] = jnp.ones((vx_ref.shape[0] - DV_D, t), BF16)
        krel = lax.broadcasted_iota(jnp.int32, (tk, 2 * tq), 0)
        c = lax.broadcasted_iota(jnp.int32, (tk, 2 * tq), 1)
        for par in range(tk // tq):
            qrel = par * tq + jnp.where(c >= tq, c - tq, c)
            ahead = (2.0 * slope) * jnp.maximum(krel - qrel, 0).astype(F32)
            own_ref[par] = jnp.where((qrel >> CHUNK_SHIFT) >= (krel >> CHUNK_SHIFT), -ahead, -1e30)

    q = q_ref[...]
    lane = lax.broadcasted_iota(jnp.int32, q.shape, 1)
    zero = jnp.zeros_like(q)
    ones2 = jnp.where(lane < 2, 1.0, 0.0).astype(BF16)
    q2 = jnp.concatenate([jnp.concatenate([jnp.where(lane < DK_D, q, zero), ones2], axis=1),
                          jnp.concatenate([jnp.where(lane >= DK_D, q, zero), ones2], axis=1)], axis=0)
    jd = (i * tq) // tk

    def scores(j):
        return _dot_nt(kx_ref[pl.ds(pl.multiple_of(j * tk, tk), tk), :], q2)

    def values(j):
        return vx_ref[:, pl.ds(pl.multiple_of(j * tk, tk), tk)]

    s = scores(jd) + own_ref[(i * tq) % tk // tq]
    m0 = jnp.max(s, axis=0, keepdims=True)
    m_ref[...] = m0
    acc_ref[...] = _dot(values(jd), jnp.exp(s - m0).astype(BF16))

    def absorb(blocks):
        ss = [scores(j) for j in blocks]
        m_old = m_ref[...]
        m_new = m_old
        for s in ss:
            m_new = jnp.maximum(m_new, jnp.max(s, axis=0, keepdims=True))
        m_ref[...] = m_new
        acc = jnp.exp(m_old - m_new) * acc_ref[...]
        for j, s in zip(blocks, ss):
            acc = acc + _dot(values(j), jnp.exp(s - m_new).astype(BF16))
        acc_ref[...] = acc

    def past_pair(jj, carry):
        absorb([2 * jj, 2 * jj + 1])
        return carry

    lax.fori_loop(0, jd // 2, past_pair, 0)

    @pl.when(jd % 2 == 1)
    def _():
        absorb([jd - 1])

    acc = acc_ref[...]
    num, den = acc[:DV_D], acc[DV_D:DV_D + 1]
    lam = _lambda_from(lam_ref, lam_init)
    o = (num[:, :tq] / den[:, :tq] - lam * (num[:, tq:] / den[:, tq:])).T
    o = o * lax.rsqrt(jnp.mean(o * o, axis=-1, keepdims=True) + EPS) * g_ref[...] * (1.0 - lam_init)
    o_ref[...] = o.astype(o_ref.dtype)


ONES_ROWS = 16


def _diff_attn_prompt(lamp, q, k, v, g_da3, lam_init, tq, tk):
    b, t, _ = q.shape
    kern = functools.partial(_da_prompt_kernel, lam_init=lam_init, tq=tq, tk=tk)
    return pl.pallas_call(
        kern,
        grid=(b, H_D, t // tq),
        in_specs=[pl.BlockSpec((4, DK_D), lambda b_, h, i: (0, 0)),
                  pl.BlockSpec((None, tq, HEAD_W), lambda b_, h, i: (b_, i, h)),
                  pl.BlockSpec((None, t, HEAD_W), lambda b_, h, i: (b_, 0, h)),
                  pl.BlockSpec((None, t, HEAD_W), lambda b_, h, i: (b_, 0, h)),
                  pl.BlockSpec((None, 1, HEAD_W), lambda b_, h, i: (h, 0, 0))],
        out_specs=pl.BlockSpec((None, tq, HEAD_W), lambda b_, h, i: (b_, i, h)),
        out_shape=jax.ShapeDtypeStruct((b, t, GROUP_W), BF16),
        scratch_shapes=[pltpu.VMEM((t, 2 * HEAD_W), BF16), pltpu.VMEM((DV_D + ONES_ROWS, t), BF16),
                        pltpu.VMEM((tk // tq, tk, 2 * tq), F32),
                        pltpu.VMEM((DV_D + ONES_ROWS, 2 * tq), F32), pltpu.VMEM((1, 2 * tq), F32)],
        compiler_params=_cparams(("parallel", "parallel", "arbitrary")),
        name="diff_attn_prompt",
    )(lamp, q, k, v, g_da3)


def _da_sample_kernel(lam_ref, q_ref, kc_ref, vc_ref, kn_ref, vn_ref, g_ref, o_ref, *, lam_init, ts, past_len):
    h = pl.program_id(1)
    slope = _select_by_head(h, [2.0 ** (-8.0 * (j + 1) / H_D) for j in range(H_D)])
    q2 = _split_maps(q_ref[...])
    rows = 2 * ts

    def scores(k, base, n):
        r = lax.broadcasted_iota(jnp.int32, (rows, n), 0)
        qpos = past_len + jnp.where(r >= ts, r - ts, r)
        kpos = base + lax.broadcasted_iota(jnp.int32, (rows, n), 1)
        return _dot_nt(q2, k) - slope * jnp.abs(qpos - kpos).astype(F32)

    sc = scores(kc_ref[...].astype(BF16), 0, past_len)
    sn = scores(kn_ref[...], past_len, ts)
    m = jnp.maximum(jnp.max(sc, axis=-1, keepdims=True), jnp.max(sn, axis=-1, keepdims=True))
    pc = jnp.exp(sc - m)
    pn = jnp.exp(sn - m)
    l = jnp.sum(pc, axis=-1, keepdims=True) + jnp.sum(pn, axis=-1, keepdims=True)
    acc = _dot(pc.astype(BF16), vc_ref[...].astype(BF16)) + _dot(pn.astype(BF16), vn_ref[...])
    lam = _lambda_from(lam_ref, lam_init)
    o_ref[...] = _diff_post(acc, l, lam, g_ref[...], lam_init, ts).astype(o_ref.dtype)


def _diff_attn_sample(lamp, q, kc, vc, kn, vn, g_da3, lam_init):
    b, ts, _ = q.shape
    past_len = kc.shape[1]
    kern = functools.partial(_da_sample_kernel, lam_init=lam_init, ts=ts, past_len=past_len)
    head = lambda rows: pl.BlockSpec((None, rows, HEAD_W), lambda b_, h: (b_, 0, h))
    return pl.pallas_call(
        kern,
        grid=(b, H_D),
        in_specs=[pl.BlockSpec((4, DK_D), lambda b_, h: (0, 0)),
                  head(ts), head(past_len), head(past_len), head(ts), head(ts),
                  pl.BlockSpec((None, 1, HEAD_W), lambda b_, h: (h, 0, 0))],
        out_specs=head(ts),
        out_shape=jax.ShapeDtypeStruct((b, ts, GROUP_W), BF16),
        compiler_params=_cparams(("parallel", "parallel")),
        name="diff_attn_sample",
    )(lamp, q, kc, vc, kn, vn, g_da3)


def _ret_kernel(*refs, lb, has_init):
    if has_init:
        q_ref, k_ref, v_ref, gate_ref, g_ref, s0_ref, o_ref, sfin_ref, s_ref = refs
    else:
        q_ref, k_ref, v_ref, gate_ref, g_ref, o_ref, sfin_ref, s_ref = refs
    h = pl.program_id(1)
    c = pl.program_id(2)
    lg = _select_by_head(h, [math.log1p(-(2.0 ** (-5.0 - j))) for j in range(H_R)])

    @pl.when(c == 0)
    def _():
        s_ref[...] = s0_ref[...] if has_init else jnp.zeros_like(s_ref)

    q, k, v = q_ref[...], k_ref[...], v_ref[...]
    i = lax.broadcasted_iota(jnp.int32, (lb, lb), 0)
    j = lax.broadcasted_iota(jnp.int32, (lb, lb), 1)
    d = (i - j).astype(F32)
    decay = jnp.where(d >= 0, jnp.exp(jnp.maximum(d, 0.0) * lg), 0.0)
    inner = _dot_nt(q, k) * decay
    ic = lax.broadcasted_iota(jnp.int32, (lb, 1), 0).astype(F32)
    s_old = s_ref[...]
    o = _dot(inner.astype(BF16), v) + _dot(q, s_old.astype(BF16)) * jnp.exp((ic + 1.0) * lg)
    tail = jnp.exp((lb - 1.0 - ic) * lg)
    kt = (k.astype(F32) * tail).astype(BF16)
    s_new = jnp.exp(lb * lg) * s_old + _dot_tn(kt, v)
    s_ref[...] = s_new

    @pl.when(c == pl.num_programs(2) - 1)
    def _():
        sfin_ref[...] = s_new

    oc = o - jnp.mean(o, axis=-1, keepdims=True)
    y = oc * lax.rsqrt(jnp.mean(oc * oc, axis=-1, keepdims=True) + EPS) * g_ref[...]
    gate = gate_ref[...]
    o_ref[...] = (y * (gate * jax.nn.sigmoid(gate))).astype(o_ref.dtype)


def _retention(q, k, v, gate, g_ret3, s0, lb):
    b, t, _ = q.shape
    has_init = s0 is not None
    kern = functools.partial(_ret_kernel, lb=lb, has_init=has_init)
    head = lambda: pl.BlockSpec((None, lb, HEAD_W), lambda b_, h, c: (b_, c, h))
    state = lambda: pl.BlockSpec((None, None, DK_R, DV_R), lambda b_, h, c: (b_, h, 0, 0))
    in_specs = [head(), head(), head(), head(), pl.BlockSpec((None, 1, HEAD_W), lambda b_, h, c: (h, 0, 0))]
    args = [q, k, v, gate, g_ret3]
    if has_init:
        in_specs.append(state())
        args.append(s0)
    return pl.pallas_call(
        kern,
        grid=(b, H_R, t // lb),
        in_specs=in_specs,
        out_specs=[head(), state()],
        out_shape=[jax.ShapeDtypeStruct((b, t, GROUP_W), BF16), jax.ShapeDtypeStruct((b, H_R, DK_R, DV_R), F32)],
        scratch_shapes=[pltpu.VMEM((DK_R, DV_R), F32)],
        compiler_params=_cparams(("parallel", "parallel", "arbitrary")),
        name="retention",
    )(*args)


def _out_proj_kernel(x_ref, mda_ref, mret_ref, wo_ref, g_ref, wq_ref, x1_ref, qx_ref):
    x1 = x_ref[...] + _dot(mda_ref[...], wo_ref[:GROUP_W, :]) + _dot(mret_ref[...], wo_ref[GROUP_W:, :])
    x1_ref[...] = x1
    hn = _rms(x1, g_ref[...]).astype(BF16)
    qx_ref[...] = (_dot(hn, wq_ref[...]) * (HD_X ** -0.5)).astype(BF16)


def _out_proj(x2d, mda, mret, wo_bf, g_cross, wq_bf, tm, first_row=0):
    n = mda.shape[0]
    off = first_row // tm
    full = lambda a: pl.BlockSpec(a.shape, lambda i: (0, 0))
    return pl.pallas_call(
        _out_proj_kernel,
        grid=(n // tm,),
        in_specs=[pl.BlockSpec((tm, D_MODEL), lambda i: (i + off, 0)),
                  pl.BlockSpec((tm, GROUP_W), lambda i: (i, 0)),
                  pl.BlockSpec((tm, GROUP_W), lambda i: (i, 0)),
                  full(wo_bf), full(g_cross), full(wq_bf)],
        out_specs=[pl.BlockSpec((tm, D_MODEL), lambda i: (i, 0)), pl.BlockSpec((tm, D_MODEL), lambda i: (i, 0))],
        out_shape=[jax.ShapeDtypeStruct((n, D_MODEL), F32), jax.ShapeDtypeStruct((n, D_MODEL), BF16)],
        compiler_params=_cparams(("parallel",)),
        name="out_proj",
    )(x2d, mda, mret, wo_bf, g_cross, wq_bf)


def _mem_kv_kernel(m_ref, g_ref, wk_ref, wv_ref, mk_ref, mv_ref, mkb_ref, mvb_ref):
    mn = _rms(m_ref[...], g_ref[...]).astype(BF16)
    mk = _dot(mn, wk_ref[...])
    mv = _dot(mn, wv_ref[...])
    mk_ref[...] = mk
    mv_ref[...] = mv
    mkb_ref[...] = mk.astype(BF16)
    mvb_ref[...] = mv.astype(BF16)


def _mem_kv(mem2d, g_mem, wk_bf, wv_bf, tm):
    n = mem2d.shape[0]
    row = lambda: pl.BlockSpec((tm, D_MODEL), lambda i: (i, 0))
    full = lambda a: pl.BlockSpec(a.shape, lambda i: (0, 0))
    sh = lambda dt: jax.ShapeDtypeStruct((n, D_MODEL), dt)
    return pl.pallas_call(
        _mem_kv_kernel,
        grid=(n // tm,),
        in_specs=[row(), full(g_mem), full(wk_bf), full(wv_bf)],
        out_specs=[row(), row(), row(), row()],
        out_shape=[sh(F32), sh(F32), sh(BF16), sh(BF16)],
        compiler_params=_cparams(("parallel",)),
        name="mem_kv",
    )(mem2d, g_mem, wk_bf, wv_bf)


def _cross_kernel(x1_ref, qx_ref, mk_ref, mv_ref, wo_ref, g_ref, wpq_ref, x2_ref, h3_ref, qp_ref):
    q = qx_ref[...]
    heads = []
    for h in range(H_X):
        sl = slice(h * HD_X, (h + 1) * HD_X)
        kh = mk_ref[:, sl] if mk_ref.dtype == BF16 else mk_ref[:, sl].astype(BF16)
        vh = mv_ref[:, sl] if mv_ref.dtype == BF16 else mv_ref[:, sl].astype(BF16)
        s = _dot_nt(q[:, sl], kh)
        p = jnp.exp(s - jnp.max(s, axis=-1, keepdims=True))
        heads.append(_dot(p.astype(BF16), vh) / jnp.sum(p, axis=-1, keepdims=True))
    o = jnp.concatenate(heads, axis=-1).astype(BF16)
    x2 = x1_ref[...] + _dot(o, wo_ref[...])
    x2_ref[...] = x2
    h3 = _rms(x2, g_ref[...])
    h3_ref[...] = h3
    qp_ref[...] = _dot(h3.astype(BF16), wpq_ref[...]).astype(BF16)


def _cross(x1, qx, mk, mv, wo_bf, g_ffn, wpq_bf, tm):
    b, t, _ = x1.shape
    row = lambda w: pl.BlockSpec((None, tm, w), lambda b_, i: (b_, i, 0))
    mem = lambda: pl.BlockSpec((None, N_MEM, D_MODEL), lambda b_, i: (b_, 0, 0))
    full = lambda a: pl.BlockSpec(a.shape, lambda b_, i: (0, 0))
    dq = wpq_bf.shape[1]
    return pl.pallas_call(
        _cross_kernel,
        grid=(b, t // tm),
        in_specs=[row(D_MODEL), row(D_MODEL), mem(), mem(), full(wo_bf), full(g_ffn), full(wpq_bf)],
        out_specs=[row(D_MODEL), row(D_MODEL), row(dq)],
        out_shape=[jax.ShapeDtypeStruct((b, t, D_MODEL), F32), jax.ShapeDtypeStruct((b, t, D_MODEL), F32),
                   jax.ShapeDtypeStruct((b, t, dq), BF16)],
        compiler_params=_cparams(("parallel", "parallel")),
        name="cross_attn",
    )(x1, qx, mk, mv, wo_bf, g_ffn, wpq_bf)


ID_PAD = 2.0 ** 29


def _topk_rows(s, ids):
    vals, sel = [], []
    for _ in range(PEER_TOPK):
        m = jnp.max(s, axis=0, keepdims=True)
        idx = jnp.min(jnp.where(s == m, ids, ID_PAD), axis=0, keepdims=True)
        vals.append(m)
        sel.append(idx)
        s = jnp.where(ids == idx, -jnp.inf, s)
    return jnp.concatenate(vals, axis=0), jnp.concatenate(sel, axis=0)


def _candidates(v1, i1, v2, i2):
    lanes = v1.shape[1]
    b8 = lax.broadcasted_iota(jnp.int32, (8, lanes), 0)
    ident = lambda a, ia, ib, b: (a * PEER_TOPK + b) * float(N_EXPERTS) + (ia * float(N_KEYS) + ib)
    b16 = lax.broadcasted_iota(jnp.int32, (PEER_TOPK, lanes), 0).astype(F32)
    vals = [v1[0:1] + v2]
    ids = [ident(0, i1[0:1], i2, b16)]
    for a in range(1, 8):
        keep = b8 < PEER_TOPK // (a + 1)
        vals.append(jnp.where(keep, v1[a:a + 1] + v2[0:8], -jnp.inf))
        ids.append(jnp.where(keep, ident(a, i1[a:a + 1], i2[0:8], b8.astype(F32)), ID_PAD))
    a_hi = (b8 + 8).astype(F32)
    vals.append(v1[8:16] + v2[0:1])
    ids.append(ident(a_hi, i1[8:16], i2[0:1], 0.0))
    return jnp.concatenate(vals, axis=0), jnp.concatenate(ids, axis=0)


def _route_kernel(qp_ref, k1_ref, k2_ref, e_ref, g_ref):
    half = N_KEYS
    tt = qp_ref.shape[0]
    key_id = lax.broadcasted_iota(jnp.int32, (N_KEYS, tt), 0).astype(F32)
    es, gs = [], []
    for p in range(PEER_HEADS):
        q1 = qp_ref[:, (2 * p) * half:(2 * p + 1) * half]
        q2 = qp_ref[:, (2 * p + 1) * half:(2 * p + 2) * half]
        v1, i1 = _topk_rows(_dot_nt(k1_ref[p], q1), key_id)
        v2, i2 = _topk_rows(_dot_nt(k2_ref[p], q2), key_id)
        sc, sel = _topk_rows(*_candidates(v1, i1, v2, i2))
        w = jnp.exp(sc - sc[0:1])
        es.append((sel.astype(jnp.int32) & (N_EXPERTS - 1)) * ROWS_PER_EXPERT)
        gs.append(w / jnp.sum(w, axis=0, keepdims=True))
    e_ref[...] = jnp.concatenate(es, axis=0).T
    g_ref[...] = jnp.concatenate(gs, axis=0).T


def _route_extra_kernel(qp_ref, k1_ref, k2_ref, *rest):
    _route_kernel(qp_ref, k1_ref, k2_ref, rest[-2], rest[-1])


def _route(qp2d, k1_bf, k2_bf, tt, first_token=0, n_tokens=None, into=None, after=()):
    n = qp2d.shape[0]
    n_tokens = n if n_tokens is None else n_tokens
    off = first_token // tt
    full = lambda a: pl.BlockSpec(a.shape, lambda i: (0, 0, 0))
    pair = lambda: pl.BlockSpec((tt, N_PAIRS), lambda i: (i + off, 0))
    in_specs = [pl.BlockSpec((tt, qp2d.shape[1]), lambda i: (i + off, 0)), full(k1_bf), full(k2_bf)]
    args = [qp2d, k1_bf, k2_bf]
    aliases = {}
    if into is not None:
        in_specs += [pl.BlockSpec(memory_space=pl.ANY), pl.BlockSpec(memory_space=pl.ANY)]
        args += list(into)
        aliases = {3: 0, 4: 1}
    in_specs += [pl.BlockSpec(memory_space=pl.ANY)] * len(after)
    args += list(after)
    return pl.pallas_call(
        _route_kernel if len(args) == 3 else _route_extra_kernel,
        grid=(n_tokens // tt,),
        in_specs=in_specs,
        out_specs=[pair(), pair()],
        out_shape=[jax.ShapeDtypeStruct((n, N_PAIRS), jnp.int32), jax.ShapeDtypeStruct((n, N_PAIRS), F32)],
        input_output_aliases=aliases,
        compiler_params=_cparams(("parallel",)),
        name="peer_route",
    )(*args)


def _unpack(words):
    hi = lax.bitcast_convert_type(words & jnp.uint32(0xFFFF0000), F32)
    lo = lax.bitcast_convert_type(words << 16, F32)
    return hi, lo


def _expert_rows(tab_ref, row0):
    return tab_ref[pl.ds(pl.multiple_of(row0, ROWS_PER_EXPERT), ROWS_PER_EXPERT), :]


REDUCE_TOKENS = 16


def _lane_sums_to_rows(y, n_tok):
    hi = y.astype(BF16)
    lo = (y - hi.astype(F32)).astype(BF16)
    ones = jnp.ones((128, N_PAIRS), BF16)
    s = (_dot(hi, ones) + _dot(lo, ones)).reshape(n_tok, N_PAIRS, N_PAIRS)
    eye = lax.broadcasted_iota(jnp.int32, (N_PAIRS, N_PAIRS), 0) == lax.broadcasted_iota(jnp.int32, (N_PAIRS, N_PAIRS), 1)
    return jnp.sum(jnp.where(eye[None], s, 0.0), axis=1)


CHUNKS = D_MODEL // 128


def _peer_u_kernel(e_ref, h_ref, gate_ref, tab_ref, w_ref, h8_ref, prod_ref, ys_ref, act_ref, *, tt):
    for c in range(CHUNKS):
        h8_ref[pl.ds(c, tt, stride=CHUNKS), :] = h_ref[:, c * 128:(c + 1) * 128]

    def token(t, slot):
        base = pl.multiple_of(t * CHUNKS, CHUNKS)
        ha = h8_ref[pl.ds(base, ROWS_PER_EXPERT), :]
        hb = h8_ref[pl.ds(pl.multiple_of(base + ROWS_PER_EXPERT, ROWS_PER_EXPERT), ROWS_PER_EXPERT), :]
        prod = prod_ref.at[slot]
        for k in range(N_PAIRS):
            hi, lo = _unpack(_expert_rows(tab_ref, e_ref[t, k]))
            prod[k * ROWS_PER_EXPERT:(k + 1) * ROWS_PER_EXPERT, :] = hi * ha + lo * hb
        y = prod[pl.ds(0, N_PAIRS, stride=ROWS_PER_EXPERT), :]
        for c in range(1, ROWS_PER_EXPERT):
            y = y + prod[pl.ds(c, N_PAIRS, stride=ROWS_PER_EXPERT), :]
        ys_ref[pl.ds(pl.multiple_of(t * N_PAIRS, N_PAIRS), N_PAIRS), :] = y

    def two_tokens(j, carry):
        token(2 * j, 0)
        token(2 * j + 1, 1)
        return carry

    lax.fori_loop(0, tt // 2, two_tokens, 0)

    def group(g, carry):
        rows = REDUCE_TOKENS * N_PAIRS
        y = ys_ref[pl.ds(pl.multiple_of(g * rows, rows), rows), :]
        act_ref[pl.ds(pl.multiple_of(g * REDUCE_TOKENS, REDUCE_TOKENS), REDUCE_TOKENS), :] = _lane_sums_to_rows(y, REDUCE_TOKENS)
        return carry

    lax.fori_loop(0, tt // REDUCE_TOKENS, group, 0)
    w_ref[...] = _gated_gelu(gate_ref[...], act_ref[...])


def _gated_gelu(gate, a):
    return gate * (0.5 * a * (1.0 + lax.erf(a * (2.0 ** -0.5))))


def _gate_act_kernel(act_ref, gate_ref, *rest):
    rest[-1][...] = _gated_gelu(gate_ref[...], act_ref[...])


def _gate_act(act, gate_t, w_t, first_token, tt, after=()):
    off = first_token // tt
    return pl.pallas_call(
        _gate_act_kernel,
        grid=(act.shape[0] // tt,),
        in_specs=[pl.BlockSpec((tt, N_PAIRS), lambda i: (i, 0)),
                  pl.BlockSpec((tt, N_PAIRS), lambda i: (i + off, 0))]
                 + [pl.BlockSpec(memory_space=pl.ANY)] * (1 + len(after)),
        out_specs=pl.BlockSpec((tt, N_PAIRS), lambda i: (i + off, 0)),
        out_shape=jax.ShapeDtypeStruct(w_t.shape, F32),
        input_output_aliases={2: 0},
        compiler_params=_cparams(("arbitrary",)),
        name="peer_gate_act",
    )(act, gate_t, w_t, *after)


def _peer_u(e_t, h, gate_t, tab, tt, n_tokens):
    n = e_t.shape[0]
    assert tt % REDUCE_TOKENS == 0
    kern = functools.partial(_peer_u_kernel, tt=tt)
    pair = lambda **kw: pl.BlockSpec((tt, N_PAIRS), lambda i: (i, 0), **kw)
    return pl.pallas_call(
        kern,
        grid=(n_tokens // tt,),
        in_specs=[pair(memory_space=pltpu.SMEM),
                  pl.BlockSpec((tt, D_MODEL), lambda i: (i, 0)),
                  pair(),
                  pl.BlockSpec(tab.shape, lambda i: (0, 0), pipeline_mode=pl.Buffered(1))],
        out_specs=pair(),
        out_shape=jax.ShapeDtypeStruct((n, N_PAIRS), F32),
        scratch_shapes=[pltpu.VMEM((tt * CHUNKS, 128), F32),
                        pltpu.VMEM((2, N_PAIRS * ROWS_PER_EXPERT, 128), F32),
                        pltpu.VMEM((tt * N_PAIRS, 128), F32),
                        pltpu.VMEM((tt, N_PAIRS), F32)],
        compiler_params=_cparams(("arbitrary",)),
        name="peer_u",
    )(e_t, h, gate_t, tab)


def _peer_v_kernel(e_ref, w_ref, x_ref, g_ref, tab_ref, o_ref, acc_ref, *, tt):
    n_acc = 2

    def token(t):
        acc_hi = [jnp.zeros((ROWS_PER_EXPERT, 128), F32) for _ in range(n_acc)]
        acc_lo = [jnp.zeros((ROWS_PER_EXPERT, 128), F32) for _ in range(n_acc)]
        for k in range(N_PAIRS):
            hi, lo = _unpack(_expert_rows(tab_ref, e_ref[t, k]))
            w = w_ref[t, k]
            acc_hi[k % n_acc] = acc_hi[k % n_acc] + w * hi
            acc_lo[k % n_acc] = acc_lo[k % n_acc] + w * lo
        base = pl.multiple_of(t * CHUNKS, CHUNKS)
        acc_ref[pl.ds(base, ROWS_PER_EXPERT), :] = sum(acc_hi[1:], acc_hi[0])
        acc_ref[pl.ds(pl.multiple_of(base + ROWS_PER_EXPERT, ROWS_PER_EXPERT), ROWS_PER_EXPERT), :] = sum(acc_lo[1:], acc_lo[0])

    def two_tokens(j, carry):
        token(2 * j)
        token(2 * j + 1)
        return carry

    lax.fori_loop(0, tt // 2, two_tokens, 0)
    _residual_rms_store(x_ref, acc_ref, g_ref, o_ref, tt)


def _residual_rms_store(x_ref, add_ref, g_ref, o_ref, tt):
    xs = [x_ref[:, c * 128:(c + 1) * 128] + add_ref[pl.ds(c, tt, stride=CHUNKS), :] for c in range(CHUNKS)]
    sq = xs[0] * xs[0]
    for x in xs[1:]:
        sq = sq + x * x
    r = lax.rsqrt(jnp.sum(sq, axis=1, keepdims=True) * (1.0 / D_MODEL) + EPS)
    for c in range(CHUNKS):
        o_ref[:, c * 128:(c + 1) * 128] = xs[c] * r * g_ref[:, c * 128:(c + 1) * 128]


def _peer_v(e_t, w_t, x, g_final, tab, tt, n_tokens, out_rows=None):
    out_rows = x.shape[0] if out_rows is None else out_rows
    kern = functools.partial(_peer_v_kernel, tt=tt)
    pair = lambda: pl.BlockSpec((tt, N_PAIRS), lambda i: (i, 0), memory_space=pltpu.SMEM)
    row = lambda: pl.BlockSpec((tt, D_MODEL), lambda i: (i, 0))
    return pl.pallas_call(
        kern,
        grid=(n_tokens // tt,),
        in_specs=[pair(), pair(), row(),
                  pl.BlockSpec((1, D_MODEL), lambda i: (0, 0)),
                  pl.BlockSpec(tab.shape, lambda i: (0, 0), pipeline_mode=pl.Buffered(1))],
        out_specs=row(),
        out_shape=jax.ShapeDtypeStruct((out_rows, D_MODEL), F32),
        scratch_shapes=[pltpu.VMEM((tt * CHUNKS, 128), F32)],
        compiler_params=_cparams(("arbitrary",)),
        name="peer_v",
    )(e_t, w_t, x, g_final, tab)


def _residual_norm_kernel(x_ref, add_ref, g_ref, y_any_ref, o_ref, *, tt):
    del y_any_ref
    _residual_rms_store(x_ref, add_ref, g_ref, o_ref, tt)


def _residual_norm(x, add8, g_final, y, x_first, y_first, tt):
    n_rows = add8.shape[0] // CHUNKS
    x_off, y_off = x_first // tt, y_first // tt
    return pl.pallas_call(
        functools.partial(_residual_norm_kernel, tt=tt),
        grid=(n_rows // tt,),
        in_specs=[pl.BlockSpec((tt, D_MODEL), lambda i: (i + x_off, 0)),
                  pl.BlockSpec((tt * CHUNKS, 128), lambda i: (i, 0)),
                  pl.BlockSpec((1, D_MODEL), lambda i: (0, 0)),
                  pl.BlockSpec(memory_space=pl.ANY)],
        out_specs=pl.BlockSpec((tt, D_MODEL), lambda i: (i + y_off, 0)),
        out_shape=jax.ShapeDtypeStruct(y.shape, F32),
        input_output_aliases={3: 0},
        compiler_params=_cparams(("arbitrary",)),
        name="peer_residual_norm",
    )(x, add8, g_final, y)


SC_CORES = 2
SC_SUBCORES = 16
SC_LANES = 16
WORD_ROWS = 128 // SC_LANES
SC_TOKEN_CHUNK = 32
SC_RING = 4


def _peer_sc(e_t, aux, tab, down):
    n = e_t.shape[0]
    workers = SC_CORES * SC_SUBCORES
    assert n % (workers * SC_TOKEN_CHUNK) == 0 and PEER_HEADS % SC_RING == 0
    per = n // workers
    head_rows = ROWS_PER_EXPERT * PEER_TOPK
    aux_rows = CHUNKS if down else 1
    out_rows = 1 if down else CHUNKS
    mesh = plsc.VectorSubcoreMesh(core_axis_name="core", subcore_axis_name="subcore",
                                  num_cores=SC_CORES, num_subcores=SC_SUBCORES)

    @pl.kernel(out_type=jax.ShapeDtypeStruct((n * out_rows, 128), F32), mesh=mesh,
               scratch_types=[pltpu.VMEM((SC_TOKEN_CHUNK, N_PAIRS), jnp.int32),
                              pltpu.VMEM((SC_TOKEN_CHUNK * aux_rows, 128), F32),
                              pltpu.VMEM((SC_RING, head_rows, 128), jnp.uint32),
                              pltpu.VMEM((SC_TOKEN_CHUNK * out_rows, 128), F32),
                              pltpu.VMEM((PEER_TOPK, SC_LANES), F32),
                              pltpu.SemaphoreType.DMA((SC_RING,))],
               compiler_params=pltpu.CompilerParams(needs_layout_passes=False),
               name="peer_u_sc" if down else "peer_v_sc")
    def body(e_hbm, aux_hbm, tab_hbm, o_hbm, e_v, aux_v, rows_v, out_v, fold_v, sems):
        wid = lax.axis_index("core") * SC_SUBCORES + lax.axis_index("subcore")
        lane = lax.broadcasted_iota(jnp.int32, (SC_LANES,), 0)

        def gathers(i, p, slot):
            first = e_v[i, pl.ds(p * PEER_TOPK, PEER_TOPK)]
            return [pltpu.make_async_copy(tab_hbm.at[first + r],
                                          rows_v.at[slot, pl.ds(r * PEER_TOPK, PEER_TOPK)], sems.at[slot])
                    for r in range(ROWS_PER_EXPERT)]

        def start_gather(i, p, slot):
            for d in gathers(i, p, slot):
                d.start()

        def weighted_sum(i, p, slot):
            ws = [plsc.load_gather(aux_v, [lane * 0 + i, lane * 0 + (p * PEER_TOPK + k)])
                  for k in range(PEER_TOPK)]
            for r in range(ROWS_PER_EXPERT):
                @pl.loop(0, WORD_ROWS)
                def _(j):
                    sl = pl.ds(j * SC_LANES, SC_LANES)
                    if p == 0:
                        a_hi = jnp.zeros((SC_LANES,), F32)
                        a_lo = jnp.zeros((SC_LANES,), F32)
                    else:
                        a_hi = out_v[i * CHUNKS + r, sl]
                        a_lo = out_v[i * CHUNKS + ROWS_PER_EXPERT + r, sl]
                    for k in range(PEER_TOPK):
                        hi, lo = _unpack(rows_v[slot, r * PEER_TOPK + k, sl])
                        a_hi = a_hi + ws[k] * hi
                        a_lo = a_lo + ws[k] * lo
                    out_v[i * CHUNKS + r, sl] = a_hi
                    out_v[i * CHUNKS + ROWS_PER_EXPERT + r, sl] = a_lo

        def pair_dots(i, p, slot):
            accs = tuple(jnp.zeros((SC_LANES,), F32) for _ in range(PEER_TOPK))
            for r in range(ROWS_PER_EXPERT):
                def piece(j, accs, r=r):
                    sl = pl.ds(j * SC_LANES, SC_LANES)
                    ha = aux_v[i * CHUNKS + r, sl]
                    hb = aux_v[i * CHUNKS + ROWS_PER_EXPERT + r, sl]
                    out = []
                    for k in range(PEER_TOPK):
                        hi, lo = _unpack(rows_v[slot, r * PEER_TOPK + k, sl])
                        out.append(accs[k] + (hi * ha + lo * hb))
                    return tuple(out)
                accs = lax.fori_loop(0, WORD_ROWS, piece, accs)
            for k in range(PEER_TOPK):
                fold_v[k, :] = accs[k]
            tot = plsc.load_gather(fold_v, [lane, lane * 0])
            for l in range(1, SC_LANES):
                tot = tot + plsc.load_gather(fold_v, [lane, lane * 0 + l])
            out_v[i, pl.ds(p * PEER_TOPK, PEER_TOPK)] = tot

        consume = pair_dots if down else weighted_sum

        @pl.loop(0, per // SC_TOKEN_CHUNK)
        def _(c):
            t0 = wid * per + c * SC_TOKEN_CHUNK
            pltpu.sync_copy(e_hbm.at[pl.ds(t0, SC_TOKEN_CHUNK)], e_v)
            pltpu.sync_copy(aux_hbm.at[pl.ds(t0 * aux_rows, SC_TOKEN_CHUNK * aux_rows)], aux_v)
            for p in range(SC_RING):
                start_gather(0, p, p)

            @pl.loop(0, SC_TOKEN_CHUNK)
            def _(i):
                for p in range(PEER_HEADS):
                    slot = p % SC_RING
                    for d in gathers(i, p, slot):
                        d.wait()
                    consume(i, p, slot)
                    if p + SC_RING < PEER_HEADS:
                        start_gather(i, p + SC_RING, slot)
                    else:
                        @pl.when(i + 1 < SC_TOKEN_CHUNK)
                        def _():
                            start_gather(i + 1, p + SC_RING - PEER_HEADS, slot)

            pltpu.sync_copy(out_v, o_hbm.at[pl.ds(t0 * out_rows, SC_TOKEN_CHUNK * out_rows)])

    return body(e_t, aux, tab)


def _pack_table(tab):
    bits = lax.bitcast_convert_type(tab.astype(BF16), jnp.uint16).astype(jnp.uint32)
    words = (bits[:, :HALF] << 16) | bits[:, HALF:]
    return words.reshape(tab.shape[0] * ROWS_PER_EXPERT, 128)


PEER_TOKENS = 128
SC_LEAD_BATCHES = (1, 4)
SC_SHARE_DOWN = (1, 3)
SC_SHARE_UP = (3, 8)
SC_MIN_TOKENS = 4096


def _sc_tokens(n):
    if n < SC_MIN_TOKENS:
        return (0, 0)
    unit = math.lcm(PEER_TOKENS, SC_CORES * SC_SUBCORES * SC_TOKEN_CHUNK)
    return tuple(n * num // den // unit * unit for num, den in (SC_SHARE_DOWN, SC_SHARE_UP))


def _row_tile(n, pref):
    while n % pref:
        pref //= 2
    return pref


def _peer_on_sc_down(h3, qp, k1_bf, k2_bf, tab_u, tt):
    n = h3.shape[0]
    e_t, gate_t = _route(qp, k1_bf, k2_bf, tt)
    return e_t, gate_t, _peer_sc(e_t, h3.reshape(n * CHUNKS, 128), tab_u, down=True)


def _peer_on_sc_up(e_t, gate_t, act, tab_v, tt, gate_after):
    w_t = _gate_act(act, gate_t, act, 0, tt, after=gate_after)
    return _peer_sc(e_t, w_t, tab_v, down=False), w_t


def _peer_and_final(x2, h3, qp, k1_bf, k2_bf, tab_u, tab_v, g_final, tt, n_sc=(0, 0), route_after=(), out_rows=None):
    n = x2.shape[0]
    n_sc_down, n_sc_up = n_sc
    tc_down, tc_up = n - n_sc_down, n - n_sc_up
    if n_sc_down:
        e_t, gate_t = _route(qp, k1_bf, k2_bf, tt, tc_down, n_sc_down, after=route_after)
        act = _peer_sc(e_t[tc_down:], h3[tc_down:].reshape(n_sc_down * CHUNKS, 128), tab_u, down=True)
        e_t, gate_t = _route(qp, k1_bf, k2_bf, tt, 0, tc_down, into=(e_t, gate_t))
    else:
        e_t, gate_t = _route(qp, k1_bf, k2_bf, tt)
    w_t = _peer_u(e_t, h3, gate_t, tab_u, tt, tc_down)
    if n_sc_down:
        w_t = _gate_act(act, gate_t, w_t, tc_down, tt)
    y = _peer_v(e_t, w_t, x2, g_final, tab_v, tt, tc_up, out_rows)
    if n_sc_up:
        add8 = _peer_sc(e_t[tc_up:], w_t[tc_up:], tab_v, down=False)
        y = _residual_norm(x2, add8, g_final, y, tc_up, tc_up, tt)
    return y


def kernel(x_prompt, x_sample, mem_prompt, cache_da_k, cache_da_v, state_ret, cache_mem_k, cache_mem_v, g_mix, w_in, lam_q1, lam_k1, lam_q2, lam_k2, g_da, g_ret, w_out, g_cross, g_mem, w_xq, w_xk, w_xv, w_xo, g_ffn, w_pq, peer_k1, peer_k2, peer_u, peer_v, g_final):
    depth = w_in.shape[0]
    assert depth == 1, "single-layer step"
    l = 0
    lam_init = 0.8 - 0.6 * math.exp(-0.3 * l)
    b, t, _ = x_prompt.shape
    bs, ts, _ = x_sample.shape
    past_len = cache_da_k.shape[2]

    row = lambda a: a.reshape(1, -1)
    w_in_bf = w_in[l].astype(BF16)
    w_out_bf = w_out[l].astype(BF16)
    w_xq_bf, w_xk_bf, w_xv_bf, w_xo_bf = (w[l].astype(BF16) for w in (w_xq, w_xk, w_xv, w_xo))
    w_pq_bf = w_pq[l].astype(BF16)
    k1_bf, k2_bf = peer_k1[l].astype(BF16), peer_k2[l].astype(BF16)
    tab_u, tab_v = _pack_table(peer_u[l]), _pack_table(peer_v[l])
    lamp = jnp.stack([lam_q1[l], lam_k1[l], lam_q2[l], lam_k2[l]])
    g_da3 = g_da[l].reshape(H_D, 1, DV_D)
    g_ret3 = g_ret[l].reshape(H_R, 1, DV_R)
    g_fin = row(g_final)

    def mixer_mid(x2d, first_row, mda, mret, mk, mv, bb, tt_rows):
        n = bb * tt_rows
        tm = _row_tile(n, 512)
        x1, qx = _out_proj(x2d, mda.reshape(n, GROUP_W), mret.reshape(n, GROUP_W), w_out_bf, row(g_cross[l]), w_xq_bf,
                           tm, first_row)
        x2, h3, qp = _cross(x1.reshape(bb, tt_rows, D_MODEL), qx.reshape(bb, tt_rows, D_MODEL), mk, mv,
                            w_xo_bf, row(g_ffn[l]), w_pq_bf, _row_tile(tt_rows, 512))
        return x2.reshape(n, D_MODEL), h3.reshape(n, D_MODEL), qp.reshape(n, -1)

    peer_args = (k1_bf, k2_bf, tab_u, tab_v, g_fin, PEER_TOKENS)
    n = b * t
    xp = x_prompt.reshape(n, D_MODEL)

    def prompt_dense(b0, bg, kv_into=None):
        rows = bg * t
        qd, kd, vd, kdb, vdb, qr, kr, vr, gr = _in_proj(xp, row(g_mix[l]), w_in_bf, _row_tile(rows, 512),
                                                       b0 * t, rows, kv_into)
        r3 = lambda a: a.reshape(bg, t, GROUP_W)
        mda = _diff_attn_prompt(lamp, r3(qd), r3(kdb), r3(vdb), g_da3, lam_init, 512, 512)
        mret, s_fin = _retention(r3(qr), r3(kr), r3(vr), r3(gr), g_ret3, None, 512)
        return mixer_mid(xp, b0 * t, mda, mret, mkb[b0:b0 + bg], mvb[b0:b0 + bg], bg, t), (kd, vd), s_fin

    mk, mv, mkb, mvb = _mem_kv(mem_prompt.reshape(b * N_MEM, D_MODEL), row(g_mem[l]), w_xk_bf, w_xv_bf, 512)
    mkb, mvb = mkb.reshape(b, N_MEM, D_MODEL), mvb.reshape(b, N_MEM, D_MODEL)
    b_rest = b - b * SC_LEAD_BATCHES[0] // SC_LEAD_BATCHES[1]
    if b_rest < b:
        (x2_l, h3_l, qp_l), kv, s_l = prompt_dense(b_rest, b - b_rest)
        e_l, gate_l, act_l = _peer_on_sc_down(h3_l, qp_l, k1_bf, k2_bf, tab_u, PEER_TOKENS)
        peer_rest, (kd, vd), s_r = prompt_dense(0, b_rest, kv_into=kv)
        add8_l, w_lead = _peer_on_sc_up(e_l, gate_l, act_l, tab_v, PEER_TOKENS, gate_after=(peer_rest[2],))
        y = _peer_and_final(*peer_rest, *peer_args, _sc_tokens(b_rest * t), route_after=(w_lead,), out_rows=n)
        y_prompt = _residual_norm(x2_l, add8_l, g_fin, y, 0, b_rest * t, PEER_TOKENS).reshape(b, t, D_MODEL)
        s_fin = jnp.concatenate([s_r, s_l])
    else:
        peer_all, (kd, vd), s_fin = prompt_dense(0, b)
        y_prompt = _peer_and_final(*peer_all, *peer_args, _sc_tokens(n)).reshape(b, t, D_MODEL)

    ns = bs * ts
    xs = x_sample.reshape(ns, D_MODEL)
    qd_s, kd_s, vd_s, kdb_s, vdb_s, qr_s, kr_s, vr_s, gr_s = _in_proj(xs, row(g_mix[l]), w_in_bf, _row_tile(ns, 512))
    s3 = lambda a: a.reshape(bs, ts, GROUP_W)
    mda_s = _diff_attn_sample(lamp, s3(qd_s), cache_da_k[l].reshape(bs, past_len, GROUP_W),
                              cache_da_v[l].reshape(bs, past_len, GROUP_W), s3(kdb_s), s3(vdb_s), g_da3, lam_init)
    mret_s, s_new = _retention(s3(qr_s), s3(kr_s), s3(vr_s), s3(gr_s), g_ret3, state_ret[l], ts)
    peer_s = mixer_mid(xs, 0, mda_s, mret_s, cache_mem_k[l].reshape(bs, N_MEM, D_MODEL),
                       cache_mem_v[l].reshape(bs, N_MEM, D_MODEL), bs, ts)
    y_sample = _peer_and_final(*peer_s, *peer_args).reshape(bs, ts, D_MODEL)

    return (y_prompt, y_sample,
            kd.reshape(1, b, t, H_D, 2, DK_D), vd.reshape(1, b, t, H_D, DV_D), s_fin[None],
            mk.reshape(1, b, N_MEM, H_X, HD_X), mv.reshape(1, b, N_MEM, H_X, HD_X),
            kd_s.reshape(1, bs, ts, H_D, 2, DK_D), vd_s.reshape(1, bs, ts, H_D, DV_D), s_new[None])
```

```python
import functools
import math

import jax
import jax.numpy as jnp
from jax import lax
from jax.experimental import pallas as pl
from jax.experimental.pallas import tpu as pltpu
from jax.experimental.pallas import tpu_sc as plsc

D_MODEL = 1024
CHUNK = 64
CHUNK_SHIFT = CHUNK.bit_length() - 1
assert 1 << CHUNK_SHIFT == CHUNK
H_D, DK_D, DV_D = 4, 64, 128
H_R, DK_R, DV_R = 4, 128, 128
N_MEM = 256
H_X = 4
HD_X = D_MODEL // H_X
PEER_HEADS = 8
N_KEYS = 128
N_EXPERTS = N_KEYS * N_KEYS
PEER_TOPK = 16
EPS = 1e-6
HEAD_W = 128
GROUP_W = 512
N_PAIRS = PEER_HEADS * PEER_TOPK
HALF = D_MODEL // 2
ROWS_PER_EXPERT = HALF // 128
VMEM_LIMIT = 56 * 1024 * 1024

BF16 = jnp.bfloat16
F32 = jnp.float32


def _cparams(sem):
    return pltpu.CompilerParams(dimension_semantics=sem, vmem_limit_bytes=VMEM_LIMIT)


def _rms(x, g):
    return x * lax.rsqrt(jnp.mean(x * x, axis=-1, keepdims=True) + EPS) * g


def _dot(a, b):
    return jnp.dot(a, b, preferred_element_type=F32)


def _dot_nt(a, b):
    return lax.dot_general(a, b, (((1,), (1,)), ((), ())), preferred_element_type=F32)


def _dot_tn(a, b):
    return lax.dot_general(a, b, (((0,), (0,)), ((), ())), preferred_element_type=F32)


def _select_by_head(h, values):
    out = jnp.float32(values[-1])
    for i in range(len(values) - 2, -1, -1):
        out = jnp.where(h == i, jnp.float32(values[i]), out)
    return out


def _in_proj_kernel(x_ref, g_ref, w_ref, *rest):
    qd_ref, kd_ref, vd_ref, kdb_ref, vdb_ref, qr_ref, kr_ref, vr_ref, gr_ref = rest[-9:]
    hb = _rms(x_ref[...], g_ref[...]).astype(BF16)
    col = lambda c: _dot(hb, w_ref[:, c * GROUP_W:(c + 1) * GROUP_W])
    qd_ref[...] = (col(0) * (DK_D ** -0.5)).astype(BF16)
    kd = col(1)
    kd_ref[...] = kd
    kdb_ref[...] = kd.astype(BF16)
    vd = col(2)
    vd_ref[...] = vd
    vdb_ref[...] = vd.astype(BF16)
    qr_ref[...] = col(3).astype(BF16)
    kr_ref[...] = (col(4) * (DK_R ** -0.5)).astype(BF16)
    vr_ref[...] = col(5).astype(BF16)
    gr_ref[...] = col(6)


def _in_proj(x2d, g, w_bf, tm, first_row=0, n_rows=None, kv_into=None):
    n_all = x2d.shape[0]
    n = n_all if n_rows is None else n_rows
    off = first_row // tm
    blk = lambda: pl.BlockSpec((tm, GROUP_W), lambda i: (i, 0))
    blk_all = lambda: pl.BlockSpec((tm, GROUP_W), lambda i: (i + off, 0))
    sh = lambda dt: jax.ShapeDtypeStruct((n, GROUP_W), dt)
    sh_all = jax.ShapeDtypeStruct((n_all, GROUP_W), F32)
    args = [x2d, g, w_bf]
    in_specs = [pl.BlockSpec((tm, D_MODEL), lambda i: (i + off, 0)),
                pl.BlockSpec((1, D_MODEL), lambda i: (0, 0)),
                pl.BlockSpec(w_bf.shape, lambda i: (0, 0))]
    aliases = {}
    if kv_into is not None:
        args += list(kv_into)
        in_specs += [pl.BlockSpec(memory_space=pl.ANY)] * 2
        aliases = {3: 1, 4: 2}
    return pl.pallas_call(
        _in_proj_kernel,
        grid=(n // tm,),
        in_specs=in_specs,
        out_specs=[blk(), blk_all(), blk_all()] + [blk() for _ in range(6)],
        out_shape=[sh(BF16), sh_all, sh_all, sh(BF16), sh(BF16), sh(BF16), sh(BF16), sh(BF16), sh(F32)],
        input_output_aliases=aliases,
        compiler_params=_cparams(("parallel",)),
        name="in_proj",
    )(*args)


def _lambda_from(lam_ref, lam_init):
    l = lam_ref[...]
    a = jnp.exp(jnp.sum(l[0:1] * l[1:2], axis=-1, keepdims=True))
    b = jnp.exp(jnp.sum(l[2:3] * l[3:4], axis=-1, keepdims=True))
    return a - b + lam_init


def _diff_post(acc, l, lam, g, lam_init, tq):
    o = acc[:tq] / l[:tq] - lam * (acc[tq:] / l[tq:])
    return o * lax.rsqrt(jnp.mean(o * o, axis=-1, keepdims=True) + EPS) * g * (1.0 - lam_init)


def _split_maps(q):
    lane = lax.broadcasted_iota(jnp.int32, q.shape, 1)
    zero = jnp.zeros_like(q)
    return jnp.concatenate([jnp.where(lane < DK_D, q, zero), jnp.where(lane >= DK_D, q, zero)], axis=0)


def _da_prompt_kernel(lam_ref, q_ref, k_ref, v_ref, g_ref, o_ref, kx_ref, vx_ref, own_ref, acc_ref, m_ref, *, lam_init, tq, tk):
    h = pl.program_id(1)
    i = pl.program_id(2)
    t = k_ref.shape[0]
    slope = _select_by_head(h, [2.0 ** (-8.0 * (j + 1) / H_D) for j in range(H_D)])

    @pl.when(i == 0)
    def _():
        pos = lax.broadcasted_iota(jnp.int32, (t, HEAD_W), 0)
        lane = lax.broadcasted_iota(jnp.int32, (t, HEAD_W), 1)
        coarse = ((pos >> CHUNK_SHIFT) << CHUNK_SHIFT).astype(F32) * slope
        fine = (pos & (CHUNK - 1)).astype(F32) * slope
        kx_ref[:, :HEAD_W] = k_ref[...]
        kx_ref[:, HEAD_W:] = jnp.where(lane == 0, coarse, jnp.where(lane == 1, fine, 0.0)).astype(BF16)
        vx_ref[:DV_D, :] = v_ref[...].astype(F32).T.astype(BF16)
        vx_ref[DV_D:, :] = jnp.ones((vx_ref.shape[0] - DV_D, t), BF16)
        krel = lax.broadcasted_iota(jnp.int32, (tk, 2 * tq), 0)
        c = lax.broadcasted_iota(jnp.int32, (tk, 2 * tq), 1)
        for par in range(tk // tq):
            qrel = par * tq + jnp.where(c >= tq, c - tq, c)
            ahead = (2.0 * slope) * jnp.maximum(krel - qrel, 0).astype(F32)
            own_ref[par] = jnp.where((qrel >> CHUNK_SHIFT) >= (krel >> CHUNK_SHIFT), -ahead, -1e30)

    q = q_ref[...]
    lane = lax.broadcasted_iota(jnp.int32, q.shape, 1)
    zero = jnp.zeros_like(q)
    ones2 = jnp.where(lane < 2, 1.0, 0.0).astype(BF16)
    q2 = jnp.concatenate([jnp.concatenate([jnp.where(lane < DK_D, q, zero), ones2], axis=1),
                          jnp.concatenate([jnp.where(lane >= DK_D, q, zero), ones2], axis=1)], axis=0)
    jd = (i * tq) // tk

    def scores(j):
        return _dot_nt(kx_ref[pl.ds(pl.multiple_of(j * tk, tk), tk), :], q2)

    def values(j):
        return vx_ref[:, pl.ds(pl.multiple_of(j * tk, tk), tk)]

    s = scores(jd) + own_ref[(i * tq) % tk // tq]
    m0 = jnp.max(s, axis=0, keepdims=True)
    m_ref[...] = m0
    acc_ref[...] = _dot(values(jd), jnp.exp(s - m0).astype(BF16))

    def absorb(blocks):
        ss = [scores(j) for j in blocks]
        m_old = m_ref[...]
        m_new = m_old
        for s in ss:
            m_new = jnp.maximum(m_new, jnp.max(s, axis=0, keepdims=True))
        m_ref[...] = m_new
        acc = jnp.exp(m_old - m_new) * acc_ref[...]
        for j, s in zip(blocks, ss):
            acc = acc + _dot(values(j), jnp.exp(s - m_new).astype(BF16))
        acc_ref[...] = acc

    def past_pair(jj, carry):
        absorb([2 * jj, 2 * jj + 1])
        return carry

    lax.fori_loop(0, jd // 2, past_pair, 0)

    @pl.when(jd % 2 == 1)
    def _():
        absorb([jd - 1])

    acc = acc_ref[...]
    num, den = acc[:DV_D], acc[DV_D:DV_D + 1]
    lam = _lambda_from(lam_ref, lam_init)
    o = (num[:, :tq] / den[:, :tq] - lam * (num[:, tq:] / den[:, tq:])).T
    o = o * lax.rsqrt(jnp.mean(o * o, axis=-1, keepdims=True) + EPS) * g_ref[...] * (1.0 - lam_init)
    o_ref[...] = o.astype(o_ref.dtype)


ONES_ROWS = 16


def _diff_attn_prompt(lamp, q, k, v, g_da3, lam_init, tq, tk):
    b, t, _ = q.shape
    kern = functools.partial(_da_prompt_kernel, lam_init=lam_init, tq=tq, tk=tk)
    return pl.pallas_call(
        kern,
        grid=(b, H_D, t // tq),
        in_specs=[pl.BlockSpec((4, DK_D), lambda b_, h, i: (0, 0)),
                  pl.BlockSpec((None, tq, HEAD_W), lambda b_, h, i: (b_, i, h)),
                  pl.BlockSpec((None, t, HEAD_W), lambda b_, h, i: (b_, 0, h)),
                  pl.BlockSpec((None, t, HEAD_W), lambda b_, h, i: (b_, 0, h)),
                  pl.BlockSpec((None, 1, HEAD_W), lambda b_, h, i: (h, 0, 0))],
        out_specs=pl.BlockSpec((None, tq, HEAD_W), lambda b_, h, i: (b_, i, h)),
        out_shape=jax.ShapeDtypeStruct((b, t, GROUP_W), BF16),
        scratch_shapes=[pltpu.VMEM((t, 2 * HEAD_W), BF16), pltpu.VMEM((DV_D + ONES_ROWS, t), BF16),
                        pltpu.VMEM((tk // tq, tk, 2 * tq), F32),
                        pltpu.VMEM((DV_D + ONES_ROWS, 2 * tq), F32), pltpu.VMEM((1, 2 * tq), F32)],
        compiler_params=_cparams(("parallel", "parallel", "arbitrary")),
        name="diff_attn_prompt",
    )(lamp, q, k, v, g_da3)


def _da_sample_kernel(lam_ref, q_ref, kc_ref, vc_ref, kn_ref, vn_ref, g_ref, o_ref, *, lam_init, ts, past_len):
    h = pl.program_id(1)
    slope = _select_by_head(h, [2.0 ** (-8.0 * (j + 1) / H_D) for j in range(H_D)])
    q2 = _split_maps(q_ref[...])
    rows = 2 * ts

    def scores(k, base, n):
        r = lax.broadcasted_iota(jnp.int32, (rows, n), 0)
        qpos = past_len + jnp.where(r >= ts, r - ts, r)
        kpos = base + lax.broadcasted_iota(jnp.int32, (rows, n), 1)
        return _dot_nt(q2, k) - slope * jnp.abs(qpos - kpos).astype(F32)

    sc = scores(kc_ref[...].astype(BF16), 0, past_len)
    sn = scores(kn_ref[...], past_len, ts)
    m = jnp.maximum(jnp.max(sc, axis=-1, keepdims=True), jnp.max(sn, axis=-1, keepdims=True))
    pc = jnp.exp(sc - m)
    pn = jnp.exp(sn - m)
    l = jnp.sum(pc, axis=-1, keepdims=True) + jnp.sum(pn, axis=-1, keepdims=True)
    acc = _dot(pc.astype(BF16), vc_ref[...].astype(BF16)) + _dot(pn.astype(BF16), vn_ref[...])
    lam = _lambda_from(lam_ref, lam_init)
    o_ref[...] = _diff_post(acc, l, lam, g_ref[...], lam_init, ts).astype(o_ref.dtype)


def _diff_attn_sample(lamp, q, kc, vc, kn, vn, g_da3, lam_init):
    b, ts, _ = q.shape
    past_len = kc.shape[1]
    kern = functools.partial(_da_sample_kernel, lam_init=lam_init, ts=ts, past_len=past_len)
    head = lambda rows: pl.BlockSpec((None, rows, HEAD_W), lambda b_, h: (b_, 0, h))
    return pl.pallas_call(
        kern,
        grid=(b, H_D),
        in_specs=[pl.BlockSpec((4, DK_D), lambda b_, h: (0, 0)),
                  head(ts), head(past_len), head(past_len), head(ts), head(ts),
                  pl.BlockSpec((None, 1, HEAD_W), lambda b_, h: (h, 0, 0))],
        out_specs=head(ts),
        out_shape=jax.ShapeDtypeStruct((b, ts, GROUP_W), BF16),
        compiler_params=_cparams(("parallel", "parallel")),
        name="diff_attn_sample",
    )(lamp, q, kc, vc, kn, vn, g_da3)


def _ret_kernel(*refs, lb, has_init):
    if has_init:
        q_ref, k_ref, v_ref, gate_ref, g_ref, s0_ref, o_ref, sfin_ref, s_ref = refs
    else:
        q_ref, k_ref, v_ref, gate_ref, g_ref, o_ref, sfin_ref, s_ref = refs
    h = pl.program_id(1)
    c = pl.program_id(2)
    lg = _select_by_head(h, [math.log1p(-(2.0 ** (-5.0 - j))) for j in range(H_R)])

    @pl.when(c == 0)
    def _():
        s_ref[...] = s0_ref[...] if has_init else jnp.zeros_like(s_ref)

    q, k, v = q_ref[...], k_ref[...], v_ref[...]
    i = lax.broadcasted_iota(jnp.int32, (lb, lb), 0)
    j = lax.broadcasted_iota(jnp.int32, (lb, lb), 1)
    d = (i - j).astype(F32)
    decay = jnp.where(d >= 0, jnp.exp(jnp.maximum(d, 0.0) * lg), 0.0)
    inner = _dot_nt(q, k) * decay
    ic = lax.broadcasted_iota(jnp.int32, (lb, 1), 0).astype(F32)
    s_old = s_ref[...]
    o = _dot(inner.astype(BF16), v) + _dot(q, s_old.astype(BF16)) * jnp.exp((ic + 1.0) * lg)
    tail = jnp.exp((lb - 1.0 - ic) * lg)
    kt = (k.astype(F32) * tail).astype(BF16)
    s_new = jnp.exp(lb * lg) * s_old + _dot_tn(kt, v)
    s_ref[...] = s_new

    @pl.when(c == pl.num_programs(2) - 1)
    def _():
        sfin_ref[...] = s_new

    oc = o - jnp.mean(o, axis=-1, keepdims=True)
    y = oc * lax.rsqrt(jnp.mean(oc * oc, axis=-1, keepdims=True) + EPS) * g_ref[...]
    gate = gate_ref[...]
    o_ref[...] = (y * (gate * jax.nn.sigmoid(gate))).astype(o_ref.dtype)


def _retention(q, k, v, gate, g_ret3, s0, lb):
    b, t, _ = q.shape
    has_init = s0 is not None
    kern = functools.partial(_ret_kernel, lb=lb, has_init=has_init)
    head = lambda: pl.BlockSpec((None, lb, HEAD_W), lambda b_, h, c: (b_, c, h))
    state = lambda: pl.BlockSpec((None, None, DK_R, DV_R), lambda b_, h, c: (b_, h, 0, 0))
    in_specs = [head(), head(), head(), head(), pl.BlockSpec((None, 1, HEAD_W), lambda b_, h, c: (h, 0, 0))]
    args = [q, k, v, gate, g_ret3]
    if has_init:
        in_specs.append(state())
        args.append(s0)
    return pl.pallas_call(
        kern,
        grid=(b, H_R, t // lb),
        in_specs=in_specs,
        out_specs=[head(), state()],
        out_shape=[jax.ShapeDtypeStruct((b, t, GROUP_W), BF16), jax.ShapeDtypeStruct((b, H_R, DK_R, DV_R), F32)],
        scratch_shapes=[pltpu.VMEM((DK_R, DV_R), F32)],
        compiler_params=_cparams(("parallel", "parallel", "arbitrary")),
        name="retention",
    )(*args)


def _out_proj_kernel(x_ref, mda_ref, mret_ref, wo_ref, g_ref, wq_ref, x1_ref, qx_ref):
    x1 = x_ref[...] + _dot(mda_ref[...], wo_ref[:GROUP_W, :]) + _dot(mret_ref[...], wo_ref[GROUP_W:, :])
    x1_ref[...] = x1
    hn = _rms(x1, g_ref[...]).astype(BF16)
    qx_ref[...] = (_dot(hn, wq_ref[...]) * (HD_X ** -0.5)).astype(BF16)


def _out_proj(x2d, mda, mret, wo_bf, g_cross, wq_bf, tm, first_row=0):
    n = mda.shape[0]
    off = first_row // tm
    full = lambda a: pl.BlockSpec(a.shape, lambda i: (0, 0))
    return pl.pallas_call(
        _out_proj_kernel,
        grid=(n // tm,),
        in_specs=[pl.BlockSpec((tm, D_MODEL), lambda i: (i + off, 0)),
                  pl.BlockSpec((tm, GROUP_W), lambda i: (i, 0)),
                  pl.BlockSpec((tm, GROUP_W), lambda i: (i, 0)),
                  full(wo_bf), full(g_cross), full(wq_bf)],
        out_specs=[pl.BlockSpec((tm, D_MODEL), lambda i: (i, 0)), pl.BlockSpec((tm, D_MODEL), lambda i: (i, 0))],
        out_shape=[jax.ShapeDtypeStruct((n, D_MODEL), F32), jax.ShapeDtypeStruct((n, D_MODEL), BF16)],
        compiler_params=_cparams(("parallel",)),
        name="out_proj",
    )(x2d, mda, mret, wo_bf, g_cross, wq_bf)


def _mem_kv_kernel(m_ref, g_ref, wk_ref, wv_ref, mk_ref, mv_ref, mkb_ref, mvb_ref):
    mn = _rms(m_ref[...], g_ref[...]).astype(BF16)
    mk = _dot(mn, wk_ref[...])
    mv = _dot(mn, wv_ref[...])
    mk_ref[...] = mk
    mv_ref[...] = mv
    mkb_ref[...] = mk.astype(BF16)
    mvb_ref[...] = mv.astype(BF16)


def _mem_kv(mem2d, g_mem, wk_bf, wv_bf, tm):
    n = mem2d.shape[0]
    row = lambda: pl.BlockSpec((tm, D_MODEL), lambda i: (i, 0))
    full = lambda a: pl.BlockSpec(a.shape, lambda i: (0, 0))
    sh = lambda dt: jax.ShapeDtypeStruct((n, D_MODEL), dt)
    return pl.pallas_call(
        _mem_kv_kernel,
        grid=(n // tm,),
        in_specs=[row(), full(g_mem), full(wk_bf), full(wv_bf)],
        out_specs=[row(), row(), row(), row()],
        out_shape=[sh(F32), sh(F32), sh(BF16), sh(BF16)],
        compiler_params=_cparams(("parallel",)),
        name="mem_kv",
    )(mem2d, g_mem, wk_bf, wv_bf)


def _cross_kernel(x1_ref, qx_ref, mk_ref, mv_ref, wo_ref, g_ref, wpq_ref, x2_ref, h3_ref, qp_ref):
    q = qx_ref[...]
    heads = []
    for h in range(H_X):
        sl = slice(h * HD_X, (h + 1) * HD_X)
        kh = mk_ref[:, sl] if mk_ref.dtype == BF16 else mk_ref[:, sl].astype(BF16)
        vh = mv_ref[:, sl] if mv_ref.dtype == BF16 else mv_ref[:, sl].astype(BF16)
        s = _dot_nt(q[:, sl], kh)
        p = jnp.exp(s - jnp.max(s, axis=-1, keepdims=True))
        heads.append(_dot(p.astype(BF16), vh) / jnp.sum(p, axis=-1, keepdims=True))
    o = jnp.concatenate(heads, axis=-1).astype(BF16)
    x2 = x1_ref[...] + _dot(o, wo_ref[...])
    x2_ref[...] = x2
    h3 = _rms(x2, g_ref[...])
    h3_ref[...] = h3
    qp_ref[...] = _dot(h3.astype(BF16), wpq_ref[...]).astype(BF16)


def _cross(x1, qx, mk, mv, wo_bf, g_ffn, wpq_bf, tm):
    b, t, _ = x1.shape
    row = lambda w: pl.BlockSpec((None, tm, w), lambda b_, i: (b_, i, 0))
    mem = lambda: pl.BlockSpec((None, N_MEM, D_MODEL), lambda b_, i: (b_, 0, 0))
    full = lambda a: pl.BlockSpec(a.shape, lambda b_, i: (0, 0))
    dq = wpq_bf.shape[1]
    return pl.pallas_call(
        _cross_kernel,
        grid=(b, t // tm),
        in_specs=[row(D_MODEL), row(D_MODEL), mem(), mem(), full(wo_bf), full(g_ffn), full(wpq_bf)],
        out_specs=[row(D_MODEL), row(D_MODEL), row(dq)],
        out_shape=[jax.ShapeDtypeStruct((b, t, D_MODEL), F32), jax.ShapeDtypeStruct((b, t, D_MODEL), F32),
                   jax.ShapeDtypeStruct((b, t, dq), BF16)],
        compiler_params=_cparams(("parallel", "parallel")),
        name="cross_attn",
    )(x1, qx, mk, mv, wo_bf, g_ffn, wpq_bf)


ID_PAD = 2.0 ** 29


def _topk_rows(s, ids):
    vals, sel = [], []
    for _ in range(PEER_TOPK):
        m = jnp.max(s, axis=0, keepdims=True)
        idx = jnp.min(jnp.where(s == m, ids, ID_PAD), axis=0, keepdims=True)
        vals.append(m)
        sel.append(idx)
        s = jnp.where(ids == idx, -jnp.inf, s)
    return jnp.concatenate(vals, axis=0), jnp.concatenate(sel, axis=0)


def _candidates(v1, i1, v2, i2):
    lanes = v1.shape[1]
    b8 = lax.broadcasted_iota(jnp.int32, (8, lanes), 0)
    ident = lambda a, ia, ib, b: (a * PEER_TOPK + b) * float(N_EXPERTS) + (ia * float(N_KEYS) + ib)
    b16 = lax.broadcasted_iota(jnp.int32, (PEER_TOPK, lanes), 0).astype(F32)
    vals = [v1[0:1] + v2]
    ids = [ident(0, i1[0:1], i2, b16)]
    for a in range(1, 8):
        keep = b8 < PEER_TOPK // (a + 1)
        vals.append(jnp.where(keep, v1[a:a + 1] + v2[0:8], -jnp.inf))
        ids.append(jnp.where(keep, ident(a, i1[a:a + 1], i2[0:8], b8.astype(F32)), ID_PAD))
    a_hi = (b8 + 8).astype(F32)
    vals.append(v1[8:16] + v2[0:1])
    ids.append(ident(a_hi, i1[8:16], i2[0:1], 0.0))
    return jnp.concatenate(vals, axis=0), jnp.concatenate(ids, axis=0)


def _route_kernel(qp_ref, k1_ref, k2_ref, e_ref, g_ref):
    half = N_KEYS
    tt = qp_ref.shape[0]
    key_id = lax.broadcasted_iota(jnp.int32, (N_KEYS, tt), 0).astype(F32)
    es, gs = [], []
    for p in range(PEER_HEADS):
        q1 = qp_ref[:, (2 * p) * half:(2 * p + 1) * half]
        q2 = qp_ref[:, (2 * p + 1) * half:(2 * p + 2) * half]
        v1, i1 = _topk_rows(_dot_nt(k1_ref[p], q1), key_id)
        v2, i2 = _topk_rows(_dot_nt(k2_ref[p], q2), key_id)
        sc, sel = _topk_rows(*_candidates(v1, i1, v2, i2))
        w = jnp.exp(sc - sc[0:1])
        es.append((sel.astype(jnp.int32) & (N_EXPERTS - 1)) * ROWS_PER_EXPERT)
        gs.append(w / jnp.sum(w, axis=0, keepdims=True))
    e_ref[...] = jnp.concatenate(es, axis=0).T
    g_ref[...] = jnp.concatenate(gs, axis=0).T


def _route_extra_kernel(qp_ref, k1_ref, k2_ref, *rest):
    _route_kernel(qp_ref, k1_ref, k2_ref, rest[-2], rest[-1])


def _route(qp2d, k1_bf, k2_bf, tt, first_token=0, n_tokens=None, into=None, after=()):
    n = qp2d.shape[0]
    n_tokens = n if n_tokens is None else n_tokens
    off = first_token // tt
    full = lambda a: pl.BlockSpec(a.shape, lambda i: (0, 0, 0))
    pair = lambda: pl.BlockSpec((tt, N_PAIRS), lambda i: (i + off, 0))
    in_specs = [pl.BlockSpec((tt, qp2d.shape[1]), lambda i: (i + off, 0)), full(k1_bf), full(k2_bf)]
    args = [qp2d, k1_bf, k2_bf]
    aliases = {}
    if into is not None:
        in_specs += [pl.BlockSpec(memory_space=pl.ANY), pl.BlockSpec(memory_space=pl.ANY)]
        args += list(into)
        aliases = {3: 0, 4: 1}
    in_specs += [pl.BlockSpec(memory_space=pl.ANY)] * len(after)
    args += list(after)
    return pl.pallas_call(
        _route_kernel if len(args) == 3 else _route_extra_kernel,
        grid=(n_tokens // tt,),
        in_specs=in_specs,
        out_specs=[pair(), pair()],
        out_shape=[jax.ShapeDtypeStruct((n, N_PAIRS), jnp.int32), jax.ShapeDtypeStruct((n, N_PAIRS), F32)],
        input_output_aliases=aliases,
        compiler_params=_cparams(("parallel",)),
        name="peer_route",
    )(*args)


def _unpack(words):
    hi = lax.bitcast_convert_type(words & jnp.uint32(0xFFFF0000), F32)
    lo = lax.bitcast_convert_type(words << 16, F32)
    return hi, lo


def _expert_rows(tab_ref, row0):
    return tab_ref[pl.ds(pl.multiple_of(row0, ROWS_PER_EXPERT), ROWS_PER_EXPERT), :]


REDUCE_TOKENS = 16


def _lane_sums_to_rows(y, n_tok):
    hi = y.astype(BF16)
    lo = (y - hi.astype(F32)).astype(BF16)
    ones = jnp.ones((128, N_PAIRS), BF16)
    s = (_dot(hi, ones) + _dot(lo, ones)).reshape(n_tok, N_PAIRS, N_PAIRS)
    eye = lax.broadcasted_iota(jnp.int32, (N_PAIRS, N_PAIRS), 0) == lax.broadcasted_iota(jnp.int32, (N_PAIRS, N_PAIRS), 1)
    return jnp.sum(jnp.where(eye[None], s, 0.0), axis=1)


CHUNKS = D_MODEL // 128


def _peer_u_kernel(e_ref, h_ref, gate_ref, tab_ref, w_ref, h8_ref, prod_ref, ys_ref, act_ref, *, tt):
    for c in range(CHUNKS):
        h8_ref[pl.ds(c, tt, stride=CHUNKS), :] = h_ref[:, c * 128:(c + 1) * 128]

    def token(t, slot):
        base = pl.multiple_of(t * CHUNKS, CHUNKS)
        ha = h8_ref[pl.ds(base, ROWS_PER_EXPERT), :]
        hb = h8_ref[pl.ds(pl.multiple_of(base + ROWS_PER_EXPERT, ROWS_PER_EXPERT), ROWS_PER_EXPERT), :]
        prod = prod_ref.at[slot]
        for k in range(N_PAIRS):
            hi, lo = _unpack(_expert_rows(tab_ref, e_ref[t, k]))
            prod[k * ROWS_PER_EXPERT:(k + 1) * ROWS_PER_EXPERT, :] = hi * ha + lo * hb
        y = prod[pl.ds(0, N_PAIRS, stride=ROWS_PER_EXPERT), :]
        for c in range(1, ROWS_PER_EXPERT):
            y = y + prod[pl.ds(c, N_PAIRS, stride=ROWS_PER_EXPERT), :]
        ys_ref[pl.ds(pl.multiple_of(t * N_PAIRS, N_PAIRS), N_PAIRS), :] = y

    def two_tokens(j, carry):
        token(2 * j, 0)
        token(2 * j + 1, 1)
        return carry

    lax.fori_loop(0, tt // 2, two_tokens, 0)

    def group(g, carry):
        rows = REDUCE_TOKENS * N_PAIRS
        y = ys_ref[pl.ds(pl.multiple_of(g * rows, rows), rows), :]
        act_ref[pl.ds(pl.multiple_of(g * REDUCE_TOKENS, REDUCE_TOKENS), REDUCE_TOKENS), :] = _lane_sums_to_rows(y, REDUCE_TOKENS)
        return carry

    lax.fori_loop(0, tt // REDUCE_TOKENS, group, 0)
    w_ref[...] = _gated_gelu(gate_ref[...], act_ref[...])


def _gated_gelu(gate, a):
    return gate * (0.5 * a * (1.0 + lax.erf(a * (2.0 ** -0.5))))


def _gate_act_kernel(act_ref, gate_ref, *rest):
    rest[-1][...] = _gated_gelu(gate_ref[...], act_ref[...])


def _gate_act(act, gate_t, w_t, first_token, tt, after=()):
    off = first_token // tt
    return pl.pallas_call(
        _gate_act_kernel,
        grid=(act.shape[0] // tt,),
        in_specs=[pl.BlockSpec((tt, N_PAIRS), lambda i: (i, 0)),
                  pl.BlockSpec((tt, N_PAIRS), lambda i: (i + off, 0))]
                 + [pl.BlockSpec(memory_space=pl.ANY)] * (1 + len(after)),
        out_specs=pl.BlockSpec((tt, N_PAIRS), lambda i: (i + off, 0)),
        out_shape=jax.ShapeDtypeStruct(w_t.shape, F32),
        input_output_aliases={2: 0},
        compiler_params=_cparams(("arbitrary",)),
        name="peer_gate_act",
    )(act, gate_t, w_t, *after)


def _peer_u(e_t, h, gate_t, tab, tt, n_tokens):
    n = e_t.shape[0]
    assert tt % REDUCE_TOKENS == 0
    kern = functools.partial(_peer_u_kernel, tt=tt)
    pair = lambda **kw: pl.BlockSpec((tt, N_PAIRS), lambda i: (i, 0), **kw)
    return pl.pallas_call(
        kern,
        grid=(n_tokens // tt,),
        in_specs=[pair(memory_space=pltpu.SMEM),
                  pl.BlockSpec((tt, D_MODEL), lambda i: (i, 0)),
                  pair(),
                  pl.BlockSpec(tab.shape, lambda i: (0, 0), pipeline_mode=pl.Buffered(1))],
        out_specs=pair(),
        out_shape=jax.ShapeDtypeStruct((n, N_PAIRS), F32),
        scratch_shapes=[pltpu.VMEM((tt * CHUNKS, 128), F32),
                        pltpu.VMEM((2, N_PAIRS * ROWS_PER_EXPERT, 128), F32),
                        pltpu.VMEM((tt * N_PAIRS, 128), F32),
                        pltpu.VMEM((tt, N_PAIRS), F32)],
        compiler_params=_cparams(("arbitrary",)),
        name="peer_u",
    )(e_t, h, gate_t, tab)


def _peer_v_kernel(e_ref, w_ref, x_ref, g_ref, tab_ref, o_ref, acc_ref, *, tt):
    n_acc = 2

    def token(t):
        acc_hi = [jnp.zeros((ROWS_PER_EXPERT, 128), F32) for _ in range(n_acc)]
        acc_lo = [jnp.zeros((ROWS_PER_EXPERT, 128), F32) for _ in range(n_acc)]
        for k in range(N_PAIRS):
            hi, lo = _unpack(_expert_rows(tab_ref, e_ref[t, k]))
            w = w_ref[t, k]
            acc_hi[k % n_acc] = acc_hi[k % n_acc] + w * hi
            acc_lo[k % n_acc] = acc_lo[k % n_acc] + w * lo
        base = pl.multiple_of(t * CHUNKS, CHUNKS)
        acc_ref[pl.ds(base, ROWS_PER_EXPERT), :] = sum(acc_hi[1:], acc_hi[0])
        acc_ref[pl.ds(pl.multiple_of(base + ROWS_PER_EXPERT, ROWS_PER_EXPERT), ROWS_PER_EXPERT), :] = sum(acc_lo[1:], acc_lo[0])

    def two_tokens(j, carry):
        token(2 * j)
        token(2 * j + 1)
        return carry

    lax.fori_loop(0, tt // 2, two_tokens, 0)
    _residual_rms_store(x_ref, acc_ref, g_ref, o_ref, tt)


def _residual_rms_store(x_ref, add_ref, g_ref, o_ref, tt):
    xs = [x_ref[:, c * 128:(c + 1) * 128] + add_ref[pl.ds(c, tt, stride=CHUNKS), :] for c in range(CHUNKS)]
    sq = xs[0] * xs[0]
    for x in xs[1:]:
        sq = sq + x * x
    r = lax.rsqrt(jnp.sum(sq, axis=1, keepdims=True) * (1.0 / D_MODEL) + EPS)
    for c in range(CHUNKS):
        o_ref[:, c * 128:(c + 1) * 128] = xs[c] * r * g_ref[:, c * 128:(c + 1) * 128]


def _peer_v(e_t, w_t, x, g_final, tab, tt, n_tokens, out_rows=None):
    out_rows = x.shape[0] if out_rows is None else out_rows
    kern = functools.partial(_peer_v_kernel, tt=tt)
    pair = lambda: pl.BlockSpec((tt, N_PAIRS), lambda i: (i, 0), memory_space=pltpu.SMEM)
    row = lambda: pl.BlockSpec((tt, D_MODEL), lambda i: (i, 0))
    return pl.pallas_call(
        kern,
        grid=(n_tokens // tt,),
        in_specs=[pair(), pair(), row(),
                  pl.BlockSpec((1, D_MODEL), lambda i: (0, 0)),
                  pl.BlockSpec(tab.shape, lambda i: (0, 0), pipeline_mode=pl.Buffered(1))],
        out_specs=row(),
        out_shape=jax.ShapeDtypeStruct((out_rows, D_MODEL), F32),
        scratch_shapes=[pltpu.VMEM((tt * CHUNKS, 128), F32)],
        compiler_params=_cparams(("arbitrary",)),
        name="peer_v",
    )(e_t, w_t, x, g_final, tab)


def _residual_norm_kernel(x_ref, add_ref, g_ref, y_any_ref, o_ref, *, tt):
    del y_any_ref
    _residual_rms_store(x_ref, add_ref, g_ref, o_ref, tt)


def _residual_norm(x, add8, g_final, y, x_first, y_first, tt):
    n_rows = add8.shape[0] // CHUNKS
    x_off, y_off = x_first // tt, y_first // tt
    return pl.pallas_call(
        functools.partial(_residual_norm_kernel, tt=tt),
        grid=(n_rows // tt,),
        in_specs=[pl.BlockSpec((tt, D_MODEL), lambda i: (i + x_off, 0)),
                  pl.BlockSpec((tt * CHUNKS, 128), lambda i: (i, 0)),
                  pl.BlockSpec((1, D_MODEL), lambda i: (0, 0)),
                  pl.BlockSpec(memory_space=pl.ANY)],
        out_specs=pl.BlockSpec((tt, D_MODEL), lambda i: (i + y_off, 0)),
        out_shape=jax.ShapeDtypeStruct(y.shape, F32),
        input_output_aliases={3: 0},
        compiler_params=_cparams(("arbitrary",)),
        name="peer_residual_norm",
    )(x, add8, g_final, y)


SC_CORES = 2
SC_SUBCORES = 16
SC_LANES = 16
WORD_ROWS = 128 // SC_LANES
SC_TOKEN_CHUNK = 32
SC_RING = 4


def _peer_sc(e_t, aux, tab, down):
    n = e_t.shape[0]
    workers = SC_CORES * SC_SUBCORES
    assert n % (workers * SC_TOKEN_CHUNK) == 0 and PEER_HEADS % SC_RING == 0
    per = n // workers
    head_rows = ROWS_PER_EXPERT * PEER_TOPK
    aux_rows = CHUNKS if down else 1
    out_rows = 1 if down else CHUNKS
    mesh = plsc.VectorSubcoreMesh(core_axis_name="core", subcore_axis_name="subcore",
                                  num_cores=SC_CORES, num_subcores=SC_SUBCORES)

    @pl.kernel(out_type=jax.ShapeDtypeStruct((n * out_rows, 128), F32), mesh=mesh,
               scratch_types=[pltpu.VMEM((SC_TOKEN_CHUNK, N_PAIRS), jnp.int32),
                              pltpu.VMEM((SC_TOKEN_CHUNK * aux_rows, 128), F32),
                              pltpu.VMEM((SC_RING, head_rows, 128), jnp.uint32),
                              pltpu.VMEM((SC_TOKEN_CHUNK * out_rows, 128), F32),
                              pltpu.VMEM((PEER_TOPK, SC_LANES), F32),
                              pltpu.SemaphoreType.DMA((SC_RING,))],
               compiler_params=pltpu.CompilerParams(needs_layout_passes=False),
               name="peer_u_sc" if down else "peer_v_sc")
    def body(e_hbm, aux_hbm, tab_hbm, o_hbm, e_v, aux_v, rows_v, out_v, fold_v, sems):
        wid = lax.axis_index("core") * SC_SUBCORES + lax.axis_index("subcore")
        lane = lax.broadcasted_iota(jnp.int32, (SC_LANES,), 0)

        def gathers(i, p, slot):
            first = e_v[i, pl.ds(p * PEER_TOPK, PEER_TOPK)]
            return [pltpu.make_async_copy(tab_hbm.at[first + r],
                                          rows_v.at[slot, pl.ds(r * PEER_TOPK, PEER_TOPK)], sems.at[slot])
                    for r in range(ROWS_PER_EXPERT)]

        def start_gather(i, p, slot):
            for d in gathers(i, p, slot):
                d.start()

        def weighted_sum(i, p, slot):
            ws = [plsc.load_gather(aux_v, [lane * 0 + i, lane * 0 + (p * PEER_TOPK + k)])
                  for k in range(PEER_TOPK)]
            for r in range(ROWS_PER_EXPERT):
                @pl.loop(0, WORD_ROWS)
                def _(j):
                    sl = pl.ds(j * SC_LANES, SC_LANES)
                    if p == 0:
                        a_hi = jnp.zeros((SC_LANES,), F32)
                        a_lo = jnp.zeros((SC_LANES,), F32)
                    else:
                        a_hi = out_v[i * CHUNKS + r, sl]
                        a_lo = out_v[i * CHUNKS + ROWS_PER_EXPERT + r, sl]
                    for k in range(PEER_TOPK):
                        hi, lo = _unpack(rows_v[slot, r * PEER_TOPK + k, sl])
                        a_hi = a_hi + ws[k] * hi
                        a_lo = a_lo + ws[k] * lo
                    out_v[i * CHUNKS + r, sl] = a_hi
                    out_v[i * CHUNKS + ROWS_PER_EXPERT + r, sl] = a_lo

        def pair_dots(i, p, slot):
            accs = tuple(jnp.zeros((SC_LANES,), F32) for _ in range(PEER_TOPK))
            for r in range(ROWS_PER_EXPERT):
                def piece(j, accs, r=r):
                    sl = pl.ds(j * SC_LANES, SC_LANES)
                    ha = aux_v[i * CHUNKS + r, sl]
                    hb = aux_v[i * CHUNKS + ROWS_PER_EXPERT + r, sl]
                    out = []
                    for k in range(PEER_TOPK):
                        hi, lo = _unpack(rows_v[slot, r * PEER_TOPK + k, sl])
                        out.append(accs[k] + (hi * ha + lo * hb))
                    return tuple(out)
                accs = lax.fori_loop(0, WORD_ROWS, piece, accs)
            for k in range(PEER_TOPK):
                fold_v[k, :] = accs[k]
            tot = plsc.load_gather(fold_v, [lane, lane * 0])
            for l in range(1, SC_LANES):
                tot = tot + plsc.load_gather(fold_v, [lane, lane * 0 + l])
            out_v[i, pl.ds(p * PEER_TOPK, PEER_TOPK)] = tot

        consume = pair_dots if down else weighted_sum

        @pl.loop(0, per // SC_TOKEN_CHUNK)
        def _(c):
            t0 = wid * per + c * SC_TOKEN_CHUNK
            pltpu.sync_copy(e_hbm.at[pl.ds(t0, SC_TOKEN_CHUNK)], e_v)
            pltpu.sync_copy(aux_hbm.at[pl.ds(t0 * aux_rows, SC_TOKEN_CHUNK * aux_rows)], aux_v)
            for p in range(SC_RING):
                start_gather(0, p, p)

            @pl.loop(0, SC_TOKEN_CHUNK)
            def _(i):
                for p in range(PEER_HEADS):
                    slot = p % SC_RING
                    for d in gathers(i, p, slot):
                        d.wait()
                    consume(i, p, slot)
                    if p + SC_RING < PEER_HEADS:
                        start_gather(i, p + SC_RING, slot)
                    else:
                        @pl.when(i + 1 < SC_TOKEN_CHUNK)
                        def _():
                            start_gather(i + 1, p + SC_RING - PEER_HEADS, slot)

            pltpu.sync_copy(out_v, o_hbm.at[pl.ds(t0 * out_rows, SC_TOKEN_CHUNK * out_rows)])

    return body(e_t, aux, tab)


def _pack_table(tab):
    bits = lax.bitcast_convert_type(tab.astype(BF16), jnp.uint16).astype(jnp.uint32)
    words = (bits[:, :HALF] << 16) | bits[:, HALF:]
    return words.reshape(tab.shape[0] * ROWS_PER_EXPERT, 128)


PEER_TOKENS = 128
SC_LEAD_BATCHES = (1, 4)
SC_SHARE_DOWN = (1, 3)
SC_SHARE_UP = (3, 8)
SC_MIN_TOKENS = 4096


def _sc_tokens(n):
    if n < SC_MIN_TOKENS:
        return (0, 0)
    unit = math.lcm(PEER_TOKENS, SC_CORES * SC_SUBCORES * SC_TOKEN_CHUNK)
    return tuple(n * num // den // unit * unit for num, den in (SC_SHARE_DOWN, SC_SHARE_UP))


def _row_tile(n, pref):
    while n % pref:
        pref //= 2
    return pref


def _peer_on_sc_down(h3, qp, k1_bf, k2_bf, tab_u, tt):
    n = h3.shape[0]
    e_t, gate_t = _route(qp, k1_bf, k2_bf, tt)
    return e_t, gate_t, _peer_sc(e_t, h3.reshape(n * CHUNKS, 128), tab_u, down=True)


def _peer_on_sc_up(e_t, gate_t, act, tab_v, tt, gate_after):
    w_t = _gate_act(act, gate_t, act, 0, tt, after=gate_after)
    return _peer_sc(e_t, w_t, tab_v, down=False), w_t


def _peer_and_final(x2, h3, qp, k1_bf, k2_bf, tab_u, tab_v, g_final, tt, n_sc=(0, 0), route_after=(), out_rows=None,
                    lead=None):
    n = x2.shape[0]
    n_sc_down, n_sc_up = n_sc
    tc_down, tc_up = n - n_sc_down, n - n_sc_up
    if n_sc_down:
        e_t, gate_t = _route(qp, k1_bf, k2_bf, tt, tc_down, n_sc_down, after=route_after)
        act = _peer_sc(e_t[tc_down:], h3[tc_down:].reshape(n_sc_down * CHUNKS, 128), tab_u, down=True)
        e_t, gate_t = _route(qp, k1_bf, k2_bf, tt, 0, tc_down, into=(e_t, gate_t))
    else:
        e_t, gate_t = _route(qp, k1_bf, k2_bf, tt)
    w_t = _peer_u(e_t, h3, gate_t, tab_u, tt, tc_down)
    if n_sc_down:
        w_t = _gate_act(act, gate_t, w_t, tc_down, tt)
    y = _peer_v(e_t, w_t, x2, g_final, tab_v, tt, tc_up, out_rows)
    if lead is not None:
        x2_lead, add8_lead = lead
        y = _residual_norm(x2_lead, add8_lead, g_final, y, 0, n, tt)
    if n_sc_up:
        add8 = _peer_sc(e_t[tc_up:], w_t[tc_up:], tab_v, down=False)
        y = _residual_norm(x2, add8, g_final, y, tc_up, tc_up, tt)
    return y


def kernel(x_prompt, x_sample, mem_prompt, cache_da_k, cache_da_v, state_ret, cache_mem_k, cache_mem_v, g_mix, w_in, lam_q1, lam_k1, lam_q2, lam_k2, g_da, g_ret, w_out, g_cross, g_mem, w_xq, w_xk, w_xv, w_xo, g_ffn, w_pq, peer_k1, peer_k2, peer_u, peer_v, g_final):
    depth = w_in.shape[0]
    assert depth == 1, "single-layer step"
    l = 0
    lam_init = 0.8 - 0.6 * math.exp(-0.3 * l)
    b, t, _ = x_prompt.shape
    bs, ts, _ = x_sample.shape
    past_len = cache_da_k.shape[2]

    row = lambda a: a.reshape(1, -1)
    w_in_bf = w_in[l].astype(BF16)
    w_out_bf = w_out[l].astype(BF16)
    w_xq_bf, w_xk_bf, w_xv_bf, w_xo_bf = (w[l].astype(BF16) for w in (w_xq, w_xk, w_xv, w_xo))
    w_pq_bf = w_pq[l].astype(BF16)
    k1_bf, k2_bf = peer_k1[l].astype(BF16), peer_k2[l].astype(BF16)
    tab_u, tab_v = _pack_table(peer_u[l]), _pack_table(peer_v[l])
    lamp = jnp.stack([lam_q1[l], lam_k1[l], lam_q2[l], lam_k2[l]])
    g_da3 = g_da[l].reshape(H_D, 1, DV_D)
    g_ret3 = g_ret[l].reshape(H_R, 1, DV_R)
    g_fin = row(g_final)

    def mixer_mid(x2d, first_row, mda, mret, mk, mv, bb, tt_rows):
        n = bb * tt_rows
        tm = _row_tile(n, 512)
        x1, qx = _out_proj(x2d, mda.reshape(n, GROUP_W), mret.reshape(n, GROUP_W), w_out_bf, row(g_cross[l]), w_xq_bf,
                           tm, first_row)
        x2, h3, qp = _cross(x1.reshape(bb, tt_rows, D_MODEL), qx.reshape(bb, tt_rows, D_MODEL), mk, mv,
                            w_xo_bf, row(g_ffn[l]), w_pq_bf, _row_tile(tt_rows, 512))
        return x2.reshape(n, D_MODEL), h3.reshape(n, D_MODEL), qp.reshape(n, -1)

    peer_args = (k1_bf, k2_bf, tab_u, tab_v, g_fin, PEER_TOKENS)
    n = b * t
    xp = x_prompt.reshape(n, D_MODEL)

    def prompt_dense(b0, bg, kv_into=None):
        rows = bg * t
        qd, kd, vd, kdb, vdb, qr, kr, vr, gr = _in_proj(xp, row(g_mix[l]), w_in_bf, _row_tile(rows, 512),
                                                       b0 * t, rows, kv_into)
        r3 = lambda a: a.reshape(bg, t, GROUP_W)
        mda = _diff_attn_prompt(lamp, r3(qd), r3(kdb), r3(vdb), g_da3, lam_init, 512, 512)
        mret, s_fin = _retention(r3(qr), r3(kr), r3(vr), r3(gr), g_ret3, None, 512)
        return mixer_mid(xp, b0 * t, mda, mret, mkb[b0:b0 + bg], mvb[b0:b0 + bg], bg, t), (kd, vd), s_fin

    mk, mv, mkb, mvb = _mem_kv(mem_prompt.reshape(b * N_MEM, D_MODEL), row(g_mem[l]), w_xk_bf, w_xv_bf, 512)
    mkb, mvb = mkb.reshape(b, N_MEM, D_MODEL), mvb.reshape(b, N_MEM, D_MODEL)
    b_rest = b - b * SC_LEAD_BATCHES[0] // SC_LEAD_BATCHES[1]
    if b_rest < b:
        (x2_l, h3_l, qp_l), kv, s_l = prompt_dense(b_rest, b - b_rest)
        e_l, gate_l, act_l = _peer_on_sc_down(h3_l, qp_l, k1_bf, k2_bf, tab_u, PEER_TOKENS)
        peer_rest, (kd, vd), s_r = prompt_dense(0, b_rest, kv_into=kv)
        add8_l, w_lead = _peer_on_sc_up(e_l, gate_l, act_l, tab_v, PEER_TOKENS, gate_after=(peer_rest[2],))
        y_prompt = _peer_and_final(*peer_rest, *peer_args, _sc_tokens(b_rest * t), route_after=(w_lead,), out_rows=n,
                                   lead=(x2_l, add8_l)).reshape(b, t, D_MODEL)
        s_fin = jnp.concatenate([s_r, s_l])
    else:
        peer_all, (kd, vd), s_fin = prompt_dense(0, b)
        y_prompt = _peer_and_final(*peer_all, *peer_args, _sc_tokens(n)).reshape(b, t, D_MODEL)

    ns = bs * ts
    xs = x_sample.reshape(ns, D_MODEL)
    qd_s, kd_s, vd_s, kdb_s, vdb_s, qr_s, kr_s, vr_s, gr_s = _in_proj(xs, row(g_mix[l]), w_in_bf, _row_tile(ns, 512))
    s3 = lambda a: a.reshape(bs, ts, GROUP_W)
    mda_s = _diff_attn_sample(lamp, s3(qd_s), cache_da_k[l].reshape(bs, past_len, GROUP_W),
                              cache_da_v[l].reshape(bs, past_len, GROUP_W), s3(kdb_s), s3(vdb_s), g_da3, lam_init)
    mret_s, s_new = _retention(s3(qr_s), s3(kr_s), s3(vr_s), s3(gr_s), g_ret3, state_ret[l], ts)
    peer_s = mixer_mid(xs, 0, mda_s, mret_s, cache_mem_k[l].reshape(bs, N_MEM, D_MODEL),
                       cache_mem_v[l].reshape(bs, N_MEM, D_MODEL), bs, ts)
    y_sample = _peer_and_final(*peer_s, *peer_args).reshape(bs, ts, D_MODEL)

    return (y_prompt, y_sample,
            kd.reshape(1, b, t, H_D, 2, DK_D), vd.reshape(1, b, t, H_D, DV_D), s_fin[None],
            mk.reshape(1, b, N_MEM, H_X, HD_X), mv.reshape(1, b, N_MEM, H_X, HD_X),
            kd_s.reshape(1, bs, ts, H_D, 2, DK_D), vd_s.reshape(1, bs, ts, H_D, DV_D), s_new[None])
```

```python
import functools
import math

import jax
import jax.numpy as jnp
from jax import lax
from jax.experimental import pallas as pl
from jax.experimental.pallas import tpu as pltpu
from jax.experimental.pallas import tpu_sc as plsc

D_MODEL = 1024
CHUNK = 64
CHUNK_SHIFT = CHUNK.bit_length() - 1
assert 1 << CHUNK_SHIFT == CHUNK
H_D, DK_D, DV_D = 4, 64, 128
H_R, DK_R, DV_R = 4, 128, 128
N_MEM = 256
H_X = 4
HD_X = D_MODEL // H_X
PEER_HEADS = 8
N_KEYS = 128
N_EXPERTS = N_KEYS * N_KEYS
PEER_TOPK = 16
EPS = 1e-6
HEAD_W = 128
GROUP_W = 512
N_PAIRS = PEER_HEADS * PEER_TOPK
HALF = D_MODEL // 2
ROWS_PER_EXPERT = HALF // 128
VMEM_LIMIT = 56 * 1024 * 1024

BF16 = jnp.bfloat16
F32 = jnp.float32


def _cparams(sem):
    return pltpu.CompilerParams(dimension_semantics=sem, vmem_limit_bytes=VMEM_LIMIT)


def _rms(x, g):
    return x * lax.rsqrt(jnp.mean(x * x, axis=-1, keepdims=True) + EPS) * g


def _dot(a, b):
    return jnp.dot(a, b, preferred_element_type=F32)


def _dot_nt(a, b):
    return lax.dot_general(a, b, (((1,), (1,)), ((), ())), preferred_element_type=F32)


def _dot_tn(a, b):
    return lax.dot_general(a, b, (((0,), (0,)), ((), ())), preferred_element_type=F32)


def _select_by_head(h, values):
    out = jnp.float32(values[-1])
    for i in range(len(values) - 2, -1, -1):
        out = jnp.where(h == i, jnp.float32(values[i]), out)
    return out


def _in_proj_kernel(x_ref, g_ref, w_ref, *rest):
    qd_ref, kd_ref, vd_ref, kdb_ref, vdb_ref, qr_ref, kr_ref, vr_ref, gr_ref = rest[-9:]
    hb = _rms(x_ref[...], g_ref[...]).astype(BF16)
    col = lambda c: _dot(hb, w_ref[:, c * GROUP_W:(c + 1) * GROUP_W])
    qd_ref[...] = (col(0) * (DK_D ** -0.5)).astype(BF16)
    kd = col(1)
    kd_ref[...] = kd
    kdb_ref[...] = kd.astype(BF16)
    vd = col(2)
    vd_ref[...] = vd
    vdb_ref[...] = vd.astype(BF16)
    qr_ref[...] = col(3).astype(BF16)
    kr_ref[...] = (col(4) * (DK_R ** -0.5)).astype(BF16)
    vr_ref[...] = col(5).astype(BF16)
    gr_ref[...] = col(6)


def _in_proj(x2d, g, w_bf, tm, first_row=0, n_rows=None, kv_into=None):
    n_all = x2d.shape[0]
    n = n_all if n_rows is None else n_rows
    off = first_row // tm
    blk = lambda: pl.BlockSpec((tm, GROUP_W), lambda i: (i, 0))
    blk_all = lambda: pl.BlockSpec((tm, GROUP_W), lambda i: (i + off, 0))
    sh = lambda dt: jax.ShapeDtypeStruct((n, GROUP_W), dt)
    sh_all = jax.ShapeDtypeStruct((n_all, GROUP_W), F32)
    args = [x2d, g, w_bf]
    in_specs = [pl.BlockSpec((tm, D_MODEL), lambda i: (i + off, 0)),
                pl.BlockSpec((1, D_MODEL), lambda i: (0, 0)),
                pl.BlockSpec(w_bf.shape, lambda i: (0, 0))]
    aliases = {}
    if kv_into is not None:
        args += list(kv_into)
        in_specs += [pl.BlockSpec(memory_space=pl.ANY)] * 2
        aliases = {3: 1, 4: 2}
    return pl.pallas_call(
        _in_proj_kernel,
        grid=(n // tm,),
        in_specs=in_specs,
        out_specs=[blk(), blk_all(), blk_all()] + [blk() for _ in range(6)],
        out_shape=[sh(BF16), sh_all, sh_all, sh(BF16), sh(BF16), sh(BF16), sh(BF16), sh(BF16), sh(F32)],
        input_output_aliases=aliases,
        compiler_params=_cparams(("parallel",)),
        name="in_proj",
    )(*args)


def _lambda_from(lam_ref, lam_init):
    l = lam_ref[...]
    a = jnp.exp(jnp.sum(l[0:1] * l[1:2], axis=-1, keepdims=True))
    b = jnp.exp(jnp.sum(l[2:3] * l[3:4], axis=-1, keepdims=True))
    return a - b + lam_init


def _diff_post(acc, l, lam, g, lam_init, tq):
    o = acc[:tq] / l[:tq] - lam * (acc[tq:] / l[tq:])
    return o * lax.rsqrt(jnp.mean(o * o, axis=-1, keepdims=True) + EPS) * g * (1.0 - lam_init)


def _split_maps(q):
    lane = lax.broadcasted_iota(jnp.int32, q.shape, 1)
    zero = jnp.zeros_like(q)
    return jnp.concatenate([jnp.where(lane < DK_D, q, zero), jnp.where(lane >= DK_D, q, zero)], axis=0)


def _da_prompt_kernel(lam_ref, q_ref, k_ref, v_ref, g_ref, o_ref, kx_ref, vx_ref, own_ref, acc_ref, m_ref, *, lam_init, tq, tk):
    h = pl.program_id(1)
    i = pl.program_id(2)
    t = k_ref.shape[0]
    slope = _select_by_head(h, [2.0 ** (-8.0 * (j + 1) / H_D) for j in range(H_D)])

    @pl.when(i == 0)
    def _():
        pos = lax.broadcasted_iota(jnp.int32, (t, HEAD_W), 0)
        lane = lax.broadcasted_iota(jnp.int32, (t, HEAD_W), 1)
        coarse = ((pos >> CHUNK_SHIFT) << CHUNK_SHIFT).astype(F32) * slope
        fine = (pos & (CHUNK - 1)).astype(F32) * slope
        kx_ref[:, :HEAD_W] = k_ref[...]
        kx_ref[:, HEAD_W:] = jnp.where(lane == 0, coarse, jnp.where(lane == 1, fine, 0.0)).astype(BF16)
        vx_ref[:DV_D, :] = v_ref[...].astype(F32).T.astype(BF16)
        vx_ref[DV_D:, :] = jnp.ones((vx_ref.shape[0] - DV_D, t), BF16)
        krel = lax.broadcasted_iota(jnp.int32, (tk, 2 * tq), 0)
        c = lax.broadcasted_iota(jnp.int32, (tk, 2 * tq), 1)
        for par in range(tk // tq):
            qrel = par * tq + jnp.where(c >= tq, c - tq, c)
            ahead = (2.0 * slope) * jnp.maximum(krel - qrel, 0).astype(F32)
            own_ref[par] = jnp.where((qrel >> CHUNK_SHIFT) >= (krel >> CHUNK_SHIFT), -ahead, -1e30)

    q = q_ref[...]
    lane = lax.broadcasted_iota(jnp.int32, q.shape, 1)
    zero = jnp.zeros_like(q)
    ones2 = jnp.where(lane < 2, 1.0, 0.0).astype(BF16)
    q2 = jnp.concatenate([jnp.concatenate([jnp.where(lane < DK_D, q, zero), ones2], axis=1),
                          jnp.concatenate([jnp.where(lane >= DK_D, q, zero), ones2], axis=1)], axis=0)
    jd = (i * tq) // tk

    def scores(j):
        return _dot_nt(kx_ref[pl.ds(pl.multiple_of(j * tk, tk), tk), :], q2)

    def values(j):
        return vx_ref[:, pl.ds(pl.multiple_of(j * tk, tk), tk)]

    s = scores(jd) + own_ref[(i * tq) % tk // tq]
    m0 = jnp.max(s, axis=0, keepdims=True)
    m_ref[...] = m0
    acc_ref[...] = _dot(values(jd), jnp.exp(s - m0).astype(BF16))

    def absorb(blocks):
        ss = [scores(j) for j in blocks]
        m_old = m_ref[...]
        m_new = m_old
        for s in ss:
            m_new = jnp.maximum(m_new, jnp.max(s, axis=0, keepdims=True))
        m_ref[...] = m_new
        acc = jnp.exp(m_old - m_new) * acc_ref[...]
        for j, s in zip(blocks, ss):
            acc = acc + _dot(values(j), jnp.exp(s - m_new).astype(BF16))
        acc_ref[...] = acc

    def past_pair(jj, carry):
        absorb([2 * jj, 2 * jj + 1])
        return carry

    lax.fori_loop(0, jd // 2, past_pair, 0)

    @pl.when(jd % 2 == 1)
    def _():
        absorb([jd - 1])

    acc = acc_ref[...]
    num, den = acc[:DV_D], acc[DV_D:DV_D + 1]
    lam = _lambda_from(lam_ref, lam_init)
    o = (num[:, :tq] / den[:, :tq] - lam * (num[:, tq:] / den[:, tq:])).T
    o = o * lax.rsqrt(jnp.mean(o * o, axis=-1, keepdims=True) + EPS) * g_ref[...] * (1.0 - lam_init)
    o_ref[...] = o.astype(o_ref.dtype)


ONES_ROWS = 16


def _diff_attn_prompt(lamp, q, k, v, g_da3, lam_init, tq, tk):
    b, t, _ = q.shape
    kern = functools.partial(_da_prompt_kernel, lam_init=lam_init, tq=tq, tk=tk)
    return pl.pallas_call(
        kern,
        grid=(b, H_D, t // tq),
        in_specs=[pl.BlockSpec((4, DK_D), lambda b_, h, i: (0, 0)),
                  pl.BlockSpec((None, tq, HEAD_W), lambda b_, h, i: (b_, i, h)),
                  pl.BlockSpec((None, t, HEAD_W), lambda b_, h, i: (b_, 0, h)),
                  pl.BlockSpec((None, t, HEAD_W), lambda b_, h, i: (b_, 0, h)),
                  pl.BlockSpec((None, 1, HEAD_W), lambda b_, h, i: (h, 0, 0))],
        out_specs=pl.BlockSpec((None, tq, HEAD_W), lambda b_, h, i: (b_, i, h)),
        out_shape=jax.ShapeDtypeStruct((b, t, GROUP_W), BF16),
        scratch_shapes=[pltpu.VMEM((t, 2 * HEAD_W), BF16), pltpu.VMEM((DV_D + ONES_ROWS, t), BF16),
                        pltpu.VMEM((tk // tq, tk, 2 * tq), F32),
                        pltpu.VMEM((DV_D + ONES_ROWS, 2 * tq), F32), pltpu.VMEM((1, 2 * tq), F32)],
        compiler_params=_cparams(("parallel", "parallel", "arbitrary")),
        name="diff_attn_prompt",
    )(lamp, q, k, v, g_da3)


def _da_sample_kernel(lam_ref, q_ref, kc_ref, vc_ref, kn_ref, vn_ref, g_ref, o_ref, *, lam_init, ts, past_len):
    h = pl.program_id(1)
    slope = _select_by_head(h, [2.0 ** (-8.0 * (j + 1) / H_D) for j in range(H_D)])
    q2 = _split_maps(q_ref[...])
    rows = 2 * ts

    def scores(k, base, n):
        r = lax.broadcasted_iota(jnp.int32, (rows, n), 0)
        qpos = past_len + jnp.where(r >= ts, r - ts, r)
        kpos = base + lax.broadcasted_iota(jnp.int32, (rows, n), 1)
        return _dot_nt(q2, k) - slope * jnp.abs(qpos - kpos).astype(F32)

    sc = scores(kc_ref[...].astype(BF16), 0, past_len)
    sn = scores(kn_ref[...], past_len, ts)
    m = jnp.maximum(jnp.max(sc, axis=-1, keepdims=True), jnp.max(sn, axis=-1, keepdims=True))
    pc = jnp.exp(sc - m)
    pn = jnp.exp(sn - m)
    l = jnp.sum(pc, axis=-1, keepdims=True) + jnp.sum(pn, axis=-1, keepdims=True)
    acc = _dot(pc.astype(BF16), vc_ref[...].astype(BF16)) + _dot(pn.astype(BF16), vn_ref[...])
    lam = _lambda_from(lam_ref, lam_init)
    o_ref[...] = _diff_post(acc, l, lam, g_ref[...], lam_init, ts).astype(o_ref.dtype)


def _diff_attn_sample(lamp, q, kc, vc, kn, vn, g_da3, lam_init):
    b, ts, _ = q.shape
    past_len = kc.shape[1]
    kern = functools.partial(_da_sample_kernel, lam_init=lam_init, ts=ts, past_len=past_len)
    head = lambda rows: pl.BlockSpec((None, rows, HEAD_W), lambda b_, h: (b_, 0, h))
    return pl.pallas_call(
        kern,
        grid=(b, H_D),
        in_specs=[pl.BlockSpec((4, DK_D), lambda b_, h: (0, 0)),
                  head(ts), head(past_len), head(past_len), head(ts), head(ts),
                  pl.BlockSpec((None, 1, HEAD_W), lambda b_, h: (h, 0, 0))],
        out_specs=head(ts),
        out_shape=jax.ShapeDtypeStruct((b, ts, GROUP_W), BF16),
        compiler_params=_cparams(("parallel", "parallel")),
        name="diff_attn_sample",
    )(lamp, q, kc, vc, kn, vn, g_da3)


def _ret_kernel(*refs, lb, has_init):
    if has_init:
        q_ref, k_ref, v_ref, gate_ref, g_ref, s0_ref, o_ref, sfin_ref, s_ref = refs
    else:
        q_ref, k_ref, v_ref, gate_ref, g_ref, o_ref, sfin_ref, s_ref = refs
    h = pl.program_id(1)
    c = pl.program_id(2)
    lg = _select_by_head(h, [math.log1p(-(2.0 ** (-5.0 - j))) for j in range(H_R)])

    @pl.when(c == 0)
    def _():
        s_ref[...] = s0_ref[...] if has_init else jnp.zeros_like(s_ref)

    q, k, v = q_ref[...], k_ref[...], v_ref[...]
    i = lax.broadcasted_iota(jnp.int32, (lb, lb), 0)
    j = lax.broadcasted_iota(jnp.int32, (lb, lb), 1)
    d = (i - j).astype(F32)
    decay = jnp.where(d >= 0, jnp.exp(jnp.maximum(d, 0.0) * lg), 0.0)
    inner = _dot_nt(q, k) * decay
    ic = lax.broadcasted_iota(jnp.int32, (lb, 1), 0).astype(F32)
    s_old = s_ref[...]
    o = _dot(inner.astype(BF16), v) + _dot(q, s_old.astype(BF16)) * jnp.exp((ic + 1.0) * lg)
    tail = jnp.exp((lb - 1.0 - ic) * lg)
    kt = (k.astype(F32) * tail).astype(BF16)
    s_new = jnp.exp(lb * lg) * s_old + _dot_tn(kt, v)
    s_ref[...] = s_new

    @pl.when(c == pl.num_programs(2) - 1)
    def _():
        sfin_ref[...] = s_new

    oc = o - jnp.mean(o, axis=-1, keepdims=True)
    y = oc * lax.rsqrt(jnp.mean(oc * oc, axis=-1, keepdims=True) + EPS) * g_ref[...]
    gate = gate_ref[...]
    o_ref[...] = (y * (gate * jax.nn.sigmoid(gate))).astype(o_ref.dtype)


def _retention(q, k, v, gate, g_ret3, s0, lb):
    b, t, _ = q.shape
    has_init = s0 is not None
    kern = functools.partial(_ret_kernel, lb=lb, has_init=has_init)
    head = lambda: pl.BlockSpec((None, lb, HEAD_W), lambda b_, h, c: (b_, c, h))
    state = lambda: pl.BlockSpec((None, None, DK_R, DV_R), lambda b_, h, c: (b_, h, 0, 0))
    in_specs = [head(), head(), head(), head(), pl.BlockSpec((None, 1, HEAD_W), lambda b_, h, c: (h, 0, 0))]
    args = [q, k, v, gate, g_ret3]
    if has_init:
        in_specs.append(state())
        args.append(s0)
    return pl.pallas_call(
        kern,
        grid=(b, H_R, t // lb),
        in_specs=in_specs,
        out_specs=[head(), state()],
        out_shape=[jax.ShapeDtypeStruct((b, t, GROUP_W), BF16), jax.ShapeDtypeStruct((b, H_R, DK_R, DV_R), F32)],
        scratch_shapes=[pltpu.VMEM((DK_R, DV_R), F32)],
        compiler_params=_cparams(("parallel", "parallel", "arbitrary")),
        name="retention",
    )(*args)


def _out_proj_kernel(x_ref, mda_ref, mret_ref, wo_ref, g_ref, wq_ref, x1_ref, qx_ref):
    x1 = x_ref[...] + _dot(mda_ref[...], wo_ref[:GROUP_W, :]) + _dot(mret_ref[...], wo_ref[GROUP_W:, :])
    x1_ref[...] = x1
    hn = _rms(x1, g_ref[...]).astype(BF16)
    qx_ref[...] = (_dot(hn, wq_ref[...]) * (HD_X ** -0.5)).astype(BF16)


def _out_proj(x2d, mda, mret, wo_bf, g_cross, wq_bf, tm, first_row=0):
    n = mda.shape[0]
    off = first_row // tm
    full = lambda a: pl.BlockSpec(a.shape, lambda i: (0, 0))
    return pl.pallas_call(
        _out_proj_kernel,
        grid=(n // tm,),
        in_specs=[pl.BlockSpec((tm, D_MODEL), lambda i: (i + off, 0)),
                  pl.BlockSpec((tm, GROUP_W), lambda i: (i, 0)),
                  pl.BlockSpec((tm, GROUP_W), lambda i: (i, 0)),
                  full(wo_bf), full(g_cross), full(wq_bf)],
        out_specs=[pl.BlockSpec((tm, D_MODEL), lambda i: (i, 0)), pl.BlockSpec((tm, D_MODEL), lambda i: (i, 0))],
        out_shape=[jax.ShapeDtypeStruct((n, D_MODEL), F32), jax.ShapeDtypeStruct((n, D_MODEL), BF16)],
        compiler_params=_cparams(("parallel",)),
        name="out_proj",
    )(x2d, mda, mret, wo_bf, g_cross, wq_bf)


def _mem_kv_kernel(m_ref, g_ref, wk_ref, wv_ref, mk_ref, mv_ref, mkb_ref, mvb_ref):
    mn = _rms(m_ref[...], g_ref[...]).astype(BF16)
    mk = _dot(mn, wk_ref[...])
    mv = _dot(mn, wv_ref[...])
    mk_ref[...] = mk
    mv_ref[...] = mv
    mkb_ref[...] = mk.astype(BF16)
    mvb_ref[...] = mv.astype(BF16)


def _mem_kv(mem2d, g_mem, wk_bf, wv_bf, tm):
    n = mem2d.shape[0]
    row = lambda: pl.BlockSpec((tm, D_MODEL), lambda i: (i, 0))
    full = lambda a: pl.BlockSpec(a.shape, lambda i: (0, 0))
    sh = lambda dt: jax.ShapeDtypeStruct((n, D_MODEL), dt)
    return pl.pallas_call(
        _mem_kv_kernel,
        grid=(n // tm,),
        in_specs=[row(), full(g_mem), full(wk_bf), full(wv_bf)],
        out_specs=[row(), row(), row(), row()],
        out_shape=[sh(F32), sh(F32), sh(BF16), sh(BF16)],
        compiler_params=_cparams(("parallel",)),
        name="mem_kv",
    )(mem2d, g_mem, wk_bf, wv_bf)


def _cross_kernel(x1_ref, qx_ref, mk_ref, mv_ref, wo_ref, g_ref, wpq_ref, x2_ref, h3_ref, qp_ref):
    q = qx_ref[...]
    heads = []
    for h in range(H_X):
        sl = slice(h * HD_X, (h + 1) * HD_X)
        kh = mk_ref[:, sl] if mk_ref.dtype == BF16 else mk_ref[:, sl].astype(BF16)
        vh = mv_ref[:, sl] if mv_ref.dtype == BF16 else mv_ref[:, sl].astype(BF16)
        s = _dot_nt(q[:, sl], kh)
        p = jnp.exp(s - jnp.max(s, axis=-1, keepdims=True))
        heads.append(_dot(p.astype(BF16), vh) / jnp.sum(p, axis=-1, keepdims=True))
    o = jnp.concatenate(heads, axis=-1).astype(BF16)
    x2 = x1_ref[...] + _dot(o, wo_ref[...])
    x2_ref[...] = x2
    h3 = _rms(x2, g_ref[...])
    h3_ref[...] = h3
    qp_ref[...] = _dot(h3.astype(BF16), wpq_ref[...]).astype(BF16)


def _cross(x1, qx, mk, mv, wo_bf, g_ffn, wpq_bf, tm):
    b, t, _ = x1.shape
    row = lambda w: pl.BlockSpec((None, tm, w), lambda b_, i: (b_, i, 0))
    mem = lambda: pl.BlockSpec((None, N_MEM, D_MODEL), lambda b_, i: (b_, 0, 0))
    full = lambda a: pl.BlockSpec(a.shape, lambda b_, i: (0, 0))
    dq = wpq_bf.shape[1]
    return pl.pallas_call(
        _cross_kernel,
        grid=(b, t // tm),
        in_specs=[row(D_MODEL), row(D_MODEL), mem(), mem(), full(wo_bf), full(g_ffn), full(wpq_bf)],
        out_specs=[row(D_MODEL), row(D_MODEL), row(dq)],
        out_shape=[jax.ShapeDtypeStruct((b, t, D_MODEL), F32), jax.ShapeDtypeStruct((b, t, D_MODEL), F32),
                   jax.ShapeDtypeStruct((b, t, dq), BF16)],
        compiler_params=_cparams(("parallel", "parallel")),
        name="cross_attn",
    )(x1, qx, mk, mv, wo_bf, g_ffn, wpq_bf)


ID_PAD = 2.0 ** 29


def _topk_rows(s, ids):
    vals, sel = [], []
    for _ in range(PEER_TOPK):
        m = jnp.max(s, axis=0, keepdims=True)
        idx = jnp.min(jnp.where(s == m, ids, ID_PAD), axis=0, keepdims=True)
        vals.append(m)
        sel.append(idx)
        s = jnp.where(ids == idx, -jnp.inf, s)
    return jnp.concatenate(vals, axis=0), jnp.concatenate(sel, axis=0)


def _candidates(v1, i1, v2, i2):
    lanes = v1.shape[1]
    b8 = lax.broadcasted_iota(jnp.int32, (8, lanes), 0)
    ident = lambda a, ia, ib, b: (a * PEER_TOPK + b) * float(N_EXPERTS) + (ia * float(N_KEYS) + ib)
    b16 = lax.broadcasted_iota(jnp.int32, (PEER_TOPK, lanes), 0).astype(F32)
    vals = [v1[0:1] + v2]
    ids = [ident(0, i1[0:1], i2, b16)]
    for a in range(1, 8):
        keep = b8 < PEER_TOPK // (a + 1)
        vals.append(jnp.where(keep, v1[a:a + 1] + v2[0:8], -jnp.inf))
        ids.append(jnp.where(keep, ident(a, i1[a:a + 1], i2[0:8], b8.astype(F32)), ID_PAD))
    a_hi = (b8 + 8).astype(F32)
    vals.append(v1[8:16] + v2[0:1])
    ids.append(ident(a_hi, i1[8:16], i2[0:1], 0.0))
    return jnp.concatenate(vals, axis=0), jnp.concatenate(ids, axis=0)


def _route_kernel(qp_ref, k1_ref, k2_ref, e_ref, g_ref):
    half = N_KEYS
    tt = qp_ref.shape[0]
    key_id = lax.broadcasted_iota(jnp.int32, (N_KEYS, tt), 0).astype(F32)
    es, gs = [], []
    for p in range(PEER_HEADS):
        q1 = qp_ref[:, (2 * p) * half:(2 * p + 1) * half]
        q2 = qp_ref[:, (2 * p + 1) * half:(2 * p + 2) * half]
        v1, i1 = _topk_rows(_dot_nt(k1_ref[p], q1), key_id)
        v2, i2 = _topk_rows(_dot_nt(k2_ref[p], q2), key_id)
        sc, sel = _topk_rows(*_candidates(v1, i1, v2, i2))
        w = jnp.exp(sc - sc[0:1])
        es.append((sel.astype(jnp.int32) & (N_EXPERTS - 1)) * ROWS_PER_EXPERT)
        gs.append(w / jnp.sum(w, axis=0, keepdims=True))
    e_ref[...] = jnp.concatenate(es, axis=0).T
    g_ref[...] = jnp.concatenate(gs, axis=0).T


def _route_extra_kernel(qp_ref, k1_ref, k2_ref, *rest):
    _route_kernel(qp_ref, k1_ref, k2_ref, rest[-2], rest[-1])


def _route(qp2d, k1_bf, k2_bf, tt, first_token=0, n_tokens=None, into=None, after=()):
    n = qp2d.shape[0]
    n_tokens = n if n_tokens is None else n_tokens
    off = first_token // tt
    full = lambda a: pl.BlockSpec(a.shape, lambda i: (0, 0, 0))
    pair = lambda: pl.BlockSpec((tt, N_PAIRS), lambda i: (i + off, 0))
    in_specs = [pl.BlockSpec((tt, qp2d.shape[1]), lambda i: (i + off, 0)), full(k1_bf), full(k2_bf)]
    args = [qp2d, k1_bf, k2_bf]
    aliases = {}
    if into is not None:
        in_specs += [pl.BlockSpec(memory_space=pl.ANY), pl.BlockSpec(memory_space=pl.ANY)]
        args += list(into)
        aliases = {3: 0, 4: 1}
    in_specs += [pl.BlockSpec(memory_space=pl.ANY)] * len(after)
    args += list(after)
    return pl.pallas_call(
        _route_kernel if len(args) == 3 else _route_extra_kernel,
        grid=(n_tokens // tt,),
        in_specs=in_specs,
        out_specs=[pair(), pair()],
        out_shape=[jax.ShapeDtypeStruct((n, N_PAIRS), jnp.int32), jax.ShapeDtypeStruct((n, N_PAIRS), F32)],
        input_output_aliases=aliases,
        compiler_params=_cparams(("parallel",)),
        name="peer_route",
    )(*args)


def _unpack(words):
    hi = lax.bitcast_convert_type(words & jnp.uint32(0xFFFF0000), F32)
    lo = lax.bitcast_convert_type(words << 16, F32)
    return hi, lo


def _expert_rows(tab_ref, row0):
    return tab_ref[pl.ds(pl.multiple_of(row0, ROWS_PER_EXPERT), ROWS_PER_EXPERT), :]


REDUCE_TOKENS = 16


def _lane_sums_to_rows(y, n_tok):
    hi = y.astype(BF16)
    lo = (y - hi.astype(F32)).astype(BF16)
    ones = jnp.ones((128, N_PAIRS), BF16)
    s = (_dot(hi, ones) + _dot(lo, ones)).reshape(n_tok, N_PAIRS, N_PAIRS)
    eye = lax.broadcasted_iota(jnp.int32, (N_PAIRS, N_PAIRS), 0) == lax.broadcasted_iota(jnp.int32, (N_PAIRS, N_PAIRS), 1)
    return jnp.sum(jnp.where(eye[None], s, 0.0), axis=1)


CHUNKS = D_MODEL // 128


def _peer_u_kernel(e_ref, h_ref, gate_ref, tab_ref, w_ref, h8_ref, prod_ref, ys_ref, act_ref, *, tt):
    for c in range(CHUNKS):
        h8_ref[pl.ds(c, tt, stride=CHUNKS), :] = h_ref[:, c * 128:(c + 1) * 128]

    def token(t, slot):
        base = pl.multiple_of(t * CHUNKS, CHUNKS)
        ha = h8_ref[pl.ds(base, ROWS_PER_EXPERT), :]
        hb = h8_ref[pl.ds(pl.multiple_of(base + ROWS_PER_EXPERT, ROWS_PER_EXPERT), ROWS_PER_EXPERT), :]
        prod = prod_ref.at[slot]
        for k in range(N_PAIRS):
            hi, lo = _unpack(_expert_rows(tab_ref, e_ref[t, k]))
            prod[k * ROWS_PER_EXPERT:(k + 1) * ROWS_PER_EXPERT, :] = hi * ha + lo * hb
        y = prod[pl.ds(0, N_PAIRS, stride=ROWS_PER_EXPERT), :]
        for c in range(1, ROWS_PER_EXPERT):
            y = y + prod[pl.ds(c, N_PAIRS, stride=ROWS_PER_EXPERT), :]
        ys_ref[pl.ds(pl.multiple_of(t * N_PAIRS, N_PAIRS), N_PAIRS), :] = y

    def two_tokens(j, carry):
        token(2 * j, 0)
        token(2 * j + 1, 1)
        return carry

    lax.fori_loop(0, tt // 2, two_tokens, 0)

    def group(g, carry):
        rows = REDUCE_TOKENS * N_PAIRS
        y = ys_ref[pl.ds(pl.multiple_of(g * rows, rows), rows), :]
        act_ref[pl.ds(pl.multiple_of(g * REDUCE_TOKENS, REDUCE_TOKENS), REDUCE_TOKENS), :] = _lane_sums_to_rows(y, REDUCE_TOKENS)
        return carry

    lax.fori_loop(0, tt // REDUCE_TOKENS, group, 0)
    w_ref[...] = _gated_gelu(gate_ref[...], act_ref[...])


def _gated_gelu(gate, a):
    return gate * (0.5 * a * (1.0 + lax.erf(a * (2.0 ** -0.5))))


def _gate_act_kernel(act_ref, gate_ref, *rest):
    rest[-1][...] = _gated_gelu(gate_ref[...], act_ref[...])


def _gate_act(act, gate_t, w_t, first_token, tt, after=()):
    off = first_token // tt
    return pl.pallas_call(
        _gate_act_kernel,
        grid=(act.shape[0] // tt,),
        in_specs=[pl.BlockSpec((tt, N_PAIRS), lambda i: (i, 0)),
                  pl.BlockSpec((tt, N_PAIRS), lambda i: (i + off, 0))]
                 + [pl.BlockSpec(memory_space=pl.ANY)] * (1 + len(after)),
        out_specs=pl.BlockSpec((tt, N_PAIRS), lambda i: (i + off, 0)),
        out_shape=jax.ShapeDtypeStruct(w_t.shape, F32),
        input_output_aliases={2: 0},
        compiler_params=_cparams(("arbitrary",)),
        name="peer_gate_act",
    )(act, gate_t, w_t, *after)


def _peer_u(e_t, h, gate_t, tab, tt, n_tokens):
    n = e_t.shape[0]
    assert tt % REDUCE_TOKENS == 0
    kern = functools.partial(_peer_u_kernel, tt=tt)
    pair = lambda **kw: pl.BlockSpec((tt, N_PAIRS), lambda i: (i, 0), **kw)
    return pl.pallas_call(
        kern,
        grid=(n_tokens // tt,),
        in_specs=[pair(memory_space=pltpu.SMEM),
                  pl.BlockSpec((tt, D_MODEL), lambda i: (i, 0)),
                  pair(),
                  pl.BlockSpec(tab.shape, lambda i: (0, 0), pipeline_mode=pl.Buffered(1))],
        out_specs=pair(),
        out_shape=jax.ShapeDtypeStruct((n, N_PAIRS), F32),
        scratch_shapes=[pltpu.VMEM((tt * CHUNKS, 128), F32),
                        pltpu.VMEM((2, N_PAIRS * ROWS_PER_EXPERT, 128), F32),
                        pltpu.VMEM((tt * N_PAIRS, 128), F32),
                        pltpu.VMEM((tt, N_PAIRS), F32)],
        compiler_params=_cparams(("arbitrary",)),
        name="peer_u",
    )(e_t, h, gate_t, tab)


def _peer_v_kernel(e_ref, w_ref, x_ref, g_ref, tab_ref, o_ref, acc_ref, *, tt):
    n_acc = 2

    def token(t):
        acc_hi = [jnp.zeros((ROWS_PER_EXPERT, 128), F32) for _ in range(n_acc)]
        acc_lo = [jnp.zeros((ROWS_PER_EXPERT, 128), F32) for _ in range(n_acc)]
        for k in range(N_PAIRS):
            hi, lo = _unpack(_expert_rows(tab_ref, e_ref[t, k]))
            w = w_ref[t, k]
            acc_hi[k % n_acc] = acc_hi[k % n_acc] + w * hi
            acc_lo[k % n_acc] = acc_lo[k % n_acc] + w * lo
        base = pl.multiple_of(t * CHUNKS, CHUNKS)
        acc_ref[pl.ds(base, ROWS_PER_EXPERT), :] = sum(acc_hi[1:], acc_hi[0])
        acc_ref[pl.ds(pl.multiple_of(base + ROWS_PER_EXPERT, ROWS_PER_EXPERT), ROWS_PER_EXPERT), :] = sum(acc_lo[1:], acc_lo[0])

    def two_tokens(j, carry):
        token(2 * j)
        token(2 * j + 1)
        return carry

    lax.fori_loop(0, tt // 2, two_tokens, 0)
    _residual_rms_store(x_ref, acc_ref, g_ref, o_ref, tt)


def _residual_rms_store(x_ref, add_ref, g_ref, o_ref, tt):
    xs = [x_ref[:, c * 128:(c + 1) * 128] + add_ref[pl.ds(c, tt, stride=CHUNKS), :] for c in range(CHUNKS)]
    sq = xs[0] * xs[0]
    for x in xs[1:]:
        sq = sq + x * x
    r = lax.rsqrt(jnp.sum(sq, axis=1, keepdims=True) * (1.0 / D_MODEL) + EPS)
    for c in range(CHUNKS):
        o_ref[:, c * 128:(c + 1) * 128] = xs[c] * r * g_ref[:, c * 128:(c + 1) * 128]


def _peer_v_into_kernel(e_ref, w_ref, x_ref, g_ref, tab_ref, y_any_ref, o_ref, acc_ref, *, tt):
    del y_any_ref
    _peer_v_kernel(e_ref, w_ref, x_ref, g_ref, tab_ref, o_ref, acc_ref, tt=tt)


def _peer_v(e_t, w_t, x, g_final, tab, tt, n_tokens, into=None):
    pair = lambda: pl.BlockSpec((tt, N_PAIRS), lambda i: (i, 0), memory_space=pltpu.SMEM)
    row = lambda: pl.BlockSpec((tt, D_MODEL), lambda i: (i, 0))
    in_specs = [pair(), pair(), row(),
                pl.BlockSpec((1, D_MODEL), lambda i: (0, 0)),
                pl.BlockSpec(tab.shape, lambda i: (0, 0), pipeline_mode=pl.Buffered(1))]
    args = [e_t, w_t, x, g_final, tab]
    if into is not None:
        in_specs.append(pl.BlockSpec(memory_space=pl.ANY))
        args.append(into)
    return pl.pallas_call(
        functools.partial(_peer_v_kernel if into is None else _peer_v_into_kernel, tt=tt),
        grid=(n_tokens // tt,),
        in_specs=in_specs,
        out_specs=row(),
        out_shape=jax.ShapeDtypeStruct(x.shape if into is None else into.shape, F32),
        input_output_aliases={} if into is None else {5: 0},
        scratch_shapes=[pltpu.VMEM((tt * CHUNKS, 128), F32)],
        compiler_params=_cparams(("arbitrary",)),
        name="peer_v",
    )(*args)


def _residual_norm_kernel(x_ref, add_ref, g_ref, *rest, tt):
    _residual_rms_store(x_ref, add_ref, g_ref, rest[-1], tt)


def _residual_norm(x, add8, g_final, y, x_first, y_first, tt, out_rows=None, after=()):
    n_rows = add8.shape[0] // CHUNKS
    x_off, y_off = x_first // tt, y_first // tt
    extra = ([] if y is None else [y]) + list(after)
    return pl.pallas_call(
        functools.partial(_residual_norm_kernel, tt=tt),
        grid=(n_rows // tt,),
        in_specs=[pl.BlockSpec((tt, D_MODEL), lambda i: (i + x_off, 0)),
                  pl.BlockSpec((tt * CHUNKS, 128), lambda i: (i, 0)),
                  pl.BlockSpec((1, D_MODEL), lambda i: (0, 0))]
                 + [pl.BlockSpec(memory_space=pl.ANY)] * len(extra),
        out_specs=pl.BlockSpec((tt, D_MODEL), lambda i: (i + y_off, 0)),
        out_shape=jax.ShapeDtypeStruct((out_rows, D_MODEL) if y is None else y.shape, F32),
        input_output_aliases={} if y is None else {3: 0},
        compiler_params=_cparams(("arbitrary",)),
        name="peer_residual_norm",
    )(x, add8, g_final, *extra)


SC_CORES = 2
SC_SUBCORES = 16
SC_LANES = 16
WORD_ROWS = 128 // SC_LANES
SC_TOKEN_CHUNK = 32
SC_RING = 4


def _peer_sc(e_t, aux, tab, down):
    n = e_t.shape[0]
    workers = SC_CORES * SC_SUBCORES
    assert n % (workers * SC_TOKEN_CHUNK) == 0 and PEER_HEADS % SC_RING == 0
    per = n // workers
    head_rows = ROWS_PER_EXPERT * PEER_TOPK
    aux_rows = CHUNKS if down else 1
    out_rows = 1 if down else CHUNKS
    mesh = plsc.VectorSubcoreMesh(core_axis_name="core", subcore_axis_name="subcore",
                                  num_cores=SC_CORES, num_subcores=SC_SUBCORES)

    @pl.kernel(out_type=jax.ShapeDtypeStruct((n * out_rows, 128), F32), mesh=mesh,
               scratch_types=[pltpu.VMEM((SC_TOKEN_CHUNK, N_PAIRS), jnp.int32),
                              pltpu.VMEM((SC_TOKEN_CHUNK * aux_rows, 128), F32),
                              pltpu.VMEM((SC_RING, head_rows, 128), jnp.uint32),
                              pltpu.VMEM((SC_TOKEN_CHUNK * out_rows, 128), F32),
                              pltpu.VMEM((PEER_TOPK, SC_LANES), F32),
                              pltpu.SemaphoreType.DMA((SC_RING,))],
               compiler_params=pltpu.CompilerParams(needs_layout_passes=False),
               name="peer_u_sc" if down else "peer_v_sc")
    def body(e_hbm, aux_hbm, tab_hbm, o_hbm, e_v, aux_v, rows_v, out_v, fold_v, sems):
        wid = lax.axis_index("core") * SC_SUBCORES + lax.axis_index("subcore")
        lane = lax.broadcasted_iota(jnp.int32, (SC_LANES,), 0)

        def gathers(i, p, slot):
            first = e_v[i, pl.ds(p * PEER_TOPK, PEER_TOPK)]
            return [pltpu.make_async_copy(tab_hbm.at[first + r],
                                          rows_v.at[slot, pl.ds(r * PEER_TOPK, PEER_TOPK)], sems.at[slot])
                    for r in range(ROWS_PER_EXPERT)]

        def start_gather(i, p, slot):
            for d in gathers(i, p, slot):
                d.start()

        def weighted_sum(i, p, slot):
            ws = [plsc.load_gather(aux_v, [lane * 0 + i, lane * 0 + (p * PEER_TOPK + k)])
                  for k in range(PEER_TOPK)]
            for r in range(ROWS_PER_EXPERT):
                @pl.loop(0, WORD_ROWS)
                def _(j):
                    sl = pl.ds(j * SC_LANES, SC_LANES)
                    if p == 0:
                        a_hi = jnp.zeros((SC_LANES,), F32)
                        a_lo = jnp.zeros((SC_LANES,), F32)
                    else:
                        a_hi = out_v[i * CHUNKS + r, sl]
                        a_lo = out_v[i * CHUNKS + ROWS_PER_EXPERT + r, sl]
                    for k in range(PEER_TOPK):
                        hi, lo = _unpack(rows_v[slot, r * PEER_TOPK + k, sl])
                        a_hi = a_hi + ws[k] * hi
                        a_lo = a_lo + ws[k] * lo
                    out_v[i * CHUNKS + r, sl] = a_hi
                    out_v[i * CHUNKS + ROWS_PER_EXPERT + r, sl] = a_lo

        def pair_dots(i, p, slot):
            accs = tuple(jnp.zeros((SC_LANES,), F32) for _ in range(PEER_TOPK))
            for r in range(ROWS_PER_EXPERT):
                def piece(j, accs, r=r):
                    sl = pl.ds(j * SC_LANES, SC_LANES)
                    ha = aux_v[i * CHUNKS + r, sl]
                    hb = aux_v[i * CHUNKS + ROWS_PER_EXPERT + r, sl]
                    out = []
                    for k in range(PEER_TOPK):
                        hi, lo = _unpack(rows_v[slot, r * PEER_TOPK + k, sl])
                        out.append(accs[k] + (hi * ha + lo * hb))
                    return tuple(out)
                accs = lax.fori_loop(0, WORD_ROWS, piece, accs)
            for k in range(PEER_TOPK):
                fold_v[k, :] = accs[k]
            tot = plsc.load_gather(fold_v, [lane, lane * 0])
            for l in range(1, SC_LANES):
                tot = tot + plsc.load_gather(fold_v, [lane, lane * 0 + l])
            out_v[i, pl.ds(p * PEER_TOPK, PEER_TOPK)] = tot

        consume = pair_dots if down else weighted_sum

        @pl.loop(0, per // SC_TOKEN_CHUNK)
        def _(c):
            t0 = wid * per + c * SC_TOKEN_CHUNK
            pltpu.sync_copy(e_hbm.at[pl.ds(t0, SC_TOKEN_CHUNK)], e_v)
            pltpu.sync_copy(aux_hbm.at[pl.ds(t0 * aux_rows, SC_TOKEN_CHUNK * aux_rows)], aux_v)
            for p in range(SC_RING):
                start_gather(0, p, p)

            @pl.loop(0, SC_TOKEN_CHUNK)
            def _(i):
                for p in range(PEER_HEADS):
                    slot = p % SC_RING
                    for d in gathers(i, p, slot):
                        d.wait()
                    consume(i, p, slot)
                    if p + SC_RING < PEER_HEADS:
                        start_gather(i, p + SC_RING, slot)
                    else:
                        @pl.when(i + 1 < SC_TOKEN_CHUNK)
                        def _():
                            start_gather(i + 1, p + SC_RING - PEER_HEADS, slot)

            pltpu.sync_copy(out_v, o_hbm.at[pl.ds(t0 * out_rows, SC_TOKEN_CHUNK * out_rows)])

    return body(e_t, aux, tab)


def _pack_table(tab):
    bits = lax.bitcast_convert_type(tab.astype(BF16), jnp.uint16).astype(jnp.uint32)
    words = (bits[:, :HALF] << 16) | bits[:, HALF:]
    return words.reshape(tab.shape[0] * ROWS_PER_EXPERT, 128)


PEER_TOKENS = 128
SC_LEAD_BATCHES = (1, 4)
SC_SHARE_DOWN = (1, 3)
SC_SHARE_UP = (7, 16)
SC_MIN_TOKENS = 4096


def _sc_tokens(n):
    if n < SC_MIN_TOKENS:
        return (0, 0)
    unit = math.lcm(PEER_TOKENS, SC_CORES * SC_SUBCORES * SC_TOKEN_CHUNK)
    return tuple(n * num // den // unit * unit for num, den in (SC_SHARE_DOWN, SC_SHARE_UP))


def _row_tile(n, pref):
    while n % pref:
        pref //= 2
    return pref


def _peer_on_sc_down(h3, qp, k1_bf, k2_bf, tab_u, tt):
    n = h3.shape[0]
    e_t, gate_t = _route(qp, k1_bf, k2_bf, tt)
    return e_t, gate_t, _peer_sc(e_t, h3.reshape(n * CHUNKS, 128), tab_u, down=True)


def _peer_on_sc_up(e_t, gate_t, act, tab_v, tt, gate_after):
    w_t = _gate_act(act, gate_t, act, 0, tt, after=gate_after)
    return _peer_sc(e_t, w_t, tab_v, down=False), w_t


def _peer_and_final(x2, h3, qp, k1_bf, k2_bf, tab_u, tab_v, g_final, tt, n_sc=(0, 0), route_after=(), out_rows=None,
                    lead=None):
    n = x2.shape[0]
    n_sc_down, n_sc_up = n_sc
    tc_down, tc_up = n - n_sc_down, n - n_sc_up
    if n_sc_down:
        e_t, gate_t = _route(qp, k1_bf, k2_bf, tt, tc_down, n_sc_down, after=route_after)
        act = _peer_sc(e_t[tc_down:], h3[tc_down:].reshape(n_sc_down * CHUNKS, 128), tab_u, down=True)
        e_t, gate_t = _route(qp, k1_bf, k2_bf, tt, 0, tc_down, into=(e_t, gate_t))
    else:
        e_t, gate_t = _route(qp, k1_bf, k2_bf, tt)
    w_t = _peer_u(e_t, h3, gate_t, tab_u, tt, tc_down)
    y = None
    if lead is not None:
        x2_lead, add8_lead = lead
        y = _residual_norm(x2_lead, add8_lead, g_final, None, 0, n, tt, out_rows=out_rows, after=(w_t,))
    if n_sc_down:
        w_t = _gate_act(act, gate_t, w_t, tc_down, tt, after=() if y is None else (y,))
    y = _peer_v(e_t, w_t, x2, g_final, tab_v, tt, tc_up, into=y)
    if n_sc_up:
        add8 = _peer_sc(e_t[tc_up:], w_t[tc_up:], tab_v, down=False)
        y = _residual_norm(x2, add8, g_final, y, tc_up, tc_up, tt)
    return y


def kernel(x_prompt, x_sample, mem_prompt, cache_da_k, cache_da_v, state_ret, cache_mem_k, cache_mem_v, g_mix, w_in, lam_q1, lam_k1, lam_q2, lam_k2, g_da, g_ret, w_out, g_cross, g_mem, w_xq, w_xk, w_xv, w_xo, g_ffn, w_pq, peer_k1, peer_k2, peer_u, peer_v, g_final):
    depth = w_in.shape[0]
    assert depth == 1, "single-layer step"
    l = 0
    lam_init = 0.8 - 0.6 * math.exp(-0.3 * l)
    b, t, _ = x_prompt.shape
    bs, ts, _ = x_sample.shape
    past_len = cache_da_k.shape[2]

    row = lambda a: a.reshape(1, -1)
    w_in_bf = w_in[l].astype(BF16)
    w_out_bf = w_out[l].astype(BF16)
    w_xq_bf, w_xk_bf, w_xv_bf, w_xo_bf = (w[l].astype(BF16) for w in (w_xq, w_xk, w_xv, w_xo))
    w_pq_bf = w_pq[l].astype(BF16)
    k1_bf, k2_bf = peer_k1[l].astype(BF16), peer_k2[l].astype(BF16)
    tab_u, tab_v = _pack_table(peer_u[l]), _pack_table(peer_v[l])
    lamp = jnp.stack([lam_q1[l], lam_k1[l], lam_q2[l], lam_k2[l]])
    g_da3 = g_da[l].reshape(H_D, 1, DV_D)
    g_ret3 = g_ret[l].reshape(H_R, 1, DV_R)
    g_fin = row(g_final)

    def mixer_mid(x2d, first_row, mda, mret, mk, mv, bb, tt_rows):
        n = bb * tt_rows
        tm = _row_tile(n, 512)
        x1, qx = _out_proj(x2d, mda.reshape(n, GROUP_W), mret.reshape(n, GROUP_W), w_out_bf, row(g_cross[l]), w_xq_bf,
                           tm, first_row)
        x2, h3, qp = _cross(x1.reshape(bb, tt_rows, D_MODEL), qx.reshape(bb, tt_rows, D_MODEL), mk, mv,
                            w_xo_bf, row(g_ffn[l]), w_pq_bf, _row_tile(tt_rows, 512))
        return x2.reshape(n, D_MODEL), h3.reshape(n, D_MODEL), qp.reshape(n, -1)

    peer_args = (k1_bf, k2_bf, tab_u, tab_v, g_fin, PEER_TOKENS)
    n = b * t
    xp = x_prompt.reshape(n, D_MODEL)

    def prompt_dense(b0, bg, kv_into=None):
        rows = bg * t
        qd, kd, vd, kdb, vdb, qr, kr, vr, gr = _in_proj(xp, row(g_mix[l]), w_in_bf, _row_tile(rows, 512),
                                                       b0 * t, rows, kv_into)
        r3 = lambda a: a.reshape(bg, t, GROUP_W)
        mda = _diff_attn_prompt(lamp, r3(qd), r3(kdb), r3(vdb), g_da3, lam_init, 512, 512)
        mret, s_fin = _retention(r3(qr), r3(kr), r3(vr), r3(gr), g_ret3, None, 512)
        return mixer_mid(xp, b0 * t, mda, mret, mkb[b0:b0 + bg], mvb[b0:b0 + bg], bg, t), (kd, vd), s_fin

    mk, mv, mkb, mvb = _mem_kv(mem_prompt.reshape(b * N_MEM, D_MODEL), row(g_mem[l]), w_xk_bf, w_xv_bf, 512)
    mkb, mvb = mkb.reshape(b, N_MEM, D_MODEL), mvb.reshape(b, N_MEM, D_MODEL)
    b_rest = b - b * SC_LEAD_BATCHES[0] // SC_LEAD_BATCHES[1]
    if b_rest < b:
        (x2_l, h3_l, qp_l), kv, s_l = prompt_dense(b_rest, b - b_rest)
        e_l, gate_l, act_l = _peer_on_sc_down(h3_l, qp_l, k1_bf, k2_bf, tab_u, PEER_TOKENS)
        peer_rest, (kd, vd), s_r = prompt_dense(0, b_rest, kv_into=kv)
        add8_l, w_lead = _peer_on_sc_up(e_l, gate_l, act_l, tab_v, PEER_TOKENS, gate_after=(peer_rest[2],))
        y_prompt = _peer_and_final(*peer_rest, *peer_args, _sc_tokens(b_rest * t), route_after=(w_lead,), out_rows=n,
                                   lead=(x2_l, add8_l)).reshape(b, t, D_MODEL)
        s_fin = jnp.concatenate([s_r, s_l])
    else:
        peer_all, (kd, vd), s_fin = prompt_dense(0, b)
        y_prompt = _peer_and_final(*peer_all, *peer_args, _sc_tokens(n)).reshape(b, t, D_MODEL)

    ns = bs * ts
    xs = x_sample.reshape(ns, D_MODEL)
    qd_s, kd_s, vd_s, kdb_s, vdb_s, qr_s, kr_s, vr_s, gr_s = _in_proj(xs, row(g_mix[l]), w_in_bf, _row_tile(ns, 512))
    s3 = lambda a: a.reshape(bs, ts, GROUP_W)
    mda_s = _diff_attn_sample(lamp, s3(qd_s), cache_da_k[l].reshape(bs, past_len, GROUP_W),
                              cache_da_v[l].reshape(bs, past_len, GROUP_W), s3(kdb_s), s3(vdb_s), g_da3, lam_init)
    mret_s, s_new = _retention(s3(qr_s), s3(kr_s), s3(vr_s), s3(gr_s), g_ret3, state_ret[l], ts)
    peer_s = mixer_mid(xs, 0, mda_s, mret_s, cache_mem_k[l].reshape(bs, N_MEM, D_MODEL),
                       cache_mem_v[l].reshape(bs, N_MEM, D_MODEL), bs, ts)
    y_sample = _peer_and_final(*peer_s, *peer_args).reshape(bs, ts, D_MODEL)

    return (y_prompt, y_sample,
            kd.reshape(1, b, t, H_D, 2, DK_D), vd.reshape(1, b, t, H_D, DV_D), s_fin[None],
            mk.reshape(1, b, N_MEM, H_X, HD_X), mv.reshape(1, b, N_MEM, H_X, HD_X),
            kd_s.reshape(1, bs, ts, H_D, 2, DK_D), vd_s.reshape(1, bs, ts, H_D, DV_D), s_new[None])
```

```python
import functools
import math

import jax
import jax.numpy as jnp
from jax import lax
from jax.experimental import pallas as pl
from jax.experimental.pallas import tpu as pltpu
from jax.experimental.pallas import tpu_sc as plsc

D_MODEL = 1024
CHUNK = 64
CHUNK_SHIFT = CHUNK.bit_length() - 1
assert 1 << CHUNK_SHIFT == CHUNK
H_D, DK_D, DV_D = 4, 64, 128
H_R, DK_R, DV_R = 4, 128, 128
N_MEM = 256
H_X = 4
HD_X = D_MODEL // H_X
PEER_HEADS = 8
N_KEYS = 128
N_EXPERTS = N_KEYS * N_KEYS
PEER_TOPK = 16
EPS = 1e-6
HEAD_W = 128
GROUP_W = 512
N_PAIRS = PEER_HEADS * PEER_TOPK
HALF = D_MODEL // 2
ROWS_PER_EXPERT = HALF // 128
VMEM_LIMIT = 56 * 1024 * 1024

BF16 = jnp.bfloat16
F32 = jnp.float32


def _cparams(sem):
    return pltpu.CompilerParams(dimension_semantics=sem, vmem_limit_bytes=VMEM_LIMIT)


def _rms(x, g):
    return x * lax.rsqrt(jnp.mean(x * x, axis=-1, keepdims=True) + EPS) * g


def _dot(a, b):
    return jnp.dot(a, b, preferred_element_type=F32)


def _dot_nt(a, b):
    return lax.dot_general(a, b, (((1,), (1,)), ((), ())), preferred_element_type=F32)


def _dot_tn(a, b):
    return lax.dot_general(a, b, (((0,), (0,)), ((), ())), preferred_element_type=F32)


def _select_by_head(h, values):
    out = jnp.float32(values[-1])
    for i in range(len(values) - 2, -1, -1):
        out = jnp.where(h == i, jnp.float32(values[i]), out)
    return out


def _in_proj_kernel(x_ref, g_ref, w_ref, *rest):
    qd_ref, kd_ref, vd_ref, kdb_ref, vdb_ref, qr_ref, kr_ref, vr_ref, gr_ref = rest[-9:]
    hb = _rms(x_ref[...], g_ref[...]).astype(BF16)
    col = lambda c: _dot(hb, w_ref[:, c * GROUP_W:(c + 1) * GROUP_W])
    qd_ref[...] = (col(0) * (DK_D ** -0.5)).astype(BF16)
    kd = col(1)
    kd_ref[...] = kd
    kdb_ref[...] = kd.astype(BF16)
    vd = col(2)
    vd_ref[...] = vd
    vdb_ref[...] = vd.astype(BF16)
    qr_ref[...] = col(3).astype(BF16)
    kr_ref[...] = (col(4) * (DK_R ** -0.5)).astype(BF16)
    vr_ref[...] = col(5).astype(BF16)
    gr_ref[...] = col(6)


def _in_proj(x2d, g, w_bf, tm, first_row=0, n_rows=None, kv_into=None):
    n_all = x2d.shape[0]
    n = n_all if n_rows is None else n_rows
    off = first_row // tm
    blk = lambda: pl.BlockSpec((tm, GROUP_W), lambda i: (i, 0))
    blk_all = lambda: pl.BlockSpec((tm, GROUP_W), lambda i: (i + off, 0))
    sh = lambda dt: jax.ShapeDtypeStruct((n, GROUP_W), dt)
    sh_all = jax.ShapeDtypeStruct((n_all, GROUP_W), F32)
    args = [x2d, g, w_bf]
    in_specs = [pl.BlockSpec((tm, D_MODEL), lambda i: (i + off, 0)),
                pl.BlockSpec((1, D_MODEL), lambda i: (0, 0)),
                pl.BlockSpec(w_bf.shape, lambda i: (0, 0))]
    aliases = {}
    if kv_into is not None:
        args += list(kv_into)
        in_specs += [pl.BlockSpec(memory_space=pl.ANY)] * 2
        aliases = {3: 1, 4: 2}
    return pl.pallas_call(
        _in_proj_kernel,
        grid=(n // tm,),
        in_specs=in_specs,
        out_specs=[blk(), blk_all(), blk_all()] + [blk() for _ in range(6)],
        out_shape=[sh(BF16), sh_all, sh_all, sh(BF16), sh(BF16), sh(BF16), sh(BF16), sh(BF16), sh(F32)],
        input_output_aliases=aliases,
        compiler_params=_cparams(("parallel",)),
        name="in_proj",
    )(*args)


def _lambda_from(lam_ref, lam_init):
    l = lam_ref[...]
    a = jnp.exp(jnp.sum(l[0:1] * l[1:2], axis=-1, keepdims=True))
    b = jnp.exp(jnp.sum(l[2:3] * l[3:4], axis=-1, keepdims=True))
    return a - b + lam_init


def _diff_post(acc, l, lam, g, lam_init, tq):
    o = acc[:tq] / l[:tq] - lam * (acc[tq:] / l[tq:])
    return o * lax.rsqrt(jnp.mean(o * o, axis=-1, keepdims=True) + EPS) * g * (1.0 - lam_init)


def _split_maps(q):
    lane = lax.broadcasted_iota(jnp.int32, q.shape, 1)
    zero = jnp.zeros_like(q)
    return jnp.concatenate([jnp.where(lane < DK_D, q, zero), jnp.where(lane >= DK_D, q, zero)], axis=0)


def _da_prompt_kernel(lam_ref, q_ref, k_ref, v_ref, g_ref, o_ref, kx_ref, vx_ref, own_ref, acc_ref, m_ref, *, lam_init, tq, tk):
    h = pl.program_id(1)
    i = pl.program_id(2)
    t = k_ref.shape[0]
    slope = _select_by_head(h, [2.0 ** (-8.0 * (j + 1) / H_D) for j in range(H_D)])

    @pl.when(i == 0)
    def _():
        pos = lax.broadcasted_iota(jnp.int32, (t, HEAD_W), 0)
        lane = lax.broadcasted_iota(jnp.int32, (t, HEAD_W), 1)
        coarse = ((pos >> CHUNK_SHIFT) << CHUNK_SHIFT).astype(F32) * slope
        fine = (pos & (CHUNK - 1)).astype(F32) * slope
        kx_ref[:, :HEAD_W] = k_ref[...]
        kx_ref[:, HEAD_W:] = jnp.where(lane == 0, coarse, jnp.where(lane == 1, fine, 0.0)).astype(BF16)
        vx_ref[:DV_D, :] = v_ref[...].astype(F32).T.astype(BF16)
        vx_ref[DV_D:, :] = jnp.ones((vx_ref.shape[0] - DV_D, t), BF16)
        krel = lax.broadcasted_iota(jnp.int32, (tk, 2 * tq), 0)
        c = lax.broadcasted_iota(jnp.int32, (tk, 2 * tq), 1)
        for par in range(tk // tq):
            qrel = par * tq + jnp.where(c >= tq, c - tq, c)
            ahead = (2.0 * slope) * jnp.maximum(krel - qrel, 0).astype(F32)
            own_ref[par] = jnp.where((qrel >> CHUNK_SHIFT) >= (krel >> CHUNK_SHIFT), -ahead, -1e30)

    q = q_ref[...]
    lane = lax.broadcasted_iota(jnp.int32, q.shape, 1)
    zero = jnp.zeros_like(q)
    ones2 = jnp.where(lane < 2, 1.0, 0.0).astype(BF16)
    q2 = jnp.concatenate([jnp.concatenate([jnp.where(lane < DK_D, q, zero), ones2], axis=1),
                          jnp.concatenate([jnp.where(lane >= DK_D, q, zero), ones2], axis=1)], axis=0)
    jd = (i * tq) // tk

    def scores(j):
        return _dot_nt(kx_ref[pl.ds(pl.multiple_of(j * tk, tk), tk), :], q2)

    def values(j):
        return vx_ref[:, pl.ds(pl.multiple_of(j * tk, tk), tk)]

    s = scores(jd) + own_ref[(i * tq) % tk // tq]
    m0 = jnp.max(s, axis=0, keepdims=True)
    m_ref[...] = m0
    acc_ref[...] = _dot(values(jd), jnp.exp(s - m0).astype(BF16))

    def absorb(blocks):
        ss = [scores(j) for j in blocks]
        m_old = m_ref[...]
        m_new = m_old
        for s in ss:
            m_new = jnp.maximum(m_new, jnp.max(s, axis=0, keepdims=True))
        m_ref[...] = m_new
        acc = jnp.exp(m_old - m_new) * acc_ref[...]
        for j, s in zip(blocks, ss):
            acc = acc + _dot(values(j), jnp.exp(s - m_new).astype(BF16))
        acc_ref[...] = acc

    def past_pair(jj, carry):
        absorb([2 * jj, 2 * jj + 1])
        return carry

    lax.fori_loop(0, jd // 2, past_pair, 0)

    @pl.when(jd % 2 == 1)
    def _():
        absorb([jd - 1])

    acc = acc_ref[...]
    num, den = acc[:DV_D], acc[DV_D:DV_D + 1]
    lam = _lambda_from(lam_ref, lam_init)
    o = (num[:, :tq] / den[:, :tq] - lam * (num[:, tq:] / den[:, tq:])).T
    o = o * lax.rsqrt(jnp.mean(o * o, axis=-1, keepdims=True) + EPS) * g_ref[...] * (1.0 - lam_init)
    o_ref[...] = o.astype(o_ref.dtype)


ONES_ROWS = 16


def _diff_attn_prompt(lamp, q, k, v, g_da3, lam_init, tq, tk):
    b, t, _ = q.shape
    kern = functools.partial(_da_prompt_kernel, lam_init=lam_init, tq=tq, tk=tk)
    return pl.pallas_call(
        kern,
        grid=(b, H_D, t // tq),
        in_specs=[pl.BlockSpec((4, DK_D), lambda b_, h, i: (0, 0)),
                  pl.BlockSpec((None, tq, HEAD_W), lambda b_, h, i: (b_, i, h)),
                  pl.BlockSpec((None, t, HEAD_W), lambda b_, h, i: (b_, 0, h)),
                  pl.BlockSpec((None, t, HEAD_W), lambda b_, h, i: (b_, 0, h)),
                  pl.BlockSpec((None, 1, HEAD_W), lambda b_, h, i: (h, 0, 0))],
        out_specs=pl.BlockSpec((None, tq, HEAD_W), lambda b_, h, i: (b_, i, h)),
        out_shape=jax.ShapeDtypeStruct((b, t, GROUP_W), BF16),
        scratch_shapes=[pltpu.VMEM((t, 2 * HEAD_W), BF16), pltpu.VMEM((DV_D + ONES_ROWS, t), BF16),
                        pltpu.VMEM((tk // tq, tk, 2 * tq), F32),
                        pltpu.VMEM((DV_D + ONES_ROWS, 2 * tq), F32), pltpu.VMEM((1, 2 * tq), F32)],
        compiler_params=_cparams(("parallel", "parallel", "arbitrary")),
        name="diff_attn_prompt",
    )(lamp, q, k, v, g_da3)


def _da_sample_kernel(lam_ref, q_ref, kc_ref, vc_ref, kn_ref, vn_ref, g_ref, o_ref, *, lam_init, ts, past_len):
    h = pl.program_id(1)
    slope = _select_by_head(h, [2.0 ** (-8.0 * (j + 1) / H_D) for j in range(H_D)])
    q2 = _split_maps(q_ref[...])
    rows = 2 * ts

    def scores(k, base, n):
        r = lax.broadcasted_iota(jnp.int32, (rows, n), 0)
        qpos = past_len + jnp.where(r >= ts, r - ts, r)
        kpos = base + lax.broadcasted_iota(jnp.int32, (rows, n), 1)
        return _dot_nt(q2, k) - slope * jnp.abs(qpos - kpos).astype(F32)

    sc = scores(kc_ref[...].astype(BF16), 0, past_len)
    sn = scores(kn_ref[...], past_len, ts)
    m = jnp.maximum(jnp.max(sc, axis=-1, keepdims=True), jnp.max(sn, axis=-1, keepdims=True))
    pc = jnp.exp(sc - m)
    pn = jnp.exp(sn - m)
    l = jnp.sum(pc, axis=-1, keepdims=True) + jnp.sum(pn, axis=-1, keepdims=True)
    acc = _dot(pc.astype(BF16), vc_ref[...].astype(BF16)) + _dot(pn.astype(BF16), vn_ref[...])
    lam = _lambda_from(lam_ref, lam_init)
    o_ref[...] = _diff_post(acc, l, lam, g_ref[...], lam_init, ts).astype(o_ref.dtype)


def _diff_attn_sample(lamp, q, kc, vc, kn, vn, g_da3, lam_init):
    b, ts, _ = q.shape
    past_len = kc.shape[1]
    kern = functools.partial(_da_sample_kernel, lam_init=lam_init, ts=ts, past_len=past_len)
    head = lambda rows: pl.BlockSpec((None, rows, HEAD_W), lambda b_, h: (b_, 0, h))
    return pl.pallas_call(
        kern,
        grid=(b, H_D),
        in_specs=[pl.BlockSpec((4, DK_D), lambda b_, h: (0, 0)),
                  head(ts), head(past_len), head(past_len), head(ts), head(ts),
                  pl.BlockSpec((None, 1, HEAD_W), lambda b_, h: (h, 0, 0))],
        out_specs=head(ts),
        out_shape=jax.ShapeDtypeStruct((b, ts, GROUP_W), BF16),
        compiler_params=_cparams(("parallel", "parallel")),
        name="diff_attn_sample",
    )(lamp, q, kc, vc, kn, vn, g_da3)


def _ret_kernel(*refs, lb, has_init):
    if has_init:
        q_ref, k_ref, v_ref, gate_ref, g_ref, s0_ref, o_ref, sfin_ref, s_ref = refs
    else:
        q_ref, k_ref, v_ref, gate_ref, g_ref, o_ref, sfin_ref, s_ref = refs
    h = pl.program_id(1)
    c = pl.program_id(2)
    lg = _select_by_head(h, [math.log1p(-(2.0 ** (-5.0 - j))) for j in range(H_R)])

    @pl.when(c == 0)
    def _():
        s_ref[...] = s0_ref[...] if has_init else jnp.zeros_like(s_ref)

    q, k, v = q_ref[...], k_ref[...], v_ref[...]
    i = lax.broadcasted_iota(jnp.int32, (lb, lb), 0)
    j = lax.broadcasted_iota(jnp.int32, (lb, lb), 1)
    d = (i - j).astype(F32)
    decay = jnp.where(d >= 0, jnp.exp(jnp.maximum(d, 0.0) * lg), 0.0)
    inner = _dot_nt(q, k) * decay
    ic = lax.broadcasted_iota(jnp.int32, (lb, 1), 0).astype(F32)
    s_old = s_ref[...]
    o = _dot(inner.astype(BF16), v) + _dot(q, s_old.astype(BF16)) * jnp.exp((ic + 1.0) * lg)
    tail = jnp.exp((lb - 1.0 - ic) * lg)
    kt = (k.astype(F32) * tail).astype(BF16)
    s_new = jnp.exp(lb * lg) * s_old + _dot_tn(kt, v)
    s_ref[...] = s_new

    @pl.when(c == pl.num_programs(2) - 1)
    def _():
        sfin_ref[...] = s_new

    oc = o - jnp.mean(o, axis=-1, keepdims=True)
    y = oc * lax.rsqrt(jnp.mean(oc * oc, axis=-1, keepdims=True) + EPS) * g_ref[...]
    gate = gate_ref[...]
    o_ref[...] = (y * (gate * jax.nn.sigmoid(gate))).astype(o_ref.dtype)


def _retention(q, k, v, gate, g_ret3, s0, lb):
    b, t, _ = q.shape
    has_init = s0 is not None
    kern = functools.partial(_ret_kernel, lb=lb, has_init=has_init)
    head = lambda: pl.BlockSpec((None, lb, HEAD_W), lambda b_, h, c: (b_, c, h))
    state = lambda: pl.BlockSpec((None, None, DK_R, DV_R), lambda b_, h, c: (b_, h, 0, 0))
    in_specs = [head(), head(), head(), head(), pl.BlockSpec((None, 1, HEAD_W), lambda b_, h, c: (h, 0, 0))]
    args = [q, k, v, gate, g_ret3]
    if has_init:
        in_specs.append(state())
        args.append(s0)
    return pl.pallas_call(
        kern,
        grid=(b, H_R, t // lb),
        in_specs=in_specs,
        out_specs=[head(), state()],
        out_shape=[jax.ShapeDtypeStruct((b, t, GROUP_W), BF16), jax.ShapeDtypeStruct((b, H_R, DK_R, DV_R), F32)],
        scratch_shapes=[pltpu.VMEM((DK_R, DV_R), F32)],
        compiler_params=_cparams(("parallel", "parallel", "arbitrary")),
        name="retention",
    )(*args)


def _out_proj_kernel(x_ref, mda_ref, mret_ref, wo_ref, g_ref, wq_ref, x1_ref, qx_ref):
    x1 = x_ref[...] + _dot(mda_ref[...], wo_ref[:GROUP_W, :]) + _dot(mret_ref[...], wo_ref[GROUP_W:, :])
    x1_ref[...] = x1
    hn = _rms(x1, g_ref[...]).astype(BF16)
    qx_ref[...] = (_dot(hn, wq_ref[...]) * (HD_X ** -0.5)).astype(BF16)


def _out_proj(x2d, mda, mret, wo_bf, g_cross, wq_bf, tm, first_row=0):
    n = mda.shape[0]
    off = first_row // tm
    full = lambda a: pl.BlockSpec(a.shape, lambda i: (0, 0))
    return pl.pallas_call(
        _out_proj_kernel,
        grid=(n // tm,),
        in_specs=[pl.BlockSpec((tm, D_MODEL), lambda i: (i + off, 0)),
                  pl.BlockSpec((tm, GROUP_W), lambda i: (i, 0)),
                  pl.BlockSpec((tm, GROUP_W), lambda i: (i, 0)),
                  full(wo_bf), full(g_cross), full(wq_bf)],
        out_specs=[pl.BlockSpec((tm, D_MODEL), lambda i: (i, 0)), pl.BlockSpec((tm, D_MODEL), lambda i: (i, 0))],
        out_shape=[jax.ShapeDtypeStruct((n, D_MODEL), F32), jax.ShapeDtypeStruct((n, D_MODEL), BF16)],
        compiler_params=_cparams(("parallel",)),
        name="out_proj",
    )(x2d, mda, mret, wo_bf, g_cross, wq_bf)


def _mem_kv_kernel(m_ref, g_ref, wk_ref, wv_ref, mk_ref, mv_ref, mkb_ref, mvb_ref):
    mn = _rms(m_ref[...], g_ref[...]).astype(BF16)
    mk = _dot(mn, wk_ref[...])
    mv = _dot(mn, wv_ref[...])
    mk_ref[...] = mk
    mv_ref[...] = mv
    mkb_ref[...] = mk.astype(BF16)
    mvb_ref[...] = mv.astype(BF16)


def _mem_kv(mem2d, g_mem, wk_bf, wv_bf, tm):
    n = mem2d.shape[0]
    row = lambda: pl.BlockSpec((tm, D_MODEL), lambda i: (i, 0))
    full = lambda a: pl.BlockSpec(a.shape, lambda i: (0, 0))
    sh = lambda dt: jax.ShapeDtypeStruct((n, D_MODEL), dt)
    return pl.pallas_call(
        _mem_kv_kernel,
        grid=(n // tm,),
        in_specs=[row(), full(g_mem), full(wk_bf), full(wv_bf)],
        out_specs=[row(), row(), row(), row()],
        out_shape=[sh(F32), sh(F32), sh(BF16), sh(BF16)],
        compiler_params=_cparams(("parallel",)),
        name="mem_kv",
    )(mem2d, g_mem, wk_bf, wv_bf)


def _cross_kernel(x1_ref, qx_ref, mk_ref, mv_ref, wo_ref, g_ref, wpq_ref, x2_ref, h3_ref, qp_ref):
    q = qx_ref[...]
    heads = []
    for h in range(H_X):
        sl = slice(h * HD_X, (h + 1) * HD_X)
        kh = mk_ref[:, sl] if mk_ref.dtype == BF16 else mk_ref[:, sl].astype(BF16)
        vh = mv_ref[:, sl] if mv_ref.dtype == BF16 else mv_ref[:, sl].astype(BF16)
        s = _dot_nt(q[:, sl], kh)
        p = jnp.exp(s - jnp.max(s, axis=-1, keepdims=True))
        heads.append(_dot(p.astype(BF16), vh) / jnp.sum(p, axis=-1, keepdims=True))
    o = jnp.concatenate(heads, axis=-1).astype(BF16)
    x2 = x1_ref[...] + _dot(o, wo_ref[...])
    x2_ref[...] = x2
    h3 = _rms(x2, g_ref[...])
    h3_ref[...] = h3
    qp_ref[...] = _dot(h3.astype(BF16), wpq_ref[...]).astype(BF16)


def _cross(x1, qx, mk, mv, wo_bf, g_ffn, wpq_bf, tm):
    b, t, _ = x1.shape
    row = lambda w: pl.BlockSpec((None, tm, w), lambda b_, i: (b_, i, 0))
    mem = lambda: pl.BlockSpec((None, N_MEM, D_MODEL), lambda b_, i: (b_, 0, 0))
    full = lambda a: pl.BlockSpec(a.shape, lambda b_, i: (0, 0))
    dq = wpq_bf.shape[1]
    return pl.pallas_call(
        _cross_kernel,
        grid=(b, t // tm),
        in_specs=[row(D_MODEL), row(D_MODEL), mem(), mem(), full(wo_bf), full(g_ffn), full(wpq_bf)],
        out_specs=[row(D_MODEL), row(D_MODEL), row(dq)],
        out_shape=[jax.ShapeDtypeStruct((b, t, D_MODEL), F32), jax.ShapeDtypeStruct((b, t, D_MODEL), F32),
                   jax.ShapeDtypeStruct((b, t, dq), BF16)],
        compiler_params=_cparams(("parallel", "parallel")),
        name="cross_attn",
    )(x1, qx, mk, mv, wo_bf, g_ffn, wpq_bf)


ID_PAD = 2.0 ** 29


def _topk_rows(s, ids):
    vals, sel = [], []
    for _ in range(PEER_TOPK):
        m = jnp.max(s, axis=0, keepdims=True)
        idx = jnp.min(jnp.where(s == m, ids, ID_PAD), axis=0, keepdims=True)
        vals.append(m)
        sel.append(idx)
        s = jnp.where(ids == idx, -jnp.inf, s)
    return jnp.concatenate(vals, axis=0), jnp.concatenate(sel, axis=0)


def _candidates(v1, i1, v2, i2):
    lanes = v1.shape[1]
    b8 = lax.broadcasted_iota(jnp.int32, (8, lanes), 0)
    ident = lambda a, ia, ib, b: (a * PEER_TOPK + b) * float(N_EXPERTS) + (ia * float(N_KEYS) + ib)
    b16 = lax.broadcasted_iota(jnp.int32, (PEER_TOPK, lanes), 0).astype(F32)
    vals = [v1[0:1] + v2]
    ids = [ident(0, i1[0:1], i2, b16)]
    for a in range(1, 8):
        keep = b8 < PEER_TOPK // (a + 1)
        vals.append(jnp.where(keep, v1[a:a + 1] + v2[0:8], -jnp.inf))
        ids.append(jnp.where(keep, ident(a, i1[a:a + 1], i2[0:8], b8.astype(F32)), ID_PAD))
    a_hi = (b8 + 8).astype(F32)
    vals.append(v1[8:16] + v2[0:1])
    ids.append(ident(a_hi, i1[8:16], i2[0:1], 0.0))
    return jnp.concatenate(vals, axis=0), jnp.concatenate(ids, axis=0)


def _route_kernel(qp_ref, k1_ref, k2_ref, e_ref, g_ref):
    half = N_KEYS
    tt = qp_ref.shape[0]
    key_id = lax.broadcasted_iota(jnp.int32, (N_KEYS, tt), 0).astype(F32)
    es, gs = [], []
    for p in range(PEER_HEADS):
        q1 = qp_ref[:, (2 * p) * half:(2 * p + 1) * half]
        q2 = qp_ref[:, (2 * p + 1) * half:(2 * p + 2) * half]
        v1, i1 = _topk_rows(_dot_nt(k1_ref[p], q1), key_id)
        v2, i2 = _topk_rows(_dot_nt(k2_ref[p], q2), key_id)
        sc, sel = _topk_rows(*_candidates(v1, i1, v2, i2))
        w = jnp.exp(sc - sc[0:1])
        es.append((sel.astype(jnp.int32) & (N_EXPERTS - 1)) * ROWS_PER_EXPERT)
        gs.append(w / jnp.sum(w, axis=0, keepdims=True))
    e_ref[...] = jnp.concatenate(es, axis=0).T
    g_ref[...] = jnp.concatenate(gs, axis=0).T


def _route_extra_kernel(qp_ref, k1_ref, k2_ref, *rest):
    _route_kernel(qp_ref, k1_ref, k2_ref, rest[-2], rest[-1])


def _route(qp2d, k1_bf, k2_bf, tt, first_token=0, n_tokens=None, into=None, after=()):
    n = qp2d.shape[0]
    n_tokens = n if n_tokens is None else n_tokens
    off = first_token // tt
    full = lambda a: pl.BlockSpec(a.shape, lambda i: (0, 0, 0))
    pair = lambda: pl.BlockSpec((tt, N_PAIRS), lambda i: (i + off, 0))
    in_specs = [pl.BlockSpec((tt, qp2d.shape[1]), lambda i: (i + off, 0)), full(k1_bf), full(k2_bf)]
    args = [qp2d, k1_bf, k2_bf]
    aliases = {}
    if into is not None:
        in_specs += [pl.BlockSpec(memory_space=pl.ANY), pl.BlockSpec(memory_space=pl.ANY)]
        args += list(into)
        aliases = {3: 0, 4: 1}
    in_specs += [pl.BlockSpec(memory_space=pl.ANY)] * len(after)
    args += list(after)
    return pl.pallas_call(
        _route_kernel if len(args) == 3 else _route_extra_kernel,
        grid=(n_tokens // tt,),
        in_specs=in_specs,
        out_specs=[pair(), pair()],
        out_shape=[jax.ShapeDtypeStruct((n, N_PAIRS), jnp.int32), jax.ShapeDtypeStruct((n, N_PAIRS), F32)],
        input_output_aliases=aliases,
        compiler_params=_cparams(("parallel",)),
        name="peer_route",
    )(*args)


def _unpack(words):
    hi = lax.bitcast_convert_type(words & jnp.uint32(0xFFFF0000), F32)
    lo = lax.bitcast_convert_type(words << 16, F32)
    return hi, lo


def _expert_rows(tab_ref, row0):
    return tab_ref[pl.ds(pl.multiple_of(row0, ROWS_PER_EXPERT), ROWS_PER_EXPERT), :]


REDUCE_TOKENS = 16


def _lane_sums_to_rows(y, n_tok):
    hi = y.astype(BF16)
    lo = (y - hi.astype(F32)).astype(BF16)
    ones = jnp.ones((128, N_PAIRS), BF16)
    s = (_dot(hi, ones) + _dot(lo, ones)).reshape(n_tok, N_PAIRS, N_PAIRS)
    eye = lax.broadcasted_iota(jnp.int32, (N_PAIRS, N_PAIRS), 0) == lax.broadcasted_iota(jnp.int32, (N_PAIRS, N_PAIRS), 1)
    return jnp.sum(jnp.where(eye[None], s, 0.0), axis=1)


CHUNKS = D_MODEL // 128


def _peer_u_kernel(e_ref, h_ref, gate_ref, tab_ref, w_ref, h8_ref, prod_ref, ys_ref, act_ref, *, tt):
    for c in range(CHUNKS):
        h8_ref[pl.ds(c, tt, stride=CHUNKS), :] = h_ref[:, c * 128:(c + 1) * 128]

    def token(t, slot):
        base = pl.multiple_of(t * CHUNKS, CHUNKS)
        ha = h8_ref[pl.ds(base, ROWS_PER_EXPERT), :]
        hb = h8_ref[pl.ds(pl.multiple_of(base + ROWS_PER_EXPERT, ROWS_PER_EXPERT), ROWS_PER_EXPERT), :]
        prod = prod_ref.at[slot]
        for k in range(N_PAIRS):
            hi, lo = _unpack(_expert_rows(tab_ref, e_ref[t, k]))
            prod[k * ROWS_PER_EXPERT:(k + 1) * ROWS_PER_EXPERT, :] = hi * ha + lo * hb
        y = prod[pl.ds(0, N_PAIRS, stride=ROWS_PER_EXPERT), :]
        for c in range(1, ROWS_PER_EXPERT):
            y = y + prod[pl.ds(c, N_PAIRS, stride=ROWS_PER_EXPERT), :]
        ys_ref[pl.ds(pl.multiple_of(t * N_PAIRS, N_PAIRS), N_PAIRS), :] = y

    def two_tokens(j, carry):
        token(2 * j, 0)
        token(2 * j + 1, 1)
        return carry

    lax.fori_loop(0, tt // 2, two_tokens, 0)

    def group(g, carry):
        rows = REDUCE_TOKENS * N_PAIRS
        y = ys_ref[pl.ds(pl.multiple_of(g * rows, rows), rows), :]
        act_ref[pl.ds(pl.multiple_of(g * REDUCE_TOKENS, REDUCE_TOKENS), REDUCE_TOKENS), :] = _lane_sums_to_rows(y, REDUCE_TOKENS)
        return carry

    lax.fori_loop(0, tt // REDUCE_TOKENS, group, 0)
    w_ref[...] = _gated_gelu(gate_ref[...], act_ref[...])


def _gated_gelu(gate, a):
    return gate * (0.5 * a * (1.0 + lax.erf(a * (2.0 ** -0.5))))


def _gate_act_kernel(act_ref, gate_ref, *rest):
    rest[-1][...] = _gated_gelu(gate_ref[...], act_ref[...])


def _gate_act(act, gate_t, w_t, first_token, tt, after=()):
    off = first_token // tt
    return pl.pallas_call(
        _gate_act_kernel,
        grid=(act.shape[0] // tt,),
        in_specs=[pl.BlockSpec((tt, N_PAIRS), lambda i: (i, 0)),
                  pl.BlockSpec((tt, N_PAIRS), lambda i: (i + off, 0))]
                 + [pl.BlockSpec(memory_space=pl.ANY)] * (1 + len(after)),
        out_specs=pl.BlockSpec((tt, N_PAIRS), lambda i: (i + off, 0)),
        out_shape=jax.ShapeDtypeStruct(w_t.shape, F32),
        input_output_aliases={2: 0},
        compiler_params=_cparams(("arbitrary",)),
        name="peer_gate_act",
    )(act, gate_t, w_t, *after)


def _peer_u(e_t, h, gate_t, tab, tt, n_tokens):
    n = e_t.shape[0]
    assert tt % REDUCE_TOKENS == 0
    kern = functools.partial(_peer_u_kernel, tt=tt)
    pair = lambda **kw: pl.BlockSpec((tt, N_PAIRS), lambda i: (i, 0), **kw)
    return pl.pallas_call(
        kern,
        grid=(n_tokens // tt,),
        in_specs=[pair(memory_space=pltpu.SMEM),
                  pl.BlockSpec((tt, D_MODEL), lambda i: (i, 0)),
                  pair(),
                  pl.BlockSpec(tab.shape, lambda i: (0, 0), pipeline_mode=pl.Buffered(1))],
        out_specs=pair(),
        out_shape=jax.ShapeDtypeStruct((n, N_PAIRS), F32),
        scratch_shapes=[pltpu.VMEM((tt * CHUNKS, 128), F32),
                        pltpu.VMEM((2, N_PAIRS * ROWS_PER_EXPERT, 128), F32),
                        pltpu.VMEM((tt * N_PAIRS, 128), F32),
                        pltpu.VMEM((tt, N_PAIRS), F32)],
        compiler_params=_cparams(("arbitrary",)),
        name="peer_u",
    )(e_t, h, gate_t, tab)


def _peer_v_kernel(e_ref, w_ref, x_ref, g_ref, tab_ref, o_ref, acc_ref, *, tt):
    n_acc = 2

    def token(t):
        acc_hi = [jnp.zeros((ROWS_PER_EXPERT, 128), F32) for _ in range(n_acc)]
        acc_lo = [jnp.zeros((ROWS_PER_EXPERT, 128), F32) for _ in range(n_acc)]
        for k in range(N_PAIRS):
            hi, lo = _unpack(_expert_rows(tab_ref, e_ref[t, k]))
            w = w_ref[t, k]
            acc_hi[k % n_acc] = acc_hi[k % n_acc] + w * hi
            acc_lo[k % n_acc] = acc_lo[k % n_acc] + w * lo
        base = pl.multiple_of(t * CHUNKS, CHUNKS)
        acc_ref[pl.ds(base, ROWS_PER_EXPERT), :] = sum(acc_hi[1:], acc_hi[0])
        acc_ref[pl.ds(pl.multiple_of(base + ROWS_PER_EXPERT, ROWS_PER_EXPERT), ROWS_PER_EXPERT), :] = sum(acc_lo[1:], acc_lo[0])

    def two_tokens(j, carry):
        token(2 * j)
        token(2 * j + 1)
        return carry

    lax.fori_loop(0, tt // 2, two_tokens, 0)
    _residual_rms_store(x_ref, acc_ref, g_ref, o_ref, tt)


def _residual_rms_store(x_ref, add_ref, g_ref, o_ref, tt):
    xs = [x_ref[:, c * 128:(c + 1) * 128] + add_ref[pl.ds(c, tt, stride=CHUNKS), :] for c in range(CHUNKS)]
    sq = xs[0] * xs[0]
    for x in xs[1:]:
        sq = sq + x * x
    r = lax.rsqrt(jnp.sum(sq, axis=1, keepdims=True) * (1.0 / D_MODEL) + EPS)
    for c in range(CHUNKS):
        o_ref[:, c * 128:(c + 1) * 128] = xs[c] * r * g_ref[:, c * 128:(c + 1) * 128]


def _peer_v_into_kernel(e_ref, w_ref, x_ref, g_ref, tab_ref, y_any_ref, o_ref, acc_ref, *, tt):
    del y_any_ref
    _peer_v_kernel(e_ref, w_ref, x_ref, g_ref, tab_ref, o_ref, acc_ref, tt=tt)


def _peer_v(e_t, w_t, x, g_final, tab, tt, n_tokens, into=None):
    pair = lambda: pl.BlockSpec((tt, N_PAIRS), lambda i: (i, 0), memory_space=pltpu.SMEM)
    row = lambda: pl.BlockSpec((tt, D_MODEL), lambda i: (i, 0))
    in_specs = [pair(), pair(), row(),
                pl.BlockSpec((1, D_MODEL), lambda i: (0, 0)),
                pl.BlockSpec(tab.shape, lambda i: (0, 0), pipeline_mode=pl.Buffered(1))]
    args = [e_t, w_t, x, g_final, tab]
    if into is not None:
        in_specs.append(pl.BlockSpec(memory_space=pl.ANY))
        args.append(into)
    return pl.pallas_call(
        functools.partial(_peer_v_kernel if into is None else _peer_v_into_kernel, tt=tt),
        grid=(n_tokens // tt,),
        in_specs=in_specs,
        out_specs=row(),
        out_shape=jax.ShapeDtypeStruct(x.shape if into is None else into.shape, F32),
        input_output_aliases={} if into is None else {5: 0},
        scratch_shapes=[pltpu.VMEM((tt * CHUNKS, 128), F32)],
        compiler_params=_cparams(("arbitrary",)),
        name="peer_v",
    )(*args)


def _residual_norm_kernel(x_ref, add_ref, g_ref, *rest, tt):
    _residual_rms_store(x_ref, add_ref, g_ref, rest[-1], tt)


def _residual_norm(x, add8, g_final, y, x_first, y_first, tt, out_rows=None, after=()):
    n_rows = add8.shape[0] // CHUNKS
    x_off, y_off = x_first // tt, y_first // tt
    extra = ([] if y is None else [y]) + list(after)
    return pl.pallas_call(
        functools.partial(_residual_norm_kernel, tt=tt),
        grid=(n_rows // tt,),
        in_specs=[pl.BlockSpec((tt, D_MODEL), lambda i: (i + x_off, 0)),
                  pl.BlockSpec((tt * CHUNKS, 128), lambda i: (i, 0)),
                  pl.BlockSpec((1, D_MODEL), lambda i: (0, 0))]
                 + [pl.BlockSpec(memory_space=pl.ANY)] * len(extra),
        out_specs=pl.BlockSpec((tt, D_MODEL), lambda i: (i + y_off, 0)),
        out_shape=jax.ShapeDtypeStruct((out_rows, D_MODEL) if y is None else y.shape, F32),
        input_output_aliases={} if y is None else {3: 0},
        compiler_params=_cparams(("arbitrary",)),
        name="peer_residual_norm",
    )(x, add8, g_final, *extra)


SC_CORES = 2
SC_SUBCORES = 16
SC_LANES = 16
WORD_ROWS = 128 // SC_LANES
SC_TOKEN_CHUNK = 32
SC_RING = 4


def _peer_sc(e_t, aux, tab, down):
    n = e_t.shape[0]
    workers = SC_CORES * SC_SUBCORES
    assert n % (workers * SC_TOKEN_CHUNK) == 0 and PEER_HEADS % SC_RING == 0
    per = n // workers
    head_rows = ROWS_PER_EXPERT * PEER_TOPK
    aux_rows = CHUNKS if down else 1
    out_rows = 1 if down else CHUNKS
    mesh = plsc.VectorSubcoreMesh(core_axis_name="core", subcore_axis_name="subcore",
                                  num_cores=SC_CORES, num_subcores=SC_SUBCORES)

    @pl.kernel(out_type=jax.ShapeDtypeStruct((n * out_rows, 128), F32), mesh=mesh,
               scratch_types=[pltpu.VMEM((SC_TOKEN_CHUNK, N_PAIRS), jnp.int32),
                              pltpu.VMEM((SC_TOKEN_CHUNK * aux_rows, 128), F32),
                              pltpu.VMEM((SC_RING, head_rows, 128), jnp.uint32),
                              pltpu.VMEM((SC_TOKEN_CHUNK * out_rows, 128), F32),
                              pltpu.VMEM((PEER_TOPK, SC_LANES), F32),
                              pltpu.SemaphoreType.DMA((SC_RING,))],
               compiler_params=pltpu.CompilerParams(needs_layout_passes=False),
               name="peer_u_sc" if down else "peer_v_sc")
    def body(e_hbm, aux_hbm, tab_hbm, o_hbm, e_v, aux_v, rows_v, out_v, fold_v, sems):
        wid = lax.axis_index("core") * SC_SUBCORES + lax.axis_index("subcore")
        lane = lax.broadcasted_iota(jnp.int32, (SC_LANES,), 0)

        def gathers(i, p, slot):
            first = e_v[i, pl.ds(p * PEER_TOPK, PEER_TOPK)]
            return [pltpu.make_async_copy(tab_hbm.at[first + r],
                                          rows_v.at[slot, pl.ds(r * PEER_TOPK, PEER_TOPK)], sems.at[slot])
                    for r in range(ROWS_PER_EXPERT)]

        def start_gather(i, p, slot):
            for d in gathers(i, p, slot):
                d.start()

        def weighted_sum(i, p, slot):
            ws = [plsc.load_gather(aux_v, [lane * 0 + i, lane * 0 + (p * PEER_TOPK + k)])
                  for k in range(PEER_TOPK)]
            for r in range(ROWS_PER_EXPERT):
                @pl.loop(0, WORD_ROWS)
                def _(j):
                    sl = pl.ds(j * SC_LANES, SC_LANES)
                    if p == 0:
                        a_hi = jnp.zeros((SC_LANES,), F32)
                        a_lo = jnp.zeros((SC_LANES,), F32)
                    else:
                        a_hi = out_v[i * CHUNKS + r, sl]
                        a_lo = out_v[i * CHUNKS + ROWS_PER_EXPERT + r, sl]
                    for k in range(PEER_TOPK):
                        hi, lo = _unpack(rows_v[slot, r * PEER_TOPK + k, sl])
                        a_hi = a_hi + ws[k] * hi
                        a_lo = a_lo + ws[k] * lo
                    out_v[i * CHUNKS + r, sl] = a_hi
                    out_v[i * CHUNKS + ROWS_PER_EXPERT + r, sl] = a_lo

        def pair_dots(i, p, slot):
            accs = tuple(jnp.zeros((SC_LANES,), F32) for _ in range(PEER_TOPK))
            for r in range(ROWS_PER_EXPERT):
                def piece(j, accs, r=r):
                    sl = pl.ds(j * SC_LANES, SC_LANES)
                    ha = aux_v[i * CHUNKS + r, sl]
                    hb = aux_v[i * CHUNKS + ROWS_PER_EXPERT + r, sl]
                    out = []
                    for k in range(PEER_TOPK):
                        hi, lo = _unpack(rows_v[slot, r * PEER_TOPK + k, sl])
                        out.append(accs[k] + (hi * ha + lo * hb))
                    return tuple(out)
                accs = lax.fori_loop(0, WORD_ROWS, piece, accs)
            for k in range(PEER_TOPK):
                fold_v[k, :] = accs[k]
            tot = plsc.load_gather(fold_v, [lane, lane * 0])
            for l in range(1, SC_LANES):
                tot = tot + plsc.load_gather(fold_v, [lane, lane * 0 + l])
            out_v[i, pl.ds(p * PEER_TOPK, PEER_TOPK)] = tot

        consume = pair_dots if down else weighted_sum

        @pl.loop(0, per // SC_TOKEN_CHUNK)
        def _(c):
            t0 = wid * per + c * SC_TOKEN_CHUNK
            pltpu.sync_copy(e_hbm.at[pl.ds(t0, SC_TOKEN_CHUNK)], e_v)
            pltpu.sync_copy(aux_hbm.at[pl.ds(t0 * aux_rows, SC_TOKEN_CHUNK * aux_rows)], aux_v)
            for p in range(SC_RING):
                start_gather(0, p, p)

            @pl.loop(0, SC_TOKEN_CHUNK)
            def _(i):
                for p in range(PEER_HEADS):
                    slot = p % SC_RING
                    for d in gathers(i, p, slot):
                        d.wait()
                    consume(i, p, slot)
                    if p + SC_RING < PEER_HEADS:
                        start_gather(i, p + SC_RING, slot)
                    else:
                        @pl.when(i + 1 < SC_TOKEN_CHUNK)
                        def _():
                            start_gather(i + 1, p + SC_RING - PEER_HEADS, slot)

            pltpu.sync_copy(out_v, o_hbm.at[pl.ds(t0 * out_rows, SC_TOKEN_CHUNK * out_rows)])

    return body(e_t, aux, tab)


def _pack_table(tab):
    bits = lax.bitcast_convert_type(tab.astype(BF16), jnp.uint16).astype(jnp.uint32)
    words = (bits[:, :HALF] << 16) | bits[:, HALF:]
    return words.reshape(tab.shape[0] * ROWS_PER_EXPERT, 128)


PEER_TOKENS = 128
SC_LEAD_BATCHES = (1, 3)
SC_SHARE_DOWN = (3, 8)
SC_SHARE_UP = (7, 16)
SC_MIN_TOKENS = 4096


def _sc_tokens(n):
    if n < SC_MIN_TOKENS:
        return (0, 0)
    unit = math.lcm(PEER_TOKENS, SC_CORES * SC_SUBCORES * SC_TOKEN_CHUNK)
    return tuple(n * num // den // unit * unit for num, den in (SC_SHARE_DOWN, SC_SHARE_UP))


def _row_tile(n, pref):
    while n % pref:
        pref //= 2
    return pref


def _peer_on_sc_down(h3, qp, k1_bf, k2_bf, tab_u, tt, route_after=()):
    n = h3.shape[0]
    e_t, gate_t = _route(qp, k1_bf, k2_bf, tt, after=route_after)
    return e_t, gate_t, _peer_sc(e_t, h3.reshape(n * CHUNKS, 128), tab_u, down=True)


def _peer_on_sc_up(e_t, gate_t, act, tab_v, tt, gate_after):
    w_t = _gate_act(act, gate_t, act, 0, tt, after=gate_after)
    return _peer_sc(e_t, w_t, tab_v, down=False), w_t


def _peer_and_final(x2, h3, qp, k1_bf, k2_bf, tab_u, tab_v, g_final, tt, n_sc=(0, 0), route_after=(), out_rows=None,
                    lead=None, y_start=None):
    n = x2.shape[0]
    n_sc_down, n_sc_up = n_sc
    tc_down, tc_up = n - n_sc_down, n - n_sc_up
    if n_sc_down:
        e_t, gate_t = _route(qp, k1_bf, k2_bf, tt, tc_down, n_sc_down, after=route_after)
        act = _peer_sc(e_t[tc_down:], h3[tc_down:].reshape(n_sc_down * CHUNKS, 128), tab_u, down=True)
        e_t, gate_t = _route(qp, k1_bf, k2_bf, tt, 0, tc_down, into=(e_t, gate_t))
    else:
        e_t, gate_t = _route(qp, k1_bf, k2_bf, tt)
    w_t = _peer_u(e_t, h3, gate_t, tab_u, tt, tc_down)
    y = y_start
    if lead is not None:
        x2_lead, add8_lead, y_first = lead
        y = _residual_norm(x2_lead, add8_lead, g_final, y, 0, y_first, tt, out_rows=out_rows, after=(w_t,))
    if n_sc_down:
        w_t = _gate_act(act, gate_t, w_t, tc_down, tt, after=() if y is None else (y,))
    y = _peer_v(e_t, w_t, x2, g_final, tab_v, tt, tc_up, into=y)
    if n_sc_up:
        add8 = _peer_sc(e_t[tc_up:], w_t[tc_up:], tab_v, down=False)
        y = _residual_norm(x2, add8, g_final, y, tc_up, tc_up, tt)
    return y


def kernel(x_prompt, x_sample, mem_prompt, cache_da_k, cache_da_v, state_ret, cache_mem_k, cache_mem_v, g_mix, w_in, lam_q1, lam_k1, lam_q2, lam_k2, g_da, g_ret, w_out, g_cross, g_mem, w_xq, w_xk, w_xv, w_xo, g_ffn, w_pq, peer_k1, peer_k2, peer_u, peer_v, g_final):
    depth = w_in.shape[0]
    assert depth == 1, "single-layer step"
    l = 0
    lam_init = 0.8 - 0.6 * math.exp(-0.3 * l)
    b, t, _ = x_prompt.shape
    bs, ts, _ = x_sample.shape
    past_len = cache_da_k.shape[2]

    row = lambda a: a.reshape(1, -1)
    w_in_bf = w_in[l].astype(BF16)
    w_out_bf = w_out[l].astype(BF16)
    w_xq_bf, w_xk_bf, w_xv_bf, w_xo_bf = (w[l].astype(BF16) for w in (w_xq, w_xk, w_xv, w_xo))
    w_pq_bf = w_pq[l].astype(BF16)
    k1_bf, k2_bf = peer_k1[l].astype(BF16), peer_k2[l].astype(BF16)
    tab_u, tab_v = _pack_table(peer_u[l]), _pack_table(peer_v[l])
    lamp = jnp.stack([lam_q1[l], lam_k1[l], lam_q2[l], lam_k2[l]])
    g_da3 = g_da[l].reshape(H_D, 1, DV_D)
    g_ret3 = g_ret[l].reshape(H_R, 1, DV_R)
    g_fin = row(g_final)

    def mixer_mid(x2d, first_row, mda, mret, mk, mv, bb, tt_rows):
        n = bb * tt_rows
        tm = _row_tile(n, 512)
        x1, qx = _out_proj(x2d, mda.reshape(n, GROUP_W), mret.reshape(n, GROUP_W), w_out_bf, row(g_cross[l]), w_xq_bf,
                           tm, first_row)
        x2, h3, qp = _cross(x1.reshape(bb, tt_rows, D_MODEL), qx.reshape(bb, tt_rows, D_MODEL), mk, mv,
                            w_xo_bf, row(g_ffn[l]), w_pq_bf, _row_tile(tt_rows, 512))
        return x2.reshape(n, D_MODEL), h3.reshape(n, D_MODEL), qp.reshape(n, -1)

    peer_args = (k1_bf, k2_bf, tab_u, tab_v, g_fin, PEER_TOKENS)
    n = b * t
    xp = x_prompt.reshape(n, D_MODEL)

    def prompt_dense(b0, bg, kv_into=None):
        rows = bg * t
        qd, kd, vd, kdb, vdb, qr, kr, vr, gr = _in_proj(xp, row(g_mix[l]), w_in_bf, _row_tile(rows, 512),
                                                       b0 * t, rows, kv_into)
        r3 = lambda a: a.reshape(bg, t, GROUP_W)
        mda = _diff_attn_prompt(lamp, r3(qd), r3(kdb), r3(vdb), g_da3, lam_init, 512, 512)
        mret, s_fin = _retention(r3(qr), r3(kr), r3(vr), r3(gr), g_ret3, None, 512)
        return mixer_mid(xp, b0 * t, mda, mret, mkb[b0:b0 + bg], mvb[b0:b0 + bg], bg, t), (kd, vd), s_fin

    mk, mv, mkb, mvb = _mem_kv(mem_prompt.reshape(b * N_MEM, D_MODEL), row(g_mem[l]), w_xk_bf, w_xv_bf, 512)
    mkb, mvb = mkb.reshape(b, N_MEM, D_MODEL), mvb.reshape(b, N_MEM, D_MODEL)
    leads = [bg for bg in SC_LEAD_BATCHES if bg] if sum(SC_LEAD_BATCHES) < b else []
    kv, states = None, []
    y = None
    launched = None
    finished = None
    hi = b
    for bg in leads + [hi - sum(leads)]:
        b0, hi = hi - bg, hi - bg
        peer_in, kv, s_g = prompt_dense(b0, bg, kv_into=kv)
        states.insert(0, s_g)
        x2_g, h3_g, qp_g = peer_in
        if finished is not None:
            y = _residual_norm(finished[0], finished[1], g_fin, y, 0, finished[2], PEER_TOKENS, out_rows=n, after=(qp_g,))
            finished = None
        route_after = ()
        if launched is not None:
            x2_p, e_p, gate_p, act_p, row_p = launched
            add8_p, w_p = _peer_on_sc_up(e_p, gate_p, act_p, tab_v, PEER_TOKENS,
                                         gate_after=(qp_g,) + (() if y is None else (y,)))
            finished, route_after = (x2_p, add8_p, row_p), (w_p,)
        if b0 > 0:
            launched = (x2_g,) + _peer_on_sc_down(h3_g, qp_g, k1_bf, k2_bf, tab_u, PEER_TOKENS, route_after) + (b0 * t,)
        else:
            y = _peer_and_final(*peer_in, *peer_args, _sc_tokens(bg * t), route_after=route_after, out_rows=n,
                                lead=finished, y_start=y)
    kd, vd = kv
    y_prompt = y.reshape(b, t, D_MODEL)
    s_fin = jnp.concatenate(states) if len(states) > 1 else states[0]

    ns = bs * ts
    xs = x_sample.reshape(ns, D_MODEL)
    qd_s, kd_s, vd_s, kdb_s, vdb_s, qr_s, kr_s, vr_s, gr_s = _in_proj(xs, row(g_mix[l]), w_in_bf, _row_tile(ns, 512))
    s3 = lambda a: a.reshape(bs, ts, GROUP_W)
    mda_s = _diff_attn_sample(lamp, s3(qd_s), cache_da_k[l].reshape(bs, past_len, GROUP_W),
                              cache_da_v[l].reshape(bs, past_len, GROUP_W), s3(kdb_s), s3(vdb_s), g_da3, lam_init)
    mret_s, s_new = _retention(s3(qr_s), s3(kr_s), s3(vr_s), s3(gr_s), g_ret3, state_ret[l], ts)
    peer_s = mixer_mid(xs, 0, mda_s, mret_s, cache_mem_k[l].reshape(bs, N_MEM, D_MODEL),
                       cache_mem_v[l].reshape(bs, N_MEM, D_MODEL), bs, ts)
    y_sample = _peer_and_final(*peer_s, *peer_args).reshape(bs, ts, D_MODEL)

    return (y_prompt, y_sample,
            kd.reshape(1, b, t, H_D, 2, DK_D), vd.reshape(1, b, t, H_D, DV_D), s_fin[None],
            mk.reshape(1, b, N_MEM, H_X, HD_X), mv.reshape(1, b, N_MEM, H_X, HD_X),
            kd_s.reshape(1, bs, ts, H_D, 2, DK_D), vd_s.reshape(1, bs, ts, H_D, DV_D), s_new[None])
```

```python
import functools
import math

import jax
import jax.numpy as jnp
from jax import lax
from jax.experimental import pallas as pl
from jax.experimental.pallas import tpu as pltpu
from jax.experimental.pallas import tpu_sc as plsc

D_MODEL = 1024
CHUNK = 64
CHUNK_SHIFT = CHUNK.bit_length() - 1
assert 1 << CHUNK_SHIFT == CHUNK
H_D, DK_D, DV_D = 4, 64, 128
H_R, DK_R, DV_R = 4, 128, 128
N_MEM = 256
H_X = 4
HD_X = D_MODEL // H_X
PEER_HEADS = 8
N_KEYS = 128
N_EXPERTS = N_KEYS * N_KEYS
PEER_TOPK = 16
EPS = 1e-6
HEAD_W = 128
GROUP_W = 512
N_PAIRS = PEER_HEADS * PEER_TOPK
HALF = D_MODEL // 2
ROWS_PER_EXPERT = HALF // 128
VMEM_LIMIT = 56 * 1024 * 1024

BF16 = jnp.bfloat16
F32 = jnp.float32


def _cparams(sem):
    return pltpu.CompilerParams(dimension_semantics=sem, vmem_limit_bytes=VMEM_LIMIT)


def _rms(x, g):
    return x * lax.rsqrt(jnp.mean(x * x, axis=-1, keepdims=True) + EPS) * g


def _dot(a, b):
    return jnp.dot(a, b, preferred_element_type=F32)


def _dot_nt(a, b):
    return lax.dot_general(a, b, (((1,), (1,)), ((), ())), preferred_element_type=F32)


def _dot_tn(a, b):
    return lax.dot_general(a, b, (((0,), (0,)), ((), ())), preferred_element_type=F32)


def _select_by_head(h, values):
    out = jnp.float32(values[-1])
    for i in range(len(values) - 2, -1, -1):
        out = jnp.where(h == i, jnp.float32(values[i]), out)
    return out


def _in_proj_kernel(x_ref, g_ref, w_ref, *rest):
    qd_ref, kd_ref, vd_ref, kdb_ref, vdb_ref, qr_ref, kr_ref, vr_ref, gr_ref = rest[-9:]
    hb = _rms(x_ref[...], g_ref[...]).astype(BF16)
    col = lambda c: _dot(hb, w_ref[:, c * GROUP_W:(c + 1) * GROUP_W])
    qd_ref[...] = (col(0) * (DK_D ** -0.5)).astype(BF16)
    kd = col(1)
    kd_ref[...] = kd.reshape(kd_ref.shape)
    kdb_ref[...] = kd.astype(BF16)
    vd = col(2)
    vd_ref[...] = vd.reshape(vd_ref.shape)
    vdb_ref[...] = vd.astype(BF16)
    qr_ref[...] = col(3).astype(BF16)
    kr_ref[...] = (col(4) * (DK_R ** -0.5)).astype(BF16)
    vr_ref[...] = col(5).astype(BF16)
    gr_ref[...] = col(6)


def _in_proj(x2d, g, w_bf, tm, first_row=0, n_rows=None, kv_into=None):
    n_all = x2d.shape[0]
    n = n_all if n_rows is None else n_rows
    off = first_row // tm
    blk = lambda: pl.BlockSpec((tm, GROUP_W), lambda i: (i, 0))
    k_dims, v_dims = (H_D, 2, DK_D), (H_D, DV_D)
    blk_all = lambda dims: pl.BlockSpec((tm,) + dims, lambda i: (i + off,) + (0,) * len(dims))
    sh = lambda dt: jax.ShapeDtypeStruct((n, GROUP_W), dt)
    sh_all = lambda dims: jax.ShapeDtypeStruct((n_all,) + dims, F32)
    args = [x2d, g, w_bf]
    in_specs = [pl.BlockSpec((tm, D_MODEL), lambda i: (i + off, 0)),
                pl.BlockSpec((1, D_MODEL), lambda i: (0, 0)),
                pl.BlockSpec(w_bf.shape, lambda i: (0, 0))]
    aliases = {}
    if kv_into is not None:
        args += list(kv_into)
        in_specs += [pl.BlockSpec(memory_space=pl.ANY)] * 2
        aliases = {3: 1, 4: 2}
    return pl.pallas_call(
        _in_proj_kernel,
        grid=(n // tm,),
        in_specs=in_specs,
        out_specs=[blk(), blk_all(k_dims), blk_all(v_dims)] + [blk() for _ in range(6)],
        out_shape=[sh(BF16), sh_all(k_dims), sh_all(v_dims), sh(BF16), sh(BF16), sh(BF16), sh(BF16), sh(BF16), sh(F32)],
        input_output_aliases=aliases,
        compiler_params=_cparams(("parallel",)),
        name="in_proj",
    )(*args)


def _lambda_from(lam_ref, lam_init):
    l = lam_ref[...]
    a = jnp.exp(jnp.sum(l[0:1] * l[1:2], axis=-1, keepdims=True))
    b = jnp.exp(jnp.sum(l[2:3] * l[3:4], axis=-1, keepdims=True))
    return a - b + lam_init


def _diff_post(acc, l, lam, g, lam_init, tq):
    o = acc[:tq] / l[:tq] - lam * (acc[tq:] / l[tq:])
    return o * lax.rsqrt(jnp.mean(o * o, axis=-1, keepdims=True) + EPS) * g * (1.0 - lam_init)


def _split_maps(q):
    lane = lax.broadcasted_iota(jnp.int32, q.shape, 1)
    zero = jnp.zeros_like(q)
    return jnp.concatenate([jnp.where(lane < DK_D, q, zero), jnp.where(lane >= DK_D, q, zero)], axis=0)


def _da_prompt_kernel(lam_ref, q_ref, k_ref, v_ref, g_ref, o_ref, kx_ref, vx_ref, own_ref, acc_ref, m_ref, *, lam_init, tq, tk):
    h = pl.program_id(1)
    i = pl.program_id(2)
    t = k_ref.shape[0]
    slope = _select_by_head(h, [2.0 ** (-8.0 * (j + 1) / H_D) for j in range(H_D)])

    @pl.when(i == 0)
    def _():
        pos = lax.broadcasted_iota(jnp.int32, (t, HEAD_W), 0)
        lane = lax.broadcasted_iota(jnp.int32, (t, HEAD_W), 1)
        coarse = ((pos >> CHUNK_SHIFT) << CHUNK_SHIFT).astype(F32) * slope
        fine = (pos & (CHUNK - 1)).astype(F32) * slope
        kx_ref[:, :HEAD_W] = k_ref[...]
        kx_ref[:, HEAD_W:] = jnp.where(lane == 0, coarse, jnp.where(lane == 1, fine, 0.0)).astype(BF16)
        vx_ref[:DV_D, :] = v_ref[...].astype(F32).T.astype(BF16)
        vx_ref[DV_D:, :] = jnp.ones((vx_ref.shape[0] - DV_D, t), BF16)
        krel = lax.broadcasted_iota(jnp.int32, (tk, 2 * tq), 0)
        c = lax.broadcasted_iota(jnp.int32, (tk, 2 * tq), 1)
        for par in range(tk // tq):
            qrel = par * tq + jnp.where(c >= tq, c - tq, c)
            ahead = (2.0 * slope) * jnp.maximum(krel - qrel, 0).astype(F32)
            own_ref[par] = jnp.where((qrel >> CHUNK_SHIFT) >= (krel >> CHUNK_SHIFT), -ahead, -1e30)

    q = q_ref[...]
    lane = lax.broadcasted_iota(jnp.int32, q.shape, 1)
    zero = jnp.zeros_like(q)
    ones2 = jnp.where(lane < 2, 1.0, 0.0).astype(BF16)
    q2 = jnp.concatenate([jnp.concatenate([jnp.where(lane < DK_D, q, zero), ones2], axis=1),
                          jnp.concatenate([jnp.where(lane >= DK_D, q, zero), ones2], axis=1)], axis=0)
    jd = (i * tq) // tk

    def scores(j):
        return _dot_nt(kx_ref[pl.ds(pl.multiple_of(j * tk, tk), tk), :], q2)

    def values(j):
        return vx_ref[:, pl.ds(pl.multiple_of(j * tk, tk), tk)]

    s = scores(jd) + own_ref[(i * tq) % tk // tq]
    m0 = jnp.max(s, axis=0, keepdims=True)
    m_ref[...] = m0
    acc_ref[...] = _dot(values(jd), jnp.exp(s - m0).astype(BF16))

    def absorb(blocks):
        ss = [scores(j) for j in blocks]
        m_old = m_ref[...]
        m_new = m_old
        for s in ss:
            m_new = jnp.maximum(m_new, jnp.max(s, axis=0, keepdims=True))
        m_ref[...] = m_new
        acc = jnp.exp(m_old - m_new) * acc_ref[...]
        for j, s in zip(blocks, ss):
            acc = acc + _dot(values(j), jnp.exp(s - m_new).astype(BF16))
        acc_ref[...] = acc

    def past_pair(jj, carry):
        absorb([2 * jj, 2 * jj + 1])
        return carry

    lax.fori_loop(0, jd // 2, past_pair, 0)

    @pl.when(jd % 2 == 1)
    def _():
        absorb([jd - 1])

    acc = acc_ref[...]
    num, den = acc[:DV_D], acc[DV_D:DV_D + 1]
    lam = _lambda_from(lam_ref, lam_init)
    o = (num[:, :tq] / den[:, :tq] - lam * (num[:, tq:] / den[:, tq:])).T
    o = o * lax.rsqrt(jnp.mean(o * o, axis=-1, keepdims=True) + EPS) * g_ref[...] * (1.0 - lam_init)
    o_ref[...] = o.astype(o_ref.dtype)


ONES_ROWS = 16


def _diff_attn_prompt(lamp, q, k, v, g_da3, lam_init, tq, tk):
    b, t, _ = q.shape
    kern = functools.partial(_da_prompt_kernel, lam_init=lam_init, tq=tq, tk=tk)
    return pl.pallas_call(
        kern,
        grid=(b, H_D, t // tq),
        in_specs=[pl.BlockSpec((4, DK_D), lambda b_, h, i: (0, 0)),
                  pl.BlockSpec((None, tq, HEAD_W), lambda b_, h, i: (b_, i, h)),
                  pl.BlockSpec((None, t, HEAD_W), lambda b_, h, i: (b_, 0, h)),
                  pl.BlockSpec((None, t, HEAD_W), lambda b_, h, i: (b_, 0, h)),
                  pl.BlockSpec((None, 1, HEAD_W), lambda b_, h, i: (h, 0, 0))],
        out_specs=pl.BlockSpec((None, tq, HEAD_W), lambda b_, h, i: (b_, i, h)),
        out_shape=jax.ShapeDtypeStruct((b, t, GROUP_W), BF16),
        scratch_shapes=[pltpu.VMEM((t, 2 * HEAD_W), BF16), pltpu.VMEM((DV_D + ONES_ROWS, t), BF16),
                        pltpu.VMEM((tk // tq, tk, 2 * tq), F32),
                        pltpu.VMEM((DV_D + ONES_ROWS, 2 * tq), F32), pltpu.VMEM((1, 2 * tq), F32)],
        compiler_params=_cparams(("parallel", "parallel", "arbitrary")),
        name="diff_attn_prompt",
    )(lamp, q, k, v, g_da3)


def _da_sample_kernel(lam_ref, q_ref, kc_ref, vc_ref, kn_ref, vn_ref, g_ref, o_ref, *, lam_init, ts, past_len):
    h = pl.program_id(1)
    slope = _select_by_head(h, [2.0 ** (-8.0 * (j + 1) / H_D) for j in range(H_D)])
    q2 = _split_maps(q_ref[...])
    rows = 2 * ts

    def scores(k, base, n):
        r = lax.broadcasted_iota(jnp.int32, (rows, n), 0)
        qpos = past_len + jnp.where(r >= ts, r - ts, r)
        kpos = base + lax.broadcasted_iota(jnp.int32, (rows, n), 1)
        return _dot_nt(q2, k) - slope * jnp.abs(qpos - kpos).astype(F32)

    sc = scores(kc_ref[...].astype(BF16), 0, past_len)
    sn = scores(kn_ref[...], past_len, ts)
    m = jnp.maximum(jnp.max(sc, axis=-1, keepdims=True), jnp.max(sn, axis=-1, keepdims=True))
    pc = jnp.exp(sc - m)
    pn = jnp.exp(sn - m)
    l = jnp.sum(pc, axis=-1, keepdims=True) + jnp.sum(pn, axis=-1, keepdims=True)
    acc = _dot(pc.astype(BF16), vc_ref[...].astype(BF16)) + _dot(pn.astype(BF16), vn_ref[...])
    lam = _lambda_from(lam_ref, lam_init)
    o_ref[...] = _diff_post(acc, l, lam, g_ref[...], lam_init, ts).astype(o_ref.dtype)


def _diff_attn_sample(lamp, q, kc, vc, kn, vn, g_da3, lam_init):
    b, ts, _ = q.shape
    past_len = kc.shape[1]
    kern = functools.partial(_da_sample_kernel, lam_init=lam_init, ts=ts, past_len=past_len)
    head = lambda rows: pl.BlockSpec((None, rows, HEAD_W), lambda b_, h: (b_, 0, h))
    return pl.pallas_call(
        kern,
        grid=(b, H_D),
        in_specs=[pl.BlockSpec((4, DK_D), lambda b_, h: (0, 0)),
                  head(ts), head(past_len), head(past_len), head(ts), head(ts),
                  pl.BlockSpec((None, 1, HEAD_W), lambda b_, h: (h, 0, 0))],
        out_specs=head(ts),
        out_shape=jax.ShapeDtypeStruct((b, ts, GROUP_W), BF16),
        compiler_params=_cparams(("parallel", "parallel")),
        name="diff_attn_sample",
    )(lamp, q, kc, vc, kn, vn, g_da3)


def _ret_kernel(*refs, lb, has_init):
    if has_init:
        q_ref, k_ref, v_ref, gate_ref, g_ref, s0_ref, o_ref, sfin_ref, s_ref = refs
    else:
        q_ref, k_ref, v_ref, gate_ref, g_ref, o_ref, sfin_ref, s_ref = refs
    h = pl.program_id(1)
    c = pl.program_id(2)
    lg = _select_by_head(h, [math.log1p(-(2.0 ** (-5.0 - j))) for j in range(H_R)])

    @pl.when(c == 0)
    def _():
        s_ref[...] = s0_ref[...] if has_init else jnp.zeros_like(s_ref)

    q, k, v = q_ref[...], k_ref[...], v_ref[...]
    i = lax.broadcasted_iota(jnp.int32, (lb, lb), 0)
    j = lax.broadcasted_iota(jnp.int32, (lb, lb), 1)
    d = (i - j).astype(F32)
    decay = jnp.where(d >= 0, jnp.exp(jnp.maximum(d, 0.0) * lg), 0.0)
    inner = _dot_nt(q, k) * decay
    ic = lax.broadcasted_iota(jnp.int32, (lb, 1), 0).astype(F32)
    s_old = s_ref[...]
    o = _dot(inner.astype(BF16), v) + _dot(q, s_old.astype(BF16)) * jnp.exp((ic + 1.0) * lg)
    tail = jnp.exp((lb - 1.0 - ic) * lg)
    kt = (k.astype(F32) * tail).astype(BF16)
    s_new = jnp.exp(lb * lg) * s_old + _dot_tn(kt, v)
    s_ref[...] = s_new

    @pl.when(c == pl.num_programs(2) - 1)
    def _():
        sfin_ref[...] = s_new

    oc = o - jnp.mean(o, axis=-1, keepdims=True)
    y = oc * lax.rsqrt(jnp.mean(oc * oc, axis=-1, keepdims=True) + EPS) * g_ref[...]
    gate = gate_ref[...]
    o_ref[...] = (y * (gate * jax.nn.sigmoid(gate))).astype(o_ref.dtype)


def _retention(q, k, v, gate, g_ret3, s0, lb):
    b, t, _ = q.shape
    has_init = s0 is not None
    kern = functools.partial(_ret_kernel, lb=lb, has_init=has_init)
    head = lambda: pl.BlockSpec((None, lb, HEAD_W), lambda b_, h, c: (b_, c, h))
    state = lambda: pl.BlockSpec((None, None, DK_R, DV_R), lambda b_, h, c: (b_, h, 0, 0))
    in_specs = [head(), head(), head(), head(), pl.BlockSpec((None, 1, HEAD_W), lambda b_, h, c: (h, 0, 0))]
    args = [q, k, v, gate, g_ret3]
    if has_init:
        in_specs.append(state())
        args.append(s0)
    return pl.pallas_call(
        kern,
        grid=(b, H_R, t // lb),
        in_specs=in_specs,
        out_specs=[head(), state()],
        out_shape=[jax.ShapeDtypeStruct((b, t, GROUP_W), BF16), jax.ShapeDtypeStruct((b, H_R, DK_R, DV_R), F32)],
        scratch_shapes=[pltpu.VMEM((DK_R, DV_R), F32)],
        compiler_params=_cparams(("parallel", "parallel", "arbitrary")),
        name="retention",
    )(*args)


def _out_proj_kernel(x_ref, mda_ref, mret_ref, wo_ref, g_ref, wq_ref, x1_ref, qx_ref):
    x1 = x_ref[...] + _dot(mda_ref[...], wo_ref[:GROUP_W, :]) + _dot(mret_ref[...], wo_ref[GROUP_W:, :])
    x1_ref[...] = x1
    hn = _rms(x1, g_ref[...]).astype(BF16)
    qx_ref[...] = (_dot(hn, wq_ref[...]) * (HD_X ** -0.5)).astype(BF16)


def _out_proj(x2d, mda, mret, wo_bf, g_cross, wq_bf, tm, first_row=0):
    n = mda.shape[0]
    off = first_row // tm
    full = lambda a: pl.BlockSpec(a.shape, lambda i: (0, 0))
    return pl.pallas_call(
        _out_proj_kernel,
        grid=(n // tm,),
        in_specs=[pl.BlockSpec((tm, D_MODEL), lambda i: (i + off, 0)),
                  pl.BlockSpec((tm, GROUP_W), lambda i: (i, 0)),
                  pl.BlockSpec((tm, GROUP_W), lambda i: (i, 0)),
                  full(wo_bf), full(g_cross), full(wq_bf)],
        out_specs=[pl.BlockSpec((tm, D_MODEL), lambda i: (i, 0)), pl.BlockSpec((tm, D_MODEL), lambda i: (i, 0))],
        out_shape=[jax.ShapeDtypeStruct((n, D_MODEL), F32), jax.ShapeDtypeStruct((n, D_MODEL), BF16)],
        compiler_params=_cparams(("parallel",)),
        name="out_proj",
    )(x2d, mda, mret, wo_bf, g_cross, wq_bf)


def _mem_kv_kernel(m_ref, g_ref, wk_ref, wv_ref, mk_ref, mv_ref, mkb_ref, mvb_ref):
    mn = _rms(m_ref[...], g_ref[...]).astype(BF16)
    mk = _dot(mn, wk_ref[...])
    mv = _dot(mn, wv_ref[...])
    mk_ref[...] = mk
    mv_ref[...] = mv
    mkb_ref[...] = mk.astype(BF16)
    mvb_ref[...] = mv.astype(BF16)


def _mem_kv(mem2d, g_mem, wk_bf, wv_bf, tm):
    n = mem2d.shape[0]
    row = lambda: pl.BlockSpec((tm, D_MODEL), lambda i: (i, 0))
    full = lambda a: pl.BlockSpec(a.shape, lambda i: (0, 0))
    sh = lambda dt: jax.ShapeDtypeStruct((n, D_MODEL), dt)
    return pl.pallas_call(
        _mem_kv_kernel,
        grid=(n // tm,),
        in_specs=[row(), full(g_mem), full(wk_bf), full(wv_bf)],
        out_specs=[row(), row(), row(), row()],
        out_shape=[sh(F32), sh(F32), sh(BF16), sh(BF16)],
        compiler_params=_cparams(("parallel",)),
        name="mem_kv",
    )(mem2d, g_mem, wk_bf, wv_bf)


def _cross_kernel(x1_ref, qx_ref, mk_ref, mv_ref, wo_ref, g_ref, wpq_ref, x2_ref, h3_ref, qp_ref):
    q = qx_ref[...]
    heads = []
    for h in range(H_X):
        sl = slice(h * HD_X, (h + 1) * HD_X)
        kh = mk_ref[:, sl] if mk_ref.dtype == BF16 else mk_ref[:, sl].astype(BF16)
        vh = mv_ref[:, sl] if mv_ref.dtype == BF16 else mv_ref[:, sl].astype(BF16)
        s = _dot_nt(q[:, sl], kh)
        p = jnp.exp(s - jnp.max(s, axis=-1, keepdims=True))
        heads.append(_dot(p.astype(BF16), vh) / jnp.sum(p, axis=-1, keepdims=True))
    o = jnp.concatenate(heads, axis=-1).astype(BF16)
    x2 = x1_ref[...] + _dot(o, wo_ref[...])
    x2_ref[...] = x2
    h3 = _rms(x2, g_ref[...])
    h3_ref[...] = h3
    qp_ref[...] = _dot(h3.astype(BF16), wpq_ref[...]).astype(BF16)


def _cross(x1, qx, mk, mv, wo_bf, g_ffn, wpq_bf, tm):
    b, t, _ = x1.shape
    row = lambda w: pl.BlockSpec((None, tm, w), lambda b_, i: (b_, i, 0))
    mem = lambda: pl.BlockSpec((None, N_MEM, D_MODEL), lambda b_, i: (b_, 0, 0))
    full = lambda a: pl.BlockSpec(a.shape, lambda b_, i: (0, 0))
    dq = wpq_bf.shape[1]
    return pl.pallas_call(
        _cross_kernel,
        grid=(b, t // tm),
        in_specs=[row(D_MODEL), row(D_MODEL), mem(), mem(), full(wo_bf), full(g_ffn), full(wpq_bf)],
        out_specs=[row(D_MODEL), row(D_MODEL), row(dq)],
        out_shape=[jax.ShapeDtypeStruct((b, t, D_MODEL), F32), jax.ShapeDtypeStruct((b, t, D_MODEL), F32),
                   jax.ShapeDtypeStruct((b, t, dq), BF16)],
        compiler_params=_cparams(("parallel", "parallel")),
        name="cross_attn",
    )(x1, qx, mk, mv, wo_bf, g_ffn, wpq_bf)


ID_PAD = 2.0 ** 29


def _topk_rows(s, ids):
    vals, sel = [], []
    for _ in range(PEER_TOPK):
        m = jnp.max(s, axis=0, keepdims=True)
        idx = jnp.min(jnp.where(s == m, ids, ID_PAD), axis=0, keepdims=True)
        vals.append(m)
        sel.append(idx)
        s = jnp.where(ids == idx, -jnp.inf, s)
    return jnp.concatenate(vals, axis=0), jnp.concatenate(sel, axis=0)


def _candidates(v1, i1, v2, i2):
    lanes = v1.shape[1]
    b8 = lax.broadcasted_iota(jnp.int32, (8, lanes), 0)
    ident = lambda a, ia, ib, b: (a * PEER_TOPK + b) * float(N_EXPERTS) + (ia * float(N_KEYS) + ib)
    b16 = lax.broadcasted_iota(jnp.int32, (PEER_TOPK, lanes), 0).astype(F32)
    vals = [v1[0:1] + v2]
    ids = [ident(0, i1[0:1], i2, b16)]
    for a in range(1, 8):
        keep = b8 < PEER_TOPK // (a + 1)
        vals.append(jnp.where(keep, v1[a:a + 1] + v2[0:8], -jnp.inf))
        ids.append(jnp.where(keep, ident(a, i1[a:a + 1], i2[0:8], b8.astype(F32)), ID_PAD))
    a_hi = (b8 + 8).astype(F32)
    vals.append(v1[8:16] + v2[0:1])
    ids.append(ident(a_hi, i1[8:16], i2[0:1], 0.0))
    return jnp.concatenate(vals, axis=0), jnp.concatenate(ids, axis=0)


def _route_kernel(qp_ref, k1_ref, k2_ref, e_ref, g_ref):
    half = N_KEYS
    tt = qp_ref.shape[0]
    key_id = lax.broadcasted_iota(jnp.int32, (N_KEYS, tt), 0).astype(F32)
    es, gs = [], []
    for p in range(PEER_HEADS):
        q1 = qp_ref[:, (2 * p) * half:(2 * p + 1) * half]
        q2 = qp_ref[:, (2 * p + 1) * half:(2 * p + 2) * half]
        v1, i1 = _topk_rows(_dot_nt(k1_ref[p], q1), key_id)
        v2, i2 = _topk_rows(_dot_nt(k2_ref[p], q2), key_id)
        sc, sel = _topk_rows(*_candidates(v1, i1, v2, i2))
        w = jnp.exp(sc - sc[0:1])
        es.append((sel.astype(jnp.int32) & (N_EXPERTS - 1)) * ROWS_PER_EXPERT)
        gs.append(w / jnp.sum(w, axis=0, keepdims=True))
    e_ref[...] = jnp.concatenate(es, axis=0).T
    g_ref[...] = jnp.concatenate(gs, axis=0).T


def _route_extra_kernel(qp_ref, k1_ref, k2_ref, *rest):
    _route_kernel(qp_ref, k1_ref, k2_ref, rest[-2], rest[-1])


def _route(qp2d, k1_bf, k2_bf, tt, first_token=0, n_tokens=None, into=None, after=()):
    n = qp2d.shape[0]
    n_tokens = n if n_tokens is None else n_tokens
    off = first_token // tt
    full = lambda a: pl.BlockSpec(a.shape, lambda i: (0, 0, 0))
    pair = lambda: pl.BlockSpec((tt, N_PAIRS), lambda i: (i + off, 0))
    in_specs = [pl.BlockSpec((tt, qp2d.shape[1]), lambda i: (i + off, 0)), full(k1_bf), full(k2_bf)]
    args = [qp2d, k1_bf, k2_bf]
    aliases = {}
    if into is not None:
        in_specs += [pl.BlockSpec(memory_space=pl.ANY), pl.BlockSpec(memory_space=pl.ANY)]
        args += list(into)
        aliases = {3: 0, 4: 1}
    in_specs += [pl.BlockSpec(memory_space=pl.ANY)] * len(after)
    args += list(after)
    return pl.pallas_call(
        _route_kernel if len(args) == 3 else _route_extra_kernel,
        grid=(n_tokens // tt,),
        in_specs=in_specs,
        out_specs=[pair(), pair()],
        out_shape=[jax.ShapeDtypeStruct((n, N_PAIRS), jnp.int32), jax.ShapeDtypeStruct((n, N_PAIRS), F32)],
        input_output_aliases=aliases,
        compiler_params=_cparams(("parallel",)),
        name="peer_route",
    )(*args)


def _unpack(words):
    hi = lax.bitcast_convert_type(words & jnp.uint32(0xFFFF0000), F32)
    lo = lax.bitcast_convert_type(words << 16, F32)
    return hi, lo


def _expert_rows(tab_ref, row0):
    return tab_ref[pl.ds(pl.multiple_of(row0, ROWS_PER_EXPERT), ROWS_PER_EXPERT), :]


REDUCE_TOKENS = 16


def _lane_sums_to_rows(y, n_tok):
    hi = y.astype(BF16)
    lo = (y - hi.astype(F32)).astype(BF16)
    ones = jnp.ones((128, N_PAIRS), BF16)
    s = (_dot(hi, ones) + _dot(lo, ones)).reshape(n_tok, N_PAIRS, N_PAIRS)
    eye = lax.broadcasted_iota(jnp.int32, (N_PAIRS, N_PAIRS), 0) == lax.broadcasted_iota(jnp.int32, (N_PAIRS, N_PAIRS), 1)
    return jnp.sum(jnp.where(eye[None], s, 0.0), axis=1)


CHUNKS = D_MODEL // 128


def _peer_u_kernel(e_ref, h_ref, gate_ref, tab_ref, w_ref, h8_ref, prod_ref, ys_ref, act_ref, *, tt):
    for c in range(CHUNKS):
        h8_ref[pl.ds(c, tt, stride=CHUNKS), :] = h_ref[:, c * 128:(c + 1) * 128]

    def token(t, slot):
        base = pl.multiple_of(t * CHUNKS, CHUNKS)
        ha = h8_ref[pl.ds(base, ROWS_PER_EXPERT), :]
        hb = h8_ref[pl.ds(pl.multiple_of(base + ROWS_PER_EXPERT, ROWS_PER_EXPERT), ROWS_PER_EXPERT), :]
        prod = prod_ref.at[slot]
        for k in range(N_PAIRS):
            hi, lo = _unpack(_expert_rows(tab_ref, e_ref[t, k]))
            prod[k * ROWS_PER_EXPERT:(k + 1) * ROWS_PER_EXPERT, :] = hi * ha + lo * hb
        y = prod[pl.ds(0, N_PAIRS, stride=ROWS_PER_EXPERT), :]
        for c in range(1, ROWS_PER_EXPERT):
            y = y + prod[pl.ds(c, N_PAIRS, stride=ROWS_PER_EXPERT), :]
        ys_ref[pl.ds(pl.multiple_of(t * N_PAIRS, N_PAIRS), N_PAIRS), :] = y

    def two_tokens(j, carry):
        token(2 * j, 0)
        token(2 * j + 1, 1)
        return carry

    lax.fori_loop(0, tt // 2, two_tokens, 0)

    def group(g, carry):
        rows = REDUCE_TOKENS * N_PAIRS
        y = ys_ref[pl.ds(pl.multiple_of(g * rows, rows), rows), :]
        act_ref[pl.ds(pl.multiple_of(g * REDUCE_TOKENS, REDUCE_TOKENS), REDUCE_TOKENS), :] = _lane_sums_to_rows(y, REDUCE_TOKENS)
        return carry

    lax.fori_loop(0, tt // REDUCE_TOKENS, group, 0)
    w_ref[...] = _gated_gelu(gate_ref[...], act_ref[...])


def _gated_gelu(gate, a):
    return gate * (0.5 * a * (1.0 + lax.erf(a * (2.0 ** -0.5))))


def _gate_act_kernel(act_ref, gate_ref, *rest):
    rest[-1][...] = _gated_gelu(gate_ref[...], act_ref[...])


def _gate_act(act, gate_t, w_t, first_token, tt, after=()):
    off = first_token // tt
    return pl.pallas_call(
        _gate_act_kernel,
        grid=(act.shape[0] // tt,),
        in_specs=[pl.BlockSpec((tt, N_PAIRS), lambda i: (i, 0)),
                  pl.BlockSpec((tt, N_PAIRS), lambda i: (i + off, 0))]
                 + [pl.BlockSpec(memory_space=pl.ANY)] * (1 + len(after)),
        out_specs=pl.BlockSpec((tt, N_PAIRS), lambda i: (i + off, 0)),
        out_shape=jax.ShapeDtypeStruct(w_t.shape, F32),
        input_output_aliases={2: 0},
        compiler_params=_cparams(("arbitrary",)),
        name="peer_gate_act",
    )(act, gate_t, w_t, *after)


def _peer_u(e_t, h, gate_t, tab, tt, n_tokens):
    n = e_t.shape[0]
    assert tt % REDUCE_TOKENS == 0
    kern = functools.partial(_peer_u_kernel, tt=tt)
    pair = lambda **kw: pl.BlockSpec((tt, N_PAIRS), lambda i: (i, 0), **kw)
    return pl.pallas_call(
        kern,
        grid=(n_tokens // tt,),
        in_specs=[pair(memory_space=pltpu.SMEM),
                  pl.BlockSpec((tt, D_MODEL), lambda i: (i, 0)),
                  pair(),
                  pl.BlockSpec(tab.shape, lambda i: (0, 0), pipeline_mode=pl.Buffered(1))],
        out_specs=pair(),
        out_shape=jax.ShapeDtypeStruct((n, N_PAIRS), F32),
        scratch_shapes=[pltpu.VMEM((tt * CHUNKS, 128), F32),
                        pltpu.VMEM((2, N_PAIRS * ROWS_PER_EXPERT, 128), F32),
                        pltpu.VMEM((tt * N_PAIRS, 128), F32),
                        pltpu.VMEM((tt, N_PAIRS), F32)],
        compiler_params=_cparams(("arbitrary",)),
        name="peer_u",
    )(e_t, h, gate_t, tab)


def _peer_v_kernel(e_ref, w_ref, x_ref, g_ref, tab_ref, o_ref, acc_ref, *, tt):
    n_acc = 2

    def token(t):
        acc_hi = [jnp.zeros((ROWS_PER_EXPERT, 128), F32) for _ in range(n_acc)]
        acc_lo = [jnp.zeros((ROWS_PER_EXPERT, 128), F32) for _ in range(n_acc)]
        for k in range(N_PAIRS):
            hi, lo = _unpack(_expert_rows(tab_ref, e_ref[t, k]))
            w = w_ref[t, k]
            acc_hi[k % n_acc] = acc_hi[k % n_acc] + w * hi
            acc_lo[k % n_acc] = acc_lo[k % n_acc] + w * lo
        base = pl.multiple_of(t * CHUNKS, CHUNKS)
        acc_ref[pl.ds(base, ROWS_PER_EXPERT), :] = sum(acc_hi[1:], acc_hi[0])
        acc_ref[pl.ds(pl.multiple_of(base + ROWS_PER_EXPERT, ROWS_PER_EXPERT), ROWS_PER_EXPERT), :] = sum(acc_lo[1:], acc_lo[0])

    def two_tokens(j, carry):
        token(2 * j)
        token(2 * j + 1)
        return carry

    lax.fori_loop(0, tt // 2, two_tokens, 0)
    _residual_rms_store(x_ref, acc_ref, g_ref, o_ref, tt)


def _residual_rms_store(x_ref, add_ref, g_ref, o_ref, tt):
    xs = [x_ref[:, c * 128:(c + 1) * 128] + add_ref[pl.ds(c, tt, stride=CHUNKS), :] for c in range(CHUNKS)]
    sq = xs[0] * xs[0]
    for x in xs[1:]:
        sq = sq + x * x
    r = lax.rsqrt(jnp.sum(sq, axis=1, keepdims=True) * (1.0 / D_MODEL) + EPS)
    for c in range(CHUNKS):
        o_ref[:, c * 128:(c + 1) * 128] = xs[c] * r * g_ref[:, c * 128:(c + 1) * 128]


def _peer_v_into_kernel(e_ref, w_ref, x_ref, g_ref, tab_ref, y_any_ref, o_ref, acc_ref, *, tt):
    del y_any_ref
    _peer_v_kernel(e_ref, w_ref, x_ref, g_ref, tab_ref, o_ref, acc_ref, tt=tt)


def _peer_v(e_t, w_t, x, g_final, tab, tt, n_tokens, into=None):
    pair = lambda: pl.BlockSpec((tt, N_PAIRS), lambda i: (i, 0), memory_space=pltpu.SMEM)
    row = lambda: pl.BlockSpec((tt, D_MODEL), lambda i: (i, 0))
    in_specs = [pair(), pair(), row(),
                pl.BlockSpec((1, D_MODEL), lambda i: (0, 0)),
                pl.BlockSpec(tab.shape, lambda i: (0, 0), pipeline_mode=pl.Buffered(1))]
    args = [e_t, w_t, x, g_final, tab]
    if into is not None:
        in_specs.append(pl.BlockSpec(memory_space=pl.ANY))
        args.append(into)
    return pl.pallas_call(
        functools.partial(_peer_v_kernel if into is None else _peer_v_into_kernel, tt=tt),
        grid=(n_tokens // tt,),
        in_specs=in_specs,
        out_specs=row(),
        out_shape=jax.ShapeDtypeStruct(x.shape if into is None else into.shape, F32),
        input_output_aliases={} if into is None else {5: 0},
        scratch_shapes=[pltpu.VMEM((tt * CHUNKS, 128), F32)],
        compiler_params=_cparams(("arbitrary",)),
        name="peer_v",
    )(*args)


def _residual_norm_kernel(x_ref, add_ref, g_ref, *rest, tt):
    _residual_rms_store(x_ref, add_ref, g_ref, rest[-1], tt)


def _residual_norm(x, add8, g_final, y, x_first, y_first, tt, out_rows=None, after=()):
    n_rows = add8.shape[0] // CHUNKS
    x_off, y_off = x_first // tt, y_first // tt
    extra = ([] if y is None else [y]) + list(after)
    return pl.pallas_call(
        functools.partial(_residual_norm_kernel, tt=tt),
        grid=(n_rows // tt,),
        in_specs=[pl.BlockSpec((tt, D_MODEL), lambda i: (i + x_off, 0)),
                  pl.BlockSpec((tt * CHUNKS, 128), lambda i: (i, 0)),
                  pl.BlockSpec((1, D_MODEL), lambda i: (0, 0))]
                 + [pl.BlockSpec(memory_space=pl.ANY)] * len(extra),
        out_specs=pl.BlockSpec((tt, D_MODEL), lambda i: (i + y_off, 0)),
        out_shape=jax.ShapeDtypeStruct((out_rows, D_MODEL) if y is None else y.shape, F32),
        input_output_aliases={} if y is None else {3: 0},
        compiler_params=_cparams(("arbitrary",)),
        name="peer_residual_norm",
    )(x, add8, g_final, *extra)


SC_CORES = 2
SC_SUBCORES = 16
SC_LANES = 16
WORD_ROWS = 128 // SC_LANES
SC_TOKEN_CHUNK = 32
SC_RING = 4


def _peer_sc(e_t, aux, tab, down):
    n = e_t.shape[0]
    workers = SC_CORES * SC_SUBCORES
    assert n % (workers * SC_TOKEN_CHUNK) == 0 and PEER_HEADS % SC_RING == 0
    per = n // workers
    head_rows = ROWS_PER_EXPERT * PEER_TOPK
    aux_rows = CHUNKS if down else 1
    out_rows = 1 if down else CHUNKS
    mesh = plsc.VectorSubcoreMesh(core_axis_name="core", subcore_axis_name="subcore",
                                  num_cores=SC_CORES, num_subcores=SC_SUBCORES)

    @pl.kernel(out_type=jax.ShapeDtypeStruct((n * out_rows, 128), F32), mesh=mesh,
               scratch_types=[pltpu.VMEM((SC_TOKEN_CHUNK, N_PAIRS), jnp.int32),
                              pltpu.VMEM((SC_TOKEN_CHUNK * aux_rows, 128), F32),
                              pltpu.VMEM((SC_RING, head_rows, 128), jnp.uint32),
                              pltpu.VMEM((SC_TOKEN_CHUNK * out_rows, 128), F32),
                              pltpu.VMEM((PEER_TOPK, SC_LANES), F32),
                              pltpu.SemaphoreType.DMA((SC_RING,))],
               compiler_params=pltpu.CompilerParams(needs_layout_passes=False),
               name="peer_u_sc" if down else "peer_v_sc")
    def body(e_hbm, aux_hbm, tab_hbm, o_hbm, e_v, aux_v, rows_v, out_v, fold_v, sems):
        wid = lax.axis_index("core") * SC_SUBCORES + lax.axis_index("subcore")
        lane = lax.broadcasted_iota(jnp.int32, (SC_LANES,), 0)

        def gathers(i, p, slot):
            first = e_v[i, pl.ds(p * PEER_TOPK, PEER_TOPK)]
            return [pltpu.make_async_copy(tab_hbm.at[first + r],
                                          rows_v.at[slot, pl.ds(r * PEER_TOPK, PEER_TOPK)], sems.at[slot])
                    for r in range(ROWS_PER_EXPERT)]

        def start_gather(i, p, slot):
            for d in gathers(i, p, slot):
                d.start()

        def weighted_sum(i, p, slot):
            ws = [plsc.load_gather(aux_v, [lane * 0 + i, lane * 0 + (p * PEER_TOPK + k)])
                  for k in range(PEER_TOPK)]
            for r in range(ROWS_PER_EXPERT):
                @pl.loop(0, WORD_ROWS)
                def _(j):
                    sl = pl.ds(j * SC_LANES, SC_LANES)
                    if p == 0:
                        a_hi = jnp.zeros((SC_LANES,), F32)
                        a_lo = jnp.zeros((SC_LANES,), F32)
                    else:
                        a_hi = out_v[i * CHUNKS + r, sl]
                        a_lo = out_v[i * CHUNKS + ROWS_PER_EXPERT + r, sl]
                    for k in range(PEER_TOPK):
                        hi, lo = _unpack(rows_v[slot, r * PEER_TOPK + k, sl])
                        a_hi = a_hi + ws[k] * hi
                        a_lo = a_lo + ws[k] * lo
                    out_v[i * CHUNKS + r, sl] = a_hi
                    out_v[i * CHUNKS + ROWS_PER_EXPERT + r, sl] = a_lo

        def pair_dots(i, p, slot):
            accs = tuple(jnp.zeros((SC_LANES,), F32) for _ in range(PEER_TOPK))
            for r in range(ROWS_PER_EXPERT):
                def piece(j, accs, r=r):
                    sl = pl.ds(j * SC_LANES, SC_LANES)
                    ha = aux_v[i * CHUNKS + r, sl]
                    hb = aux_v[i * CHUNKS + ROWS_PER_EXPERT + r, sl]
                    out = []
                    for k in range(PEER_TOPK):
                        hi, lo = _unpack(rows_v[slot, r * PEER_TOPK + k, sl])
                        out.append(accs[k] + (hi * ha + lo * hb))
                    return tuple(out)
                accs = lax.fori_loop(0, WORD_ROWS, piece, accs)
            for k in range(PEER_TOPK):
                fold_v[k, :] = accs[k]
            tot = plsc.load_gather(fold_v, [lane, lane * 0])
            for l in range(1, SC_LANES):
                tot = tot + plsc.load_gather(fold_v, [lane, lane * 0 + l])
            out_v[i, pl.ds(p * PEER_TOPK, PEER_TOPK)] = tot

        consume = pair_dots if down else weighted_sum

        @pl.loop(0, per // SC_TOKEN_CHUNK)
        def _(c):
            t0 = wid * per + c * SC_TOKEN_CHUNK
            pltpu.sync_copy(e_hbm.at[pl.ds(t0, SC_TOKEN_CHUNK)], e_v)
            pltpu.sync_copy(aux_hbm.at[pl.ds(t0 * aux_rows, SC_TOKEN_CHUNK * aux_rows)], aux_v)
            for p in range(SC_RING):
                start_gather(0, p, p)

            @pl.loop(0, SC_TOKEN_CHUNK)
            def _(i):
                for p in range(PEER_HEADS):
                    slot = p % SC_RING
                    for d in gathers(i, p, slot):
                        d.wait()
                    consume(i, p, slot)
                    if p + SC_RING < PEER_HEADS:
                        start_gather(i, p + SC_RING, slot)
                    else:
                        @pl.when(i + 1 < SC_TOKEN_CHUNK)
                        def _():
                            start_gather(i + 1, p + SC_RING - PEER_HEADS, slot)

            pltpu.sync_copy(out_v, o_hbm.at[pl.ds(t0 * out_rows, SC_TOKEN_CHUNK * out_rows)])

    return body(e_t, aux, tab)


def _pack_table(tab):
    bits = lax.bitcast_convert_type(tab.astype(BF16), jnp.uint16).astype(jnp.uint32)
    words = (bits[:, :HALF] << 16) | bits[:, HALF:]
    return words.reshape(tab.shape[0] * ROWS_PER_EXPERT, 128)


PEER_TOKENS = 128
SC_LEAD_BATCHES = (1, 3)
SC_SHARE_DOWN = (3, 8)
SC_SHARE_UP = (7, 16)
SC_MIN_TOKENS = 4096


def _sc_tokens(n):
    if n < SC_MIN_TOKENS:
        return (0, 0)
    unit = math.lcm(PEER_TOKENS, SC_CORES * SC_SUBCORES * SC_TOKEN_CHUNK)
    return tuple(n * num // den // unit * unit for num, den in (SC_SHARE_DOWN, SC_SHARE_UP))


def _row_tile(n, pref):
    while n % pref:
        pref //= 2
    return pref


def _peer_on_sc_down(h3, qp, k1_bf, k2_bf, tab_u, tt, route_after=()):
    n = h3.shape[0]
    e_t, gate_t = _route(qp, k1_bf, k2_bf, tt, after=route_after)
    return e_t, gate_t, _peer_sc(e_t, h3.reshape(n * CHUNKS, 128), tab_u, down=True)


def _peer_on_sc_up(e_t, gate_t, act, tab_v, tt, gate_after):
    w_t = _gate_act(act, gate_t, act, 0, tt, after=gate_after)
    return _peer_sc(e_t, w_t, tab_v, down=False), w_t


def _peer_and_final(x2, h3, qp, k1_bf, k2_bf, tab_u, tab_v, g_final, tt, n_sc=(0, 0), route_after=(), out_rows=None,
                    lead=None, y_start=None):
    n = x2.shape[0]
    n_sc_down, n_sc_up = n_sc
    tc_down, tc_up = n - n_sc_down, n - n_sc_up
    if n_sc_down:
        e_t, gate_t = _route(qp, k1_bf, k2_bf, tt, tc_down, n_sc_down, after=route_after)
        act = _peer_sc(e_t[tc_down:], h3[tc_down:].reshape(n_sc_down * CHUNKS, 128), tab_u, down=True)
        e_t, gate_t = _route(qp, k1_bf, k2_bf, tt, 0, tc_down, into=(e_t, gate_t))
    else:
        e_t, gate_t = _route(qp, k1_bf, k2_bf, tt)
    w_t = _peer_u(e_t, h3, gate_t, tab_u, tt, tc_down)
    y = y_start
    if lead is not None:
        x2_lead, add8_lead, y_first = lead
        y = _residual_norm(x2_lead, add8_lead, g_final, y, 0, y_first, tt, out_rows=out_rows, after=(w_t,))
    if n_sc_down:
        w_t = _gate_act(act, gate_t, w_t, tc_down, tt, after=() if y is None else (y,))
    y = _peer_v(e_t, w_t, x2, g_final, tab_v, tt, tc_up, into=y)
    if n_sc_up:
        add8 = _peer_sc(e_t[tc_up:], w_t[tc_up:], tab_v, down=False)
        y = _residual_norm(x2, add8, g_final, y, tc_up, tc_up, tt)
    return y


def kernel(x_prompt, x_sample, mem_prompt, cache_da_k, cache_da_v, state_ret, cache_mem_k, cache_mem_v, g_mix, w_in, lam_q1, lam_k1, lam_q2, lam_k2, g_da, g_ret, w_out, g_cross, g_mem, w_xq, w_xk, w_xv, w_xo, g_ffn, w_pq, peer_k1, peer_k2, peer_u, peer_v, g_final):
    depth = w_in.shape[0]
    assert depth == 1, "single-layer step"
    l = 0
    lam_init = 0.8 - 0.6 * math.exp(-0.3 * l)
    b, t, _ = x_prompt.shape
    bs, ts, _ = x_sample.shape
    past_len = cache_da_k.shape[2]

    row = lambda a: a.reshape(1, -1)
    w_in_bf = w_in[l].astype(BF16)
    w_out_bf = w_out[l].astype(BF16)
    w_xq_bf, w_xk_bf, w_xv_bf, w_xo_bf = (w[l].astype(BF16) for w in (w_xq, w_xk, w_xv, w_xo))
    w_pq_bf = w_pq[l].astype(BF16)
    k1_bf, k2_bf = peer_k1[l].astype(BF16), peer_k2[l].astype(BF16)
    tab_u, tab_v = _pack_table(peer_u[l]), _pack_table(peer_v[l])
    lamp = jnp.stack([lam_q1[l], lam_k1[l], lam_q2[l], lam_k2[l]])
    g_da3 = g_da[l].reshape(H_D, 1, DV_D)
    g_ret3 = g_ret[l].reshape(H_R, 1, DV_R)
    g_fin = row(g_final)

    def mixer_mid(x2d, first_row, mda, mret, mk, mv, bb, tt_rows):
        n = bb * tt_rows
        tm = _row_tile(n, 512)
        x1, qx = _out_proj(x2d, mda.reshape(n, GROUP_W), mret.reshape(n, GROUP_W), w_out_bf, row(g_cross[l]), w_xq_bf,
                           tm, first_row)
        x2, h3, qp = _cross(x1.reshape(bb, tt_rows, D_MODEL), qx.reshape(bb, tt_rows, D_MODEL), mk, mv,
                            w_xo_bf, row(g_ffn[l]), w_pq_bf, _row_tile(tt_rows, 512))
        return x2.reshape(n, D_MODEL), h3.reshape(n, D_MODEL), qp.reshape(n, -1)

    peer_args = (k1_bf, k2_bf, tab_u, tab_v, g_fin, PEER_TOKENS)
    n = b * t
    xp = x_prompt.reshape(n, D_MODEL)

    def prompt_dense(b0, bg, kv_into=None):
        rows = bg * t
        qd, kd, vd, kdb, vdb, qr, kr, vr, gr = _in_proj(xp, row(g_mix[l]), w_in_bf, _row_tile(rows, 512),
                                                       b0 * t, rows, kv_into)
        r3 = lambda a: a.reshape(bg, t, GROUP_W)
        mda = _diff_attn_prompt(lamp, r3(qd), r3(kdb), r3(vdb), g_da3, lam_init, 512, 512)
        mret, s_fin = _retention(r3(qr), r3(kr), r3(vr), r3(gr), g_ret3, None, 512)
        return mixer_mid(xp, b0 * t, mda, mret, mkb[b0:b0 + bg], mvb[b0:b0 + bg], bg, t), (kd, vd), s_fin

    mk, mv, mkb, mvb = _mem_kv(mem_prompt.reshape(b * N_MEM, D_MODEL), row(g_mem[l]), w_xk_bf, w_xv_bf, 512)
    mkb, mvb = mkb.reshape(b, N_MEM, D_MODEL), mvb.reshape(b, N_MEM, D_MODEL)
    leads = [bg for bg in SC_LEAD_BATCHES if bg] if sum(SC_LEAD_BATCHES) < b else []
    kv, states = None, []
    y = None
    launched = None
    finished = None
    hi = b
    for bg in leads + [hi - sum(leads)]:
        b0, hi = hi - bg, hi - bg
        peer_in, kv, s_g = prompt_dense(b0, bg, kv_into=kv)
        states.insert(0, s_g)
        x2_g, h3_g, qp_g = peer_in
        if finished is not None:
            y = _residual_norm(finished[0], finished[1], g_fin, y, 0, finished[2], PEER_TOKENS, out_rows=n, after=(qp_g,))
            finished = None
        route_after = ()
        if launched is not None:
            x2_p, e_p, gate_p, act_p, row_p = launched
            add8_p, w_p = _peer_on_sc_up(e_p, gate_p, act_p, tab_v, PEER_TOKENS,
                                         gate_after=(qp_g,) + (() if y is None else (y,)))
            finished, route_after = (x2_p, add8_p, row_p), (w_p,)
        if b0 > 0:
            launched = (x2_g,) + _peer_on_sc_down(h3_g, qp_g, k1_bf, k2_bf, tab_u, PEER_TOKENS, route_after) + (b0 * t,)
        else:
            y = _peer_and_final(*peer_in, *peer_args, _sc_tokens(bg * t), route_after=route_after, out_rows=n,
                                lead=finished, y_start=y)
    kd, vd = kv
    y_prompt = y.reshape(b, t, D_MODEL)
    s_fin = jnp.concatenate(states) if len(states) > 1 else states[0]

    ns = bs * ts
    xs = x_sample.reshape(ns, D_MODEL)
    qd_s, kd_s, vd_s, kdb_s, vdb_s, qr_s, kr_s, vr_s, gr_s = _in_proj(xs, row(g_mix[l]), w_in_bf, _row_tile(ns, 512))
    s3 = lambda a: a.reshape(bs, ts, GROUP_W)
    mda_s = _diff_attn_sample(lamp, s3(qd_s), cache_da_k[l].reshape(bs, past_len, GROUP_W),
                              cache_da_v[l].reshape(bs, past_len, GROUP_W), s3(kdb_s), s3(vdb_s), g_da3, lam_init)
    mret_s, s_new = _retention(s3(qr_s), s3(kr_s), s3(vr_s), s3(gr_s), g_ret3, state_ret[l], ts)
    peer_s = mixer_mid(xs, 0, mda_s, mret_s, cache_mem_k[l].reshape(bs, N_MEM, D_MODEL),
                       cache_mem_v[l].reshape(bs, N_MEM, D_MODEL), bs, ts)
    y_sample = _peer_and_final(*peer_s, *peer_args).reshape(bs, ts, D_MODEL)

    return (y_prompt, y_sample,
            kd.reshape(1, b, t, H_D, 2, DK_D), vd.reshape(1, b, t, H_D, DV_D), s_fin[None],
            mk.reshape(1, b, N_MEM, H_X, HD_X), mv.reshape(1, b, N_MEM, H_X, HD_X),
            kd_s.reshape(1, bs, ts, H_D, 2, DK_D), vd_s.reshape(1, bs, ts, H_D, DV_D), s_new[None])
```

```python
import functools
import math

import jax
import jax.numpy as jnp
from jax import lax
from jax.experimental import pallas as pl
from jax.experimental.pallas import tpu as pltpu
from jax.experimental.pallas import tpu_sc as plsc

D_MODEL = 1024
CHUNK = 64
CHUNK_SHIFT = CHUNK.bit_length() - 1
assert 1 << CHUNK_SHIFT == CHUNK
H_D, DK_D, DV_D = 4, 64, 128
H_R, DK_R, DV_R = 4, 128, 128
N_MEM = 256
H_X = 4
HD_X = D_MODEL // H_X
PEER_HEADS = 8
N_KEYS = 128
N_EXPERTS = N_KEYS * N_KEYS
PEER_TOPK = 16
EPS = 1e-6
HEAD_W = 128
GROUP_W = 512
N_PAIRS = PEER_HEADS * PEER_TOPK
HALF = D_MODEL // 2
ROWS_PER_EXPERT = HALF // 128
VMEM_LIMIT = 56 * 1024 * 1024

BF16 = jnp.bfloat16
F32 = jnp.float32


def _cparams(sem):
    return pltpu.CompilerParams(dimension_semantics=sem, vmem_limit_bytes=VMEM_LIMIT)


def _rms(x, g):
    return x * lax.rsqrt(jnp.mean(x * x, axis=-1, keepdims=True) + EPS) * g


def _dot(a, b):
    return jnp.dot(a, b, preferred_element_type=F32)


def _dot_nt(a, b):
    return lax.dot_general(a, b, (((1,), (1,)), ((), ())), preferred_element_type=F32)


def _dot_tn(a, b):
    return lax.dot_general(a, b, (((0,), (0,)), ((), ())), preferred_element_type=F32)


def _select_by_head(h, values):
    out = jnp.float32(values[-1])
    for i in range(len(values) - 2, -1, -1):
        out = jnp.where(h == i, jnp.float32(values[i]), out)
    return out


def _in_proj_kernel(x_ref, g_ref, w_ref, *rest):
    qd_ref, kd_ref, vd_ref, kdb_ref, vdb_ref, qr_ref, kr_ref, vr_ref, gr_ref = rest[-9:]
    hb = _rms(x_ref[...], g_ref[...]).astype(BF16)
    col = lambda c: _dot(hb, w_ref[:, c * GROUP_W:(c + 1) * GROUP_W])
    qd_ref[...] = (col(0) * (DK_D ** -0.5)).astype(BF16)
    kd = col(1)
    kd_ref[...] = kd.reshape(kd_ref.shape)
    kdb_ref[...] = kd.astype(BF16)
    vd = col(2)
    vd_ref[...] = vd.reshape(vd_ref.shape)
    vdb_ref[...] = vd.astype(BF16)
    qr_ref[...] = col(3).astype(BF16)
    kr_ref[...] = (col(4) * (DK_R ** -0.5)).astype(BF16)
    vr_ref[...] = col(5).astype(BF16)
    gr_ref[...] = col(6)


def _in_proj(x2d, g, w_bf, tm, first_row=0, n_rows=None, kv_into=None):
    n_all = x2d.shape[0]
    n = n_all if n_rows is None else n_rows
    off = first_row // tm
    blk = lambda: pl.BlockSpec((tm, GROUP_W), lambda i: (i, 0))
    k_dims, v_dims = (H_D, 2, DK_D), (H_D, DV_D)
    blk_all = lambda dims: pl.BlockSpec((tm,) + dims, lambda i: (i + off,) + (0,) * len(dims))
    sh = lambda dt: jax.ShapeDtypeStruct((n, GROUP_W), dt)
    sh_all = lambda dims: jax.ShapeDtypeStruct((n_all,) + dims, F32)
    args = [x2d, g, w_bf]
    in_specs = [pl.BlockSpec((tm, D_MODEL), lambda i: (i + off, 0)),
                pl.BlockSpec((1, D_MODEL), lambda i: (0, 0)),
                pl.BlockSpec(w_bf.shape, lambda i: (0, 0))]
    aliases = {}
    if kv_into is not None:
        args += list(kv_into)
        in_specs += [pl.BlockSpec(memory_space=pl.ANY)] * 2
        aliases = {3: 1, 4: 2}
    return pl.pallas_call(
        _in_proj_kernel,
        grid=(n // tm,),
        in_specs=in_specs,
        out_specs=[blk(), blk_all(k_dims), blk_all(v_dims)] + [blk() for _ in range(6)],
        out_shape=[sh(BF16), sh_all(k_dims), sh_all(v_dims), sh(BF16), sh(BF16), sh(BF16), sh(BF16), sh(BF16), sh(F32)],
        input_output_aliases=aliases,
        compiler_params=_cparams(("parallel",)),
        name="in_proj",
    )(*args)


def _lambda_from(lam_ref, lam_init):
    l = lam_ref[...]
    a = jnp.exp(jnp.sum(l[0:1] * l[1:2], axis=-1, keepdims=True))
    b = jnp.exp(jnp.sum(l[2:3] * l[3:4], axis=-1, keepdims=True))
    return a - b + lam_init


def _diff_post(acc, l, lam, g, lam_init, tq):
    o = acc[:tq] / l[:tq] - lam * (acc[tq:] / l[tq:])
    return o * lax.rsqrt(jnp.mean(o * o, axis=-1, keepdims=True) + EPS) * g * (1.0 - lam_init)


def _split_maps(q):
    lane = lax.broadcasted_iota(jnp.int32, q.shape, 1)
    zero = jnp.zeros_like(q)
    return jnp.concatenate([jnp.where(lane < DK_D, q, zero), jnp.where(lane >= DK_D, q, zero)], axis=0)


def _da_prompt_kernel(lam_ref, q_ref, k_ref, v_ref, g_ref, o_ref, kx_ref, vx_ref, own_ref, acc_ref, m_ref, *, lam_init, tq, tk):
    h = pl.program_id(1)
    i = pl.program_id(2)
    t = k_ref.shape[0]
    slope = _select_by_head(h, [2.0 ** (-8.0 * (j + 1) / H_D) for j in range(H_D)])

    @pl.when(i == 0)
    def _():
        pos = lax.broadcasted_iota(jnp.int32, (t, HEAD_W), 0)
        lane = lax.broadcasted_iota(jnp.int32, (t, HEAD_W), 1)
        coarse = ((pos >> CHUNK_SHIFT) << CHUNK_SHIFT).astype(F32) * slope
        fine = (pos & (CHUNK - 1)).astype(F32) * slope
        kx_ref[:, :HEAD_W] = k_ref[...]
        kx_ref[:, HEAD_W:] = jnp.where(lane == 0, coarse, jnp.where(lane == 1, fine, 0.0)).astype(BF16)
        vx_ref[:DV_D, :] = v_ref[...].astype(F32).T.astype(BF16)
        vx_ref[DV_D:, :] = jnp.ones((vx_ref.shape[0] - DV_D, t), BF16)
        krel = lax.broadcasted_iota(jnp.int32, (tk, 2 * tq), 0)
        c = lax.broadcasted_iota(jnp.int32, (tk, 2 * tq), 1)
        for par in range(tk // tq):
            qrel = par * tq + jnp.where(c >= tq, c - tq, c)
            ahead = (2.0 * slope) * jnp.maximum(krel - qrel, 0).astype(F32)
            own_ref[par] = jnp.where((qrel >> CHUNK_SHIFT) >= (krel >> CHUNK_SHIFT), -ahead, -1e30)

    q = q_ref[...]
    lane = lax.broadcasted_iota(jnp.int32, q.shape, 1)
    zero = jnp.zeros_like(q)
    ones2 = jnp.where(lane < 2, 1.0, 0.0).astype(BF16)
    q2 = jnp.concatenate([jnp.concatenate([jnp.where(lane < DK_D, q, zero), ones2], axis=1),
                          jnp.concatenate([jnp.where(lane >= DK_D, q, zero), ones2], axis=1)], axis=0)
    jd = (i * tq) // tk

    def scores(j):
        return _dot_nt(kx_ref[pl.ds(pl.multiple_of(j * tk, tk), tk), :], q2)

    def values(j):
        return vx_ref[:, pl.ds(pl.multiple_of(j * tk, tk), tk)]

    s = scores(jd) + own_ref[(i * tq) % tk // tq]
    m0 = jnp.max(s, axis=0, keepdims=True)
    m_ref[...] = m0
    acc_ref[...] = _dot(values(jd), jnp.exp(s - m0).astype(BF16))

    def absorb(blocks):
        ss = [scores(j) for j in blocks]
        m_old = m_ref[...]
        m_new = m_old
        for s in ss:
            m_new = jnp.maximum(m_new, jnp.max(s, axis=0, keepdims=True))
        m_ref[...] = m_new
        acc = jnp.exp(m_old - m_new) * acc_ref[...]
        for j, s in zip(blocks, ss):
            acc = acc + _dot(values(j), jnp.exp(s - m_new).astype(BF16))
        acc_ref[...] = acc

    def past_pair(jj, carry):
        absorb([2 * jj, 2 * jj + 1])
        return carry

    lax.fori_loop(0, jd // 2, past_pair, 0)

    @pl.when(jd % 2 == 1)
    def _():
        absorb([jd - 1])

    acc = acc_ref[...]
    num, den = acc[:DV_D], acc[DV_D:DV_D + 1]
    lam = _lambda_from(lam_ref, lam_init)
    o = (num[:, :tq] / den[:, :tq] - lam * (num[:, tq:] / den[:, tq:])).T
    o = o * lax.rsqrt(jnp.mean(o * o, axis=-1, keepdims=True) + EPS) * g_ref[...] * (1.0 - lam_init)
    o_ref[...] = o.astype(o_ref.dtype)


ONES_ROWS = 16


def _diff_attn_prompt(lamp, q, k, v, g_da3, lam_init, tq, tk):
    b, t, _ = q.shape
    kern = functools.partial(_da_prompt_kernel, lam_init=lam_init, tq=tq, tk=tk)
    return pl.pallas_call(
        kern,
        grid=(b, H_D, t // tq),
        in_specs=[pl.BlockSpec((4, DK_D), lambda b_, h, i: (0, 0)),
                  pl.BlockSpec((None, tq, HEAD_W), lambda b_, h, i: (b_, i, h)),
                  pl.BlockSpec((None, t, HEAD_W), lambda b_, h, i: (b_, 0, h)),
                  pl.BlockSpec((None, t, HEAD_W), lambda b_, h, i: (b_, 0, h)),
                  pl.BlockSpec((None, 1, HEAD_W), lambda b_, h, i: (h, 0, 0))],
        out_specs=pl.BlockSpec((None, tq, HEAD_W), lambda b_, h, i: (b_, i, h)),
        out_shape=jax.ShapeDtypeStruct((b, t, GROUP_W), BF16),
        scratch_shapes=[pltpu.VMEM((t, 2 * HEAD_W), BF16), pltpu.VMEM((DV_D + ONES_ROWS, t), BF16),
                        pltpu.VMEM((tk // tq, tk, 2 * tq), F32),
                        pltpu.VMEM((DV_D + ONES_ROWS, 2 * tq), F32), pltpu.VMEM((1, 2 * tq), F32)],
        compiler_params=_cparams(("parallel", "parallel", "arbitrary")),
        name="diff_attn_prompt",
    )(lamp, q, k, v, g_da3)


def _da_sample_kernel(lam_ref, q_ref, kc_ref, vc_ref, kn_ref, vn_ref, g_ref, o_ref, *, lam_init, ts, past_len):
    h = pl.program_id(1)
    slope = _select_by_head(h, [2.0 ** (-8.0 * (j + 1) / H_D) for j in range(H_D)])
    q2 = _split_maps(q_ref[...])
    rows = 2 * ts

    def scores(k, base, n):
        r = lax.broadcasted_iota(jnp.int32, (rows, n), 0)
        qpos = past_len + jnp.where(r >= ts, r - ts, r)
        kpos = base + lax.broadcasted_iota(jnp.int32, (rows, n), 1)
        return _dot_nt(q2, k) - slope * jnp.abs(qpos - kpos).astype(F32)

    sc = scores(kc_ref[...].astype(BF16), 0, past_len)
    sn = scores(kn_ref[...], past_len, ts)
    m = jnp.maximum(jnp.max(sc, axis=-1, keepdims=True), jnp.max(sn, axis=-1, keepdims=True))
    pc = jnp.exp(sc - m)
    pn = jnp.exp(sn - m)
    l = jnp.sum(pc, axis=-1, keepdims=True) + jnp.sum(pn, axis=-1, keepdims=True)
    acc = _dot(pc.astype(BF16), vc_ref[...].astype(BF16)) + _dot(pn.astype(BF16), vn_ref[...])
    lam = _lambda_from(lam_ref, lam_init)
    o_ref[...] = _diff_post(acc, l, lam, g_ref[...], lam_init, ts).astype(o_ref.dtype)


def _diff_attn_sample(lamp, q, kc, vc, kn, vn, g_da3, lam_init):
    b, ts, _ = q.shape
    past_len = kc.shape[1]
    kern = functools.partial(_da_sample_kernel, lam_init=lam_init, ts=ts, past_len=past_len)
    head = lambda rows: pl.BlockSpec((None, rows, HEAD_W), lambda b_, h: (b_, 0, h))
    return pl.pallas_call(
        kern,
        grid=(b, H_D),
        in_specs=[pl.BlockSpec((4, DK_D), lambda b_, h: (0, 0)),
                  head(ts), head(past_len), head(past_len), head(ts), head(ts),
                  pl.BlockSpec((None, 1, HEAD_W), lambda b_, h: (h, 0, 0))],
        out_specs=head(ts),
        out_shape=jax.ShapeDtypeStruct((b, ts, GROUP_W), BF16),
        compiler_params=_cparams(("parallel", "parallel")),
        name="diff_attn_sample",
    )(lamp, q, kc, vc, kn, vn, g_da3)


def _ret_kernel(*refs, lb, has_init):
    if has_init:
        q_ref, k_ref, v_ref, gate_ref, g_ref, s0_ref, o_ref, sfin_ref, s_ref = refs
    else:
        q_ref, k_ref, v_ref, gate_ref, g_ref, o_ref, sfin_ref, s_ref = refs
    h = pl.program_id(1)
    c = pl.program_id(2)
    lg = _select_by_head(h, [math.log1p(-(2.0 ** (-5.0 - j))) for j in range(H_R)])

    @pl.when(c == 0)
    def _():
        s_ref[...] = s0_ref[...] if has_init else jnp.zeros_like(s_ref)

    q, k, v = q_ref[...], k_ref[...], v_ref[...]
    i = lax.broadcasted_iota(jnp.int32, (lb, lb), 0)
    j = lax.broadcasted_iota(jnp.int32, (lb, lb), 1)
    d = (i - j).astype(F32)
    decay = jnp.where(d >= 0, jnp.exp(jnp.maximum(d, 0.0) * lg), 0.0)
    inner = _dot_nt(q, k) * decay
    ic = lax.broadcasted_iota(jnp.int32, (lb, 1), 0).astype(F32)
    s_old = s_ref[...]
    o = _dot(inner.astype(BF16), v) + _dot(q, s_old.astype(BF16)) * jnp.exp((ic + 1.0) * lg)
    tail = jnp.exp((lb - 1.0 - ic) * lg)
    kt = (k.astype(F32) * tail).astype(BF16)
    s_new = jnp.exp(lb * lg) * s_old + _dot_tn(kt, v)
    s_ref[...] = s_new

    @pl.when(c == pl.num_programs(2) - 1)
    def _():
        sfin_ref[...] = s_new

    oc = o - jnp.mean(o, axis=-1, keepdims=True)
    y = oc * lax.rsqrt(jnp.mean(oc * oc, axis=-1, keepdims=True) + EPS) * g_ref[...]
    gate = gate_ref[...]
    o_ref[...] = (y * (gate * jax.nn.sigmoid(gate))).astype(o_ref.dtype)


def _retention(q, k, v, gate, g_ret3, s0, lb):
    b, t, _ = q.shape
    has_init = s0 is not None
    kern = functools.partial(_ret_kernel, lb=lb, has_init=has_init)
    head = lambda: pl.BlockSpec((None, lb, HEAD_W), lambda b_, h, c: (b_, c, h))
    state = lambda: pl.BlockSpec((None, None, DK_R, DV_R), lambda b_, h, c: (b_, h, 0, 0))
    in_specs = [head(), head(), head(), head(), pl.BlockSpec((None, 1, HEAD_W), lambda b_, h, c: (h, 0, 0))]
    args = [q, k, v, gate, g_ret3]
    if has_init:
        in_specs.append(state())
        args.append(s0)
    return pl.pallas_call(
        kern,
        grid=(b, H_R, t // lb),
        in_specs=in_specs,
        out_specs=[head(), state()],
        out_shape=[jax.ShapeDtypeStruct((b, t, GROUP_W), BF16), jax.ShapeDtypeStruct((b, H_R, DK_R, DV_R), F32)],
        scratch_shapes=[pltpu.VMEM((DK_R, DV_R), F32)],
        compiler_params=_cparams(("parallel", "parallel", "arbitrary")),
        name="retention",
    )(*args)


def _out_proj_kernel(x_ref, mda_ref, mret_ref, wo_ref, g_ref, wq_ref, x1_ref, qx_ref):
    x1 = x_ref[...] + _dot(mda_ref[...], wo_ref[:GROUP_W, :]) + _dot(mret_ref[...], wo_ref[GROUP_W:, :])
    x1_ref[...] = x1
    hn = _rms(x1, g_ref[...]).astype(BF16)
    qx_ref[...] = (_dot(hn, wq_ref[...]) * (HD_X ** -0.5)).astype(BF16)


def _out_proj(x2d, mda, mret, wo_bf, g_cross, wq_bf, tm, first_row=0):
    n = mda.shape[0]
    off = first_row // tm
    full = lambda a: pl.BlockSpec(a.shape, lambda i: (0, 0))
    return pl.pallas_call(
        _out_proj_kernel,
        grid=(n // tm,),
        in_specs=[pl.BlockSpec((tm, D_MODEL), lambda i: (i + off, 0)),
                  pl.BlockSpec((tm, GROUP_W), lambda i: (i, 0)),
                  pl.BlockSpec((tm, GROUP_W), lambda i: (i, 0)),
                  full(wo_bf), full(g_cross), full(wq_bf)],
        out_specs=[pl.BlockSpec((tm, D_MODEL), lambda i: (i, 0)), pl.BlockSpec((tm, D_MODEL), lambda i: (i, 0))],
        out_shape=[jax.ShapeDtypeStruct((n, D_MODEL), F32), jax.ShapeDtypeStruct((n, D_MODEL), BF16)],
        compiler_params=_cparams(("parallel",)),
        name="out_proj",
    )(x2d, mda, mret, wo_bf, g_cross, wq_bf)


def _mem_kv_kernel(m_ref, g_ref, wk_ref, wv_ref, mk_ref, mv_ref, mkb_ref, mvb_ref):
    mn = _rms(m_ref[...], g_ref[...]).astype(BF16)
    mk = _dot(mn, wk_ref[...])
    mv = _dot(mn, wv_ref[...])
    mk_ref[...] = mk
    mv_ref[...] = mv
    mkb_ref[...] = mk.astype(BF16)
    mvb_ref[...] = mv.astype(BF16)


def _mem_kv(mem2d, g_mem, wk_bf, wv_bf, tm):
    n = mem2d.shape[0]
    row = lambda: pl.BlockSpec((tm, D_MODEL), lambda i: (i, 0))
    full = lambda a: pl.BlockSpec(a.shape, lambda i: (0, 0))
    sh = lambda dt: jax.ShapeDtypeStruct((n, D_MODEL), dt)
    return pl.pallas_call(
        _mem_kv_kernel,
        grid=(n // tm,),
        in_specs=[row(), full(g_mem), full(wk_bf), full(wv_bf)],
        out_specs=[row(), row(), row(), row()],
        out_shape=[sh(F32), sh(F32), sh(BF16), sh(BF16)],
        compiler_params=_cparams(("parallel",)),
        name="mem_kv",
    )(mem2d, g_mem, wk_bf, wv_bf)


def _cross_kernel(x1_ref, qx_ref, mk_ref, mv_ref, wo_ref, g_ref, wpq_ref, x2_ref, h3_ref, qp_ref):
    q = qx_ref[...]
    heads = []
    for h in range(H_X):
        sl = slice(h * HD_X, (h + 1) * HD_X)
        kh = mk_ref[:, sl] if mk_ref.dtype == BF16 else mk_ref[:, sl].astype(BF16)
        vh = mv_ref[:, sl] if mv_ref.dtype == BF16 else mv_ref[:, sl].astype(BF16)
        s = _dot_nt(q[:, sl], kh)
        p = jnp.exp(s - jnp.max(s, axis=-1, keepdims=True))
        heads.append(_dot(p.astype(BF16), vh) / jnp.sum(p, axis=-1, keepdims=True))
    o = jnp.concatenate(heads, axis=-1).astype(BF16)
    x2 = x1_ref[...] + _dot(o, wo_ref[...])
    x2_ref[...] = x2
    h3 = _rms(x2, g_ref[...])
    h3_ref[...] = h3
    qp_ref[...] = _dot(h3.astype(BF16), wpq_ref[...]).astype(BF16)


def _cross(x1, qx, mk, mv, wo_bf, g_ffn, wpq_bf, tm):
    b, t, _ = x1.shape
    row = lambda w: pl.BlockSpec((None, tm, w), lambda b_, i: (b_, i, 0))
    mem = lambda: pl.BlockSpec((None, N_MEM, D_MODEL), lambda b_, i: (b_, 0, 0))
    full = lambda a: pl.BlockSpec(a.shape, lambda b_, i: (0, 0))
    dq = wpq_bf.shape[1]
    return pl.pallas_call(
        _cross_kernel,
        grid=(b, t // tm),
        in_specs=[row(D_MODEL), row(D_MODEL), mem(), mem(), full(wo_bf), full(g_ffn), full(wpq_bf)],
        out_specs=[row(D_MODEL), row(D_MODEL), row(dq)],
        out_shape=[jax.ShapeDtypeStruct((b, t, D_MODEL), F32), jax.ShapeDtypeStruct((b, t, D_MODEL), F32),
                   jax.ShapeDtypeStruct((b, t, dq), BF16)],
        compiler_params=_cparams(("parallel", "parallel")),
        name="cross_attn",
    )(x1, qx, mk, mv, wo_bf, g_ffn, wpq_bf)


ID_PAD = 2.0 ** 29


def _topk_rows(s, ids):
    vals, sel = [], []
    for _ in range(PEER_TOPK):
        m = jnp.max(s, axis=0, keepdims=True)
        idx = jnp.min(jnp.where(s == m, ids, ID_PAD), axis=0, keepdims=True)
        vals.append(m)
        sel.append(idx)
        s = jnp.where(ids == idx, -jnp.inf, s)
    return jnp.concatenate(vals, axis=0), jnp.concatenate(sel, axis=0)


def _candidates(v1, i1, v2, i2):
    lanes = v1.shape[1]
    b8 = lax.broadcasted_iota(jnp.int32, (8, lanes), 0)
    ident = lambda a, ia, ib, b: (a * PEER_TOPK + b) * float(N_EXPERTS) + (ia * float(N_KEYS) + ib)
    b16 = lax.broadcasted_iota(jnp.int32, (PEER_TOPK, lanes), 0).astype(F32)
    vals = [v1[0:1] + v2]
    ids = [ident(0, i1[0:1], i2, b16)]
    for a in range(1, 8):
        keep = b8 < PEER_TOPK // (a + 1)
        vals.append(jnp.where(keep, v1[a:a + 1] + v2[0:8], -jnp.inf))
        ids.append(jnp.where(keep, ident(a, i1[a:a + 1], i2[0:8], b8.astype(F32)), ID_PAD))
    a_hi = (b8 + 8).astype(F32)
    vals.append(v1[8:16] + v2[0:1])
    ids.append(ident(a_hi, i1[8:16], i2[0:1], 0.0))
    return jnp.concatenate(vals, axis=0), jnp.concatenate(ids, axis=0)


def _route_kernel(qp_ref, k1_ref, k2_ref, e_ref, g_ref):
    half = N_KEYS
    tt = qp_ref.shape[0]
    key_id = lax.broadcasted_iota(jnp.int32, (N_KEYS, tt), 0).astype(F32)
    es, gs = [], []
    for p in range(PEER_HEADS):
        q1 = qp_ref[:, (2 * p) * half:(2 * p + 1) * half]
        q2 = qp_ref[:, (2 * p + 1) * half:(2 * p + 2) * half]
        v1, i1 = _topk_rows(_dot_nt(k1_ref[p], q1), key_id)
        v2, i2 = _topk_rows(_dot_nt(k2_ref[p], q2), key_id)
        sc, sel = _topk_rows(*_candidates(v1, i1, v2, i2))
        w = jnp.exp(sc - sc[0:1])
        es.append((sel.astype(jnp.int32) & (N_EXPERTS - 1)) * ROWS_PER_EXPERT)
        gs.append(w / jnp.sum(w, axis=0, keepdims=True))
    e_ref[...] = jnp.concatenate(es, axis=0).T
    g_ref[...] = jnp.concatenate(gs, axis=0).T


def _route_extra_kernel(qp_ref, k1_ref, k2_ref, *rest):
    _route_kernel(qp_ref, k1_ref, k2_ref, rest[-2], rest[-1])


def _route(qp2d, k1_bf, k2_bf, tt, first_token=0, n_tokens=None, into=None, after=()):
    n = qp2d.shape[0]
    n_tokens = n if n_tokens is None else n_tokens
    off = first_token // tt
    full = lambda a: pl.BlockSpec(a.shape, lambda i: (0, 0, 0))
    pair = lambda: pl.BlockSpec((tt, N_PAIRS), lambda i: (i + off, 0))
    in_specs = [pl.BlockSpec((tt, qp2d.shape[1]), lambda i: (i + off, 0)), full(k1_bf), full(k2_bf)]
    args = [qp2d, k1_bf, k2_bf]
    aliases = {}
    if into is not None:
        in_specs += [pl.BlockSpec(memory_space=pl.ANY), pl.BlockSpec(memory_space=pl.ANY)]
        args += list(into)
        aliases = {3: 0, 4: 1}
    in_specs += [pl.BlockSpec(memory_space=pl.ANY)] * len(after)
    args += list(after)
    return pl.pallas_call(
        _route_kernel if len(args) == 3 else _route_extra_kernel,
        grid=(n_tokens // tt,),
        in_specs=in_specs,
        out_specs=[pair(), pair()],
        out_shape=[jax.ShapeDtypeStruct((n, N_PAIRS), jnp.int32), jax.ShapeDtypeStruct((n, N_PAIRS), F32)],
        input_output_aliases=aliases,
        compiler_params=_cparams(("parallel",)),
        name="peer_route",
    )(*args)


def _unpack(words):
    hi = lax.bitcast_convert_type(words & jnp.uint32(0xFFFF0000), F32)
    lo = lax.bitcast_convert_type(words << 16, F32)
    return hi, lo


def _expert_rows(tab_ref, row0):
    return tab_ref[pl.ds(pl.multiple_of(row0, ROWS_PER_EXPERT), ROWS_PER_EXPERT), :]


REDUCE_TOKENS = 16


def _lane_sums_to_rows(y, n_tok):
    hi = y.astype(BF16)
    lo = (y - hi.astype(F32)).astype(BF16)
    ones = jnp.ones((128, N_PAIRS), BF16)
    s = (_dot(hi, ones) + _dot(lo, ones)).reshape(n_tok, N_PAIRS, N_PAIRS)
    eye = lax.broadcasted_iota(jnp.int32, (N_PAIRS, N_PAIRS), 0) == lax.broadcasted_iota(jnp.int32, (N_PAIRS, N_PAIRS), 1)
    return jnp.sum(jnp.where(eye[None], s, 0.0), axis=1)


CHUNKS = D_MODEL // 128


def _peer_u_kernel(e_ref, h_ref, gate_ref, tab_ref, w_ref, h8_ref, prod_ref, ys_ref, act_ref, *, tt):
    for c in range(CHUNKS):
        h8_ref[pl.ds(c, tt, stride=CHUNKS), :] = h_ref[:, c * 128:(c + 1) * 128]

    def token(t, slot):
        base = pl.multiple_of(t * CHUNKS, CHUNKS)
        ha = h8_ref[pl.ds(base, ROWS_PER_EXPERT), :]
        hb = h8_ref[pl.ds(pl.multiple_of(base + ROWS_PER_EXPERT, ROWS_PER_EXPERT), ROWS_PER_EXPERT), :]
        prod = prod_ref.at[slot]
        for k in range(N_PAIRS):
            hi, lo = _unpack(_expert_rows(tab_ref, e_ref[t, k]))
            prod[k * ROWS_PER_EXPERT:(k + 1) * ROWS_PER_EXPERT, :] = hi * ha + lo * hb
        y = prod[pl.ds(0, N_PAIRS, stride=ROWS_PER_EXPERT), :]
        for c in range(1, ROWS_PER_EXPERT):
            y = y + prod[pl.ds(c, N_PAIRS, stride=ROWS_PER_EXPERT), :]
        ys_ref[pl.ds(pl.multiple_of(t * N_PAIRS, N_PAIRS), N_PAIRS), :] = y

    def two_tokens(j, carry):
        token(2 * j, 0)
        token(2 * j + 1, 1)
        return carry

    lax.fori_loop(0, tt // 2, two_tokens, 0)

    def group(g, carry):
        rows = REDUCE_TOKENS * N_PAIRS
        y = ys_ref[pl.ds(pl.multiple_of(g * rows, rows), rows), :]
        act_ref[pl.ds(pl.multiple_of(g * REDUCE_TOKENS, REDUCE_TOKENS), REDUCE_TOKENS), :] = _lane_sums_to_rows(y, REDUCE_TOKENS)
        return carry

    lax.fori_loop(0, tt // REDUCE_TOKENS, group, 0)
    w_ref[...] = _gated_gelu(gate_ref[...], act_ref[...])


def _gated_gelu(gate, a):
    return gate * (0.5 * a * (1.0 + lax.erf(a * (2.0 ** -0.5))))


def _gate_act_kernel(act_ref, gate_ref, *rest):
    rest[-1][...] = _gated_gelu(gate_ref[...], act_ref[...])


def _gate_act(act, gate_t, w_t, first_token, tt, after=()):
    off = first_token // tt
    return pl.pallas_call(
        _gate_act_kernel,
        grid=(act.shape[0] // tt,),
        in_specs=[pl.BlockSpec((tt, N_PAIRS), lambda i: (i, 0)),
                  pl.BlockSpec((tt, N_PAIRS), lambda i: (i + off, 0))]
                 + [pl.BlockSpec(memory_space=pl.ANY)] * (1 + len(after)),
        out_specs=pl.BlockSpec((tt, N_PAIRS), lambda i: (i + off, 0)),
        out_shape=jax.ShapeDtypeStruct(w_t.shape, F32),
        input_output_aliases={2: 0},
        compiler_params=_cparams(("arbitrary",)),
        name="peer_gate_act",
    )(act, gate_t, w_t, *after)


def _peer_u(e_t, h, gate_t, tab, tt, n_tokens):
    n = e_t.shape[0]
    assert tt % REDUCE_TOKENS == 0
    kern = functools.partial(_peer_u_kernel, tt=tt)
    pair = lambda **kw: pl.BlockSpec((tt, N_PAIRS), lambda i: (i, 0), **kw)
    return pl.pallas_call(
        kern,
        grid=(n_tokens // tt,),
        in_specs=[pair(memory_space=pltpu.SMEM),
                  pl.BlockSpec((tt, D_MODEL), lambda i: (i, 0)),
                  pair(),
                  pl.BlockSpec(tab.shape, lambda i: (0, 0), pipeline_mode=pl.Buffered(1))],
        out_specs=pair(),
        out_shape=jax.ShapeDtypeStruct((n, N_PAIRS), F32),
        scratch_shapes=[pltpu.VMEM((tt * CHUNKS, 128), F32),
                        pltpu.VMEM((2, N_PAIRS * ROWS_PER_EXPERT, 128), F32),
                        pltpu.VMEM((tt * N_PAIRS, 128), F32),
                        pltpu.VMEM((tt, N_PAIRS), F32)],
        compiler_params=_cparams(("arbitrary",)),
        name="peer_u",
    )(e_t, h, gate_t, tab)


def _peer_v_kernel(e_ref, w_ref, x_ref, g_ref, tab_ref, o_ref, acc_ref, *, tt):
    n_acc = 2

    def token(t):
        acc_hi = [jnp.zeros((ROWS_PER_EXPERT, 128), F32) for _ in range(n_acc)]
        acc_lo = [jnp.zeros((ROWS_PER_EXPERT, 128), F32) for _ in range(n_acc)]
        for k in range(N_PAIRS):
            hi, lo = _unpack(_expert_rows(tab_ref, e_ref[t, k]))
            w = w_ref[t, k]
            acc_hi[k % n_acc] = acc_hi[k % n_acc] + w * hi
            acc_lo[k % n_acc] = acc_lo[k % n_acc] + w * lo
        base = pl.multiple_of(t * CHUNKS, CHUNKS)
        acc_ref[pl.ds(base, ROWS_PER_EXPERT), :] = sum(acc_hi[1:], acc_hi[0])
        acc_ref[pl.ds(pl.multiple_of(base + ROWS_PER_EXPERT, ROWS_PER_EXPERT), ROWS_PER_EXPERT), :] = sum(acc_lo[1:], acc_lo[0])

    def two_tokens(j, carry):
        token(2 * j)
        token(2 * j + 1)
        return carry

    lax.fori_loop(0, tt // 2, two_tokens, 0)
    _residual_rms_store(x_ref, acc_ref, g_ref, o_ref, tt)


def _residual_rms_store(x_ref, add_ref, g_ref, o_ref, tt):
    xs = [x_ref[:, c * 128:(c + 1) * 128] + add_ref[pl.ds(c, tt, stride=CHUNKS), :] for c in range(CHUNKS)]
    sq = xs[0] * xs[0]
    for x in xs[1:]:
        sq = sq + x * x
    r = lax.rsqrt(jnp.sum(sq, axis=1, keepdims=True) * (1.0 / D_MODEL) + EPS)
    for c in range(CHUNKS):
        o_ref[:, c * 128:(c + 1) * 128] = xs[c] * r * g_ref[:, c * 128:(c + 1) * 128]


def _peer_v_into_kernel(e_ref, w_ref, x_ref, g_ref, tab_ref, y_any_ref, o_ref, acc_ref, *, tt):
    del y_any_ref
    _peer_v_kernel(e_ref, w_ref, x_ref, g_ref, tab_ref, o_ref, acc_ref, tt=tt)


def _peer_v(e_t, w_t, x, g_final, tab, tt, n_tokens, into=None):
    pair = lambda: pl.BlockSpec((tt, N_PAIRS), lambda i: (i, 0), memory_space=pltpu.SMEM)
    row = lambda: pl.BlockSpec((tt, D_MODEL), lambda i: (i, 0))
    in_specs = [pair(), pair(), row(),
                pl.BlockSpec((1, D_MODEL), lambda i: (0, 0)),
                pl.BlockSpec(tab.shape, lambda i: (0, 0), pipeline_mode=pl.Buffered(1))]
    args = [e_t, w_t, x, g_final, tab]
    if into is not None:
        in_specs.append(pl.BlockSpec(memory_space=pl.ANY))
        args.append(into)
    return pl.pallas_call(
        functools.partial(_peer_v_kernel if into is None else _peer_v_into_kernel, tt=tt),
        grid=(n_tokens // tt,),
        in_specs=in_specs,
        out_specs=row(),
        out_shape=jax.ShapeDtypeStruct(x.shape if into is None else into.shape, F32),
        input_output_aliases={} if into is None else {5: 0},
        scratch_shapes=[pltpu.VMEM((tt * CHUNKS, 128), F32)],
        compiler_params=_cparams(("arbitrary",)),
        name="peer_v",
    )(*args)


def _residual_norm_kernel(x_ref, add_ref, g_ref, *rest, tt):
    _residual_rms_store(x_ref, add_ref, g_ref, rest[-1], tt)


def _residual_norm(x, add8, g_final, y, x_first, y_first, tt, out_rows=None, after=()):
    n_rows = add8.shape[0] // CHUNKS
    x_off, y_off = x_first // tt, y_first // tt
    extra = ([] if y is None else [y]) + list(after)
    return pl.pallas_call(
        functools.partial(_residual_norm_kernel, tt=tt),
        grid=(n_rows // tt,),
        in_specs=[pl.BlockSpec((tt, D_MODEL), lambda i: (i + x_off, 0)),
                  pl.BlockSpec((tt * CHUNKS, 128), lambda i: (i, 0)),
                  pl.BlockSpec((1, D_MODEL), lambda i: (0, 0))]
                 + [pl.BlockSpec(memory_space=pl.ANY)] * len(extra),
        out_specs=pl.BlockSpec((tt, D_MODEL), lambda i: (i + y_off, 0)),
        out_shape=jax.ShapeDtypeStruct((out_rows, D_MODEL) if y is None else y.shape, F32),
        input_output_aliases={} if y is None else {3: 0},
        compiler_params=_cparams(("arbitrary",)),
        name="peer_residual_norm",
    )(x, add8, g_final, *extra)


SC_CORES = 2
SC_SUBCORES = 16
SC_LANES = 16
WORD_ROWS = 128 // SC_LANES
SC_TOKEN_CHUNK = 32
SC_RING = 4


def _peer_sc(e_t, aux, tab, down):
    n = e_t.shape[0]
    workers = SC_CORES * SC_SUBCORES
    assert n % (workers * SC_TOKEN_CHUNK) == 0 and PEER_HEADS % SC_RING == 0
    per = n // workers
    head_rows = ROWS_PER_EXPERT * PEER_TOPK
    aux_rows = CHUNKS if down else 1
    out_rows = 1 if down else CHUNKS
    mesh = plsc.VectorSubcoreMesh(core_axis_name="core", subcore_axis_name="subcore",
                                  num_cores=SC_CORES, num_subcores=SC_SUBCORES)

    @pl.kernel(out_type=jax.ShapeDtypeStruct((n * out_rows, 128), F32), mesh=mesh,
               scratch_types=[pltpu.VMEM((SC_TOKEN_CHUNK, N_PAIRS), jnp.int32),
                              pltpu.VMEM((SC_TOKEN_CHUNK * aux_rows, 128), F32),
                              pltpu.VMEM((SC_RING, head_rows, 128), jnp.uint32),
                              pltpu.VMEM((SC_TOKEN_CHUNK * out_rows, 128), F32),
                              pltpu.VMEM((PEER_TOPK, SC_LANES), F32),
                              pltpu.SemaphoreType.DMA((SC_RING,))],
               compiler_params=pltpu.CompilerParams(needs_layout_passes=False),
               name="peer_u_sc" if down else "peer_v_sc")
    def body(e_hbm, aux_hbm, tab_hbm, o_hbm, e_v, aux_v, rows_v, out_v, fold_v, sems):
        wid = lax.axis_index("core") * SC_SUBCORES + lax.axis_index("subcore")
        lane = lax.broadcasted_iota(jnp.int32, (SC_LANES,), 0)

        def gathers(i, p, slot):
            first = e_v[i, pl.ds(p * PEER_TOPK, PEER_TOPK)]
            return [pltpu.make_async_copy(tab_hbm.at[first + r],
                                          rows_v.at[slot, pl.ds(r * PEER_TOPK, PEER_TOPK)], sems.at[slot])
                    for r in range(ROWS_PER_EXPERT)]

        def start_gather(i, p, slot):
            for d in gathers(i, p, slot):
                d.start()

        def weighted_sum(i, p, slot):
            ws = [plsc.load_gather(aux_v, [lane * 0 + i, lane * 0 + (p * PEER_TOPK + k)])
                  for k in range(PEER_TOPK)]
            for r in range(ROWS_PER_EXPERT):
                @pl.loop(0, WORD_ROWS)
                def _(j):
                    sl = pl.ds(j * SC_LANES, SC_LANES)
                    if p == 0:
                        a_hi = jnp.zeros((SC_LANES,), F32)
                        a_lo = jnp.zeros((SC_LANES,), F32)
                    else:
                        a_hi = out_v[i * CHUNKS + r, sl]
                        a_lo = out_v[i * CHUNKS + ROWS_PER_EXPERT + r, sl]
                    for k in range(PEER_TOPK):
                        hi, lo = _unpack(rows_v[slot, r * PEER_TOPK + k, sl])
                        a_hi = a_hi + ws[k] * hi
                        a_lo = a_lo + ws[k] * lo
                    out_v[i * CHUNKS + r, sl] = a_hi
                    out_v[i * CHUNKS + ROWS_PER_EXPERT + r, sl] = a_lo

        def pair_dots(i, p, slot):
            accs = tuple(jnp.zeros((SC_LANES,), F32) for _ in range(PEER_TOPK))
            for r in range(ROWS_PER_EXPERT):
                def piece(j, accs, r=r):
                    sl = pl.ds(j * SC_LANES, SC_LANES)
                    ha = aux_v[i * CHUNKS + r, sl]
                    hb = aux_v[i * CHUNKS + ROWS_PER_EXPERT + r, sl]
                    out = []
                    for k in range(PEER_TOPK):
                        hi, lo = _unpack(rows_v[slot, r * PEER_TOPK + k, sl])
                        out.append(accs[k] + (hi * ha + lo * hb))
                    return tuple(out)
                accs = lax.fori_loop(0, WORD_ROWS, piece, accs)
            for k in range(PEER_TOPK):
                fold_v[k, :] = accs[k]
            tot = plsc.load_gather(fold_v, [lane, lane * 0])
            for l in range(1, SC_LANES):
                tot = tot + plsc.load_gather(fold_v, [lane, lane * 0 + l])
            out_v[i, pl.ds(p * PEER_TOPK, PEER_TOPK)] = tot

        consume = pair_dots if down else weighted_sum

        @pl.loop(0, per // SC_TOKEN_CHUNK)
        def _(c):
            t0 = wid * per + c * SC_TOKEN_CHUNK
            pltpu.sync_copy(e_hbm.at[pl.ds(t0, SC_TOKEN_CHUNK)], e_v)
            pltpu.sync_copy(aux_hbm.at[pl.ds(t0 * aux_rows, SC_TOKEN_CHUNK * aux_rows)], aux_v)
            for p in range(SC_RING):
                start_gather(0, p, p)

            @pl.loop(0, SC_TOKEN_CHUNK)
            def _(i):
                for p in range(PEER_HEADS):
                    slot = p % SC_RING
                    for d in gathers(i, p, slot):
                        d.wait()
                    consume(i, p, slot)
                    if p + SC_RING < PEER_HEADS:
                        start_gather(i, p + SC_RING, slot)
                    else:
                        @pl.when(i + 1 < SC_TOKEN_CHUNK)
                        def _():
                            start_gather(i + 1, p + SC_RING - PEER_HEADS, slot)

            pltpu.sync_copy(out_v, o_hbm.at[pl.ds(t0 * out_rows, SC_TOKEN_CHUNK * out_rows)])

    return body(e_t, aux, tab)


def _pack_table(tab):
    bits = lax.bitcast_convert_type(tab.astype(BF16), jnp.uint16).astype(jnp.uint32)
    words = (bits[:, :HALF] << 16) | bits[:, HALF:]
    return words.reshape(tab.shape[0] * ROWS_PER_EXPERT, 128)


PEER_TOKENS = 128
SC_LEAD_BATCHES = (1, 3)
SC_SHARE_DOWN = (3, 8)
SC_SHARE_UP = (19, 48)
SC_MIN_TOKENS = 4096


def _sc_tokens(n):
    if n < SC_MIN_TOKENS:
        return (0, 0)
    unit = math.lcm(PEER_TOKENS, SC_CORES * SC_SUBCORES * SC_TOKEN_CHUNK)
    return tuple(n * num // den // unit * unit for num, den in (SC_SHARE_DOWN, SC_SHARE_UP))


def _row_tile(n, pref):
    while n % pref:
        pref //= 2
    return pref


def _peer_on_sc_down(h3, qp, k1_bf, k2_bf, tab_u, tt, route_after=()):
    n = h3.shape[0]
    e_t, gate_t = _route(qp, k1_bf, k2_bf, tt, after=route_after)
    return e_t, gate_t, _peer_sc(e_t, h3.reshape(n * CHUNKS, 128), tab_u, down=True)


def _peer_on_sc_up(e_t, gate_t, act, tab_v, tt, gate_after):
    w_t = _gate_act(act, gate_t, act, 0, tt, after=gate_after)
    return _peer_sc(e_t, w_t, tab_v, down=False), w_t


def _peer_and_final(x2, h3, qp, k1_bf, k2_bf, tab_u, tab_v, g_final, tt, n_sc=(0, 0), route_after=(), out_rows=None,
                    lead=None, y_start=None):
    n = x2.shape[0]
    n_sc_down, n_sc_up = n_sc
    tc_down, tc_up = n - n_sc_down, n - n_sc_up
    if n_sc_down:
        e_t, gate_t = _route(qp, k1_bf, k2_bf, tt, tc_down, n_sc_down, after=route_after)
        act = _peer_sc(e_t[tc_down:], h3[tc_down:].reshape(n_sc_down * CHUNKS, 128), tab_u, down=True)
        e_t, gate_t = _route(qp, k1_bf, k2_bf, tt, 0, tc_down, into=(e_t, gate_t))
    else:
        e_t, gate_t = _route(qp, k1_bf, k2_bf, tt)
    w_t = _peer_u(e_t, h3, gate_t, tab_u, tt, tc_down)
    y = y_start
    if lead is not None:
        x2_lead, add8_lead, y_first = lead
        y = _residual_norm(x2_lead, add8_lead, g_final, y, 0, y_first, tt, out_rows=out_rows, after=(w_t,))
    if n_sc_down:
        w_t = _gate_act(act, gate_t, w_t, tc_down, tt, after=() if y is None else (y,))
    y = _peer_v(e_t, w_t, x2, g_final, tab_v, tt, tc_up, into=y)
    if n_sc_up:
        add8 = _peer_sc(e_t[tc_up:], w_t[tc_up:], tab_v, down=False)
        y = _residual_norm(x2, add8, g_final, y, tc_up, tc_up, tt)
    return y


def kernel(x_prompt, x_sample, mem_prompt, cache_da_k, cache_da_v, state_ret, cache_mem_k, cache_mem_v, g_mix, w_in, lam_q1, lam_k1, lam_q2, lam_k2, g_da, g_ret, w_out, g_cross, g_mem, w_xq, w_xk, w_xv, w_xo, g_ffn, w_pq, peer_k1, peer_k2, peer_u, peer_v, g_final):
    depth = w_in.shape[0]
    assert depth == 1, "single-layer step"
    l = 0
    lam_init = 0.8 - 0.6 * math.exp(-0.3 * l)
    b, t, _ = x_prompt.shape
    bs, ts, _ = x_sample.shape
    past_len = cache_da_k.shape[2]

    row = lambda a: a.reshape(1, -1)
    w_in_bf = w_in[l].astype(BF16)
    w_out_bf = w_out[l].astype(BF16)
    w_xq_bf, w_xk_bf, w_xv_bf, w_xo_bf = (w[l].astype(BF16) for w in (w_xq, w_xk, w_xv, w_xo))
    w_pq_bf = w_pq[l].astype(BF16)
    k1_bf, k2_bf = peer_k1[l].astype(BF16), peer_k2[l].astype(BF16)
    tab_u, tab_v = _pack_table(peer_u[l]), _pack_table(peer_v[l])
    lamp = jnp.stack([lam_q1[l], lam_k1[l], lam_q2[l], lam_k2[l]])
    g_da3 = g_da[l].reshape(H_D, 1, DV_D)
    g_ret3 = g_ret[l].reshape(H_R, 1, DV_R)
    g_fin = row(g_final)

    def mixer_mid(x2d, first_row, mda, mret, mk, mv, bb, tt_rows):
        n = bb * tt_rows
        tm = _row_tile(n, 512)
        x1, qx = _out_proj(x2d, mda.reshape(n, GROUP_W), mret.reshape(n, GROUP_W), w_out_bf, row(g_cross[l]), w_xq_bf,
                           tm, first_row)
        x2, h3, qp = _cross(x1.reshape(bb, tt_rows, D_MODEL), qx.reshape(bb, tt_rows, D_MODEL), mk, mv,
                            w_xo_bf, row(g_ffn[l]), w_pq_bf, _row_tile(tt_rows, 512))
        return x2.reshape(n, D_MODEL), h3.reshape(n, D_MODEL), qp.reshape(n, -1)

    peer_args = (k1_bf, k2_bf, tab_u, tab_v, g_fin, PEER_TOKENS)
    n = b * t
    xp = x_prompt.reshape(n, D_MODEL)

    def prompt_dense(b0, bg, kv_into=None):
        rows = bg * t
        qd, kd, vd, kdb, vdb, qr, kr, vr, gr = _in_proj(xp, row(g_mix[l]), w_in_bf, _row_tile(rows, 512),
                                                       b0 * t, rows, kv_into)
        r3 = lambda a: a.reshape(bg, t, GROUP_W)
        mda = _diff_attn_prompt(lamp, r3(qd), r3(kdb), r3(vdb), g_da3, lam_init, 512, 512)
        mret, s_fin = _retention(r3(qr), r3(kr), r3(vr), r3(gr), g_ret3, None, 512)
        return mixer_mid(xp, b0 * t, mda, mret, mkb[b0:b0 + bg], mvb[b0:b0 + bg], bg, t), (kd, vd), s_fin

    mk, mv, mkb, mvb = _mem_kv(mem_prompt.reshape(b * N_MEM, D_MODEL), row(g_mem[l]), w_xk_bf, w_xv_bf, 512)
    mkb, mvb = mkb.reshape(b, N_MEM, D_MODEL), mvb.reshape(b, N_MEM, D_MODEL)
    leads = [bg for bg in SC_LEAD_BATCHES if bg] if sum(SC_LEAD_BATCHES) < b else []
    kv, states = None, []
    y = None
    launched = None
    finished = None
    hi = b
    for bg in leads + [hi - sum(leads)]:
        b0, hi = hi - bg, hi - bg
        peer_in, kv, s_g = prompt_dense(b0, bg, kv_into=kv)
        states.insert(0, s_g)
        x2_g, h3_g, qp_g = peer_in
        if finished is not None:
            y = _residual_norm(finished[0], finished[1], g_fin, y, 0, finished[2], PEER_TOKENS, out_rows=n, after=(qp_g,))
            finished = None
        route_after = ()
        if launched is not None:
            x2_p, e_p, gate_p, act_p, row_p = launched
            add8_p, w_p = _peer_on_sc_up(e_p, gate_p, act_p, tab_v, PEER_TOKENS,
                                         gate_after=(qp_g,) + (() if y is None else (y,)))
            finished, route_after = (x2_p, add8_p, row_p), (w_p,)
        if b0 > 0:
            launched = (x2_g,) + _peer_on_sc_down(h3_g, qp_g, k1_bf, k2_bf, tab_u, PEER_TOKENS, route_after) + (b0 * t,)
        else:
            y = _peer_and_final(*peer_in, *peer_args, _sc_tokens(bg * t), route_after=route_after, out_rows=n,
                                lead=finished, y_start=y)
    kd, vd = kv
    y_prompt = y.reshape(b, t, D_MODEL)
    s_fin = jnp.concatenate(states) if len(states) > 1 else states[0]

    ns = bs * ts
    xs = x_sample.reshape(ns, D_MODEL)
    qd_s, kd_s, vd_s, kdb_s, vdb_s, qr_s, kr_s, vr_s, gr_s = _in_proj(xs, row(g_mix[l]), w_in_bf, _row_tile(ns, 512))
    s3 = lambda a: a.reshape(bs, ts, GROUP_W)
    mda_s = _diff_attn_sample(lamp, s3(qd_s), cache_da_k[l].reshape(bs, past_len, GROUP_W),
                              cache_da_v[l].reshape(bs, past_len, GROUP_W), s3(kdb_s), s3(vdb_s), g_da3, lam_init)
    mret_s, s_new = _retention(s3(qr_s), s3(kr_s), s3(vr_s), s3(gr_s), g_ret3, state_ret[l], ts)
    peer_s = mixer_mid(xs, 0, mda_s, mret_s, cache_mem_k[l].reshape(bs, N_MEM, D_MODEL),
                       cache_mem_v[l].reshape(bs, N_MEM, D_MODEL), bs, ts)
    y_sample = _peer_and_final(*peer_s, *peer_args).reshape(bs, ts, D_MODEL)

    return (y_prompt, y_sample,
            kd.reshape(1, b, t, H_D, 2, DK_D), vd.reshape(1, b, t, H_D, DV_D), s_fin[None],
            mk.reshape(1, b, N_MEM, H_X, HD_X), mv.reshape(1, b, N_MEM, H_X, HD_X),
            kd_s.reshape(1, bs, ts, H_D, 2, DK_D), vd_s.reshape(1, bs, ts, H_D, DV_D), s_new[None])
```

```python
import functools
import math

import jax
import jax.numpy as jnp
from jax import lax
from jax.experimental import pallas as pl
from jax.experimental.pallas import tpu as pltpu
from jax.experimental.pallas import tpu_sc as plsc

D_MODEL = 1024
CHUNK = 64
CHUNK_SHIFT = CHUNK.bit_length() - 1
assert 1 << CHUNK_SHIFT == CHUNK
H_D, DK_D, DV_D = 4, 64, 128
H_R, DK_R, DV_R = 4, 128, 128
N_MEM = 256
H_X = 4
HD_X = D_MODEL // H_X
PEER_HEADS = 8
N_KEYS = 128
N_EXPERTS = N_KEYS * N_KEYS
PEER_TOPK = 16
EPS = 1e-6
HEAD_W = 128
GROUP_W = 512
N_PAIRS = PEER_HEADS * PEER_TOPK
HALF = D_MODEL // 2
ROWS_PER_EXPERT = HALF // 128
VMEM_LIMIT = 56 * 1024 * 1024

BF16 = jnp.bfloat16
F32 = jnp.float32


def _cparams(sem):
    return pltpu.CompilerParams(dimension_semantics=sem, vmem_limit_bytes=VMEM_LIMIT)


def _rms(x, g):
    return x * lax.rsqrt(jnp.mean(x * x, axis=-1, keepdims=True) + EPS) * g


def _dot(a, b):
    return jnp.dot(a, b, preferred_element_type=F32)


def _dot_nt(a, b):
    return lax.dot_general(a, b, (((1,), (1,)), ((), ())), preferred_element_type=F32)


def _dot_tn(a, b):
    return lax.dot_general(a, b, (((0,), (0,)), ((), ())), preferred_element_type=F32)


def _select_by_head(h, values):
    out = jnp.float32(values[-1])
    for i in range(len(values) - 2, -1, -1):
        out = jnp.where(h == i, jnp.float32(values[i]), out)
    return out


def _in_proj_kernel(x_ref, g_ref, w_ref, *rest):
    qd_ref, kd_ref, vd_ref, kdb_ref, vdb_ref, qr_ref, kr_ref, vr_ref, gr_ref = rest[-9:]
    hb = _rms(x_ref[...], g_ref[...]).astype(BF16)
    col = lambda c: _dot(hb, w_ref[:, c * GROUP_W:(c + 1) * GROUP_W])
    qd_ref[...] = (col(0) * (DK_D ** -0.5)).astype(BF16)
    kd = col(1)
    kd_ref[...] = kd.reshape(kd_ref.shape)
    kdb_ref[...] = kd.astype(BF16)
    vd = col(2)
    vd_ref[...] = vd.reshape(vd_ref.shape)
    vdb_ref[...] = vd.astype(BF16)
    qr_ref[...] = col(3).astype(BF16)
    kr_ref[...] = (col(4) * (DK_R ** -0.5)).astype(BF16)
    vr_ref[...] = col(5).astype(BF16)
    gr_ref[...] = col(6)


def _in_proj(x2d, g, w_bf, tm, first_row=0, n_rows=None, kv_into=None):
    n_all = x2d.shape[0]
    n = n_all if n_rows is None else n_rows
    off = first_row // tm
    blk = lambda: pl.BlockSpec((tm, GROUP_W), lambda i: (i, 0))
    k_dims, v_dims = (H_D, 2, DK_D), (H_D, DV_D)
    blk_all = lambda dims: pl.BlockSpec((tm,) + dims, lambda i: (i + off,) + (0,) * len(dims))
    sh = lambda dt: jax.ShapeDtypeStruct((n, GROUP_W), dt)
    sh_all = lambda dims: jax.ShapeDtypeStruct((n_all,) + dims, F32)
    args = [x2d, g, w_bf]
    in_specs = [pl.BlockSpec((tm, D_MODEL), lambda i: (i + off, 0)),
                pl.BlockSpec((1, D_MODEL), lambda i: (0, 0)),
                pl.BlockSpec(w_bf.shape, lambda i: (0, 0))]
    aliases = {}
    if kv_into is not None:
        args += list(kv_into)
        in_specs += [pl.BlockSpec(memory_space=pl.ANY)] * 2
        aliases = {3: 1, 4: 2}
    return pl.pallas_call(
        _in_proj_kernel,
        grid=(n // tm,),
        in_specs=in_specs,
        out_specs=[blk(), blk_all(k_dims), blk_all(v_dims)] + [blk() for _ in range(6)],
        out_shape=[sh(BF16), sh_all(k_dims), sh_all(v_dims), sh(BF16), sh(BF16), sh(BF16), sh(BF16), sh(BF16), sh(F32)],
        input_output_aliases=aliases,
        compiler_params=_cparams(("parallel",)),
        name="in_proj",
    )(*args)


def _lambda_from(lam_ref, lam_init):
    l = lam_ref[...]
    a = jnp.exp(jnp.sum(l[0:1] * l[1:2], axis=-1, keepdims=True))
    b = jnp.exp(jnp.sum(l[2:3] * l[3:4], axis=-1, keepdims=True))
    return a - b + lam_init


def _diff_post(acc, l, lam, g, lam_init, tq):
    o = acc[:tq] / l[:tq] - lam * (acc[tq:] / l[tq:])
    return o * lax.rsqrt(jnp.mean(o * o, axis=-1, keepdims=True) + EPS) * g * (1.0 - lam_init)


def _split_maps(q):
    lane = lax.broadcasted_iota(jnp.int32, q.shape, 1)
    zero = jnp.zeros_like(q)
    return jnp.concatenate([jnp.where(lane < DK_D, q, zero), jnp.where(lane >= DK_D, q, zero)], axis=0)


def _da_prompt_kernel(lam_ref, q_ref, k_ref, v_ref, g_ref, o_ref, kx_ref, vx_ref, own_ref, acc_ref, m_ref, *, lam_init, tq, tk):
    h = pl.program_id(1)
    i = pl.program_id(2)
    t = k_ref.shape[0]
    slope = _select_by_head(h, [2.0 ** (-8.0 * (j + 1) / H_D) for j in range(H_D)])

    @pl.when(i == 0)
    def _():
        pos = lax.broadcasted_iota(jnp.int32, (t, HEAD_W), 0)
        lane = lax.broadcasted_iota(jnp.int32, (t, HEAD_W), 1)
        coarse = ((pos >> CHUNK_SHIFT) << CHUNK_SHIFT).astype(F32) * slope
        fine = (pos & (CHUNK - 1)).astype(F32) * slope
        kx_ref[:, :HEAD_W] = k_ref[...]
        kx_ref[:, HEAD_W:] = jnp.where(lane == 0, coarse, jnp.where(lane == 1, fine, 0.0)).astype(BF16)
        vx_ref[:DV_D, :] = v_ref[...].astype(F32).T.astype(BF16)
        vx_ref[DV_D:, :] = jnp.ones((vx_ref.shape[0] - DV_D, t), BF16)
        krel = lax.broadcasted_iota(jnp.int32, (tk, 2 * tq), 0)
        c = lax.broadcasted_iota(jnp.int32, (tk, 2 * tq), 1)
        for par in range(tk // tq):
            qrel = par * tq + jnp.where(c >= tq, c - tq, c)
            ahead = (2.0 * slope) * jnp.maximum(krel - qrel, 0).astype(F32)
            own_ref[par] = jnp.where((qrel >> CHUNK_SHIFT) >= (krel >> CHUNK_SHIFT), -ahead, -1e30)

    q = q_ref[...]
    lane = lax.broadcasted_iota(jnp.int32, q.shape, 1)
    zero = jnp.zeros_like(q)
    ones2 = jnp.where(lane < 2, 1.0, 0.0).astype(BF16)
    q2 = jnp.concatenate([jnp.concatenate([jnp.where(lane < DK_D, q, zero), ones2], axis=1),
                          jnp.concatenate([jnp.where(lane >= DK_D, q, zero), ones2], axis=1)], axis=0)
    jd = (i * tq) // tk

    def scores(j):
        return _dot_nt(kx_ref[pl.ds(pl.multiple_of(j * tk, tk), tk), :], q2)

    def values(j):
        return vx_ref[:, pl.ds(pl.multiple_of(j * tk, tk), tk)]

    s = scores(jd) + own_ref[(i * tq) % tk // tq]
    m0 = jnp.max(s, axis=0, keepdims=True)
    m_ref[...] = m0
    acc_ref[...] = _dot(values(jd), jnp.exp(s - m0).astype(BF16))

    def absorb(blocks):
        ss = [scores(j) for j in blocks]
        m_old = m_ref[...]
        m_new = m_old
        for s in ss:
            m_new = jnp.maximum(m_new, jnp.max(s, axis=0, keepdims=True))
        m_ref[...] = m_new
        acc = jnp.exp(m_old - m_new) * acc_ref[...]
        for j, s in zip(blocks, ss):
            acc = acc + _dot(values(j), jnp.exp(s - m_new).astype(BF16))
        acc_ref[...] = acc

    def past_pair(jj, carry):
        absorb([2 * jj, 2 * jj + 1])
        return carry

    lax.fori_loop(0, jd // 2, past_pair, 0)

    @pl.when(jd % 2 == 1)
    def _():
        absorb([jd - 1])

    acc = acc_ref[...]
    num, den = acc[:DV_D], acc[DV_D:DV_D + 1]
    lam = _lambda_from(lam_ref, lam_init)
    o = (num[:, :tq] / den[:, :tq] - lam * (num[:, tq:] / den[:, tq:])).T
    o = o * lax.rsqrt(jnp.mean(o * o, axis=-1, keepdims=True) + EPS) * g_ref[...] * (1.0 - lam_init)
    o_ref[...] = o.astype(o_ref.dtype)


ONES_ROWS = 16


def _diff_attn_prompt(lamp, q, k, v, g_da3, lam_init, tq, tk):
    b, t, _ = q.shape
    kern = functools.partial(_da_prompt_kernel, lam_init=lam_init, tq=tq, tk=tk)
    return pl.pallas_call(
        kern,
        grid=(b, H_D, t // tq),
        in_specs=[pl.BlockSpec((4, DK_D), lambda b_, h, i: (0, 0)),
                  pl.BlockSpec((None, tq, HEAD_W), lambda b_, h, i: (b_, i, h)),
                  pl.BlockSpec((None, t, HEAD_W), lambda b_, h, i: (b_, 0, h)),
                  pl.BlockSpec((None, t, HEAD_W), lambda b_, h, i: (b_, 0, h)),
                  pl.BlockSpec((None, 1, HEAD_W), lambda b_, h, i: (h, 0, 0))],
        out_specs=pl.BlockSpec((None, tq, HEAD_W), lambda b_, h, i: (b_, i, h)),
        out_shape=jax.ShapeDtypeStruct((b, t, GROUP_W), BF16),
        scratch_shapes=[pltpu.VMEM((t, 2 * HEAD_W), BF16), pltpu.VMEM((DV_D + ONES_ROWS, t), BF16),
                        pltpu.VMEM((tk // tq, tk, 2 * tq), F32),
                        pltpu.VMEM((DV_D + ONES_ROWS, 2 * tq), F32), pltpu.VMEM((1, 2 * tq), F32)],
        compiler_params=_cparams(("parallel", "parallel", "arbitrary")),
        name="diff_attn_prompt",
    )(lamp, q, k, v, g_da3)


def _da_sample_kernel(lam_ref, q_ref, kc_ref, vc_ref, kn_ref, vn_ref, g_ref, o_ref, *, lam_init, ts, past_len):
    h = pl.program_id(1)
    slope = _select_by_head(h, [2.0 ** (-8.0 * (j + 1) / H_D) for j in range(H_D)])
    q2 = _split_maps(q_ref[...])
    rows = 2 * ts

    def scores(k, base, n):
        r = lax.broadcasted_iota(jnp.int32, (rows, n), 0)
        qpos = past_len + jnp.where(r >= ts, r - ts, r)
        kpos = base + lax.broadcasted_iota(jnp.int32, (rows, n), 1)
        return _dot_nt(q2, k) - slope * jnp.abs(qpos - kpos).astype(F32)

    sc = scores(kc_ref[...].astype(BF16), 0, past_len)
    sn = scores(kn_ref[...], past_len, ts)
    m = jnp.maximum(jnp.max(sc, axis=-1, keepdims=True), jnp.max(sn, axis=-1, keepdims=True))
    pc = jnp.exp(sc - m)
    pn = jnp.exp(sn - m)
    l = jnp.sum(pc, axis=-1, keepdims=True) + jnp.sum(pn, axis=-1, keepdims=True)
    acc = _dot(pc.astype(BF16), vc_ref[...].astype(BF16)) + _dot(pn.astype(BF16), vn_ref[...])
    lam = _lambda_from(lam_ref, lam_init)
    o_ref[...] = _diff_post(acc, l, lam, g_ref[...], lam_init, ts).astype(o_ref.dtype)


def _diff_attn_sample(lamp, q, kc, vc, kn, vn, g_da3, lam_init):
    b, ts, _ = q.shape
    past_len = kc.shape[1]
    kern = functools.partial(_da_sample_kernel, lam_init=lam_init, ts=ts, past_len=past_len)
    head = lambda rows: pl.BlockSpec((None, rows, HEAD_W), lambda b_, h: (b_, 0, h))
    return pl.pallas_call(
        kern,
        grid=(b, H_D),
        in_specs=[pl.BlockSpec((4, DK_D), lambda b_, h: (0, 0)),
                  head(ts), head(past_len), head(past_len), head(ts), head(ts),
                  pl.BlockSpec((None, 1, HEAD_W), lambda b_, h: (h, 0, 0))],
        out_specs=head(ts),
        out_shape=jax.ShapeDtypeStruct((b, ts, GROUP_W), BF16),
        compiler_params=_cparams(("parallel", "parallel")),
        name="diff_attn_sample",
    )(lamp, q, kc, vc, kn, vn, g_da3)


def _ret_kernel(*refs, lb, has_init):
    if has_init:
        q_ref, k_ref, v_ref, gate_ref, g_ref, s0_ref, o_ref, sfin_ref, s_ref = refs
    else:
        q_ref, k_ref, v_ref, gate_ref, g_ref, o_ref, sfin_ref, s_ref = refs
    h = pl.program_id(1)
    c = pl.program_id(2)
    lg = _select_by_head(h, [math.log1p(-(2.0 ** (-5.0 - j))) for j in range(H_R)])

    @pl.when(c == 0)
    def _():
        s_ref[...] = s0_ref[...] if has_init else jnp.zeros_like(s_ref)

    q, k, v = q_ref[...], k_ref[...], v_ref[...]
    i = lax.broadcasted_iota(jnp.int32, (lb, lb), 0)
    j = lax.broadcasted_iota(jnp.int32, (lb, lb), 1)
    d = (i - j).astype(F32)
    decay = jnp.where(d >= 0, jnp.exp(jnp.maximum(d, 0.0) * lg), 0.0)
    inner = _dot_nt(q, k) * decay
    ic = lax.broadcasted_iota(jnp.int32, (lb, 1), 0).astype(F32)
    s_old = s_ref[...]
    o = _dot(inner.astype(BF16), v) + _dot(q, s_old.astype(BF16)) * jnp.exp((ic + 1.0) * lg)
    tail = jnp.exp((lb - 1.0 - ic) * lg)
    kt = (k.astype(F32) * tail).astype(BF16)
    s_new = jnp.exp(lb * lg) * s_old + _dot_tn(kt, v)
    s_ref[...] = s_new

    @pl.when(c == pl.num_programs(2) - 1)
    def _():
        sfin_ref[...] = s_new

    oc = o - jnp.mean(o, axis=-1, keepdims=True)
    y = oc * lax.rsqrt(jnp.mean(oc * oc, axis=-1, keepdims=True) + EPS) * g_ref[...]
    gate = gate_ref[...]
    o_ref[...] = (y * (gate * jax.nn.sigmoid(gate))).astype(o_ref.dtype)


def _retention(q, k, v, gate, g_ret3, s0, lb):
    b, t, _ = q.shape
    has_init = s0 is not None
    kern = functools.partial(_ret_kernel, lb=lb, has_init=has_init)
    head = lambda: pl.BlockSpec((None, lb, HEAD_W), lambda b_, h, c: (b_, c, h))
    state = lambda: pl.BlockSpec((None, None, DK_R, DV_R), lambda b_, h, c: (b_, h, 0, 0))
    in_specs = [head(), head(), head(), head(), pl.BlockSpec((None, 1, HEAD_W), lambda b_, h, c: (h, 0, 0))]
    args = [q, k, v, gate, g_ret3]
    if has_init:
        in_specs.append(state())
        args.append(s0)
    return pl.pallas_call(
        kern,
        grid=(b, H_R, t // lb),
        in_specs=in_specs,
        out_specs=[head(), state()],
        out_shape=[jax.ShapeDtypeStruct((b, t, GROUP_W), BF16), jax.ShapeDtypeStruct((b, H_R, DK_R, DV_R), F32)],
        scratch_shapes=[pltpu.VMEM((DK_R, DV_R), F32)],
        compiler_params=_cparams(("parallel", "parallel", "arbitrary")),
        name="retention",
    )(*args)


def _out_proj_kernel(x_ref, mda_ref, mret_ref, wo_ref, g_ref, wq_ref, x1_ref, qx_ref):
    x1 = x_ref[...] + _dot(mda_ref[...], wo_ref[:GROUP_W, :]) + _dot(mret_ref[...], wo_ref[GROUP_W:, :])
    x1_ref[...] = x1
    hn = _rms(x1, g_ref[...]).astype(BF16)
    qx_ref[...] = (_dot(hn, wq_ref[...]) * (HD_X ** -0.5)).astype(BF16)


def _out_proj(x2d, mda, mret, wo_bf, g_cross, wq_bf, tm, first_row=0):
    n = mda.shape[0]
    off = first_row // tm
    full = lambda a: pl.BlockSpec(a.shape, lambda i: (0, 0))
    return pl.pallas_call(
        _out_proj_kernel,
        grid=(n // tm,),
        in_specs=[pl.BlockSpec((tm, D_MODEL), lambda i: (i + off, 0)),
                  pl.BlockSpec((tm, GROUP_W), lambda i: (i, 0)),
                  pl.BlockSpec((tm, GROUP_W), lambda i: (i, 0)),
                  full(wo_bf), full(g_cross), full(wq_bf)],
        out_specs=[pl.BlockSpec((tm, D_MODEL), lambda i: (i, 0)), pl.BlockSpec((tm, D_MODEL), lambda i: (i, 0))],
        out_shape=[jax.ShapeDtypeStruct((n, D_MODEL), F32), jax.ShapeDtypeStruct((n, D_MODEL), BF16)],
        compiler_params=_cparams(("parallel",)),
        name="out_proj",
    )(x2d, mda, mret, wo_bf, g_cross, wq_bf)


def _mem_kv_kernel(m_ref, g_ref, wk_ref, wv_ref, mk_ref, mv_ref, mkb_ref, mvb_ref):
    mn = _rms(m_ref[...], g_ref[...]).astype(BF16)
    mk = _dot(mn, wk_ref[...])
    mv = _dot(mn, wv_ref[...])
    mk_ref[...] = mk
    mv_ref[...] = mv
    mkb_ref[...] = mk.astype(BF16)
    mvb_ref[...] = mv.astype(BF16)


def _mem_kv(mem2d, g_mem, wk_bf, wv_bf, tm):
    n = mem2d.shape[0]
    row = lambda: pl.BlockSpec((tm, D_MODEL), lambda i: (i, 0))
    full = lambda a: pl.BlockSpec(a.shape, lambda i: (0, 0))
    sh = lambda dt: jax.ShapeDtypeStruct((n, D_MODEL), dt)
    return pl.pallas_call(
        _mem_kv_kernel,
        grid=(n // tm,),
        in_specs=[row(), full(g_mem), full(wk_bf), full(wv_bf)],
        out_specs=[row(), row(), row(), row()],
        out_shape=[sh(F32), sh(F32), sh(BF16), sh(BF16)],
        compiler_params=_cparams(("parallel",)),
        name="mem_kv",
    )(mem2d, g_mem, wk_bf, wv_bf)


def _cross_kernel(x1_ref, qx_ref, mk_ref, mv_ref, wo_ref, g_ref, wpq_ref, x2_ref, h3_ref, qp_ref):
    q = qx_ref[...]
    heads = []
    for h in range(H_X):
        sl = slice(h * HD_X, (h + 1) * HD_X)
        kh = mk_ref[:, sl] if mk_ref.dtype == BF16 else mk_ref[:, sl].astype(BF16)
        vh = mv_ref[:, sl] if mv_ref.dtype == BF16 else mv_ref[:, sl].astype(BF16)
        s = _dot_nt(q[:, sl], kh)
        p = jnp.exp(s - jnp.max(s, axis=-1, keepdims=True))
        heads.append(_dot(p.astype(BF16), vh) / jnp.sum(p, axis=-1, keepdims=True))
    o = jnp.concatenate(heads, axis=-1).astype(BF16)
    x2 = x1_ref[...] + _dot(o, wo_ref[...])
    x2_ref[...] = x2
    h3 = _rms(x2, g_ref[...])
    h3_ref[...] = h3.reshape(h3_ref.shape)
    qp_ref[...] = _dot(h3.astype(BF16), wpq_ref[...]).astype(BF16)


def _cross(x1, qx, mk, mv, wo_bf, g_ffn, wpq_bf, tm):
    b, t, _ = x1.shape
    row = lambda w: pl.BlockSpec((None, tm, w), lambda b_, i: (b_, i, 0))
    mem = lambda: pl.BlockSpec((None, N_MEM, D_MODEL), lambda b_, i: (b_, 0, 0))
    full = lambda a: pl.BlockSpec(a.shape, lambda b_, i: (0, 0))
    dq = wpq_bf.shape[1]
    return pl.pallas_call(
        _cross_kernel,
        grid=(b, t // tm),
        in_specs=[row(D_MODEL), row(D_MODEL), mem(), mem(), full(wo_bf), full(g_ffn), full(wpq_bf)],
        out_specs=[row(D_MODEL),
                   pl.BlockSpec((tm * CHUNKS, 128), lambda b_, i: (b_ * (t // tm) + i, 0)),
                   row(dq)],
        out_shape=[jax.ShapeDtypeStruct((b, t, D_MODEL), F32),
                   jax.ShapeDtypeStruct((b * t * CHUNKS, 128), F32),
                   jax.ShapeDtypeStruct((b, t, dq), BF16)],
        compiler_params=_cparams(("parallel", "parallel")),
        name="cross_attn",
    )(x1, qx, mk, mv, wo_bf, g_ffn, wpq_bf)


ID_PAD = 2.0 ** 29


def _topk_rows(s, ids):
    vals, sel = [], []
    for _ in range(PEER_TOPK):
        m = jnp.max(s, axis=0, keepdims=True)
        idx = jnp.min(jnp.where(s == m, ids, ID_PAD), axis=0, keepdims=True)
        vals.append(m)
        sel.append(idx)
        s = jnp.where(ids == idx, -jnp.inf, s)
    return jnp.concatenate(vals, axis=0), jnp.concatenate(sel, axis=0)


def _candidates(v1, i1, v2, i2):
    lanes = v1.shape[1]
    b8 = lax.broadcasted_iota(jnp.int32, (8, lanes), 0)
    ident = lambda a, ia, ib, b: (a * PEER_TOPK + b) * float(N_EXPERTS) + (ia * float(N_KEYS) + ib)
    b16 = lax.broadcasted_iota(jnp.int32, (PEER_TOPK, lanes), 0).astype(F32)
    vals = [v1[0:1] + v2]
    ids = [ident(0, i1[0:1], i2, b16)]
    for a in range(1, 8):
        keep = b8 < PEER_TOPK // (a + 1)
        vals.append(jnp.where(keep, v1[a:a + 1] + v2[0:8], -jnp.inf))
        ids.append(jnp.where(keep, ident(a, i1[a:a + 1], i2[0:8], b8.astype(F32)), ID_PAD))
    a_hi = (b8 + 8).astype(F32)
    vals.append(v1[8:16] + v2[0:1])
    ids.append(ident(a_hi, i1[8:16], i2[0:1], 0.0))
    return jnp.concatenate(vals, axis=0), jnp.concatenate(ids, axis=0)


def _route_kernel(qp_ref, k1_ref, k2_ref, e_ref, g_ref):
    half = N_KEYS
    tt = qp_ref.shape[0]
    key_id = lax.broadcasted_iota(jnp.int32, (N_KEYS, tt), 0).astype(F32)
    es, gs = [], []
    for p in range(PEER_HEADS):
        q1 = qp_ref[:, (2 * p) * half:(2 * p + 1) * half]
        q2 = qp_ref[:, (2 * p + 1) * half:(2 * p + 2) * half]
        v1, i1 = _topk_rows(_dot_nt(k1_ref[p], q1), key_id)
        v2, i2 = _topk_rows(_dot_nt(k2_ref[p], q2), key_id)
        sc, sel = _topk_rows(*_candidates(v1, i1, v2, i2))
        w = jnp.exp(sc - sc[0:1])
        es.append((sel.astype(jnp.int32) & (N_EXPERTS - 1)) * ROWS_PER_EXPERT)
        gs.append(w / jnp.sum(w, axis=0, keepdims=True))
    e_ref[...] = jnp.concatenate(es, axis=0).T
    g_ref[...] = jnp.concatenate(gs, axis=0).T


def _route_extra_kernel(qp_ref, k1_ref, k2_ref, *rest):
    _route_kernel(qp_ref, k1_ref, k2_ref, rest[-2], rest[-1])


def _route(qp2d, k1_bf, k2_bf, tt, first_token=0, n_tokens=None, into=None, after=()):
    n = qp2d.shape[0]
    n_tokens = n if n_tokens is None else n_tokens
    off = first_token // tt
    full = lambda a: pl.BlockSpec(a.shape, lambda i: (0, 0, 0))
    pair = lambda: pl.BlockSpec((tt, N_PAIRS), lambda i: (i + off, 0))
    in_specs = [pl.BlockSpec((tt, qp2d.shape[1]), lambda i: (i + off, 0)), full(k1_bf), full(k2_bf)]
    args = [qp2d, k1_bf, k2_bf]
    aliases = {}
    if into is not None:
        in_specs += [pl.BlockSpec(memory_space=pl.ANY), pl.BlockSpec(memory_space=pl.ANY)]
        args += list(into)
        aliases = {3: 0, 4: 1}
    in_specs += [pl.BlockSpec(memory_space=pl.ANY)] * len(after)
    args += list(after)
    return pl.pallas_call(
        _route_kernel if len(args) == 3 else _route_extra_kernel,
        grid=(n_tokens // tt,),
        in_specs=in_specs,
        out_specs=[pair(), pair()],
        out_shape=[jax.ShapeDtypeStruct((n, N_PAIRS), jnp.int32), jax.ShapeDtypeStruct((n, N_PAIRS), F32)],
        input_output_aliases=aliases,
        compiler_params=_cparams(("parallel",)),
        name="peer_route",
    )(*args)


def _unpack(words):
    hi = lax.bitcast_convert_type(words & jnp.uint32(0xFFFF0000), F32)
    lo = lax.bitcast_convert_type(words << 16, F32)
    return hi, lo


def _expert_rows(tab_ref, row0):
    return tab_ref[pl.ds(pl.multiple_of(row0, ROWS_PER_EXPERT), ROWS_PER_EXPERT), :]


REDUCE_TOKENS = 16


def _lane_sums_to_rows(y, n_tok):
    hi = y.astype(BF16)
    lo = (y - hi.astype(F32)).astype(BF16)
    ones = jnp.ones((128, N_PAIRS), BF16)
    s = (_dot(hi, ones) + _dot(lo, ones)).reshape(n_tok, N_PAIRS, N_PAIRS)
    eye = lax.broadcasted_iota(jnp.int32, (N_PAIRS, N_PAIRS), 0) == lax.broadcasted_iota(jnp.int32, (N_PAIRS, N_PAIRS), 1)
    return jnp.sum(jnp.where(eye[None], s, 0.0), axis=1)


CHUNKS = D_MODEL // 128


def _peer_u_kernel(e_ref, h8_ref, gate_ref, tab_ref, w_ref, prod_ref, ys_ref, act_ref, *, tt):

    def token(t, slot):
        base = pl.multiple_of(t * CHUNKS, CHUNKS)
        ha = h8_ref[pl.ds(base, ROWS_PER_EXPERT), :]
        hb = h8_ref[pl.ds(pl.multiple_of(base + ROWS_PER_EXPERT, ROWS_PER_EXPERT), ROWS_PER_EXPERT), :]
        prod = prod_ref.at[slot]
        for k in range(N_PAIRS):
            hi, lo = _unpack(_expert_rows(tab_ref, e_ref[t, k]))
            prod[k * ROWS_PER_EXPERT:(k + 1) * ROWS_PER_EXPERT, :] = hi * ha + lo * hb
        y = prod[pl.ds(0, N_PAIRS, stride=ROWS_PER_EXPERT), :]
        for c in range(1, ROWS_PER_EXPERT):
            y = y + prod[pl.ds(c, N_PAIRS, stride=ROWS_PER_EXPERT), :]
        ys_ref[pl.ds(pl.multiple_of(t * N_PAIRS, N_PAIRS), N_PAIRS), :] = y

    def two_tokens(j, carry):
        token(2 * j, 0)
        token(2 * j + 1, 1)
        return carry

    lax.fori_loop(0, tt // 2, two_tokens, 0)

    def group(g, carry):
        rows = REDUCE_TOKENS * N_PAIRS
        y = ys_ref[pl.ds(pl.multiple_of(g * rows, rows), rows), :]
        act_ref[pl.ds(pl.multiple_of(g * REDUCE_TOKENS, REDUCE_TOKENS), REDUCE_TOKENS), :] = _lane_sums_to_rows(y, REDUCE_TOKENS)
        return carry

    lax.fori_loop(0, tt // REDUCE_TOKENS, group, 0)
    w_ref[...] = _gated_gelu(gate_ref[...], act_ref[...])


def _gated_gelu(gate, a):
    return gate * (0.5 * a * (1.0 + lax.erf(a * (2.0 ** -0.5))))


def _gate_act_kernel(act_ref, gate_ref, *rest):
    rest[-1][...] = _gated_gelu(gate_ref[...], act_ref[...])


def _gate_act(act, gate_t, w_t, first_token, tt, after=()):
    off = first_token // tt
    return pl.pallas_call(
        _gate_act_kernel,
        grid=(act.shape[0] // tt,),
        in_specs=[pl.BlockSpec((tt, N_PAIRS), lambda i: (i, 0)),
                  pl.BlockSpec((tt, N_PAIRS), lambda i: (i + off, 0))]
                 + [pl.BlockSpec(memory_space=pl.ANY)] * (1 + len(after)),
        out_specs=pl.BlockSpec((tt, N_PAIRS), lambda i: (i + off, 0)),
        out_shape=jax.ShapeDtypeStruct(w_t.shape, F32),
        input_output_aliases={2: 0},
        compiler_params=_cparams(("arbitrary",)),
        name="peer_gate_act",
    )(act, gate_t, w_t, *after)


def _peer_u(e_t, h, gate_t, tab, tt, n_tokens):
    n = e_t.shape[0]
    assert tt % REDUCE_TOKENS == 0
    kern = functools.partial(_peer_u_kernel, tt=tt)
    pair = lambda **kw: pl.BlockSpec((tt, N_PAIRS), lambda i: (i, 0), **kw)
    return pl.pallas_call(
        kern,
        grid=(n_tokens // tt,),
        in_specs=[pair(memory_space=pltpu.SMEM),
                  pl.BlockSpec((tt * CHUNKS, 128), lambda i: (i, 0)),
                  pair(),
                  pl.BlockSpec(tab.shape, lambda i: (0, 0), pipeline_mode=pl.Buffered(1))],
        out_specs=pair(),
        out_shape=jax.ShapeDtypeStruct((n, N_PAIRS), F32),
        scratch_shapes=[pltpu.VMEM((2, N_PAIRS * ROWS_PER_EXPERT, 128), F32),
                        pltpu.VMEM((tt * N_PAIRS, 128), F32),
                        pltpu.VMEM((tt, N_PAIRS), F32)],
        compiler_params=_cparams(("arbitrary",)),
        name="peer_u",
    )(e_t, h, gate_t, tab)


def _peer_v_kernel(e_ref, w_ref, x_ref, g_ref, tab_ref, o_ref, acc_ref, *, tt):
    n_acc = 2

    def token(t):
        acc_hi = [jnp.zeros((ROWS_PER_EXPERT, 128), F32) for _ in range(n_acc)]
        acc_lo = [jnp.zeros((ROWS_PER_EXPERT, 128), F32) for _ in range(n_acc)]
        for k in range(N_PAIRS):
            hi, lo = _unpack(_expert_rows(tab_ref, e_ref[t, k]))
            w = w_ref[t, k]
            acc_hi[k % n_acc] = acc_hi[k % n_acc] + w * hi
            acc_lo[k % n_acc] = acc_lo[k % n_acc] + w * lo
        base = pl.multiple_of(t * CHUNKS, CHUNKS)
        acc_ref[pl.ds(base, ROWS_PER_EXPERT), :] = sum(acc_hi[1:], acc_hi[0])
        acc_ref[pl.ds(pl.multiple_of(base + ROWS_PER_EXPERT, ROWS_PER_EXPERT), ROWS_PER_EXPERT), :] = sum(acc_lo[1:], acc_lo[0])

    def two_tokens(j, carry):
        token(2 * j)
        token(2 * j + 1)
        return carry

    lax.fori_loop(0, tt // 2, two_tokens, 0)
    _residual_rms_store(x_ref, acc_ref, g_ref, o_ref, tt)


def _residual_rms_store(x_ref, add_ref, g_ref, o_ref, tt):
    xs = [x_ref[:, c * 128:(c + 1) * 128] + add_ref[pl.ds(c, tt, stride=CHUNKS), :] for c in range(CHUNKS)]
    sq = xs[0] * xs[0]
    for x in xs[1:]:
        sq = sq + x * x
    r = lax.rsqrt(jnp.sum(sq, axis=1, keepdims=True) * (1.0 / D_MODEL) + EPS)
    for c in range(CHUNKS):
        o_ref[:, c * 128:(c + 1) * 128] = xs[c] * r * g_ref[:, c * 128:(c + 1) * 128]


def _peer_v_into_kernel(e_ref, w_ref, x_ref, g_ref, tab_ref, y_any_ref, o_ref, acc_ref, *, tt):
    del y_any_ref
    _peer_v_kernel(e_ref, w_ref, x_ref, g_ref, tab_ref, o_ref, acc_ref, tt=tt)


def _peer_v(e_t, w_t, x, g_final, tab, tt, n_tokens, into=None):
    pair = lambda: pl.BlockSpec((tt, N_PAIRS), lambda i: (i, 0), memory_space=pltpu.SMEM)
    row = lambda: pl.BlockSpec((tt, D_MODEL), lambda i: (i, 0))
    in_specs = [pair(), pair(), row(),
                pl.BlockSpec((1, D_MODEL), lambda i: (0, 0)),
                pl.BlockSpec(tab.shape, lambda i: (0, 0), pipeline_mode=pl.Buffered(1))]
    args = [e_t, w_t, x, g_final, tab]
    if into is not None:
        in_specs.append(pl.BlockSpec(memory_space=pl.ANY))
        args.append(into)
    return pl.pallas_call(
        functools.partial(_peer_v_kernel if into is None else _peer_v_into_kernel, tt=tt),
        grid=(n_tokens // tt,),
        in_specs=in_specs,
        out_specs=row(),
        out_shape=jax.ShapeDtypeStruct(x.shape if into is None else into.shape, F32),
        input_output_aliases={} if into is None else {5: 0},
        scratch_shapes=[pltpu.VMEM((tt * CHUNKS, 128), F32)],
        compiler_params=_cparams(("arbitrary",)),
        name="peer_v",
    )(*args)


def _residual_norm_kernel(x_ref, add_ref, g_ref, *rest, tt):
    _residual_rms_store(x_ref, add_ref, g_ref, rest[-1], tt)


def _residual_norm(x, add8, g_final, y, x_first, y_first, tt, out_rows=None, after=()):
    n_rows = add8.shape[0] // CHUNKS
    x_off, y_off = x_first // tt, y_first // tt
    extra = ([] if y is None else [y]) + list(after)
    return pl.pallas_call(
        functools.partial(_residual_norm_kernel, tt=tt),
        grid=(n_rows // tt,),
        in_specs=[pl.BlockSpec((tt, D_MODEL), lambda i: (i + x_off, 0)),
                  pl.BlockSpec((tt * CHUNKS, 128), lambda i: (i, 0)),
                  pl.BlockSpec((1, D_MODEL), lambda i: (0, 0))]
                 + [pl.BlockSpec(memory_space=pl.ANY)] * len(extra),
        out_specs=pl.BlockSpec((tt, D_MODEL), lambda i: (i + y_off, 0)),
        out_shape=jax.ShapeDtypeStruct((out_rows, D_MODEL) if y is None else y.shape, F32),
        input_output_aliases={} if y is None else {3: 0},
        compiler_params=_cparams(("arbitrary",)),
        name="peer_residual_norm",
    )(x, add8, g_final, *extra)


SC_CORES = 2
SC_SUBCORES = 16
SC_LANES = 16
WORD_ROWS = 128 // SC_LANES
SC_TOKEN_CHUNK = 32
SC_RING = 4


def _peer_sc(e_t, aux, tab, down):
    n = e_t.shape[0]
    workers = SC_CORES * SC_SUBCORES
    assert n % (workers * SC_TOKEN_CHUNK) == 0 and PEER_HEADS % SC_RING == 0
    per = n // workers
    head_rows = ROWS_PER_EXPERT * PEER_TOPK
    aux_rows = CHUNKS if down else 1
    out_rows = 1 if down else CHUNKS
    mesh = plsc.VectorSubcoreMesh(core_axis_name="core", subcore_axis_name="subcore",
                                  num_cores=SC_CORES, num_subcores=SC_SUBCORES)

    @pl.kernel(out_type=jax.ShapeDtypeStruct((n * out_rows, 128), F32), mesh=mesh,
               scratch_types=[pltpu.VMEM((SC_TOKEN_CHUNK, N_PAIRS), jnp.int32),
                              pltpu.VMEM((SC_TOKEN_CHUNK * aux_rows, 128), F32),
                              pltpu.VMEM((SC_RING, head_rows, 128), jnp.uint32),
                              pltpu.VMEM((SC_TOKEN_CHUNK * out_rows, 128), F32),
                              pltpu.VMEM((PEER_TOPK, SC_LANES), F32),
                              pltpu.SemaphoreType.DMA((SC_RING,))],
               compiler_params=pltpu.CompilerParams(needs_layout_passes=False),
               name="peer_u_sc" if down else "peer_v_sc")
    def body(e_hbm, aux_hbm, tab_hbm, o_hbm, e_v, aux_v, rows_v, out_v, fold_v, sems):
        wid = lax.axis_index("core") * SC_SUBCORES + lax.axis_index("subcore")
        lane = lax.broadcasted_iota(jnp.int32, (SC_LANES,), 0)

        def gathers(i, p, slot):
            first = e_v[i, pl.ds(p * PEER_TOPK, PEER_TOPK)]
            return [pltpu.make_async_copy(tab_hbm.at[first + r],
                                          rows_v.at[slot, pl.ds(r * PEER_TOPK, PEER_TOPK)], sems.at[slot])
                    for r in range(ROWS_PER_EXPERT)]

        def start_gather(i, p, slot):
            for d in gathers(i, p, slot):
                d.start()

        def weighted_sum(i, p, slot):
            ws = [plsc.load_gather(aux_v, [lane * 0 + i, lane * 0 + (p * PEER_TOPK + k)])
                  for k in range(PEER_TOPK)]
            for r in range(ROWS_PER_EXPERT):
                @pl.loop(0, WORD_ROWS)
                def _(j):
                    sl = pl.ds(j * SC_LANES, SC_LANES)
                    if p == 0:
                        a_hi = jnp.zeros((SC_LANES,), F32)
                        a_lo = jnp.zeros((SC_LANES,), F32)
                    else:
                        a_hi = out_v[i * CHUNKS + r, sl]
                        a_lo = out_v[i * CHUNKS + ROWS_PER_EXPERT + r, sl]
                    for k in range(PEER_TOPK):
                        hi, lo = _unpack(rows_v[slot, r * PEER_TOPK + k, sl])
                        a_hi = a_hi + ws[k] * hi
                        a_lo = a_lo + ws[k] * lo
                    out_v[i * CHUNKS + r, sl] = a_hi
                    out_v[i * CHUNKS + ROWS_PER_EXPERT + r, sl] = a_lo

        def pair_dots(i, p, slot):
            accs = tuple(jnp.zeros((SC_LANES,), F32) for _ in range(PEER_TOPK))
            for r in range(ROWS_PER_EXPERT):
                def piece(j, accs, r=r):
                    sl = pl.ds(j * SC_LANES, SC_LANES)
                    ha = aux_v[i * CHUNKS + r, sl]
                    hb = aux_v[i * CHUNKS + ROWS_PER_EXPERT + r, sl]
                    out = []
                    for k in range(PEER_TOPK):
                        hi, lo = _unpack(rows_v[slot, r * PEER_TOPK + k, sl])
                        out.append(accs[k] + (hi * ha + lo * hb))
                    return tuple(out)
                accs = lax.fori_loop(0, WORD_ROWS, piece, accs)
            for k in range(PEER_TOPK):
                fold_v[k, :] = accs[k]
            tot = plsc.load_gather(fold_v, [lane, lane * 0])
            for l in range(1, SC_LANES):
                tot = tot + plsc.load_gather(fold_v, [lane, lane * 0 + l])
            out_v[i, pl.ds(p * PEER_TOPK, PEER_TOPK)] = tot

        consume = pair_dots if down else weighted_sum

        @pl.loop(0, per // SC_TOKEN_CHUNK)
        def _(c):
            t0 = wid * per + c * SC_TOKEN_CHUNK
            pltpu.sync_copy(e_hbm.at[pl.ds(t0, SC_TOKEN_CHUNK)], e_v)
            pltpu.sync_copy(aux_hbm.at[pl.ds(t0 * aux_rows, SC_TOKEN_CHUNK * aux_rows)], aux_v)
            for p in range(SC_RING):
                start_gather(0, p, p)

            @pl.loop(0, SC_TOKEN_CHUNK)
            def _(i):
                for p in range(PEER_HEADS):
                    slot = p % SC_RING
                    for d in gathers(i, p, slot):
                        d.wait()
                    consume(i, p, slot)
                    if p + SC_RING < PEER_HEADS:
                        start_gather(i, p + SC_RING, slot)
                    else:
                        @pl.when(i + 1 < SC_TOKEN_CHUNK)
                        def _():
                            start_gather(i + 1, p + SC_RING - PEER_HEADS, slot)

            pltpu.sync_copy(out_v, o_hbm.at[pl.ds(t0 * out_rows, SC_TOKEN_CHUNK * out_rows)])

    return body(e_t, aux, tab)


def _pack_table(tab):
    bits = lax.bitcast_convert_type(tab.astype(BF16), jnp.uint16).astype(jnp.uint32)
    words = (bits[:, :HALF] << 16) | bits[:, HALF:]
    return words.reshape(tab.shape[0] * ROWS_PER_EXPERT, 128)


PEER_TOKENS = 128
SC_LEAD_BATCHES = (1, 3)
SC_SHARE_DOWN = (3, 8)
SC_SHARE_UP = (19, 48)
SC_MIN_TOKENS = 4096


def _sc_tokens(n):
    if n < SC_MIN_TOKENS:
        return (0, 0)
    unit = math.lcm(PEER_TOKENS, SC_CORES * SC_SUBCORES * SC_TOKEN_CHUNK)
    return tuple(n * num // den // unit * unit for num, den in (SC_SHARE_DOWN, SC_SHARE_UP))


def _row_tile(n, pref):
    while n % pref:
        pref //= 2
    return pref


def _peer_on_sc_down(h3, qp, k1_bf, k2_bf, tab_u, tt, route_after=()):
    e_t, gate_t = _route(qp, k1_bf, k2_bf, tt, after=route_after)
    return e_t, gate_t, _peer_sc(e_t, h3, tab_u, down=True)


def _peer_on_sc_up(e_t, gate_t, act, tab_v, tt, gate_after):
    w_t = _gate_act(act, gate_t, act, 0, tt, after=gate_after)
    return _peer_sc(e_t, w_t, tab_v, down=False), w_t


def _peer_and_final(x2, h3, qp, k1_bf, k2_bf, tab_u, tab_v, g_final, tt, n_sc=(0, 0), route_after=(), out_rows=None,
                    lead=None, y_start=None):
    n = x2.shape[0]
    n_sc_down, n_sc_up = n_sc
    tc_down, tc_up = n - n_sc_down, n - n_sc_up
    if n_sc_down:
        e_t, gate_t = _route(qp, k1_bf, k2_bf, tt, tc_down, n_sc_down, after=route_after)
        act = _peer_sc(e_t[tc_down:], h3[tc_down * CHUNKS:], tab_u, down=True)
        e_t, gate_t = _route(qp, k1_bf, k2_bf, tt, 0, tc_down, into=(e_t, gate_t))
    else:
        e_t, gate_t = _route(qp, k1_bf, k2_bf, tt)
    w_t = _peer_u(e_t, h3, gate_t, tab_u, tt, tc_down)
    y = y_start
    if lead is not None:
        x2_lead, add8_lead, y_first = lead
        y = _residual_norm(x2_lead, add8_lead, g_final, y, 0, y_first, tt, out_rows=out_rows, after=(w_t,))
    if n_sc_down:
        w_t = _gate_act(act, gate_t, w_t, tc_down, tt, after=() if y is None else (y,))
    y = _peer_v(e_t, w_t, x2, g_final, tab_v, tt, tc_up, into=y)
    if n_sc_up:
        add8 = _peer_sc(e_t[tc_up:], w_t[tc_up:], tab_v, down=False)
        y = _residual_norm(x2, add8, g_final, y, tc_up, tc_up, tt)
    return y


def kernel(x_prompt, x_sample, mem_prompt, cache_da_k, cache_da_v, state_ret, cache_mem_k, cache_mem_v, g_mix, w_in, lam_q1, lam_k1, lam_q2, lam_k2, g_da, g_ret, w_out, g_cross, g_mem, w_xq, w_xk, w_xv, w_xo, g_ffn, w_pq, peer_k1, peer_k2, peer_u, peer_v, g_final):
    depth = w_in.shape[0]
    assert depth == 1, "single-layer step"
    l = 0
    lam_init = 0.8 - 0.6 * math.exp(-0.3 * l)
    b, t, _ = x_prompt.shape
    bs, ts, _ = x_sample.shape
    past_len = cache_da_k.shape[2]

    row = lambda a: a.reshape(1, -1)
    w_in_bf = w_in[l].astype(BF16)
    w_out_bf = w_out[l].astype(BF16)
    w_xq_bf, w_xk_bf, w_xv_bf, w_xo_bf = (w[l].astype(BF16) for w in (w_xq, w_xk, w_xv, w_xo))
    w_pq_bf = w_pq[l].astype(BF16)
    k1_bf, k2_bf = peer_k1[l].astype(BF16), peer_k2[l].astype(BF16)
    tab_u, tab_v = _pack_table(peer_u[l]), _pack_table(peer_v[l])
    lamp = jnp.stack([lam_q1[l], lam_k1[l], lam_q2[l], lam_k2[l]])
    g_da3 = g_da[l].reshape(H_D, 1, DV_D)
    g_ret3 = g_ret[l].reshape(H_R, 1, DV_R)
    g_fin = row(g_final)

    def mixer_mid(x2d, first_row, mda, mret, mk, mv, bb, tt_rows):
        n = bb * tt_rows
        tm = _row_tile(n, 512)
        x1, qx = _out_proj(x2d, mda.reshape(n, GROUP_W), mret.reshape(n, GROUP_W), w_out_bf, row(g_cross[l]), w_xq_bf,
                           tm, first_row)
        x2, h3, qp = _cross(x1.reshape(bb, tt_rows, D_MODEL), qx.reshape(bb, tt_rows, D_MODEL), mk, mv,
                            w_xo_bf, row(g_ffn[l]), w_pq_bf, _row_tile(tt_rows, 512))
        return x2.reshape(n, D_MODEL), h3, qp.reshape(n, -1)

    peer_args = (k1_bf, k2_bf, tab_u, tab_v, g_fin, PEER_TOKENS)
    n = b * t
    xp = x_prompt.reshape(n, D_MODEL)

    def prompt_dense(b0, bg, kv_into=None):
        rows = bg * t
        qd, kd, vd, kdb, vdb, qr, kr, vr, gr = _in_proj(xp, row(g_mix[l]), w_in_bf, _row_tile(rows, 512),
                                                       b0 * t, rows, kv_into)
        r3 = lambda a: a.reshape(bg, t, GROUP_W)
        mda = _diff_attn_prompt(lamp, r3(qd), r3(kdb), r3(vdb), g_da3, lam_init, 512, 512)
        mret, s_fin = _retention(r3(qr), r3(kr), r3(vr), r3(gr), g_ret3, None, 512)
        return mixer_mid(xp, b0 * t, mda, mret, mkb[b0:b0 + bg], mvb[b0:b0 + bg], bg, t), (kd, vd), s_fin

    mk, mv, mkb, mvb = _mem_kv(mem_prompt.reshape(b * N_MEM, D_MODEL), row(g_mem[l]), w_xk_bf, w_xv_bf, 512)
    mkb, mvb = mkb.reshape(b, N_MEM, D_MODEL), mvb.reshape(b, N_MEM, D_MODEL)
    leads = [bg for bg in SC_LEAD_BATCHES if bg] if sum(SC_LEAD_BATCHES) < b else []
    kv, states = None, []
    y = None
    launched = None
    finished = None
    hi = b
    for bg in leads + [hi - sum(leads)]:
        b0, hi = hi - bg, hi - bg
        peer_in, kv, s_g = prompt_dense(b0, bg, kv_into=kv)
        states.insert(0, s_g)
        x2_g, h3_g, qp_g = peer_in
        if finished is not None:
            y = _residual_norm(finished[0], finished[1], g_fin, y, 0, finished[2], PEER_TOKENS, out_rows=n, after=(qp_g,))
            finished = None
        route_after = ()
        if launched is not None:
            x2_p, e_p, gate_p, act_p, row_p = launched
            add8_p, w_p = _peer_on_sc_up(e_p, gate_p, act_p, tab_v, PEER_TOKENS,
                                         gate_after=(qp_g,) + (() if y is None else (y,)))
            finished, route_after = (x2_p, add8_p, row_p), (w_p,)
        if b0 > 0:
            launched = (x2_g,) + _peer_on_sc_down(h3_g, qp_g, k1_bf, k2_bf, tab_u, PEER_TOKENS, route_after) + (b0 * t,)
        else:
            y = _peer_and_final(*peer_in, *peer_args, _sc_tokens(bg * t), route_after=route_after, out_rows=n,
                                lead=finished, y_start=y)
    kd, vd = kv
    y_prompt = y.reshape(b, t, D_MODEL)
    s_fin = jnp.concatenate(states) if len(states) > 1 else states[0]

    ns = bs * ts
    xs = x_sample.reshape(ns, D_MODEL)
    qd_s, kd_s, vd_s, kdb_s, vdb_s, qr_s, kr_s, vr_s, gr_s = _in_proj(xs, row(g_mix[l]), w_in_bf, _row_tile(ns, 512))
    s3 = lambda a: a.reshape(bs, ts, GROUP_W)
    mda_s = _diff_attn_sample(lamp, s3(qd_s), cache_da_k[l].reshape(bs, past_len, GROUP_W),
                              cache_da_v[l].reshape(bs, past_len, GROUP_W), s3(kdb_s), s3(vdb_s), g_da3, lam_init)
    mret_s, s_new = _retention(s3(qr_s), s3(kr_s), s3(vr_s), s3(gr_s), g_ret3, state_ret[l], ts)
    peer_s = mixer_mid(xs, 0, mda_s, mret_s, cache_mem_k[l].reshape(bs, N_MEM, D_MODEL),
                       cache_mem_v[l].reshape(bs, N_MEM, D_MODEL), bs, ts)
    y_sample = _peer_and_final(*peer_s, *peer_args).reshape(bs, ts, D_MODEL)

    return (y_prompt, y_sample,
            kd.reshape(1, b, t, H_D, 2, DK_D), vd.reshape(1, b, t, H_D, DV_D), s_fin[None],
            mk.reshape(1, b, N_MEM, H_X, HD_X), mv.reshape(1, b, N_MEM, H_X, HD_X),
            kd_s.reshape(1, bs, ts, H_D, 2, DK_D), vd_s.reshape(1, bs, ts, H_D, DV_D), s_new[None])
```

```python
import functools
import math

import jax
import jax.numpy as jnp
from jax import lax
from jax.experimental import pallas as pl
from jax.experimental.pallas import tpu as pltpu
from jax.experimental.pallas import tpu_sc as plsc

D_MODEL = 1024
CHUNK = 64
CHUNK_SHIFT = CHUNK.bit_length() - 1
assert 1 << CHUNK_SHIFT == CHUNK
H_D, DK_D, DV_D = 4, 64, 128
H_R, DK_R, DV_R = 4, 128, 128
N_MEM = 256
H_X = 4
HD_X = D_MODEL // H_X
PEER_HEADS = 8
N_KEYS = 128
N_EXPERTS = N_KEYS * N_KEYS
PEER_TOPK = 16
EPS = 1e-6
HEAD_W = 128
GROUP_W = 512
N_PAIRS = PEER_HEADS * PEER_TOPK
HALF = D_MODEL // 2
ROWS_PER_EXPERT = HALF // 128
VMEM_LIMIT = 56 * 1024 * 1024

BF16 = jnp.bfloat16
F32 = jnp.float32


def _cparams(sem):
    return pltpu.CompilerParams(dimension_semantics=sem, vmem_limit_bytes=VMEM_LIMIT)


def _rms(x, g):
    return x * lax.rsqrt(jnp.mean(x * x, axis=-1, keepdims=True) + EPS) * g


def _dot(a, b):
    return jnp.dot(a, b, preferred_element_type=F32)


def _dot_nt(a, b):
    return lax.dot_general(a, b, (((1,), (1,)), ((), ())), preferred_element_type=F32)


def _dot_tn(a, b):
    return lax.dot_general(a, b, (((0,), (0,)), ((), ())), preferred_element_type=F32)


def _select_by_head(h, values):
    out = jnp.float32(values[-1])
    for i in range(len(values) - 2, -1, -1):
        out = jnp.where(h == i, jnp.float32(values[i]), out)
    return out


def _in_proj_kernel(x_ref, g_ref, w_ref, *rest):
    qd_ref, kd_ref, vd_ref, kdb_ref, vdb_ref, qr_ref, kr_ref, vr_ref, gr_ref = rest[-9:]
    hb = _rms(x_ref[...], g_ref[...]).astype(BF16)
    col = lambda c: _dot(hb, w_ref[:, c * GROUP_W:(c + 1) * GROUP_W])
    qd_ref[...] = (col(0) * (DK_D ** -0.5)).astype(BF16)
    kd = col(1)
    kd_ref[...] = kd.reshape(kd_ref.shape)
    kdb_ref[...] = kd.astype(BF16)
    vd = col(2)
    vd_ref[...] = vd.reshape(vd_ref.shape)
    vdb_ref[...] = vd.astype(BF16)
    qr_ref[...] = col(3).astype(BF16)
    kr_ref[...] = (col(4) * (DK_R ** -0.5)).astype(BF16)
    vr_ref[...] = col(5).astype(BF16)
    gr_ref[...] = col(6)


def _in_proj(x2d, g, w_bf, tm, first_row=0, n_rows=None, kv_into=None):
    n_all = x2d.shape[0]
    n = n_all if n_rows is None else n_rows
    off = first_row // tm
    blk = lambda: pl.BlockSpec((tm, GROUP_W), lambda i: (i, 0))
    k_dims, v_dims = (H_D, 2, DK_D), (H_D, DV_D)
    blk_all = lambda dims: pl.BlockSpec((tm,) + dims, lambda i: (i + off,) + (0,) * len(dims))
    sh = lambda dt: jax.ShapeDtypeStruct((n, GROUP_W), dt)
    sh_all = lambda dims: jax.ShapeDtypeStruct((n_all,) + dims, F32)
    args = [x2d, g, w_bf]
    in_specs = [pl.BlockSpec((tm, D_MODEL), lambda i: (i + off, 0)),
                pl.BlockSpec((1, D_MODEL), lambda i: (0, 0)),
                pl.BlockSpec(w_bf.shape, lambda i: (0, 0))]
    aliases = {}
    if kv_into is not None:
        args += list(kv_into)
        in_specs += [pl.BlockSpec(memory_space=pl.ANY)] * 2
        aliases = {3: 1, 4: 2}
    return pl.pallas_call(
        _in_proj_kernel,
        grid=(n // tm,),
        in_specs=in_specs,
        out_specs=[blk(), blk_all(k_dims), blk_all(v_dims)] + [blk() for _ in range(6)],
        out_shape=[sh(BF16), sh_all(k_dims), sh_all(v_dims), sh(BF16), sh(BF16), sh(BF16), sh(BF16), sh(BF16), sh(F32)],
        input_output_aliases=aliases,
        compiler_params=_cparams(("parallel",)),
        name="in_proj",
    )(*args)


def _lambda_from(lam_ref, lam_init):
    l = lam_ref[...]
    a = jnp.exp(jnp.sum(l[0:1] * l[1:2], axis=-1, keepdims=True))
    b = jnp.exp(jnp.sum(l[2:3] * l[3:4], axis=-1, keepdims=True))
    return a - b + lam_init


def _diff_post(acc, l, lam, g, lam_init, tq):
    o = acc[:tq] / l[:tq] - lam * (acc[tq:] / l[tq:])
    return o * lax.rsqrt(jnp.mean(o * o, axis=-1, keepdims=True) + EPS) * g * (1.0 - lam_init)


def _split_maps(q):
    lane = lax.broadcasted_iota(jnp.int32, q.shape, 1)
    zero = jnp.zeros_like(q)
    return jnp.concatenate([jnp.where(lane < DK_D, q, zero), jnp.where(lane >= DK_D, q, zero)], axis=0)


def _da_prompt_kernel(lam_ref, q_ref, k_ref, v_ref, g_ref, o_ref, kx_ref, vx_ref, own_ref, acc_ref, m_ref, *, lam_init, tq, tk):
    h = pl.program_id(1)
    i = pl.program_id(2)
    t = k_ref.shape[0]
    slope = _select_by_head(h, [2.0 ** (-8.0 * (j + 1) / H_D) for j in range(H_D)])

    @pl.when(i == 0)
    def _():
        pos = lax.broadcasted_iota(jnp.int32, (t, HEAD_W), 0)
        lane = lax.broadcasted_iota(jnp.int32, (t, HEAD_W), 1)
        coarse = ((pos >> CHUNK_SHIFT) << CHUNK_SHIFT).astype(F32) * slope
        fine = (pos & (CHUNK - 1)).astype(F32) * slope
        kx_ref[:, :HEAD_W] = k_ref[...]
        kx_ref[:, HEAD_W:] = jnp.where(lane == 0, coarse, jnp.where(lane == 1, fine, 0.0)).astype(BF16)
        vx_ref[:DV_D, :] = v_ref[...].astype(F32).T.astype(BF16)
        vx_ref[DV_D:, :] = jnp.ones((vx_ref.shape[0] - DV_D, t), BF16)
        krel = lax.broadcasted_iota(jnp.int32, (tk, 2 * tq), 0)
        c = lax.broadcasted_iota(jnp.int32, (tk, 2 * tq), 1)
        for par in range(tk // tq):
            qrel = par * tq + jnp.where(c >= tq, c - tq, c)
            ahead = (2.0 * slope) * jnp.maximum(krel - qrel, 0).astype(F32)
            own_ref[par] = jnp.where((qrel >> CHUNK_SHIFT) >= (krel >> CHUNK_SHIFT), -ahead, -1e30)

    q = q_ref[...]
    lane = lax.broadcasted_iota(jnp.int32, q.shape, 1)
    zero = jnp.zeros_like(q)
    ones2 = jnp.where(lane < 2, 1.0, 0.0).astype(BF16)
    q2 = jnp.concatenate([jnp.concatenate([jnp.where(lane < DK_D, q, zero), ones2], axis=1),
                          jnp.concatenate([jnp.where(lane >= DK_D, q, zero), ones2], axis=1)], axis=0)
    jd = (i * tq) // tk

    def scores(j):
        return _dot_nt(kx_ref[pl.ds(pl.multiple_of(j * tk, tk), tk), :], q2)

    def values(j):
        return vx_ref[:, pl.ds(pl.multiple_of(j * tk, tk), tk)]

    s = scores(jd) + own_ref[(i * tq) % tk // tq]
    m0 = jnp.max(s, axis=0, keepdims=True)
    m_ref[...] = m0
    acc_ref[...] = _dot(values(jd), jnp.exp(s - m0).astype(BF16))

    def absorb(blocks):
        ss = [scores(j) for j in blocks]
        m_old = m_ref[...]
        m_new = m_old
        for s in ss:
            m_new = jnp.maximum(m_new, jnp.max(s, axis=0, keepdims=True))
        m_ref[...] = m_new
        acc = jnp.exp(m_old - m_new) * acc_ref[...]
        for j, s in zip(blocks, ss):
            acc = acc + _dot(values(j), jnp.exp(s - m_new).astype(BF16))
        acc_ref[...] = acc

    def past_pair(jj, carry):
        absorb([2 * jj, 2 * jj + 1])
        return carry

    lax.fori_loop(0, jd // 2, past_pair, 0)

    @pl.when(jd % 2 == 1)
    def _():
        absorb([jd - 1])

    acc = acc_ref[...]
    num, den = acc[:DV_D], acc[DV_D:DV_D + 1]
    lam = _lambda_from(lam_ref, lam_init)
    o = (num[:, :tq] / den[:, :tq] - lam * (num[:, tq:] / den[:, tq:])).T
    o = o * lax.rsqrt(jnp.mean(o * o, axis=-1, keepdims=True) + EPS) * g_ref[...] * (1.0 - lam_init)
    o_ref[...] = o.astype(o_ref.dtype)


ONES_ROWS = 16


def _diff_attn_prompt(lamp, q, k, v, g_da3, lam_init, tq, tk):
    b, t, _ = q.shape
    kern = functools.partial(_da_prompt_kernel, lam_init=lam_init, tq=tq, tk=tk)
    return pl.pallas_call(
        kern,
        grid=(b, H_D, t // tq),
        in_specs=[pl.BlockSpec((4, DK_D), lambda b_, h, i: (0, 0)),
                  pl.BlockSpec((None, tq, HEAD_W), lambda b_, h, i: (b_, i, h)),
                  pl.BlockSpec((None, t, HEAD_W), lambda b_, h, i: (b_, 0, h)),
                  pl.BlockSpec((None, t, HEAD_W), lambda b_, h, i: (b_, 0, h)),
                  pl.BlockSpec((None, 1, HEAD_W), lambda b_, h, i: (h, 0, 0))],
        out_specs=pl.BlockSpec((None, tq, HEAD_W), lambda b_, h, i: (b_, i, h)),
        out_shape=jax.ShapeDtypeStruct((b, t, GROUP_W), BF16),
        scratch_shapes=[pltpu.VMEM((t, 2 * HEAD_W), BF16), pltpu.VMEM((DV_D + ONES_ROWS, t), BF16),
                        pltpu.VMEM((tk // tq, tk, 2 * tq), F32),
                        pltpu.VMEM((DV_D + ONES_ROWS, 2 * tq), F32), pltpu.VMEM((1, 2 * tq), F32)],
        compiler_params=_cparams(("parallel", "parallel", "arbitrary")),
        name="diff_attn_prompt",
    )(lamp, q, k, v, g_da3)


def _da_sample_kernel(lam_ref, q_ref, kc_ref, vc_ref, kn_ref, vn_ref, g_ref, o_ref, *, lam_init, ts, past_len):
    h = pl.program_id(1)
    slope = _select_by_head(h, [2.0 ** (-8.0 * (j + 1) / H_D) for j in range(H_D)])
    q2 = _split_maps(q_ref[...])
    rows = 2 * ts

    def scores(k, base, n):
        r = lax.broadcasted_iota(jnp.int32, (rows, n), 0)
        qpos = past_len + jnp.where(r >= ts, r - ts, r)
        kpos = base + lax.broadcasted_iota(jnp.int32, (rows, n), 1)
        return _dot_nt(q2, k) - slope * jnp.abs(qpos - kpos).astype(F32)

    sc = scores(kc_ref[...].astype(BF16), 0, past_len)
    sn = scores(kn_ref[...], past_len, ts)
    m = jnp.maximum(jnp.max(sc, axis=-1, keepdims=True), jnp.max(sn, axis=-1, keepdims=True))
    pc = jnp.exp(sc - m)
    pn = jnp.exp(sn - m)
    l = jnp.sum(pc, axis=-1, keepdims=True) + jnp.sum(pn, axis=-1, keepdims=True)
    acc = _dot(pc.astype(BF16), vc_ref[...].astype(BF16)) + _dot(pn.astype(BF16), vn_ref[...])
    lam = _lambda_from(lam_ref, lam_init)
    o_ref[...] = _diff_post(acc, l, lam, g_ref[...], lam_init, ts).astype(o_ref.dtype)


def _diff_attn_sample(lamp, q, kc, vc, kn, vn, g_da3, lam_init):
    b, ts, _ = q.shape
    past_len = kc.shape[1]
    kern = functools.partial(_da_sample_kernel, lam_init=lam_init, ts=ts, past_len=past_len)
    head = lambda rows: pl.BlockSpec((None, rows, HEAD_W), lambda b_, h: (b_, 0, h))
    return pl.pallas_call(
        kern,
        grid=(b, H_D),
        in_specs=[pl.BlockSpec((4, DK_D), lambda b_, h: (0, 0)),
                  head(ts), head(past_len), head(past_len), head(ts), head(ts),
                  pl.BlockSpec((None, 1, HEAD_W), lambda b_, h: (h, 0, 0))],
        out_specs=head(ts),
        out_shape=jax.ShapeDtypeStruct((b, ts, GROUP_W), BF16),
        compiler_params=_cparams(("parallel", "parallel")),
        name="diff_attn_sample",
    )(lamp, q, kc, vc, kn, vn, g_da3)


def _ret_kernel(*refs, lb, has_init):
    if has_init:
        q_ref, k_ref, v_ref, gate_ref, g_ref, s0_ref, o_ref, sfin_ref, s_ref = refs
    else:
        q_ref, k_ref, v_ref, gate_ref, g_ref, o_ref, sfin_ref, s_ref = refs
    h = pl.program_id(1)
    c = pl.program_id(2)
    lg = _select_by_head(h, [math.log1p(-(2.0 ** (-5.0 - j))) for j in range(H_R)])

    @pl.when(c == 0)
    def _():
        s_ref[...] = s0_ref[...] if has_init else jnp.zeros_like(s_ref)

    q, k, v = q_ref[...], k_ref[...], v_ref[...]
    i = lax.broadcasted_iota(jnp.int32, (lb, lb), 0)
    j = lax.broadcasted_iota(jnp.int32, (lb, lb), 1)
    d = (i - j).astype(F32)
    decay = jnp.where(d >= 0, jnp.exp(jnp.maximum(d, 0.0) * lg), 0.0)
    inner = _dot_nt(q, k) * decay
    ic = lax.broadcasted_iota(jnp.int32, (lb, 1), 0).astype(F32)
    s_old = s_ref[...]
    o = _dot(inner.astype(BF16), v) + _dot(q, s_old.astype(BF16)) * jnp.exp((ic + 1.0) * lg)
    tail = jnp.exp((lb - 1.0 - ic) * lg)
    kt = (k.astype(F32) * tail).astype(BF16)
    s_new = jnp.exp(lb * lg) * s_old + _dot_tn(kt, v)
    s_ref[...] = s_new

    @pl.when(c == pl.num_programs(2) - 1)
    def _():
        sfin_ref[...] = s_new

    oc = o - jnp.mean(o, axis=-1, keepdims=True)
    y = oc * lax.rsqrt(jnp.mean(oc * oc, axis=-1, keepdims=True) + EPS) * g_ref[...]
    gate = gate_ref[...]
    o_ref[...] = (y * (gate * jax.nn.sigmoid(gate))).astype(o_ref.dtype)


def _retention(q, k, v, gate, g_ret3, s0, lb):
    b, t, _ = q.shape
    has_init = s0 is not None
    kern = functools.partial(_ret_kernel, lb=lb, has_init=has_init)
    head = lambda: pl.BlockSpec((None, lb, HEAD_W), lambda b_, h, c: (b_, c, h))
    state = lambda: pl.BlockSpec((None, None, DK_R, DV_R), lambda b_, h, c: (b_, h, 0, 0))
    in_specs = [head(), head(), head(), head(), pl.BlockSpec((None, 1, HEAD_W), lambda b_, h, c: (h, 0, 0))]
    args = [q, k, v, gate, g_ret3]
    if has_init:
        in_specs.append(state())
        args.append(s0)
    return pl.pallas_call(
        kern,
        grid=(b, H_R, t // lb),
        in_specs=in_specs,
        out_specs=[head(), state()],
        out_shape=[jax.ShapeDtypeStruct((b, t, GROUP_W), BF16), jax.ShapeDtypeStruct((b, H_R, DK_R, DV_R), F32)],
        scratch_shapes=[pltpu.VMEM((DK_R, DV_R), F32)],
        compiler_params=_cparams(("parallel", "parallel", "arbitrary")),
        name="retention",
    )(*args)


def _out_proj_kernel(x_ref, mda_ref, mret_ref, wo_ref, g_ref, wq_ref, x1_ref, qx_ref):
    x1 = x_ref[...] + _dot(mda_ref[...], wo_ref[:GROUP_W, :]) + _dot(mret_ref[...], wo_ref[GROUP_W:, :])
    x1_ref[...] = x1
    hn = _rms(x1, g_ref[...]).astype(BF16)
    qx_ref[...] = (_dot(hn, wq_ref[...]) * (HD_X ** -0.5)).astype(BF16)


def _out_proj(x2d, mda, mret, wo_bf, g_cross, wq_bf, tm, first_row=0):
    n = mda.shape[0]
    off = first_row // tm
    full = lambda a: pl.BlockSpec(a.shape, lambda i: (0, 0))
    return pl.pallas_call(
        _out_proj_kernel,
        grid=(n // tm,),
        in_specs=[pl.BlockSpec((tm, D_MODEL), lambda i: (i + off, 0)),
                  pl.BlockSpec((tm, GROUP_W), lambda i: (i, 0)),
                  pl.BlockSpec((tm, GROUP_W), lambda i: (i, 0)),
                  full(wo_bf), full(g_cross), full(wq_bf)],
        out_specs=[pl.BlockSpec((tm, D_MODEL), lambda i: (i, 0)), pl.BlockSpec((tm, D_MODEL), lambda i: (i, 0))],
        out_shape=[jax.ShapeDtypeStruct((n, D_MODEL), F32), jax.ShapeDtypeStruct((n, D_MODEL), BF16)],
        compiler_params=_cparams(("parallel",)),
        name="out_proj",
    )(x2d, mda, mret, wo_bf, g_cross, wq_bf)


def _mem_kv_kernel(m_ref, g_ref, wk_ref, wv_ref, mk_ref, mv_ref, mkb_ref, mvb_ref):
    mn = _rms(m_ref[...], g_ref[...]).astype(BF16)
    mk = _dot(mn, wk_ref[...])
    mv = _dot(mn, wv_ref[...])
    mk_ref[...] = mk
    mv_ref[...] = mv
    mkb_ref[...] = mk.astype(BF16)
    mvb_ref[...] = mv.astype(BF16)


def _mem_kv(mem2d, g_mem, wk_bf, wv_bf, tm):
    n = mem2d.shape[0]
    row = lambda: pl.BlockSpec((tm, D_MODEL), lambda i: (i, 0))
    full = lambda a: pl.BlockSpec(a.shape, lambda i: (0, 0))
    sh = lambda dt: jax.ShapeDtypeStruct((n, D_MODEL), dt)
    return pl.pallas_call(
        _mem_kv_kernel,
        grid=(n // tm,),
        in_specs=[row(), full(g_mem), full(wk_bf), full(wv_bf)],
        out_specs=[row(), row(), row(), row()],
        out_shape=[sh(F32), sh(F32), sh(BF16), sh(BF16)],
        compiler_params=_cparams(("parallel",)),
        name="mem_kv",
    )(mem2d, g_mem, wk_bf, wv_bf)


def _cross_kernel(x1_ref, qx_ref, mk_ref, mv_ref, wo_ref, g_ref, wpq_ref, x2_ref, h3_ref, qp_ref):
    q = qx_ref[...]
    heads = []
    for h in range(H_X):
        sl = slice(h * HD_X, (h + 1) * HD_X)
        kh = mk_ref[:, sl] if mk_ref.dtype == BF16 else mk_ref[:, sl].astype(BF16)
        vh = mv_ref[:, sl] if mv_ref.dtype == BF16 else mv_ref[:, sl].astype(BF16)
        s = _dot_nt(q[:, sl], kh)
        p = jnp.exp(s - jnp.max(s, axis=-1, keepdims=True))
        heads.append(_dot(p.astype(BF16), vh) / jnp.sum(p, axis=-1, keepdims=True))
    o = jnp.concatenate(heads, axis=-1).astype(BF16)
    x2 = x1_ref[...] + _dot(o, wo_ref[...])
    x2_ref[...] = x2
    h3 = _rms(x2, g_ref[...])
    h3_ref[...] = h3.reshape(h3_ref.shape)
    qp_ref[...] = _dot(h3.astype(BF16), wpq_ref[...]).astype(BF16)


def _mix_cross_kernel(x_ref, mda_ref, mret_ref, wout_ref, gc_ref, wq_ref, mk_ref, mv_ref, wo_ref, g_ref, wpq_ref,
                      x2_ref, h3_ref, qp_ref, x1_ref, qx_ref):
    _out_proj_kernel(x_ref, mda_ref, mret_ref, wout_ref, gc_ref, wq_ref, x1_ref, qx_ref)
    _cross_kernel(x1_ref, qx_ref, mk_ref, mv_ref, wo_ref, g_ref, wpq_ref, x2_ref, h3_ref, qp_ref)


def _mix_cross(x2d, first_row, mda, mret, wout_bf, g_cross, wq_bf, mk, mv, wo_bf, g_ffn, wpq_bf, bb, t, tm):
    nt = t // tm
    off = first_row // tm
    full = lambda a: pl.BlockSpec(a.shape, lambda b_, i: (0,) * a.ndim)
    rows2d = lambda w, o: pl.BlockSpec((tm, w), lambda b_, i: (o + b_ * nt + i, 0))
    mem = lambda: pl.BlockSpec((None, N_MEM, D_MODEL), lambda b_, i: (b_, 0, 0))
    dq = wpq_bf.shape[1]
    n = bb * t
    return pl.pallas_call(
        _mix_cross_kernel,
        grid=(bb, nt),
        in_specs=[rows2d(D_MODEL, off), rows2d(GROUP_W, 0), rows2d(GROUP_W, 0), full(wout_bf), full(g_cross), full(wq_bf),
                  mem(), mem(), full(wo_bf), full(g_ffn), full(wpq_bf)],
        out_specs=[rows2d(D_MODEL, 0), pl.BlockSpec((tm * CHUNKS, 128), lambda b_, i: (b_ * nt + i, 0)), rows2d(dq, 0)],
        out_shape=[jax.ShapeDtypeStruct((n, D_MODEL), F32), jax.ShapeDtypeStruct((n * CHUNKS, 128), F32),
                   jax.ShapeDtypeStruct((n, dq), BF16)],
        scratch_shapes=[pltpu.VMEM((tm, D_MODEL), F32), pltpu.VMEM((tm, D_MODEL), BF16)],
        compiler_params=_cparams(("parallel", "parallel")),
        name="mix_cross",
    )(x2d, mda, mret, wout_bf, g_cross, wq_bf, mk, mv, wo_bf, g_ffn, wpq_bf)


def _cross(x1, qx, mk, mv, wo_bf, g_ffn, wpq_bf, tm):
    b, t, _ = x1.shape
    row = lambda w: pl.BlockSpec((None, tm, w), lambda b_, i: (b_, i, 0))
    mem = lambda: pl.BlockSpec((None, N_MEM, D_MODEL), lambda b_, i: (b_, 0, 0))
    full = lambda a: pl.BlockSpec(a.shape, lambda b_, i: (0, 0))
    dq = wpq_bf.shape[1]
    return pl.pallas_call(
        _cross_kernel,
        grid=(b, t // tm),
        in_specs=[row(D_MODEL), row(D_MODEL), mem(), mem(), full(wo_bf), full(g_ffn), full(wpq_bf)],
        out_specs=[row(D_MODEL),
                   pl.BlockSpec((tm * CHUNKS, 128), lambda b_, i: (b_ * (t // tm) + i, 0)),
                   row(dq)],
        out_shape=[jax.ShapeDtypeStruct((b, t, D_MODEL), F32),
                   jax.ShapeDtypeStruct((b * t * CHUNKS, 128), F32),
                   jax.ShapeDtypeStruct((b, t, dq), BF16)],
        compiler_params=_cparams(("parallel", "parallel")),
        name="cross_attn",
    )(x1, qx, mk, mv, wo_bf, g_ffn, wpq_bf)


ID_PAD = 2.0 ** 29


def _topk_rows(s, ids):
    vals, sel = [], []
    for _ in range(PEER_TOPK):
        m = jnp.max(s, axis=0, keepdims=True)
        idx = jnp.min(jnp.where(s == m, ids, ID_PAD), axis=0, keepdims=True)
        vals.append(m)
        sel.append(idx)
        s = jnp.where(ids == idx, -jnp.inf, s)
    return jnp.concatenate(vals, axis=0), jnp.concatenate(sel, axis=0)


def _candidates(v1, i1, v2, i2):
    lanes = v1.shape[1]
    b8 = lax.broadcasted_iota(jnp.int32, (8, lanes), 0)
    ident = lambda a, ia, ib, b: (a * PEER_TOPK + b) * float(N_EXPERTS) + (ia * float(N_KEYS) + ib)
    b16 = lax.broadcasted_iota(jnp.int32, (PEER_TOPK, lanes), 0).astype(F32)
    vals = [v1[0:1] + v2]
    ids = [ident(0, i1[0:1], i2, b16)]
    for a in range(1, 8):
        keep = b8 < PEER_TOPK // (a + 1)
        vals.append(jnp.where(keep, v1[a:a + 1] + v2[0:8], -jnp.inf))
        ids.append(jnp.where(keep, ident(a, i1[a:a + 1], i2[0:8], b8.astype(F32)), ID_PAD))
    a_hi = (b8 + 8).astype(F32)
    vals.append(v1[8:16] + v2[0:1])
    ids.append(ident(a_hi, i1[8:16], i2[0:1], 0.0))
    return jnp.concatenate(vals, axis=0), jnp.concatenate(ids, axis=0)


def _route_kernel(qp_ref, k1_ref, k2_ref, e_ref, g_ref):
    half = N_KEYS
    tt = qp_ref.shape[0]
    key_id = lax.broadcasted_iota(jnp.int32, (N_KEYS, tt), 0).astype(F32)
    es, gs = [], []
    for p in range(PEER_HEADS):
        q1 = qp_ref[:, (2 * p) * half:(2 * p + 1) * half]
        q2 = qp_ref[:, (2 * p + 1) * half:(2 * p + 2) * half]
        v1, i1 = _topk_rows(_dot_nt(k1_ref[p], q1), key_id)
        v2, i2 = _topk_rows(_dot_nt(k2_ref[p], q2), key_id)
        sc, sel = _topk_rows(*_candidates(v1, i1, v2, i2))
        w = jnp.exp(sc - sc[0:1])
        es.append((sel.astype(jnp.int32) & (N_EXPERTS - 1)) * ROWS_PER_EXPERT)
        gs.append(w / jnp.sum(w, axis=0, keepdims=True))
    e_ref[...] = jnp.concatenate(es, axis=0).T
    g_ref[...] = jnp.concatenate(gs, axis=0).T


def _route_extra_kernel(qp_ref, k1_ref, k2_ref, *rest):
    _route_kernel(qp_ref, k1_ref, k2_ref, rest[-2], rest[-1])


def _route(qp2d, k1_bf, k2_bf, tt, first_token=0, n_tokens=None, into=None, after=()):
    n = qp2d.shape[0]
    n_tokens = n if n_tokens is None else n_tokens
    off = first_token // tt
    full = lambda a: pl.BlockSpec(a.shape, lambda i: (0, 0, 0))
    pair = lambda: pl.BlockSpec((tt, N_PAIRS), lambda i: (i + off, 0))
    in_specs = [pl.BlockSpec((tt, qp2d.shape[1]), lambda i: (i + off, 0)), full(k1_bf), full(k2_bf)]
    args = [qp2d, k1_bf, k2_bf]
    aliases = {}
    if into is not None:
        in_specs += [pl.BlockSpec(memory_space=pl.ANY), pl.BlockSpec(memory_space=pl.ANY)]
        args += list(into)
        aliases = {3: 0, 4: 1}
    in_specs += [pl.BlockSpec(memory_space=pl.ANY)] * len(after)
    args += list(after)
    return pl.pallas_call(
        _route_kernel if len(args) == 3 else _route_extra_kernel,
        grid=(n_tokens // tt,),
        in_specs=in_specs,
        out_specs=[pair(), pair()],
        out_shape=[jax.ShapeDtypeStruct((n, N_PAIRS), jnp.int32), jax.ShapeDtypeStruct((n, N_PAIRS), F32)],
        input_output_aliases=aliases,
        compiler_params=_cparams(("parallel",)),
        name="peer_route",
    )(*args)


def _unpack(words):
    hi = lax.bitcast_convert_type(words & jnp.uint32(0xFFFF0000), F32)
    lo = lax.bitcast_convert_type(words << 16, F32)
    return hi, lo


def _expert_rows(tab_ref, row0):
    return tab_ref[pl.ds(pl.multiple_of(row0, ROWS_PER_EXPERT), ROWS_PER_EXPERT), :]


REDUCE_TOKENS = 16


def _lane_sums_to_rows(y, n_tok):
    hi = y.astype(BF16)
    lo = (y - hi.astype(F32)).astype(BF16)
    ones = jnp.ones((128, N_PAIRS), BF16)
    s = (_dot(hi, ones) + _dot(lo, ones)).reshape(n_tok, N_PAIRS, N_PAIRS)
    eye = lax.broadcasted_iota(jnp.int32, (N_PAIRS, N_PAIRS), 0) == lax.broadcasted_iota(jnp.int32, (N_PAIRS, N_PAIRS), 1)
    return jnp.sum(jnp.where(eye[None], s, 0.0), axis=1)


CHUNKS = D_MODEL // 128


def _peer_u_kernel(e_ref, h8_ref, gate_ref, tab_ref, w_ref, prod_ref, ys_ref, act_ref, *, tt):

    def token(t, slot):
        base = pl.multiple_of(t * CHUNKS, CHUNKS)
        ha = h8_ref[pl.ds(base, ROWS_PER_EXPERT), :]
        hb = h8_ref[pl.ds(pl.multiple_of(base + ROWS_PER_EXPERT, ROWS_PER_EXPERT), ROWS_PER_EXPERT), :]
        prod = prod_ref.at[slot]
        for k in range(N_PAIRS):
            hi, lo = _unpack(_expert_rows(tab_ref, e_ref[t, k]))
            prod[k * ROWS_PER_EXPERT:(k + 1) * ROWS_PER_EXPERT, :] = hi * ha + lo * hb
        y = prod[pl.ds(0, N_PAIRS, stride=ROWS_PER_EXPERT), :]
        for c in range(1, ROWS_PER_EXPERT):
            y = y + prod[pl.ds(c, N_PAIRS, stride=ROWS_PER_EXPERT), :]
        ys_ref[pl.ds(pl.multiple_of(t * N_PAIRS, N_PAIRS), N_PAIRS), :] = y

    def two_tokens(j, carry):
        token(2 * j, 0)
        token(2 * j + 1, 1)
        return carry

    lax.fori_loop(0, tt // 2, two_tokens, 0)

    def group(g, carry):
        rows = REDUCE_TOKENS * N_PAIRS
        y = ys_ref[pl.ds(pl.multiple_of(g * rows, rows), rows), :]
        act_ref[pl.ds(pl.multiple_of(g * REDUCE_TOKENS, REDUCE_TOKENS), REDUCE_TOKENS), :] = _lane_sums_to_rows(y, REDUCE_TOKENS)
        return carry

    lax.fori_loop(0, tt // REDUCE_TOKENS, group, 0)
    w_ref[...] = _gated_gelu(gate_ref[...], act_ref[...])


def _gated_gelu(gate, a):
    return gate * (0.5 * a * (1.0 + lax.erf(a * (2.0 ** -0.5))))


def _gate_act_kernel(act_ref, gate_ref, *rest):
    rest[-1][...] = _gated_gelu(gate_ref[...], act_ref[...])


def _gate_act(act, gate_t, w_t, first_token, tt, after=()):
    off = first_token // tt
    return pl.pallas_call(
        _gate_act_kernel,
        grid=(act.shape[0] // tt,),
        in_specs=[pl.BlockSpec((tt, N_PAIRS), lambda i: (i, 0)),
                  pl.BlockSpec((tt, N_PAIRS), lambda i: (i + off, 0))]
                 + [pl.BlockSpec(memory_space=pl.ANY)] * (1 + len(after)),
        out_specs=pl.BlockSpec((tt, N_PAIRS), lambda i: (i + off, 0)),
        out_shape=jax.ShapeDtypeStruct(w_t.shape, F32),
        input_output_aliases={2: 0},
        compiler_params=_cparams(("arbitrary",)),
        name="peer_gate_act",
    )(act, gate_t, w_t, *after)


def _peer_u(e_t, h, gate_t, tab, tt, n_tokens):
    n = e_t.shape[0]
    assert tt % REDUCE_TOKENS == 0
    kern = functools.partial(_peer_u_kernel, tt=tt)
    pair = lambda **kw: pl.BlockSpec((tt, N_PAIRS), lambda i: (i, 0), **kw)
    return pl.pallas_call(
        kern,
        grid=(n_tokens // tt,),
        in_specs=[pair(memory_space=pltpu.SMEM),
                  pl.BlockSpec((tt * CHUNKS, 128), lambda i: (i, 0)),
                  pair(),
                  pl.BlockSpec(tab.shape, lambda i: (0, 0), pipeline_mode=pl.Buffered(1))],
        out_specs=pair(),
        out_shape=jax.ShapeDtypeStruct((n, N_PAIRS), F32),
        scratch_shapes=[pltpu.VMEM((2, N_PAIRS * ROWS_PER_EXPERT, 128), F32),
                        pltpu.VMEM((tt * N_PAIRS, 128), F32),
                        pltpu.VMEM((tt, N_PAIRS), F32)],
        compiler_params=_cparams(("arbitrary",)),
        name="peer_u",
    )(e_t, h, gate_t, tab)


def _peer_v_kernel(e_ref, w_ref, x_ref, g_ref, tab_ref, o_ref, acc_ref, *, tt):
    n_acc = 2

    def token(t):
        acc_hi = [jnp.zeros((ROWS_PER_EXPERT, 128), F32) for _ in range(n_acc)]
        acc_lo = [jnp.zeros((ROWS_PER_EXPERT, 128), F32) for _ in range(n_acc)]
        for k in range(N_PAIRS):
            hi, lo = _unpack(_expert_rows(tab_ref, e_ref[t, k]))
            w = w_ref[t, k]
            acc_hi[k % n_acc] = acc_hi[k % n_acc] + w * hi
            acc_lo[k % n_acc] = acc_lo[k % n_acc] + w * lo
        base = pl.multiple_of(t * CHUNKS, CHUNKS)
        acc_ref[pl.ds(base, ROWS_PER_EXPERT), :] = sum(acc_hi[1:], acc_hi[0])
        acc_ref[pl.ds(pl.multiple_of(base + ROWS_PER_EXPERT, ROWS_PER_EXPERT), ROWS_PER_EXPERT), :] = sum(acc_lo[1:], acc_lo[0])

    def two_tokens(j, carry):
        token(2 * j)
        token(2 * j + 1)
        return carry

    lax.fori_loop(0, tt // 2, two_tokens, 0)
    _residual_rms_store(x_ref, acc_ref, g_ref, o_ref, tt)


def _residual_rms_store(x_ref, add_ref, g_ref, o_ref, tt):
    xs = [x_ref[:, c * 128:(c + 1) * 128] + add_ref[pl.ds(c, tt, stride=CHUNKS), :] for c in range(CHUNKS)]
    sq = xs[0] * xs[0]
    for x in xs[1:]:
        sq = sq + x * x
    r = lax.rsqrt(jnp.sum(sq, axis=1, keepdims=True) * (1.0 / D_MODEL) + EPS)
    for c in range(CHUNKS):
        o_ref[:, c * 128:(c + 1) * 128] = xs[c] * r * g_ref[:, c * 128:(c + 1) * 128]


def _peer_v_into_kernel(e_ref, w_ref, x_ref, g_ref, tab_ref, y_any_ref, o_ref, acc_ref, *, tt):
    del y_any_ref
    _peer_v_kernel(e_ref, w_ref, x_ref, g_ref, tab_ref, o_ref, acc_ref, tt=tt)


def _peer_v(e_t, w_t, x, g_final, tab, tt, n_tokens, into=None):
    pair = lambda: pl.BlockSpec((tt, N_PAIRS), lambda i: (i, 0), memory_space=pltpu.SMEM)
    row = lambda: pl.BlockSpec((tt, D_MODEL), lambda i: (i, 0))
    in_specs = [pair(), pair(), row(),
                pl.BlockSpec((1, D_MODEL), lambda i: (0, 0)),
                pl.BlockSpec(tab.shape, lambda i: (0, 0), pipeline_mode=pl.Buffered(1))]
    args = [e_t, w_t, x, g_final, tab]
    if into is not None:
        in_specs.append(pl.BlockSpec(memory_space=pl.ANY))
        args.append(into)
    return pl.pallas_call(
        functools.partial(_peer_v_kernel if into is None else _peer_v_into_kernel, tt=tt),
        grid=(n_tokens // tt,),
        in_specs=in_specs,
        out_specs=row(),
        out_shape=jax.ShapeDtypeStruct(x.shape if into is None else into.shape, F32),
        input_output_aliases={} if into is None else {5: 0},
        scratch_shapes=[pltpu.VMEM((tt * CHUNKS, 128), F32)],
        compiler_params=_cparams(("arbitrary",)),
        name="peer_v",
    )(*args)


def _residual_norm_kernel(x_ref, add_ref, g_ref, *rest, tt):
    _residual_rms_store(x_ref, add_ref, g_ref, rest[-1], tt)


def _residual_norm(x, add8, g_final, y, x_first, y_first, tt, out_rows=None, after=()):
    n_rows = add8.shape[0] // CHUNKS
    x_off, y_off = x_first // tt, y_first // tt
    extra = ([] if y is None else [y]) + list(after)
    return pl.pallas_call(
        functools.partial(_residual_norm_kernel, tt=tt),
        grid=(n_rows // tt,),
        in_specs=[pl.BlockSpec((tt, D_MODEL), lambda i: (i + x_off, 0)),
                  pl.BlockSpec((tt * CHUNKS, 128), lambda i: (i, 0)),
                  pl.BlockSpec((1, D_MODEL), lambda i: (0, 0))]
                 + [pl.BlockSpec(memory_space=pl.ANY)] * len(extra),
        out_specs=pl.BlockSpec((tt, D_MODEL), lambda i: (i + y_off, 0)),
        out_shape=jax.ShapeDtypeStruct((out_rows, D_MODEL) if y is None else y.shape, F32),
        input_output_aliases={} if y is None else {3: 0},
        compiler_params=_cparams(("arbitrary",)),
        name="peer_residual_norm",
    )(x, add8, g_final, *extra)


SC_CORES = 2
SC_SUBCORES = 16
SC_LANES = 16
WORD_ROWS = 128 // SC_LANES
SC_TOKEN_CHUNK = 32
SC_RING = 4


def _peer_sc(e_t, aux, tab, down):
    n = e_t.shape[0]
    workers = SC_CORES * SC_SUBCORES
    assert n % (workers * SC_TOKEN_CHUNK) == 0 and PEER_HEADS % SC_RING == 0
    per = n // workers
    head_rows = ROWS_PER_EXPERT * PEER_TOPK
    aux_rows = CHUNKS if down else 1
    out_rows = 1 if down else CHUNKS
    mesh = plsc.VectorSubcoreMesh(core_axis_name="core", subcore_axis_name="subcore",
                                  num_cores=SC_CORES, num_subcores=SC_SUBCORES)

    @pl.kernel(out_type=jax.ShapeDtypeStruct((n * out_rows, 128), F32), mesh=mesh,
               scratch_types=[pltpu.VMEM((SC_TOKEN_CHUNK, N_PAIRS), jnp.int32),
                              pltpu.VMEM((SC_TOKEN_CHUNK * aux_rows, 128), F32),
                              pltpu.VMEM((SC_RING, head_rows, 128), jnp.uint32),
                              pltpu.VMEM((SC_TOKEN_CHUNK * out_rows, 128), F32),
                              pltpu.VMEM((PEER_TOPK, SC_LANES), F32),
                              pltpu.SemaphoreType.DMA((SC_RING,))],
               compiler_params=pltpu.CompilerParams(needs_layout_passes=False),
               name="peer_u_sc" if down else "peer_v_sc")
    def body(e_hbm, aux_hbm, tab_hbm, o_hbm, e_v, aux_v, rows_v, out_v, fold_v, sems):
        wid = lax.axis_index("core") * SC_SUBCORES + lax.axis_index("subcore")
        lane = lax.broadcasted_iota(jnp.int32, (SC_LANES,), 0)

        def gathers(i, p, slot):
            first = e_v[i, pl.ds(p * PEER_TOPK, PEER_TOPK)]
            return [pltpu.make_async_copy(tab_hbm.at[first + r],
                                          rows_v.at[slot, pl.ds(r * PEER_TOPK, PEER_TOPK)], sems.at[slot])
                    for r in range(ROWS_PER_EXPERT)]

        def start_gather(i, p, slot):
            for d in gathers(i, p, slot):
                d.start()

        def weighted_sum(i, p, slot):
            ws = [plsc.load_gather(aux_v, [lane * 0 + i, lane * 0 + (p * PEER_TOPK + k)])
                  for k in range(PEER_TOPK)]
            for r in range(ROWS_PER_EXPERT):
                @pl.loop(0, WORD_ROWS)
                def _(j):
                    sl = pl.ds(j * SC_LANES, SC_LANES)
                    if p == 0:
                        a_hi = jnp.zeros((SC_LANES,), F32)
                        a_lo = jnp.zeros((SC_LANES,), F32)
                    else:
                        a_hi = out_v[i * CHUNKS + r, sl]
                        a_lo = out_v[i * CHUNKS + ROWS_PER_EXPERT + r, sl]
                    for k in range(PEER_TOPK):
                        hi, lo = _unpack(rows_v[slot, r * PEER_TOPK + k, sl])
                        a_hi = a_hi + ws[k] * hi
                        a_lo = a_lo + ws[k] * lo
                    out_v[i * CHUNKS + r, sl] = a_hi
                    out_v[i * CHUNKS + ROWS_PER_EXPERT + r, sl] = a_lo

        def pair_dots(i, p, slot):
            accs = tuple(jnp.zeros((SC_LANES,), F32) for _ in range(PEER_TOPK))
            for r in range(ROWS_PER_EXPERT):
                def piece(j, accs, r=r):
                    sl = pl.ds(j * SC_LANES, SC_LANES)
                    ha = aux_v[i * CHUNKS + r, sl]
                    hb = aux_v[i * CHUNKS + ROWS_PER_EXPERT + r, sl]
                    out = []
                    for k in range(PEER_TOPK):
                        hi, lo = _unpack(rows_v[slot, r * PEER_TOPK + k, sl])
                        out.append(accs[k] + (hi * ha + lo * hb))
                    return tuple(out)
                accs = lax.fori_loop(0, WORD_ROWS, piece, accs)
            for k in range(PEER_TOPK):
                fold_v[k, :] = accs[k]
            tot = plsc.load_gather(fold_v, [lane, lane * 0])
            for l in range(1, SC_LANES):
                tot = tot + plsc.load_gather(fold_v, [lane, lane * 0 + l])
            out_v[i, pl.ds(p * PEER_TOPK, PEER_TOPK)] = tot

        consume = pair_dots if down else weighted_sum

        @pl.loop(0, per // SC_TOKEN_CHUNK)
        def _(c):
            t0 = wid * per + c * SC_TOKEN_CHUNK
            pltpu.sync_copy(e_hbm.at[pl.ds(t0, SC_TOKEN_CHUNK)], e_v)
            pltpu.sync_copy(aux_hbm.at[pl.ds(t0 * aux_rows, SC_TOKEN_CHUNK * aux_rows)], aux_v)
            for p in range(SC_RING):
                start_gather(0, p, p)

            @pl.loop(0, SC_TOKEN_CHUNK)
            def _(i):
                for p in range(PEER_HEADS):
                    slot = p % SC_RING
                    for d in gathers(i, p, slot):
                        d.wait()
                    consume(i, p, slot)
                    if p + SC_RING < PEER_HEADS:
                        start_gather(i, p + SC_RING, slot)
                    else:
                        @pl.when(i + 1 < SC_TOKEN_CHUNK)
                        def _():
                            start_gather(i + 1, p + SC_RING - PEER_HEADS, slot)

            pltpu.sync_copy(out_v, o_hbm.at[pl.ds(t0 * out_rows, SC_TOKEN_CHUNK * out_rows)])

    return body(e_t, aux, tab)


def _pack_table(tab):
    bits = lax.bitcast_convert_type(tab.astype(BF16), jnp.uint16).astype(jnp.uint32)
    words = (bits[:, :HALF] << 16) | bits[:, HALF:]
    return words.reshape(tab.shape[0] * ROWS_PER_EXPERT, 128)


PEER_TOKENS = 128
SC_LEAD_BATCHES = (1, 3)
SC_SHARE_DOWN = (3, 8)
SC_SHARE_UP = (19, 48)
SC_MIN_TOKENS = 4096


def _sc_tokens(n):
    if n < SC_MIN_TOKENS:
        return (0, 0)
    unit = math.lcm(PEER_TOKENS, SC_CORES * SC_SUBCORES * SC_TOKEN_CHUNK)
    return tuple(n * num // den // unit * unit for num, den in (SC_SHARE_DOWN, SC_SHARE_UP))


def _row_tile(n, pref):
    while n % pref:
        pref //= 2
    return pref


def _peer_on_sc_down(h3, qp, k1_bf, k2_bf, tab_u, tt, route_after=()):
    e_t, gate_t = _route(qp, k1_bf, k2_bf, tt, after=route_after)
    return e_t, gate_t, _peer_sc(e_t, h3, tab_u, down=True)


def _peer_on_sc_up(e_t, gate_t, act, tab_v, tt, gate_after):
    w_t = _gate_act(act, gate_t, act, 0, tt, after=gate_after)
    return _peer_sc(e_t, w_t, tab_v, down=False), w_t


def _peer_and_final(x2, h3, qp, k1_bf, k2_bf, tab_u, tab_v, g_final, tt, n_sc=(0, 0), route_after=(), out_rows=None,
                    lead=None, y_start=None):
    n = x2.shape[0]
    n_sc_down, n_sc_up = n_sc
    tc_down, tc_up = n - n_sc_down, n - n_sc_up
    if n_sc_down:
        e_t, gate_t = _route(qp, k1_bf, k2_bf, tt, tc_down, n_sc_down, after=route_after)
        act = _peer_sc(e_t[tc_down:], h3[tc_down * CHUNKS:], tab_u, down=True)
        e_t, gate_t = _route(qp, k1_bf, k2_bf, tt, 0, tc_down, into=(e_t, gate_t))
    else:
        e_t, gate_t = _route(qp, k1_bf, k2_bf, tt)
    w_t = _peer_u(e_t, h3, gate_t, tab_u, tt, tc_down)
    y = y_start
    if lead is not None:
        x2_lead, add8_lead, y_first = lead
        y = _residual_norm(x2_lead, add8_lead, g_final, y, 0, y_first, tt, out_rows=out_rows, after=(w_t,))
    if n_sc_down:
        w_t = _gate_act(act, gate_t, w_t, tc_down, tt, after=() if y is None else (y,))
    y = _peer_v(e_t, w_t, x2, g_final, tab_v, tt, tc_up, into=y)
    if n_sc_up:
        add8 = _peer_sc(e_t[tc_up:], w_t[tc_up:], tab_v, down=False)
        y = _residual_norm(x2, add8, g_final, y, tc_up, tc_up, tt)
    return y


def kernel(x_prompt, x_sample, mem_prompt, cache_da_k, cache_da_v, state_ret, cache_mem_k, cache_mem_v, g_mix, w_in, lam_q1, lam_k1, lam_q2, lam_k2, g_da, g_ret, w_out, g_cross, g_mem, w_xq, w_xk, w_xv, w_xo, g_ffn, w_pq, peer_k1, peer_k2, peer_u, peer_v, g_final):
    depth = w_in.shape[0]
    assert depth == 1, "single-layer step"
    l = 0
    lam_init = 0.8 - 0.6 * math.exp(-0.3 * l)
    b, t, _ = x_prompt.shape
    bs, ts, _ = x_sample.shape
    past_len = cache_da_k.shape[2]

    row = lambda a: a.reshape(1, -1)
    w_in_bf = w_in[l].astype(BF16)
    w_out_bf = w_out[l].astype(BF16)
    w_xq_bf, w_xk_bf, w_xv_bf, w_xo_bf = (w[l].astype(BF16) for w in (w_xq, w_xk, w_xv, w_xo))
    w_pq_bf = w_pq[l].astype(BF16)
    k1_bf, k2_bf = peer_k1[l].astype(BF16), peer_k2[l].astype(BF16)
    tab_u, tab_v = _pack_table(peer_u[l]), _pack_table(peer_v[l])
    lamp = jnp.stack([lam_q1[l], lam_k1[l], lam_q2[l], lam_k2[l]])
    g_da3 = g_da[l].reshape(H_D, 1, DV_D)
    g_ret3 = g_ret[l].reshape(H_R, 1, DV_R)
    g_fin = row(g_final)

    def mixer_mid(x2d, first_row, mda, mret, mk, mv, bb, tt_rows):
        n = bb * tt_rows
        return _mix_cross(x2d, first_row, mda.reshape(n, GROUP_W), mret.reshape(n, GROUP_W), w_out_bf, row(g_cross[l]),
                          w_xq_bf, mk, mv, w_xo_bf, row(g_ffn[l]), w_pq_bf, bb, tt_rows, _row_tile(tt_rows, 512))

    peer_args = (k1_bf, k2_bf, tab_u, tab_v, g_fin, PEER_TOKENS)
    n = b * t
    xp = x_prompt.reshape(n, D_MODEL)

    def prompt_dense(b0, bg, kv_into=None):
        rows = bg * t
        qd, kd, vd, kdb, vdb, qr, kr, vr, gr = _in_proj(xp, row(g_mix[l]), w_in_bf, _row_tile(rows, 512),
                                                       b0 * t, rows, kv_into)
        r3 = lambda a: a.reshape(bg, t, GROUP_W)
        mda = _diff_attn_prompt(lamp, r3(qd), r3(kdb), r3(vdb), g_da3, lam_init, 512, 512)
        mret, s_fin = _retention(r3(qr), r3(kr), r3(vr), r3(gr), g_ret3, None, 512)
        return mixer_mid(xp, b0 * t, mda, mret, mkb[b0:b0 + bg], mvb[b0:b0 + bg], bg, t), (kd, vd), s_fin

    mk, mv, mkb, mvb = _mem_kv(mem_prompt.reshape(b * N_MEM, D_MODEL), row(g_mem[l]), w_xk_bf, w_xv_bf, 512)
    mkb, mvb = mkb.reshape(b, N_MEM, D_MODEL), mvb.reshape(b, N_MEM, D_MODEL)
    leads = [bg for bg in SC_LEAD_BATCHES if bg] if sum(SC_LEAD_BATCHES) < b else []
    kv, states = None, []
    y = None
    launched = None
    finished = None
    hi = b
    for bg in leads + [hi - sum(leads)]:
        b0, hi = hi - bg, hi - bg
        peer_in, kv, s_g = prompt_dense(b0, bg, kv_into=kv)
        states.insert(0, s_g)
        x2_g, h3_g, qp_g = peer_in
        if finished is not None:
            y = _residual_norm(finished[0], finished[1], g_fin, y, 0, finished[2], PEER_TOKENS, out_rows=n, after=(qp_g,))
            finished = None
        route_after = ()
        if launched is not None:
            x2_p, e_p, gate_p, act_p, row_p = launched
            add8_p, w_p = _peer_on_sc_up(e_p, gate_p, act_p, tab_v, PEER_TOKENS,
                                         gate_after=(qp_g,) + (() if y is None else (y,)))
            finished, route_after = (x2_p, add8_p, row_p), (w_p,)
        if b0 > 0:
            launched = (x2_g,) + _peer_on_sc_down(h3_g, qp_g, k1_bf, k2_bf, tab_u, PEER_TOKENS, route_after) + (b0 * t,)
        else:
            y = _peer_and_final(*peer_in, *peer_args, _sc_tokens(bg * t), route_after=route_after, out_rows=n,
                                lead=finished, y_start=y)
    kd, vd = kv
    y_prompt = y.reshape(b, t, D_MODEL)
    s_fin = jnp.concatenate(states) if len(states) > 1 else states[0]

    ns = bs * ts
    xs = x_sample.reshape(ns, D_MODEL)
    qd_s, kd_s, vd_s, kdb_s, vdb_s, qr_s, kr_s, vr_s, gr_s = _in_proj(xs, row(g_mix[l]), w_in_bf, _row_tile(ns, 512))
    s3 = lambda a: a.reshape(bs, ts, GROUP_W)
    mda_s = _diff_attn_sample(lamp, s3(qd_s), cache_da_k[l].reshape(bs, past_len, GROUP_W),
                              cache_da_v[l].reshape(bs, past_len, GROUP_W), s3(kdb_s), s3(vdb_s), g_da3, lam_init)
    mret_s, s_new = _retention(s3(qr_s), s3(kr_s), s3(vr_s), s3(gr_s), g_ret3, state_ret[l], ts)
    peer_s = mixer_mid(xs, 0, mda_s, mret_s, cache_mem_k[l].reshape(bs, N_MEM, D_MODEL),
                       cache_mem_v[l].reshape(bs, N_MEM, D_MODEL), bs, ts)
    y_sample = _peer_and_final(*peer_s, *peer_args).reshape(bs, ts, D_MODEL)

    return (y_prompt, y_sample,
            kd.reshape(1, b, t, H_D, 2, DK_D), vd.reshape(1, b, t, H_D, DV_D), s_fin[None],
            mk.reshape(1, b, N_MEM, H_X, HD_X), mv.reshape(1, b, N_MEM, H_X, HD_X),
            kd_s.reshape(1, bs, ts, H_D, 2, DK_D), vd_s.reshape(1, bs, ts, H_D, DV_D), s_new[None])
```

```python
import functools
import math

import jax
import jax.numpy as jnp
from jax import lax
from jax.experimental import pallas as pl
from jax.experimental.pallas import tpu as pltpu
from jax.experimental.pallas import tpu_sc as plsc

D_MODEL = 1024
CHUNK = 64
CHUNK_SHIFT = CHUNK.bit_length() - 1
assert 1 << CHUNK_SHIFT == CHUNK
H_D, DK_D, DV_D = 4, 64, 128
H_R, DK_R, DV_R = 4, 128, 128
N_MEM = 256
H_X = 4
HD_X = D_MODEL // H_X
PEER_HEADS = 8
N_KEYS = 128
N_EXPERTS = N_KEYS * N_KEYS
PEER_TOPK = 16
EPS = 1e-6
HEAD_W = 128
GROUP_W = 512
N_PAIRS = PEER_HEADS * PEER_TOPK
HALF = D_MODEL // 2
ROWS_PER_EXPERT = HALF // 128
VMEM_LIMIT = 56 * 1024 * 1024

BF16 = jnp.bfloat16
F32 = jnp.float32


def _cparams(sem):
    return pltpu.CompilerParams(dimension_semantics=sem, vmem_limit_bytes=VMEM_LIMIT)


def _rms(x, g):
    return x * lax.rsqrt(jnp.mean(x * x, axis=-1, keepdims=True) + EPS) * g


def _dot(a, b):
    return jnp.dot(a, b, preferred_element_type=F32)


def _dot_nt(a, b):
    return lax.dot_general(a, b, (((1,), (1,)), ((), ())), preferred_element_type=F32)


def _dot_tn(a, b):
    return lax.dot_general(a, b, (((0,), (0,)), ((), ())), preferred_element_type=F32)


def _select_by_head(h, values):
    out = jnp.float32(values[-1])
    for i in range(len(values) - 2, -1, -1):
        out = jnp.where(h == i, jnp.float32(values[i]), out)
    return out


def _in_proj_kernel(x_ref, g_ref, w_ref, *rest):
    qd_ref, kd_ref, vd_ref, kdb_ref, vdb_ref, qr_ref, kr_ref, vr_ref, gr_ref = rest[-9:]
    hb = _rms(x_ref[...], g_ref[...]).astype(BF16)
    col = lambda c: _dot(hb, w_ref[:, c * GROUP_W:(c + 1) * GROUP_W])
    qd_ref[...] = (col(0) * (DK_D ** -0.5)).astype(BF16)
    kd = col(1)
    kd_ref[...] = kd.reshape(kd_ref.shape)
    kdb_ref[...] = kd.astype(BF16)
    vd = col(2)
    vd_ref[...] = vd.reshape(vd_ref.shape)
    vdb_ref[...] = vd.astype(BF16)
    qr_ref[...] = col(3).astype(BF16)
    kr_ref[...] = (col(4) * (DK_R ** -0.5)).astype(BF16)
    vr_ref[...] = col(5).astype(BF16)
    gr_ref[...] = col(6)


def _in_proj(x2d, g, w_bf, tm, first_row=0, n_rows=None, kv_into=None):
    n_all = x2d.shape[0]
    n = n_all if n_rows is None else n_rows
    off = first_row // tm
    blk = lambda: pl.BlockSpec((tm, GROUP_W), lambda i: (i, 0))
    k_dims, v_dims = (H_D, 2, DK_D), (H_D, DV_D)
    blk_all = lambda dims: pl.BlockSpec((tm,) + dims, lambda i: (i + off,) + (0,) * len(dims))
    sh = lambda dt: jax.ShapeDtypeStruct((n, GROUP_W), dt)
    sh_all = lambda dims: jax.ShapeDtypeStruct((n_all,) + dims, F32)
    args = [x2d, g, w_bf]
    in_specs = [pl.BlockSpec((tm, D_MODEL), lambda i: (i + off, 0)),
                pl.BlockSpec((1, D_MODEL), lambda i: (0, 0)),
                pl.BlockSpec(w_bf.shape, lambda i: (0, 0))]
    aliases = {}
    if kv_into is not None:
        args += list(kv_into)
        in_specs += [pl.BlockSpec(memory_space=pl.ANY)] * 2
        aliases = {3: 1, 4: 2}
    return pl.pallas_call(
        _in_proj_kernel,
        grid=(n // tm,),
        in_specs=in_specs,
        out_specs=[blk(), blk_all(k_dims), blk_all(v_dims)] + [blk() for _ in range(6)],
        out_shape=[sh(BF16), sh_all(k_dims), sh_all(v_dims), sh(BF16), sh(BF16), sh(BF16), sh(BF16), sh(BF16), sh(F32)],
        input_output_aliases=aliases,
        compiler_params=_cparams(("parallel",)),
        name="in_proj",
    )(*args)


def _lambda_from(lam_ref, lam_init):
    l = lam_ref[...]
    a = jnp.exp(jnp.sum(l[0:1] * l[1:2], axis=-1, keepdims=True))
    b = jnp.exp(jnp.sum(l[2:3] * l[3:4], axis=-1, keepdims=True))
    return a - b + lam_init


def _diff_post(acc, l, lam, g, lam_init, tq):
    o = acc[:tq] / l[:tq] - lam * (acc[tq:] / l[tq:])
    return o * lax.rsqrt(jnp.mean(o * o, axis=-1, keepdims=True) + EPS) * g * (1.0 - lam_init)


def _split_maps(q):
    lane = lax.broadcasted_iota(jnp.int32, q.shape, 1)
    zero = jnp.zeros_like(q)
    return jnp.concatenate([jnp.where(lane < DK_D, q, zero), jnp.where(lane >= DK_D, q, zero)], axis=0)


def _da_prompt_kernel(lam_ref, q_ref, k_ref, v_ref, g_ref, o_ref, kx_ref, vx_ref, own_ref, acc_ref, m_ref, *, lam_init, tq, tk):
    h = pl.program_id(1)
    i = pl.program_id(2)
    t = k_ref.shape[0]
    slope = _select_by_head(h, [2.0 ** (-8.0 * (j + 1) / H_D) for j in range(H_D)])

    @pl.when(i == 0)
    def _():
        pos = lax.broadcasted_iota(jnp.int32, (t, HEAD_W), 0)
        lane = lax.broadcasted_iota(jnp.int32, (t, HEAD_W), 1)
        coarse = ((pos >> CHUNK_SHIFT) << CHUNK_SHIFT).astype(F32) * slope
        fine = (pos & (CHUNK - 1)).astype(F32) * slope
        kx_ref[:, :HEAD_W] = k_ref[...]
        kx_ref[:, HEAD_W:] = jnp.where(lane == 0, coarse, jnp.where(lane == 1, fine, 0.0)).astype(BF16)
        vx_ref[:DV_D, :] = v_ref[...].astype(F32).T.astype(BF16)
        vx_ref[DV_D:, :] = jnp.ones((vx_ref.shape[0] - DV_D, t), BF16)
        krel = lax.broadcasted_iota(jnp.int32, (tk, 2 * tq), 0)
        c = lax.broadcasted_iota(jnp.int32, (tk, 2 * tq), 1)
        for par in range(tk // tq):
            qrel = par * tq + jnp.where(c >= tq, c - tq, c)
            ahead = (2.0 * slope) * jnp.maximum(krel - qrel, 0).astype(F32)
            own_ref[par] = jnp.where((qrel >> CHUNK_SHIFT) >= (krel >> CHUNK_SHIFT), -ahead, -1e30)

    q = q_ref[...]
    lane = lax.broadcasted_iota(jnp.int32, q.shape, 1)
    zero = jnp.zeros_like(q)
    ones2 = jnp.where(lane < 2, 1.0, 0.0).astype(BF16)
    q2 = jnp.concatenate([jnp.concatenate([jnp.where(lane < DK_D, q, zero), ones2], axis=1),
                          jnp.concatenate([jnp.where(lane >= DK_D, q, zero), ones2], axis=1)], axis=0)
    jd = (i * tq) // tk

    def scores(j):
        return _dot_nt(kx_ref[pl.ds(pl.multiple_of(j * tk, tk), tk), :], q2)

    def values(j):
        return vx_ref[:, pl.ds(pl.multiple_of(j * tk, tk), tk)]

    s = scores(jd) + own_ref[(i * tq) % tk // tq]
    m0 = jnp.max(s, axis=0, keepdims=True)
    m_ref[...] = m0
    acc_ref[...] = _dot(values(jd), jnp.exp(s - m0).astype(BF16))

    def absorb(blocks):
        ss = [scores(j) for j in blocks]
        m_old = m_ref[...]
        m_new = m_old
        for s in ss:
            m_new = jnp.maximum(m_new, jnp.max(s, axis=0, keepdims=True))
        m_ref[...] = m_new
        acc = jnp.exp(m_old - m_new) * acc_ref[...]
        for j, s in zip(blocks, ss):
            acc = acc + _dot(values(j), jnp.exp(s - m_new).astype(BF16))
        acc_ref[...] = acc

    def past_pair(jj, carry):
        absorb([2 * jj, 2 * jj + 1])
        return carry

    lax.fori_loop(0, jd // 2, past_pair, 0)

    @pl.when(jd % 2 == 1)
    def _():
        absorb([jd - 1])

    acc = acc_ref[...]
    num, den = acc[:DV_D], acc[DV_D:DV_D + 1]
    lam = _lambda_from(lam_ref, lam_init)
    o = (num[:, :tq] / den[:, :tq] - lam * (num[:, tq:] / den[:, tq:])).T
    o = o * lax.rsqrt(jnp.mean(o * o, axis=-1, keepdims=True) + EPS) * g_ref[...] * (1.0 - lam_init)
    o_ref[...] = o.astype(o_ref.dtype)


ONES_ROWS = 16


def _diff_attn_prompt(lamp, q, k, v, g_da3, lam_init, tq, tk):
    b, t, _ = q.shape
    kern = functools.partial(_da_prompt_kernel, lam_init=lam_init, tq=tq, tk=tk)
    return pl.pallas_call(
        kern,
        grid=(b, H_D, t // tq),
        in_specs=[pl.BlockSpec((4, DK_D), lambda b_, h, i: (0, 0)),
                  pl.BlockSpec((None, tq, HEAD_W), lambda b_, h, i: (b_, i, h)),
                  pl.BlockSpec((None, t, HEAD_W), lambda b_, h, i: (b_, 0, h)),
                  pl.BlockSpec((None, t, HEAD_W), lambda b_, h, i: (b_, 0, h)),
                  pl.BlockSpec((None, 1, HEAD_W), lambda b_, h, i: (h, 0, 0))],
        out_specs=pl.BlockSpec((None, tq, HEAD_W), lambda b_, h, i: (b_, i, h)),
        out_shape=jax.ShapeDtypeStruct((b, t, GROUP_W), BF16),
        scratch_shapes=[pltpu.VMEM((t, 2 * HEAD_W), BF16), pltpu.VMEM((DV_D + ONES_ROWS, t), BF16),
                        pltpu.VMEM((tk // tq, tk, 2 * tq), F32),
                        pltpu.VMEM((DV_D + ONES_ROWS, 2 * tq), F32), pltpu.VMEM((1, 2 * tq), F32)],
        compiler_params=_cparams(("parallel", "parallel", "arbitrary")),
        name="diff_attn_prompt",
    )(lamp, q, k, v, g_da3)


def _da_sample_kernel(lam_ref, q_ref, kc_ref, vc_ref, kn_ref, vn_ref, g_ref, o_ref, *, lam_init, ts, past_len):
    h = pl.program_id(1)
    slope = _select_by_head(h, [2.0 ** (-8.0 * (j + 1) / H_D) for j in range(H_D)])
    q2 = _split_maps(q_ref[...])
    rows = 2 * ts

    def scores(k, base, n):
        r = lax.broadcasted_iota(jnp.int32, (rows, n), 0)
        qpos = past_len + jnp.where(r >= ts, r - ts, r)
        kpos = base + lax.broadcasted_iota(jnp.int32, (rows, n), 1)
        return _dot_nt(q2, k) - slope * jnp.abs(qpos - kpos).astype(F32)

    sc = scores(kc_ref[...].astype(BF16), 0, past_len)
    sn = scores(kn_ref[...], past_len, ts)
    m = jnp.maximum(jnp.max(sc, axis=-1, keepdims=True), jnp.max(sn, axis=-1, keepdims=True))
    pc = jnp.exp(sc - m)
    pn = jnp.exp(sn - m)
    l = jnp.sum(pc, axis=-1, keepdims=True) + jnp.sum(pn, axis=-1, keepdims=True)
    acc = _dot(pc.astype(BF16), vc_ref[...].astype(BF16)) + _dot(pn.astype(BF16), vn_ref[...])
    lam = _lambda_from(lam_ref, lam_init)
    o_ref[...] = _diff_post(acc, l, lam, g_ref[...], lam_init, ts).astype(o_ref.dtype)


def _diff_attn_sample(lamp, q, kc, vc, kn, vn, g_da3, lam_init):
    b, ts, _ = q.shape
    past_len = kc.shape[1]
    kern = functools.partial(_da_sample_kernel, lam_init=lam_init, ts=ts, past_len=past_len)
    head = lambda rows: pl.BlockSpec((None, rows, HEAD_W), lambda b_, h: (b_, 0, h))
    return pl.pallas_call(
        kern,
        grid=(b, H_D),
        in_specs=[pl.BlockSpec((4, DK_D), lambda b_, h: (0, 0)),
                  head(ts), head(past_len), head(past_len), head(ts), head(ts),
                  pl.BlockSpec((None, 1, HEAD_W), lambda b_, h: (h, 0, 0))],
        out_specs=head(ts),
        out_shape=jax.ShapeDtypeStruct((b, ts, GROUP_W), BF16),
        compiler_params=_cparams(("parallel", "parallel")),
        name="diff_attn_sample",
    )(lamp, q, kc, vc, kn, vn, g_da3)


def _ret_kernel(*refs, lb, has_init):
    if has_init:
        q_ref, k_ref, v_ref, gate_ref, g_ref, s0_ref, o_ref, sfin_ref, s_ref = refs
    else:
        q_ref, k_ref, v_ref, gate_ref, g_ref, o_ref, sfin_ref, s_ref = refs
    h = pl.program_id(1)
    c = pl.program_id(2)
    lg = _select_by_head(h, [math.log1p(-(2.0 ** (-5.0 - j))) for j in range(H_R)])

    @pl.when(c == 0)
    def _():
        s_ref[...] = s0_ref[...] if has_init else jnp.zeros_like(s_ref)

    q, k, v = q_ref[...], k_ref[...], v_ref[...]
    i = lax.broadcasted_iota(jnp.int32, (lb, lb), 0)
    j = lax.broadcasted_iota(jnp.int32, (lb, lb), 1)
    d = (i - j).astype(F32)
    decay = jnp.where(d >= 0, jnp.exp(jnp.maximum(d, 0.0) * lg), 0.0)
    inner = _dot_nt(q, k) * decay
    ic = lax.broadcasted_iota(jnp.int32, (lb, 1), 0).astype(F32)
    s_old = s_ref[...]
    o = _dot(inner.astype(BF16), v) + _dot(q, s_old.astype(BF16)) * jnp.exp((ic + 1.0) * lg)
    tail = jnp.exp((lb - 1.0 - ic) * lg)
    kt = (k.astype(F32) * tail).astype(BF16)
    s_new = jnp.exp(lb * lg) * s_old + _dot_tn(kt, v)
    s_ref[...] = s_new

    @pl.when(c == pl.num_programs(2) - 1)
    def _():
        sfin_ref[...] = s_new

    oc = o - jnp.mean(o, axis=-1, keepdims=True)
    y = oc * lax.rsqrt(jnp.mean(oc * oc, axis=-1, keepdims=True) + EPS) * g_ref[...]
    gate = gate_ref[...]
    o_ref[...] = (y * (gate * jax.nn.sigmoid(gate))).astype(o_ref.dtype)


def _retention(q, k, v, gate, g_ret3, s0, lb):
    b, t, _ = q.shape
    has_init = s0 is not None
    kern = functools.partial(_ret_kernel, lb=lb, has_init=has_init)
    head = lambda: pl.BlockSpec((None, lb, HEAD_W), lambda b_, h, c: (b_, c, h))
    state = lambda: pl.BlockSpec((None, None, DK_R, DV_R), lambda b_, h, c: (b_, h, 0, 0))
    in_specs = [head(), head(), head(), head(), pl.BlockSpec((None, 1, HEAD_W), lambda b_, h, c: (h, 0, 0))]
    args = [q, k, v, gate, g_ret3]
    if has_init:
        in_specs.append(state())
        args.append(s0)
    return pl.pallas_call(
        kern,
        grid=(b, H_R, t // lb),
        in_specs=in_specs,
        out_specs=[head(), state()],
        out_shape=[jax.ShapeDtypeStruct((b, t, GROUP_W), BF16), jax.ShapeDtypeStruct((b, H_R, DK_R, DV_R), F32)],
        scratch_shapes=[pltpu.VMEM((DK_R, DV_R), F32)],
        compiler_params=_cparams(("parallel", "parallel", "arbitrary")),
        name="retention",
    )(*args)


def _out_proj_kernel(x_ref, mda_ref, mret_ref, wo_ref, g_ref, wq_ref, x1_ref, qx_ref):
    x1 = x_ref[...] + _dot(mda_ref[...], wo_ref[:GROUP_W, :]) + _dot(mret_ref[...], wo_ref[GROUP_W:, :])
    x1_ref[...] = x1
    hn = _rms(x1, g_ref[...]).astype(BF16)
    qx_ref[...] = (_dot(hn, wq_ref[...]) * (HD_X ** -0.5)).astype(BF16)


def _mem_kv_kernel(m_ref, g_ref, wk_ref, wv_ref, mk_ref, mv_ref, mkb_ref, mvb_ref):
    mn = _rms(m_ref[...], g_ref[...]).astype(BF16)
    mk = _dot(mn, wk_ref[...])
    mv = _dot(mn, wv_ref[...])
    mk_ref[...] = mk
    mv_ref[...] = mv
    mkb_ref[...] = mk.astype(BF16)
    mvb_ref[...] = mv.astype(BF16)


def _mem_kv(mem2d, g_mem, wk_bf, wv_bf, tm):
    n = mem2d.shape[0]
    row = lambda: pl.BlockSpec((tm, D_MODEL), lambda i: (i, 0))
    full = lambda a: pl.BlockSpec(a.shape, lambda i: (0, 0))
    sh = lambda dt: jax.ShapeDtypeStruct((n, D_MODEL), dt)
    return pl.pallas_call(
        _mem_kv_kernel,
        grid=(n // tm,),
        in_specs=[row(), full(g_mem), full(wk_bf), full(wv_bf)],
        out_specs=[row(), row(), row(), row()],
        out_shape=[sh(F32), sh(F32), sh(BF16), sh(BF16)],
        compiler_params=_cparams(("parallel",)),
        name="mem_kv",
    )(mem2d, g_mem, wk_bf, wv_bf)


def _cross_kernel(x1_ref, qx_ref, mk_ref, mv_ref, wo_ref, g_ref, wpq_ref, x2_ref, h3_ref, qp_ref):
    q = qx_ref[...]
    heads = []
    for h in range(H_X):
        sl = slice(h * HD_X, (h + 1) * HD_X)
        kh = mk_ref[:, sl] if mk_ref.dtype == BF16 else mk_ref[:, sl].astype(BF16)
        vh = mv_ref[:, sl] if mv_ref.dtype == BF16 else mv_ref[:, sl].astype(BF16)
        s = _dot_nt(q[:, sl], kh)
        p = jnp.exp(s - jnp.max(s, axis=-1, keepdims=True))
        heads.append(_dot(p.astype(BF16), vh) / jnp.sum(p, axis=-1, keepdims=True))
    o = jnp.concatenate(heads, axis=-1).astype(BF16)
    x2 = x1_ref[...] + _dot(o, wo_ref[...])
    x2_ref[...] = x2
    h3 = _rms(x2, g_ref[...])
    h3_ref[...] = h3.reshape(h3_ref.shape)
    qp_ref[...] = _dot(h3.astype(BF16), wpq_ref[...]).astype(BF16)


def _mix_cross_kernel(x_ref, mda_ref, mret_ref, wout_ref, gc_ref, wq_ref, mk_ref, mv_ref, wo_ref, g_ref, wpq_ref,
                      x2_ref, h3_ref, qp_ref, x1_ref, qx_ref):
    _out_proj_kernel(x_ref, mda_ref, mret_ref, wout_ref, gc_ref, wq_ref, x1_ref, qx_ref)
    _cross_kernel(x1_ref, qx_ref, mk_ref, mv_ref, wo_ref, g_ref, wpq_ref, x2_ref, h3_ref, qp_ref)


def _mix_cross(x2d, first_row, mda, mret, wout_bf, g_cross, wq_bf, mk, mv, wo_bf, g_ffn, wpq_bf, bb, t, tm):
    nt = t // tm
    off = first_row // tm
    full = lambda a: pl.BlockSpec(a.shape, lambda b_, i: (0,) * a.ndim)
    rows2d = lambda w, o: pl.BlockSpec((tm, w), lambda b_, i: (o + b_ * nt + i, 0))
    mem = lambda: pl.BlockSpec((None, N_MEM, D_MODEL), lambda b_, i: (b_, 0, 0))
    dq = wpq_bf.shape[1]
    n = bb * t
    return pl.pallas_call(
        _mix_cross_kernel,
        grid=(bb, nt),
        in_specs=[rows2d(D_MODEL, off), rows2d(GROUP_W, 0), rows2d(GROUP_W, 0), full(wout_bf), full(g_cross), full(wq_bf),
                  mem(), mem(), full(wo_bf), full(g_ffn), full(wpq_bf)],
        out_specs=[rows2d(D_MODEL, 0), pl.BlockSpec((tm * CHUNKS, 128), lambda b_, i: (b_ * nt + i, 0)), rows2d(dq, 0)],
        out_shape=[jax.ShapeDtypeStruct((n, D_MODEL), F32), jax.ShapeDtypeStruct((n * CHUNKS, 128), F32),
                   jax.ShapeDtypeStruct((n, dq), BF16)],
        scratch_shapes=[pltpu.VMEM((tm, D_MODEL), F32), pltpu.VMEM((tm, D_MODEL), BF16)],
        compiler_params=_cparams(("parallel", "parallel")),
        name="mix_cross",
    )(x2d, mda, mret, wout_bf, g_cross, wq_bf, mk, mv, wo_bf, g_ffn, wpq_bf)


ID_PAD = 2.0 ** 29


def _topk_rows(s, ids):
    vals, sel = [], []
    for _ in range(PEER_TOPK):
        m = jnp.max(s, axis=0, keepdims=True)
        idx = jnp.min(jnp.where(s == m, ids, ID_PAD), axis=0, keepdims=True)
        vals.append(m)
        sel.append(idx)
        s = jnp.where(ids == idx, -jnp.inf, s)
    return jnp.concatenate(vals, axis=0), jnp.concatenate(sel, axis=0)


def _candidates(v1, i1, v2, i2):
    lanes = v1.shape[1]
    b8 = lax.broadcasted_iota(jnp.int32, (8, lanes), 0)
    ident = lambda a, ia, ib, b: (a * PEER_TOPK + b) * float(N_EXPERTS) + (ia * float(N_KEYS) + ib)
    b16 = lax.broadcasted_iota(jnp.int32, (PEER_TOPK, lanes), 0).astype(F32)
    vals = [v1[0:1] + v2]
    ids = [ident(0, i1[0:1], i2, b16)]
    for a in range(1, 8):
        keep = b8 < PEER_TOPK // (a + 1)
        vals.append(jnp.where(keep, v1[a:a + 1] + v2[0:8], -jnp.inf))
        ids.append(jnp.where(keep, ident(a, i1[a:a + 1], i2[0:8], b8.astype(F32)), ID_PAD))
    a_hi = (b8 + 8).astype(F32)
    vals.append(v1[8:16] + v2[0:1])
    ids.append(ident(a_hi, i1[8:16], i2[0:1], 0.0))
    return jnp.concatenate(vals, axis=0), jnp.concatenate(ids, axis=0)


def _route_kernel(qp_ref, k1_ref, k2_ref, e_ref, g_ref):
    half = N_KEYS
    tt = qp_ref.shape[0]
    key_id = lax.broadcasted_iota(jnp.int32, (N_KEYS, tt), 0).astype(F32)
    es, gs = [], []
    for p in range(PEER_HEADS):
        q1 = qp_ref[:, (2 * p) * half:(2 * p + 1) * half]
        q2 = qp_ref[:, (2 * p + 1) * half:(2 * p + 2) * half]
        v1, i1 = _topk_rows(_dot_nt(k1_ref[p], q1), key_id)
        v2, i2 = _topk_rows(_dot_nt(k2_ref[p], q2), key_id)
        sc, sel = _topk_rows(*_candidates(v1, i1, v2, i2))
        w = jnp.exp(sc - sc[0:1])
        es.append((sel.astype(jnp.int32) & (N_EXPERTS - 1)) * ROWS_PER_EXPERT)
        gs.append(w / jnp.sum(w, axis=0, keepdims=True))
    e_ref[...] = jnp.concatenate(es, axis=0).T
    g_ref[...] = jnp.concatenate(gs, axis=0).T


def _route_extra_kernel(qp_ref, k1_ref, k2_ref, *rest):
    _route_kernel(qp_ref, k1_ref, k2_ref, rest[-2], rest[-1])


def _route(qp2d, k1_bf, k2_bf, tt, first_token=0, n_tokens=None, into=None, after=()):
    n = qp2d.shape[0]
    n_tokens = n if n_tokens is None else n_tokens
    off = first_token // tt
    full = lambda a: pl.BlockSpec(a.shape, lambda i: (0, 0, 0))
    pair = lambda: pl.BlockSpec((tt, N_PAIRS), lambda i: (i + off, 0))
    in_specs = [pl.BlockSpec((tt, qp2d.shape[1]), lambda i: (i + off, 0)), full(k1_bf), full(k2_bf)]
    args = [qp2d, k1_bf, k2_bf]
    aliases = {}
    if into is not None:
        in_specs += [pl.BlockSpec(memory_space=pl.ANY), pl.BlockSpec(memory_space=pl.ANY)]
        args += list(into)
        aliases = {3: 0, 4: 1}
    in_specs += [pl.BlockSpec(memory_space=pl.ANY)] * len(after)
    args += list(after)
    return pl.pallas_call(
        _route_kernel if len(args) == 3 else _route_extra_kernel,
        grid=(n_tokens // tt,),
        in_specs=in_specs,
        out_specs=[pair(), pair()],
        out_shape=[jax.ShapeDtypeStruct((n, N_PAIRS), jnp.int32), jax.ShapeDtypeStruct((n, N_PAIRS), F32)],
        input_output_aliases=aliases,
        compiler_params=_cparams(("parallel",)),
        name="peer_route",
    )(*args)


def _unpack(words):
    hi = lax.bitcast_convert_type(words & jnp.uint32(0xFFFF0000), F32)
    lo = lax.bitcast_convert_type(words << 16, F32)
    return hi, lo


def _expert_rows(tab_ref, row0):
    return tab_ref[pl.ds(pl.multiple_of(row0, ROWS_PER_EXPERT), ROWS_PER_EXPERT), :]


REDUCE_TOKENS = 16


def _lane_sums_to_rows(y, n_tok):
    hi = y.astype(BF16)
    lo = (y - hi.astype(F32)).astype(BF16)
    ones = jnp.ones((128, N_PAIRS), BF16)
    s = (_dot(hi, ones) + _dot(lo, ones)).reshape(n_tok, N_PAIRS, N_PAIRS)
    eye = lax.broadcasted_iota(jnp.int32, (N_PAIRS, N_PAIRS), 0) == lax.broadcasted_iota(jnp.int32, (N_PAIRS, N_PAIRS), 1)
    return jnp.sum(jnp.where(eye[None], s, 0.0), axis=1)


CHUNKS = D_MODEL // 128


def _peer_u_kernel(e_ref, h8_ref, gate_ref, tab_ref, w_ref, prod_ref, ys_ref, act_ref, *, tt):

    def token(t, slot):
        base = pl.multiple_of(t * CHUNKS, CHUNKS)
        ha = h8_ref[pl.ds(base, ROWS_PER_EXPERT), :]
        hb = h8_ref[pl.ds(pl.multiple_of(base + ROWS_PER_EXPERT, ROWS_PER_EXPERT), ROWS_PER_EXPERT), :]
        prod = prod_ref.at[slot]
        for k in range(N_PAIRS):
            hi, lo = _unpack(_expert_rows(tab_ref, e_ref[t, k]))
            prod[k * ROWS_PER_EXPERT:(k + 1) * ROWS_PER_EXPERT, :] = hi * ha + lo * hb
        y = prod[pl.ds(0, N_PAIRS, stride=ROWS_PER_EXPERT), :]
        for c in range(1, ROWS_PER_EXPERT):
            y = y + prod[pl.ds(c, N_PAIRS, stride=ROWS_PER_EXPERT), :]
        ys_ref[pl.ds(pl.multiple_of(t * N_PAIRS, N_PAIRS), N_PAIRS), :] = y

    def two_tokens(j, carry):
        token(2 * j, 0)
        token(2 * j + 1, 1)
        return carry

    lax.fori_loop(0, tt // 2, two_tokens, 0)

    def group(g, carry):
        rows = REDUCE_TOKENS * N_PAIRS
        y = ys_ref[pl.ds(pl.multiple_of(g * rows, rows), rows), :]
        act_ref[pl.ds(pl.multiple_of(g * REDUCE_TOKENS, REDUCE_TOKENS), REDUCE_TOKENS), :] = _lane_sums_to_rows(y, REDUCE_TOKENS)
        return carry

    lax.fori_loop(0, tt // REDUCE_TOKENS, group, 0)
    w_ref[...] = _gated_gelu(gate_ref[...], act_ref[...])


def _gated_gelu(gate, a):
    return gate * (0.5 * a * (1.0 + lax.erf(a * (2.0 ** -0.5))))


def _gate_act_kernel(act_ref, gate_ref, *rest):
    rest[-1][...] = _gated_gelu(gate_ref[...], act_ref[...])


def _gate_act(act, gate_t, w_t, first_token, tt, after=()):
    off = first_token // tt
    return pl.pallas_call(
        _gate_act_kernel,
        grid=(act.shape[0] // tt,),
        in_specs=[pl.BlockSpec((tt, N_PAIRS), lambda i: (i, 0)),
                  pl.BlockSpec((tt, N_PAIRS), lambda i: (i + off, 0))]
                 + [pl.BlockSpec(memory_space=pl.ANY)] * (1 + len(after)),
        out_specs=pl.BlockSpec((tt, N_PAIRS), lambda i: (i + off, 0)),
        out_shape=jax.ShapeDtypeStruct(w_t.shape, F32),
        input_output_aliases={2: 0},
        compiler_params=_cparams(("arbitrary",)),
        name="peer_gate_act",
    )(act, gate_t, w_t, *after)


def _peer_u(e_t, h, gate_t, tab, tt, n_tokens):
    n = e_t.shape[0]
    assert tt % REDUCE_TOKENS == 0
    kern = functools.partial(_peer_u_kernel, tt=tt)
    pair = lambda **kw: pl.BlockSpec((tt, N_PAIRS), lambda i: (i, 0), **kw)
    return pl.pallas_call(
        kern,
        grid=(n_tokens // tt,),
        in_specs=[pair(memory_space=pltpu.SMEM),
                  pl.BlockSpec((tt * CHUNKS, 128), lambda i: (i, 0)),
                  pair(),
                  pl.BlockSpec(tab.shape, lambda i: (0, 0), pipeline_mode=pl.Buffered(1))],
        out_specs=pair(),
        out_shape=jax.ShapeDtypeStruct((n, N_PAIRS), F32),
        scratch_shapes=[pltpu.VMEM((2, N_PAIRS * ROWS_PER_EXPERT, 128), F32),
                        pltpu.VMEM((tt * N_PAIRS, 128), F32),
                        pltpu.VMEM((tt, N_PAIRS), F32)],
        compiler_params=_cparams(("arbitrary",)),
        name="peer_u",
    )(e_t, h, gate_t, tab)


def _peer_v_kernel(e_ref, w_ref, x_ref, g_ref, tab_ref, o_ref, acc_ref, *, tt):
    n_acc = 2

    def token(t):
        acc_hi = [jnp.zeros((ROWS_PER_EXPERT, 128), F32) for _ in range(n_acc)]
        acc_lo = [jnp.zeros((ROWS_PER_EXPERT, 128), F32) for _ in range(n_acc)]
        for k in range(N_PAIRS):
            hi, lo = _unpack(_expert_rows(tab_ref, e_ref[t, k]))
            w = w_ref[t, k]
            acc_hi[k % n_acc] = acc_hi[k % n_acc] + w * hi
            acc_lo[k % n_acc] = acc_lo[k % n_acc] + w * lo
        base = pl.multiple_of(t * CHUNKS, CHUNKS)
        acc_ref[pl.ds(base, ROWS_PER_EXPERT), :] = sum(acc_hi[1:], acc_hi[0])
        acc_ref[pl.ds(pl.multiple_of(base + ROWS_PER_EXPERT, ROWS_PER_EXPERT), ROWS_PER_EXPERT), :] = sum(acc_lo[1:], acc_lo[0])

    def two_tokens(j, carry):
        token(2 * j)
        token(2 * j + 1)
        return carry

    lax.fori_loop(0, tt // 2, two_tokens, 0)
    _residual_rms_store(x_ref, acc_ref, g_ref, o_ref, tt)


def _residual_rms_store(x_ref, add_ref, g_ref, o_ref, tt):
    xs = [x_ref[:, c * 128:(c + 1) * 128] + add_ref[pl.ds(c, tt, stride=CHUNKS), :] for c in range(CHUNKS)]
    sq = xs[0] * xs[0]
    for x in xs[1:]:
        sq = sq + x * x
    r = lax.rsqrt(jnp.sum(sq, axis=1, keepdims=True) * (1.0 / D_MODEL) + EPS)
    for c in range(CHUNKS):
        o_ref[:, c * 128:(c + 1) * 128] = xs[c] * r * g_ref[:, c * 128:(c + 1) * 128]


def _peer_v_into_kernel(e_ref, w_ref, x_ref, g_ref, tab_ref, y_any_ref, o_ref, acc_ref, *, tt):
    del y_any_ref
    _peer_v_kernel(e_ref, w_ref, x_ref, g_ref, tab_ref, o_ref, acc_ref, tt=tt)


def _peer_v(e_t, w_t, x, g_final, tab, tt, n_tokens, into=None):
    pair = lambda: pl.BlockSpec((tt, N_PAIRS), lambda i: (i, 0), memory_space=pltpu.SMEM)
    row = lambda: pl.BlockSpec((tt, D_MODEL), lambda i: (i, 0))
    in_specs = [pair(), pair(), row(),
                pl.BlockSpec((1, D_MODEL), lambda i: (0, 0)),
                pl.BlockSpec(tab.shape, lambda i: (0, 0), pipeline_mode=pl.Buffered(1))]
    args = [e_t, w_t, x, g_final, tab]
    if into is not None:
        in_specs.append(pl.BlockSpec(memory_space=pl.ANY))
        args.append(into)
    return pl.pallas_call(
        functools.partial(_peer_v_kernel if into is None else _peer_v_into_kernel, tt=tt),
        grid=(n_tokens // tt,),
        in_specs=in_specs,
        out_specs=row(),
        out_shape=jax.ShapeDtypeStruct(x.shape if into is None else into.shape, F32),
        input_output_aliases={} if into is None else {5: 0},
        scratch_shapes=[pltpu.VMEM((tt * CHUNKS, 128), F32)],
        compiler_params=_cparams(("arbitrary",)),
        name="peer_v",
    )(*args)


def _residual_norm_kernel(x_ref, add_ref, g_ref, *rest, tt):
    _residual_rms_store(x_ref, add_ref, g_ref, rest[-1], tt)


def _residual_norm(x, add8, g_final, y, x_first, y_first, tt, out_rows=None, after=()):
    n_rows = add8.shape[0] // CHUNKS
    x_off, y_off = x_first // tt, y_first // tt
    extra = ([] if y is None else [y]) + list(after)
    return pl.pallas_call(
        functools.partial(_residual_norm_kernel, tt=tt),
        grid=(n_rows // tt,),
        in_specs=[pl.BlockSpec((tt, D_MODEL), lambda i: (i + x_off, 0)),
                  pl.BlockSpec((tt * CHUNKS, 128), lambda i: (i, 0)),
                  pl.BlockSpec((1, D_MODEL), lambda i: (0, 0))]
                 + [pl.BlockSpec(memory_space=pl.ANY)] * len(extra),
        out_specs=pl.BlockSpec((tt, D_MODEL), lambda i: (i + y_off, 0)),
        out_shape=jax.ShapeDtypeStruct((out_rows, D_MODEL) if y is None else y.shape, F32),
        input_output_aliases={} if y is None else {3: 0},
        compiler_params=_cparams(("arbitrary",)),
        name="peer_residual_norm",
    )(x, add8, g_final, *extra)


SC_CORES = 2
SC_SUBCORES = 16
SC_LANES = 16
WORD_ROWS = 128 // SC_LANES
SC_TOKEN_CHUNK = 32
SC_RING = 4


def _peer_sc(e_t, aux, tab, down):
    n = e_t.shape[0]
    workers = SC_CORES * SC_SUBCORES
    assert n % (workers * SC_TOKEN_CHUNK) == 0 and PEER_HEADS % SC_RING == 0
    per = n // workers
    head_rows = ROWS_PER_EXPERT * PEER_TOPK
    aux_rows = CHUNKS if down else 1
    out_rows = 1 if down else CHUNKS
    mesh = plsc.VectorSubcoreMesh(core_axis_name="core", subcore_axis_name="subcore",
                                  num_cores=SC_CORES, num_subcores=SC_SUBCORES)

    @pl.kernel(out_type=jax.ShapeDtypeStruct((n * out_rows, 128), F32), mesh=mesh,
               scratch_types=[pltpu.VMEM((SC_TOKEN_CHUNK, N_PAIRS), jnp.int32),
                              pltpu.VMEM((SC_TOKEN_CHUNK * aux_rows, 128), F32),
                              pltpu.VMEM((SC_RING, head_rows, 128), jnp.uint32),
                              pltpu.VMEM((SC_TOKEN_CHUNK * out_rows, 128), F32),
                              pltpu.VMEM((PEER_TOPK, SC_LANES), F32),
                              pltpu.SemaphoreType.DMA((SC_RING,))],
               compiler_params=pltpu.CompilerParams(needs_layout_passes=False),
               name="peer_u_sc" if down else "peer_v_sc")
    def body(e_hbm, aux_hbm, tab_hbm, o_hbm, e_v, aux_v, rows_v, out_v, fold_v, sems):
        wid = lax.axis_index("core") * SC_SUBCORES + lax.axis_index("subcore")
        lane = lax.broadcasted_iota(jnp.int32, (SC_LANES,), 0)

        def gathers(i, p, slot):
            first = e_v[i, pl.ds(p * PEER_TOPK, PEER_TOPK)]
            return [pltpu.make_async_copy(tab_hbm.at[first + r],
                                          rows_v.at[slot, pl.ds(r * PEER_TOPK, PEER_TOPK)], sems.at[slot])
                    for r in range(ROWS_PER_EXPERT)]

        def start_gather(i, p, slot):
            for d in gathers(i, p, slot):
                d.start()

        def weighted_sum(i, p, slot):
            ws = [plsc.load_gather(aux_v, [lane * 0 + i, lane * 0 + (p * PEER_TOPK + k)])
                  for k in range(PEER_TOPK)]
            for r in range(ROWS_PER_EXPERT):
                @pl.loop(0, WORD_ROWS)
                def _(j):
                    sl = pl.ds(j * SC_LANES, SC_LANES)
                    if p == 0:
                        a_hi = jnp.zeros((SC_LANES,), F32)
                        a_lo = jnp.zeros((SC_LANES,), F32)
                    else:
                        a_hi = out_v[i * CHUNKS + r, sl]
                        a_lo = out_v[i * CHUNKS + ROWS_PER_EXPERT + r, sl]
                    for k in range(PEER_TOPK):
                        hi, lo = _unpack(rows_v[slot, r * PEER_TOPK + k, sl])
                        a_hi = a_hi + ws[k] * hi
                        a_lo = a_lo + ws[k] * lo
                    out_v[i * CHUNKS + r, sl] = a_hi
                    out_v[i * CHUNKS + ROWS_PER_EXPERT + r, sl] = a_lo

        def pair_dots(i, p, slot):
            accs = tuple(jnp.zeros((SC_LANES,), F32) for _ in range(PEER_TOPK))
            for r in range(ROWS_PER_EXPERT):
                def piece(j, accs, r=r):
                    sl = pl.ds(j * SC_LANES, SC_LANES)
                    ha = aux_v[i * CHUNKS + r, sl]
                    hb = aux_v[i * CHUNKS + ROWS_PER_EXPERT + r, sl]
                    out = []
                    for k in range(PEER_TOPK):
                        hi, lo = _unpack(rows_v[slot, r * PEER_TOPK + k, sl])
                        out.append(accs[k] + (hi * ha + lo * hb))
                    return tuple(out)
                accs = lax.fori_loop(0, WORD_ROWS, piece, accs)
            for k in range(PEER_TOPK):
                fold_v[k, :] = accs[k]
            tot = plsc.load_gather(fold_v, [lane, lane * 0])
            for l in range(1, SC_LANES):
                tot = tot + plsc.load_gather(fold_v, [lane, lane * 0 + l])
            out_v[i, pl.ds(p * PEER_TOPK, PEER_TOPK)] = tot

        consume = pair_dots if down else weighted_sum

        @pl.loop(0, per // SC_TOKEN_CHUNK)
        def _(c):
            t0 = wid * per + c * SC_TOKEN_CHUNK
            pltpu.sync_copy(e_hbm.at[pl.ds(t0, SC_TOKEN_CHUNK)], e_v)
            pltpu.sync_copy(aux_hbm.at[pl.ds(t0 * aux_rows, SC_TOKEN_CHUNK * aux_rows)], aux_v)
            for p in range(SC_RING):
                start_gather(0, p, p)

            @pl.loop(0, SC_TOKEN_CHUNK)
            def _(i):
                for p in range(PEER_HEADS):
                    slot = p % SC_RING
                    for d in gathers(i, p, slot):
                        d.wait()
                    consume(i, p, slot)
                    if p + SC_RING < PEER_HEADS:
                        start_gather(i, p + SC_RING, slot)
                    else:
                        @pl.when(i + 1 < SC_TOKEN_CHUNK)
                        def _():
                            start_gather(i + 1, p + SC_RING - PEER_HEADS, slot)

            pltpu.sync_copy(out_v, o_hbm.at[pl.ds(t0 * out_rows, SC_TOKEN_CHUNK * out_rows)])

    return body(e_t, aux, tab)


def _pack_table(tab):
    bits = lax.bitcast_convert_type(tab.astype(BF16), jnp.uint16).astype(jnp.uint32)
    words = (bits[:, :HALF] << 16) | bits[:, HALF:]
    return words.reshape(tab.shape[0] * ROWS_PER_EXPERT, 128)


PEER_TOKENS = 128
SC_LEAD_BATCHES = (1, 3)
SC_SHARE_DOWN = (3, 8)
SC_SHARE_UP = (19, 48)
SC_MIN_TOKENS = 4096


def _sc_tokens(n):
    if n < SC_MIN_TOKENS:
        return (0, 0)
    unit = math.lcm(PEER_TOKENS, SC_CORES * SC_SUBCORES * SC_TOKEN_CHUNK)
    return tuple(n * num // den // unit * unit for num, den in (SC_SHARE_DOWN, SC_SHARE_UP))


def _row_tile(n, pref):
    while n % pref:
        pref //= 2
    return pref


def _peer_on_sc_down(h3, qp, k1_bf, k2_bf, tab_u, tt, route_after=()):
    e_t, gate_t = _route(qp, k1_bf, k2_bf, tt, after=route_after)
    return e_t, gate_t, _peer_sc(e_t, h3, tab_u, down=True)


def _peer_on_sc_up(e_t, gate_t, act, tab_v, tt, gate_after):
    w_t = _gate_act(act, gate_t, act, 0, tt, after=gate_after)
    return _peer_sc(e_t, w_t, tab_v, down=False), w_t


def _peer_and_final(x2, h3, qp, k1_bf, k2_bf, tab_u, tab_v, g_final, tt, n_sc=(0, 0), route_after=(), out_rows=None,
                    lead=None, y_start=None):
    n = x2.shape[0]
    n_sc_down, n_sc_up = n_sc
    tc_down, tc_up = n - n_sc_down, n - n_sc_up
    if n_sc_down:
        e_t, gate_t = _route(qp, k1_bf, k2_bf, tt, tc_down, n_sc_down, after=route_after)
        act = _peer_sc(e_t[tc_down:], h3[tc_down * CHUNKS:], tab_u, down=True)
        e_t, gate_t = _route(qp, k1_bf, k2_bf, tt, 0, tc_down, into=(e_t, gate_t))
    else:
        e_t, gate_t = _route(qp, k1_bf, k2_bf, tt)
    w_t = _peer_u(e_t, h3, gate_t, tab_u, tt, tc_down)
    y = y_start
    if lead is not None:
        x2_lead, add8_lead, y_first = lead
        y = _residual_norm(x2_lead, add8_lead, g_final, y, 0, y_first, tt, out_rows=out_rows, after=(w_t,))
    if n_sc_down:
        w_t = _gate_act(act, gate_t, w_t, tc_down, tt, after=() if y is None else (y,))
    y = _peer_v(e_t, w_t, x2, g_final, tab_v, tt, tc_up, into=y)
    if n_sc_up:
        add8 = _peer_sc(e_t[tc_up:], w_t[tc_up:], tab_v, down=False)
        y = _residual_norm(x2, add8, g_final, y, tc_up, tc_up, tt)
    return y


def kernel(x_prompt, x_sample, mem_prompt, cache_da_k, cache_da_v, state_ret, cache_mem_k, cache_mem_v, g_mix, w_in, lam_q1, lam_k1, lam_q2, lam_k2, g_da, g_ret, w_out, g_cross, g_mem, w_xq, w_xk, w_xv, w_xo, g_ffn, w_pq, peer_k1, peer_k2, peer_u, peer_v, g_final):
    depth = w_in.shape[0]
    assert depth == 1, "single-layer step"
    l = 0
    lam_init = 0.8 - 0.6 * math.exp(-0.3 * l)
    b, t, _ = x_prompt.shape
    bs, ts, _ = x_sample.shape
    past_len = cache_da_k.shape[2]

    row = lambda a: a.reshape(1, -1)
    w_in_bf = w_in[l].astype(BF16)
    w_out_bf = w_out[l].astype(BF16)
    w_xq_bf, w_xk_bf, w_xv_bf, w_xo_bf = (w[l].astype(BF16) for w in (w_xq, w_xk, w_xv, w_xo))
    w_pq_bf = w_pq[l].astype(BF16)
    k1_bf, k2_bf = peer_k1[l].astype(BF16), peer_k2[l].astype(BF16)
    tab_u, tab_v = _pack_table(peer_u[l]), _pack_table(peer_v[l])
    lamp = jnp.stack([lam_q1[l], lam_k1[l], lam_q2[l], lam_k2[l]])
    g_da3 = g_da[l].reshape(H_D, 1, DV_D)
    g_ret3 = g_ret[l].reshape(H_R, 1, DV_R)
    g_fin = row(g_final)

    def mixer_mid(x2d, first_row, mda, mret, mk, mv, bb, tt_rows):
        n = bb * tt_rows
        return _mix_cross(x2d, first_row, mda.reshape(n, GROUP_W), mret.reshape(n, GROUP_W), w_out_bf, row(g_cross[l]),
                          w_xq_bf, mk, mv, w_xo_bf, row(g_ffn[l]), w_pq_bf, bb, tt_rows, _row_tile(tt_rows, 512))

    peer_args = (k1_bf, k2_bf, tab_u, tab_v, g_fin, PEER_TOKENS)
    n = b * t
    xp = x_prompt.reshape(n, D_MODEL)

    def prompt_dense(b0, bg, kv_into=None):
        rows = bg * t
        qd, kd, vd, kdb, vdb, qr, kr, vr, gr = _in_proj(xp, row(g_mix[l]), w_in_bf, _row_tile(rows, 512),
                                                       b0 * t, rows, kv_into)
        r3 = lambda a: a.reshape(bg, t, GROUP_W)
        mda = _diff_attn_prompt(lamp, r3(qd), r3(kdb), r3(vdb), g_da3, lam_init, 512, 512)
        mret, s_fin = _retention(r3(qr), r3(kr), r3(vr), r3(gr), g_ret3, None, 512)
        return mixer_mid(xp, b0 * t, mda, mret, mkb[b0:b0 + bg], mvb[b0:b0 + bg], bg, t), (kd, vd), s_fin

    mk, mv, mkb, mvb = _mem_kv(mem_prompt.reshape(b * N_MEM, D_MODEL), row(g_mem[l]), w_xk_bf, w_xv_bf, 512)
    mkb, mvb = mkb.reshape(b, N_MEM, D_MODEL), mvb.reshape(b, N_MEM, D_MODEL)
    leads = [bg for bg in SC_LEAD_BATCHES if bg] if sum(SC_LEAD_BATCHES) < b else []
    kv, states = None, []
    y = None
    launched = None
    finished = None
    hi = b
    for bg in leads + [hi - sum(leads)]:
        b0, hi = hi - bg, hi - bg
        peer_in, kv, s_g = prompt_dense(b0, bg, kv_into=kv)
        states.insert(0, s_g)
        x2_g, h3_g, qp_g = peer_in
        if finished is not None:
            y = _residual_norm(finished[0], finished[1], g_fin, y, 0, finished[2], PEER_TOKENS, out_rows=n, after=(qp_g,))
            finished = None
        route_after = ()
        if launched is not None:
            x2_p, e_p, gate_p, act_p, row_p = launched
            add8_p, w_p = _peer_on_sc_up(e_p, gate_p, act_p, tab_v, PEER_TOKENS,
                                         gate_after=(qp_g,) + (() if y is None else (y,)))
            finished, route_after = (x2_p, add8_p, row_p), (w_p,)
        if b0 > 0:
            launched = (x2_g,) + _peer_on_sc_down(h3_g, qp_g, k1_bf, k2_bf, tab_u, PEER_TOKENS, route_after) + (b0 * t,)
        else:
            y = _peer_and_final(*peer_in, *peer_args, _sc_tokens(bg * t), route_after=route_after, out_rows=n,
                                lead=finished, y_start=y)
    kd, vd = kv
    y_prompt = y.reshape(b, t, D_MODEL)
    s_fin = jnp.concatenate(states) if len(states) > 1 else states[0]

    ns = bs * ts
    xs = x_sample.reshape(ns, D_MODEL)
    qd_s, kd_s, vd_s, kdb_s, vdb_s, qr_s, kr_s, vr_s, gr_s = _in_proj(xs, row(g_mix[l]), w_in_bf, _row_tile(ns, 512))
    s3 = lambda a: a.reshape(bs, ts, GROUP_W)
    mda_s = _diff_attn_sample(lamp, s3(qd_s), cache_da_k[l].reshape(bs, past_len, GROUP_W),
                              cache_da_v[l].reshape(bs, past_len, GROUP_W), s3(kdb_s), s3(vdb_s), g_da3, lam_init)
    mret_s, s_new = _retention(s3(qr_s), s3(kr_s), s3(vr_s), s3(gr_s), g_ret3, state_ret[l], ts)
    peer_s = mixer_mid(xs, 0, mda_s, mret_s, cache_mem_k[l].reshape(bs, N_MEM, D_MODEL),
                       cache_mem_v[l].reshape(bs, N_MEM, D_MODEL), bs, ts)
    y_sample = _peer_and_final(*peer_s, *peer_args).reshape(bs, ts, D_MODEL)

    return (y_prompt, y_sample,
            kd.reshape(1, b, t, H_D, 2, DK_D), vd.reshape(1, b, t, H_D, DV_D), s_fin[None],
            mk.reshape(1, b, N_MEM, H_X, HD_X), mv.reshape(1, b, N_MEM, H_X, HD_X),
            kd_s.reshape(1, bs, ts, H_D, 2, DK_D), vd_s.reshape(1, bs, ts, H_D, DV_D), s_new[None])
```
